```python
import math
import jax
import jax.numpy as jnp
from jax import lax
import numpy as np

D_MODEL = 1024
BATCH = 8
SEQ = 2048
DEPTH = 4
DEC_BATCH = 128
DEC_SEQ = 4
PAST_LEN = 2048
PAGE_SIZE = 128

N_A_LAYERS = DEPTH // 2
N_B_LAYERS = DEPTH - N_A_LAYERS
N_DENSE = (DEPTH + 1) // 2
N_MOE = DEPTH // 2

HEAD_DIM = 64
EPS = 1e-6
NEG = -1e30
ROPE_THETA = 10000.0

MEM_HEADS = 4
MEM_LEN = 256
MEM_W = MEM_HEADS * HEAD_DIM

A_HEADS = 6
A_DK = 128
A_DV = 128
A_W = A_HEADS * A_DV
CONV_W = 4
CONV_DIM = 3 * A_W
DELTA_CHUNK = 64
A_IN = MEM_W + CONV_DIM + A_W + 2 * A_HEADS

B_HEADS = 12
B_KV_HEADS = 2
B_GROUP = B_HEADS // B_KV_HEADS
B_W = B_HEADS * HEAD_DIM
B_IN = MEM_W + B_W + 3 * B_HEADS
KV_SHARED = 6 * B_KV_HEADS * HEAD_DIM
CMP_BLK = 32
CMP_STRIDE = 16
CMP_HID = 256
SEL_BLK = 64
SEL_TOP = 16
WINDOW = 512
Q_BLOCK = 128

D_FF = 2816
N_EXPERTS = 8
TOP_K = 2
D_FF_EXPERT = 3584

kernel_name = 'yoco_deltanet_nsa_memxattn_step'


def rmsnorm(x, g):
    xf = x.astype(jnp.float32)
    y = xf * lax.rsqrt(jnp.mean(xf * xf, axis=-1, keepdims=True) + EPS)
    return (y * g.astype(jnp.float32)).astype(x.dtype)


def l2norm(x):
    xf = x.astype(jnp.float32)
    return (xf * lax.rsqrt(jnp.sum(xf * xf, axis=-1, keepdims=True) + EPS)).astype(x.dtype)


def rope(x, pos):
    half = x.shape[-1] // 2
    inv = ROPE_THETA ** (-jnp.arange(half, dtype=jnp.float32) / half)
    ang = pos.astype(jnp.float32)[:, None] * inv[None, :]
    cos = jnp.cos(ang)[:, None, :].astype(x.dtype)
    sin = jnp.sin(ang)[:, None, :].astype(x.dtype)
    x1, x2 = x[..., :half], x[..., half:]
    return jnp.concatenate([x1 * cos - x2 * sin, x2 * cos + x1 * sin], axis=-1)


def masked_softmax(s, mask):
    s = jnp.where(mask, s.astype(jnp.float32), NEG)
    p = jax.nn.softmax(s, axis=-1)
    return jnp.where(mask, p, 0.0)


def swiglu(x, w_gu, w_down):
    g, u = jnp.split(x @ w_gu, 2, axis=-1)
    return (jax.nn.silu(g) * u) @ w_down


def moe_swiglu(x, w_router, w_gu, w_down):
    logits = (x @ w_router).astype(jnp.float32)
    top_val, top_idx = lax.top_k(logits, TOP_K)
    top_w = jax.nn.softmax(top_val, axis=-1)
    gate = jnp.sum(jax.nn.one_hot(top_idx, N_EXPERTS, dtype=jnp.float32) * top_w[..., None], axis=-2)
    y = jnp.zeros(x.shape, jnp.float32)
    for e in range(N_EXPERTS):
        y = y + gate[..., e:e + 1] * swiglu(x, w_gu[e], w_down[e]).astype(jnp.float32)
    return y.astype(x.dtype)


def mem_attend(qm, mk, mv):
    s = jnp.einsum('blhd,bmhd->blhm', qm, mk).astype(jnp.float32) * (HEAD_DIM ** -0.5)
    p = jax.nn.softmax(s, axis=-1)
    return jnp.einsum('blhm,bmhd->blhd', p.astype(mv.dtype), mv)


def gated_delta_rule(q, k, v, g, beta, s0):
    out_dtype = v.dtype
    bn, L, H, dk = k.shape
    dv = v.shape[-1]
    C = min(DELTA_CHUNK, L)
    pad = (-L) % C
    q, k, v, g, beta = (t.astype(jnp.float32) for t in (q, k, v, g, beta))
    if pad:
        q, k, v, g, beta = (jnp.pad(t, [(0, 0), (0, pad)] + [(0, 0)] * (t.ndim - 2)) for t in (q, k, v, g, beta))
    nc = (L + pad) // C

    def chunks(t):
        return jnp.moveaxis(t.reshape((bn, nc, C) + t.shape[2:]), 3, 1)

    q, k, v, g, beta = (chunks(t) for t in (q, k, v, g, beta))
    gc = jnp.cumsum(g, axis=-1)
    tri = jnp.tril(jnp.ones((C, C), bool))
    strict = jnp.tril(jnp.ones((C, C), bool), -1)
    diff = gc[..., :, None] - gc[..., None, :]
    decay = jnp.where(tri, jnp.exp(jnp.where(tri, diff, 0.0)), 0.0)
    kb = k * beta[..., None]
    lmat = jnp.where(strict, jnp.einsum('bhnid,bhnjd->bhnij', kb, k) * decay, 0.0)
    eye = jnp.eye(C, dtype=jnp.float32)
    tinv = lax.linalg.triangular_solve(eye + lmat, jnp.broadcast_to(eye, lmat.shape),
                                       left_side=True, lower=True, unit_diagonal=True)
    u = tinv @ (v * beta[..., None])
    w = tinv @ (kb * jnp.exp(gc)[..., None])
    a_intra = jnp.where(tri, jnp.einsum('bhnid,bhnjd->bhnij', q, k) * decay, 0.0)
    qg = q * jnp.exp(gc)[..., None]
    kg = k * jnp.exp(gc[..., -1:] - gc)[..., None]
    glast = jnp.exp(gc[..., -1])

    def step(S, xs):
        qg_i, kg_i, u_i, w_i, a_i, gl_i = xs
        v_new = u_i - w_i @ S
        o = qg_i @ S + a_i @ v_new
        S = S * gl_i[..., None, None] + jnp.swapaxes(kg_i, -1, -2) @ v_new
        return S, o

    xs = tuple(jnp.moveaxis(t, 2, 0) for t in (qg, kg, u, w, a_intra, glast))
    S, o = lax.scan(step, s0.astype(jnp.float32), xs)
    o = jnp.moveaxis(jnp.moveaxis(o, 0, 2), 1, 3).reshape(bn, nc * C, H, dv)[:, :L]
    return o.astype(out_dtype), S


def delta_mixer(xa, conv_buf, s0, w_conv, a_log, dt_bias, g_out):
    bn, L, _ = xa.shape
    qkv = xa[..., :CONV_DIM]
    z = xa[..., CONV_DIM:CONV_DIM + A_W]
    b = xa[..., CONV_DIM + A_W:CONV_DIM + A_W + A_HEADS]
    a = xa[..., CONV_DIM + A_W + A_HEADS:]
    xp = jnp.concatenate([conv_buf.astype(qkv.dtype), qkv], axis=1)
    conv = xp[:, 0:L] * w_conv[0]
    for j in range(1, CONV_W):
        conv = conv + xp[:, j:j + L] * w_conv[j]
    new_buf = xp[:, L:]
    q, k, v = jnp.split(jax.nn.silu(conv), 3, axis=-1)
    q = l2norm(q.reshape(bn, L, A_HEADS, A_DK)) * (A_DK ** -0.5)
    k = l2norm(k.reshape(bn, L, A_HEADS, A_DK))
    v = v.reshape(bn, L, A_HEADS, A_DV)
    beta = jax.nn.sigmoid(b.astype(jnp.float32))
    g = -jnp.exp(a_log.astype(jnp.float32)) * jax.nn.softplus(a.astype(jnp.float32) + dt_bias.astype(jnp.float32))
    o, s_new = gated_delta_rule(q, k, v, g, beta, s0)
    o = rmsnorm(o, g_out) * jax.nn.silu(z.reshape(bn, L, A_HEADS, A_DV))
    return o.reshape(bn, L, A_W), new_buf, s_new.astype(s0.dtype)


def shared_kv_rows(h, norm_kv, w_kv):
    bn, L, _ = h.shape
    kv = (rmsnorm(h, norm_kv) @ w_kv).reshape(bn, L, 6, B_KV_HEADS, HEAD_DIM)
    return tuple(kv[:, :, i] for i in range(6))


def compress(rows, pe, w1, w2):
    T = rows.shape[1]
    ncmp = (T - CMP_BLK) // CMP_STRIDE + 1
    idx = jnp.arange(ncmp)[:, None] * CMP_STRIDE + jnp.arange(CMP_BLK)[None, :]
    blk = rows[:, idx] + pe[:, None, :]
    h = jnp.einsum('bnlgd,ldh->bngh', blk, w1.reshape(CMP_BLK, HEAD_DIM, CMP_HID))
    return jax.nn.silu(h) @ w2


def nsa_context(cmp_k_rows, cmp_v_rows, sel_k_rows, sel_v_rows, pe_k, w1_k, w2_k, pe_v, w1_v, w2_v):
    bn, T = sel_k_rows.shape[:2]
    ck = compress(cmp_k_rows, pe_k, w1_k, w2_k)
    cv = compress(cmp_v_rows, pe_v, w1_v, w2_v)
    cpos = jnp.arange(ck.shape[1], dtype=jnp.int32) * CMP_STRIDE + CMP_BLK - 1
    ck = rope(ck, cpos)
    nsel = -(-T // SEL_BLK)
    pad = nsel * SEL_BLK - T

    def blocks(r):
        r = jnp.pad(r, ((0, 0), (0, pad), (0, 0), (0, 0)))
        return r.reshape(bn, nsel, SEL_BLK, B_KV_HEADS, HEAD_DIM).transpose(0, 3, 1, 2, 4)

    return ck, cv, cpos, blocks(sel_k_rows), blocks(sel_v_rows)


def cmp_to_sel_weights(ncmp, nsel):
    per = SEL_BLK // CMP_STRIDE
    sub = CMP_BLK // CMP_STRIDE
    i = jnp.arange(ncmp)[:, None]
    j = jnp.arange(nsel)[None, :]
    m = ((i // per) == j).astype(jnp.float32)
    for r in range(1, sub):
        m = m + (((i + r) // per) == j).astype(jnp.float32)
    return m / sub


def nsa_attend(q, qpos, gates, ck, cv, cpos, sk, sv, wk, wv, wpos):
    scale = HEAD_DIM ** -0.5
    bn, nq = q.shape[:2]
    G, nsel = sk.shape[1], sk.shape[2]
    s_c = jnp.einsum('bqghd,bngd->bqghn', q, ck) * scale
    m_c = (cpos[None, :] <= qpos[:, None])[None, :, None, None, :]
    p_c = masked_softmax(s_c, m_c)
    o_c = jnp.einsum('bqghn,bngd->bqghd', p_c.astype(cv.dtype), cv)
    imp = jnp.einsum('bqgn,ns->bqgs', p_c.sum(axis=3), cmp_to_sel_weights(ck.shape[1], nsel))
    blk = jnp.arange(nsel)[None, :]
    cur = (qpos // SEL_BLK)[:, None]
    valid = blk * SEL_BLK <= qpos[:, None]
    forced = (blk == 0) | (blk == cur) | (blk == cur - 1)
    imp = jnp.where(forced[None, :, None, :], 1e9, jnp.where(valid[None, :, None, :], imp, -1e9))
    _, idx = lax.top_k(imp, min(SEL_TOP, nsel))
    n = idx.shape[-1]
    idx_g = jnp.swapaxes(idx, 1, 2)
    take = jax.vmap(jax.vmap(lambda rows, ii: rows[ii]))
    kg = take(sk, idx_g).reshape(bn, G, nq, n * SEL_BLK, HEAD_DIM)
    vg = take(sv, idx_g).reshape(bn, G, nq, n * SEL_BLK, HEAD_DIM)
    s_s = jnp.einsum('bqghd,bgqkd->bqghk', q, kg) * scale
    pos_s = (idx_g[..., None] * SEL_BLK + jnp.arange(SEL_BLK)).reshape(bn, G, nq, n * SEL_BLK)
    m_s = jnp.swapaxes(pos_s <= qpos[None, None, :, None], 1, 2)[:, :, :, None, :]
    p_s = masked_softmax(s_s, m_s)
    o_s = jnp.einsum('bqghk,bgqkd->bqghd', p_s.astype(vg.dtype), vg)
    s_w = jnp.einsum('bqghd,bwgd->bqghw', q, wk) * scale
    dist = qpos[:, None] - wpos[None, :]
    m_w = ((dist >= 0) & (dist < WINDOW) & (wpos[None, :] >= 0))[None, :, None, None, :]
    p_w = masked_softmax(s_w, m_w)
    o_w = jnp.einsum('bqghw,bwgd->bqghd', p_w.astype(wv.dtype), wv)
    gates = gates.astype(o_c.dtype)
    return gates[..., 0:1] * o_c + gates[..., 1:2] * o_s + gates[..., 2:3] * o_w


def nsa_prompt(q, gates, ck, cv, cpos, sk, sv, wk, wv):
    bn, L = q.shape[:2]
    qb = min(Q_BLOCK, L)
    nqb = L // qb
    wk_pad = jnp.pad(wk, ((0, 0), (WINDOW, 0), (0, 0), (0, 0)))
    wv_pad = jnp.pad(wv, ((0, 0), (WINDOW, 0), (0, 0), (0, 0)))

    def block_fn(i):
        s = i * qb
        qi = lax.dynamic_slice_in_dim(q, s, qb, 1)
        gi = lax.dynamic_slice_in_dim(gates, s, qb, 1)
        qpos = s + jnp.arange(qb, dtype=jnp.int32)
        wki = lax.dynamic_slice_in_dim(wk_pad, s, WINDOW + qb, 1)
        wvi = lax.dynamic_slice_in_dim(wv_pad, s, WINDOW + qb, 1)
        wpos = s - WINDOW + jnp.arange(WINDOW + qb, dtype=jnp.int32)
        return nsa_attend(qi, qpos, gi, ck, cv, cpos, sk, sv, wki, wvi, wpos)

    o = lax.map(block_fn, jnp.arange(nqb, dtype=jnp.int32))
    return jnp.moveaxis(o, 0, 1).reshape(q.shape)


def nsa_queries(xb, pos):
    bn, L, _ = xb.shape
    q = rope(xb[..., :B_W].reshape(bn, L, B_HEADS, HEAD_DIM), pos)
    gates = jax.nn.sigmoid(xb[..., B_W:].astype(jnp.float32)).reshape(bn, L, B_KV_HEADS, B_GROUP, 3)
    return q.reshape(bn, L, B_KV_HEADS, B_GROUP, HEAD_DIM), gates


def gather_pages(pool, page_table):
    pages = pool[page_table]
    return pages.reshape((page_table.shape[0], -1) + pool.shape[2:])


def setup_inputs(seed: int = 0) -> dict:
    key = jax.random.key(seed)
    ks = iter(jax.random.split(key, 64))

    def nrm(shape, scale):
        return jax.random.normal(next(ks), shape, jnp.float32) * scale

    def gain(shape):
        return 1.0 + nrm(shape, 0.02)

    n_pages = PAST_LEN // PAGE_SIZE
    n_pool = (5 * DEC_BATCH * n_pages) // 4
    wb = min(WINDOW, PAST_LEN)
    kvrow = (B_KV_HEADS, HEAD_DIM)
    page_table = jax.random.permutation(next(ks), n_pool)[:DEC_BATCH * n_pages].reshape(DEC_BATCH, n_pages).astype(jnp.int32)
    dt = jnp.exp(jax.random.uniform(next(ks), (N_A_LAYERS, A_HEADS), jnp.float32, math.log(1e-3), math.log(1e-1)))
    a_log = jnp.log(jax.random.uniform(next(ks), (N_A_LAYERS, A_HEADS), jnp.float32, 1.0, 16.0))
    return {
        'x_prompt': nrm((BATCH, SEQ, D_MODEL), 1.0),
        'x_sample': nrm((DEC_BATCH, DEC_SEQ, D_MODEL), 1.0),
        'mem_prompt': nrm((BATCH, MEM_LEN, D_MODEL), 1.0),
        'state_conv': nrm((N_A_LAYERS, DEC_BATCH, CONV_W - 1, CONV_DIM), 1.0),
        'state_ssm': nrm((N_A_LAYERS, DEC_BATCH, A_HEADS, A_DK, A_DV), 0.5),
        'cache_cmp_k': nrm((n_pool, PAGE_SIZE) + kvrow, 1.0),
        'cache_cmp_v': nrm((n_pool, PAGE_SIZE) + kvrow, 1.0),
        'cache_sel_k': nrm((n_pool, PAGE_SIZE) + kvrow, 1.0),
        'cache_sel_v': nrm((n_pool, PAGE_SIZE) + kvrow, 1.0),
        'cache_win_k': nrm((DEC_BATCH, wb) + kvrow, 1.0),
        'cache_win_v': nrm((DEC_BATCH, wb) + kvrow, 1.0),
        'cache_mem_k': nrm((DEPTH, DEC_BATCH, MEM_LEN, MEM_HEADS, HEAD_DIM), 1.0),
        'cache_mem_v': nrm((DEPTH, DEC_BATCH, MEM_LEN, MEM_HEADS, HEAD_DIM), 1.0),
        'page_table': page_table,
        'norm_mix': gain((DEPTH, D_MODEL)),
        'norm_ffn': gain((DEPTH, D_MODEL)),
        'norm_mem': gain((DEPTH, D_MODEL)),
        'w_mem_kv': nrm((DEPTH, D_MODEL, 2 * MEM_W), D_MODEL ** -0.5),
        'w_in_a': nrm((N_A_LAYERS, D_MODEL, A_IN), D_MODEL ** -0.5),
        'conv_w_a': nrm((N_A_LAYERS, CONV_W, CONV_DIM), CONV_W ** -0.5),
        'a_log': a_log,
        'dt_bias': dt + jnp.log(-jnp.expm1(-dt)),
        'norm_out_a': gain((N_A_LAYERS, A_DV)),
        'w_out_a': nrm((N_A_LAYERS, A_W + MEM_W, D_MODEL), (A_W + MEM_W) ** -0.5),
        'w_in_b': nrm((N_B_LAYERS, D_MODEL, B_IN), D_MODEL ** -0.5),
        'w_out_b': nrm((N_B_LAYERS, B_W + MEM_W, D_MODEL), (B_W + MEM_W) ** -0.5),
        'norm_kv': gain((D_MODEL,)),
        'w_kv_shared': nrm((D_MODEL, KV_SHARED), D_MODEL ** -0.5),
        'cmp_pe_k': nrm((CMP_BLK, HEAD_DIM), 0.1),
        'cmp_w1_k': nrm((CMP_BLK * HEAD_DIM, CMP_HID), (CMP_BLK * HEAD_DIM) ** -0.5),
        'cmp_w2_k': nrm((CMP_HID, HEAD_DIM), CMP_HID ** -0.5),
        'cmp_pe_v': nrm((CMP_BLK, HEAD_DIM), 0.1),
        'cmp_w1_v': nrm((CMP_BLK * HEAD_DIM, CMP_HID), (CMP_BLK * HEAD_DIM) ** -0.5),
        'cmp_w2_v': nrm((CMP_HID, HEAD_DIM), CMP_HID ** -0.5),
        'w_gu_dense': nrm((N_DENSE, D_MODEL, 2 * D_FF), D_MODEL ** -0.5),
        'w_down_dense': nrm((N_DENSE, D_FF, D_MODEL), D_FF ** -0.5),
        'w_router': nrm((N_MOE, D_MODEL, N_EXPERTS), D_MODEL ** -0.5),
        'w_gu_exp': nrm((N_MOE, N_EXPERTS, D_MODEL, 2 * D_FF_EXPERT), D_MODEL ** -0.5),
        'w_down_exp': nrm((N_MOE, N_EXPERTS, D_FF_EXPERT, D_MODEL), D_FF_EXPERT ** -0.5),
        'norm_final': gain((D_MODEL,)),
    }


def reference(x_prompt, x_sample, mem_prompt, state_conv, state_ssm, cache_cmp_k, cache_cmp_v,
              cache_sel_k, cache_sel_v, cache_win_k, cache_win_v, cache_mem_k, cache_mem_v, page_table,
              norm_mix, norm_ffn, norm_mem, w_mem_kv, w_in_a, conv_w_a, a_log, dt_bias, norm_out_a, w_out_a,
              w_in_b, w_out_b, norm_kv, w_kv_shared, cmp_pe_k, cmp_w1_k, cmp_w2_k, cmp_pe_v, cmp_w1_v, cmp_w2_v,
              w_gu_dense, w_down_dense, w_router, w_gu_exp, w_down_exp, norm_final):
    bp, lp, _ = x_prompt.shape
    bs, ls, _ = x_sample.shape
    pos_p = jnp.arange(lp, dtype=jnp.int32)
    pos_s = PAST_LEN + jnp.arange(ls, dtype=jnp.int32)
    hp, hs = x_prompt, x_sample
    conv_p, ssm_p, conv_s, ssm_s, mem_k_p, mem_v_p = [], [], [], [], [], []
    for layer in range(DEPTH):
        if layer == N_A_LAYERS:
            ckp, cvp, skp, svp, wkp, wvp = shared_kv_rows(hp, norm_kv, w_kv_shared)
            skp = rope(skp, pos_p)
            wkp = rope(wkp, pos_p)
            cks, cvs, sks, svs, wks, wvs = shared_kv_rows(hs, norm_kv, w_kv_shared)
            sks = rope(sks, pos_s)
            wks = rope(wks, pos_s)
            ctx_p = nsa_context(ckp, cvp, skp, svp, cmp_pe_k, cmp_w1_k, cmp_w2_k, cmp_pe_v, cmp_w1_v, cmp_w2_v)
            ctx_s = nsa_context(jnp.concatenate([gather_pages(cache_cmp_k, page_table), cks], axis=1),
                                jnp.concatenate([gather_pages(cache_cmp_v, page_table), cvs], axis=1),
                                jnp.concatenate([gather_pages(cache_sel_k, page_table), sks], axis=1),
                                jnp.concatenate([gather_pages(cache_sel_v, page_table), svs], axis=1),
                                cmp_pe_k, cmp_w1_k, cmp_w2_k, cmp_pe_v, cmp_w1_v, cmp_w2_v)
            wb = cache_win_k.shape[1]
            wk_s_all = jnp.concatenate([cache_win_k, wks], axis=1)
            wv_s_all = jnp.concatenate([cache_win_v, wvs], axis=1)
            wpos_s = PAST_LEN - wb + jnp.arange(wb + ls, dtype=jnp.int32)
        is_a = layer < N_A_LAYERS
        w_in = w_in_a[layer] if is_a else w_in_b[layer - N_A_LAYERS]
        w_out = w_out_a[layer] if is_a else w_out_b[layer - N_A_LAYERS]
        pp = rmsnorm(hp, norm_mix[layer]) @ w_in
        ps = rmsnorm(hs, norm_mix[layer]) @ w_in
        mkv = (rmsnorm(mem_prompt, norm_mem[layer]) @ w_mem_kv[layer]).reshape(bp, MEM_LEN, 2, MEM_HEADS, HEAD_DIM)
        mk, mv = mkv[:, :, 0], mkv[:, :, 1]
        mem_k_p.append(mk)
        mem_v_p.append(mv)
        om_p = mem_attend(pp[..., :MEM_W].reshape(bp, lp, MEM_HEADS, HEAD_DIM), mk, mv).reshape(bp, lp, MEM_W)
        om_s = mem_attend(ps[..., :MEM_W].reshape(bs, ls, MEM_HEADS, HEAD_DIM),
                          cache_mem_k[layer], cache_mem_v[layer]).reshape(bs, ls, MEM_W)
        if is_a:
            zero_buf = jnp.zeros((bp, CONV_W - 1, CONV_DIM), pp.dtype)
            zero_state = jnp.zeros((bp, A_HEADS, A_DK, A_DV), state_ssm.dtype)
            mo_p, cb_p, sb_p = delta_mixer(pp[..., MEM_W:], zero_buf, zero_state, conv_w_a[layer],
                                           a_log[layer], dt_bias[layer], norm_out_a[layer])
            mo_s, cb_s, sb_s = delta_mixer(ps[..., MEM_W:], state_conv[layer], state_ssm[layer], conv_w_a[layer],
                                           a_log[layer], dt_bias[layer], norm_out_a[layer])
            conv_p.append(cb_p)
            ssm_p.append(sb_p)
            conv_s.append(cb_s)
            ssm_s.append(sb_s)
        else:
            qp, gp = nsa_queries(pp[..., MEM_W:], pos_p)
            mo_p = nsa_prompt(qp, gp, ctx_p[0], ctx_p[1], ctx_p[2], ctx_p[3], ctx_p[4], wkp, wvp).reshape(bp, lp, B_W)
            qs, gs = nsa_queries(ps[..., MEM_W:], pos_s)
            mo_s = nsa_attend(qs, pos_s, gs, ctx_s[0], ctx_s[1], ctx_s[2], ctx_s[3], ctx_s[4],
                              wk_s_all, wv_s_all, wpos_s).reshape(bs, ls, B_W)
        hp = hp + (jnp.concatenate([mo_p.astype(hp.dtype), om_p.astype(hp.dtype)], axis=-1) @ w_out).astype(hp.dtype)
        hs = hs + (jnp.concatenate([mo_s.astype(hs.dtype), om_s.astype(hs.dtype)], axis=-1) @ w_out).astype(hs.dtype)
        fp = rmsnorm(hp, norm_ffn[layer])
        fs = rmsnorm(hs, norm_ffn[layer])
        if layer % 2 == 0:
            d = layer // 2
            hp = hp + swiglu(fp, w_gu_dense[d], w_down_dense[d]).astype(hp.dtype)
            hs = hs + swiglu(fs, w_gu_dense[d], w_down_dense[d]).astype(hs.dtype)
        else:
            m = layer // 2
            hp = hp + moe_swiglu(fp, w_router[m], w_gu_exp[m], w_down_exp[m])
            hs = hs + moe_swiglu(fs, w_router[m], w_gu_exp[m], w_down_exp[m])
    y_prompt = rmsnorm(hp, norm_final)
    y_sample = rmsnorm(hs, norm_final)
    p_state_conv = jnp.stack(conv_p)
    p_state_ssm = jnp.stack(ssm_p)
    s_state_conv = jnp.stack(conv_s)
    s_state_ssm = jnp.stack(ssm_s)
    p_mem_k = jnp.stack(mem_k_p)
    p_mem_v = jnp.stack(mem_v_p)
    p_win_k = wkp[:, -min(WINDOW, lp):]
    p_win_v = wvp[:, -min(WINDOW, lp):]
    s_win_k = wk_s_all[:, -wb:]
    s_win_v = wv_s_all[:, -wb:]
    return (y_prompt, y_sample, p_state_conv, p_state_ssm, ckp, cvp, skp, svp, p_win_k, p_win_v, p_mem_k, p_mem_v,
            s_state_conv, s_state_ssm, cks, cvs, sks, svs, s_win_k, s_win_v)
```

```python
import functools
import math

import jax
import jax.numpy as jnp
from jax import lax
from jax.experimental import pallas as pl
from jax.experimental.pallas import tpu as pltpu

f32 = jnp.float32
bf16 = jnp.bfloat16

EPS = 1e-6
NEG = -1e30
ROPE_THETA = 10000.0
HEAD_DIM = 64
LANES = 128
SUBLANES = 8
VMEM_LIMIT = 48 * 1024 * 1024

D_MODEL = 1024
MEM_LEN = 256
MEM_W = 256
A_HEADS = 6
A_W = 768
CONV_DIM = 2304
A_IN_PAD = 3456
B_HEADS = 12
B_W = 768
B_IN_PAD = 1152
CMP_BLK = 32
CMP_STRIDE = 16
CMP_HID = 256
SEL_BLK = 64
SEL_TOP = 16
WINDOW = 512
Q_BLOCK = 128
D_FF = 2816
N_EXPERTS = 8
D_FF_EXPERT = 3584
PAGE = 128
S_PAD = 8


def _cparams(sem):
    return pltpu.CompilerParams(dimension_semantics=sem, vmem_limit_bytes=VMEM_LIMIT)


def _nt(a, b):
    return lax.dot_general(a, b, (((1,), (1,)), ((), ())), preferred_element_type=f32)


def _tn(a, b):
    return lax.dot_general(a, b, (((0,), (0,)), ((), ())), preferred_element_type=f32)


def _mm(a, b):
    return jnp.dot(a, b, preferred_element_type=f32)


def _rms(x, g):
    return (x * lax.rsqrt(jnp.mean(x * x, axis=-1, keepdims=True) + EPS)) * g


def _sigmoid(x):
    return 1.0 / (1.0 + jnp.exp(-x))


def _silu(x):
    return x * _sigmoid(x)


def _softplus(x):
    return jnp.maximum(x, 0.0) + jnp.log(1.0 + jnp.exp(-jnp.abs(x)))


def _rope_lanes(x, cos, sin_signed):
    lane = lax.broadcasted_iota(jnp.int32, x.shape, x.ndim - 1)
    lo = (lane % HEAD_DIM) < (HEAD_DIM // 2)
    partner = jnp.where(lo, pltpu.roll(x, LANES - HEAD_DIM // 2, axis=x.ndim - 1),
                        pltpu.roll(x, HEAD_DIM // 2, axis=x.ndim - 1))
    return x * cos + partner * sin_signed


def _norm_matmul_kernel(x_ref, g_ref, w_ref, o_ref, xn_ref):
    @pl.when(pl.program_id(1) == 0)
    def _():
        xn_ref[...] = _rms(x_ref[...], g_ref[...]).astype(bf16)

    o_ref[...] = _mm(xn_ref[...], w_ref[...])


def norm_matmul(x, g, w, tm, tn):
    T, K = x.shape
    N = w.shape[1]
    return pl.pallas_call(
        _norm_matmul_kernel,
        grid=(T // tm, N // tn),
        in_specs=[pl.BlockSpec((tm, K), lambda i, j: (i, 0)),
                  pl.BlockSpec((1, K), lambda i, j: (0, 0)),
                  pl.BlockSpec((K, tn), lambda i, j: (0, j))],
        out_specs=pl.BlockSpec((tm, tn), lambda i, j: (i, j)),
        out_shape=jax.ShapeDtypeStruct((T, N), f32),
        scratch_shapes=[pltpu.VMEM((tm, K), bf16)],
        compiler_params=_cparams(("parallel", "arbitrary")),
        name="norm_matmul",
    )(x, g, w)


def _mem_kv_kernel(x_ref, g_ref, w_ref, k_ref, v_ref):
    y = _mm(_rms(x_ref[...], g_ref[...]).astype(bf16), w_ref[...])
    k_ref[...] = y[:, :MEM_W]
    v_ref[...] = y[:, MEM_W:]


def mem_kv(mem, g, w, tm):
    R, K = mem.shape
    NL = w.shape[0]
    out = jax.ShapeDtypeStruct((NL, R, MEM_W), f32)
    return pl.pallas_call(
        _mem_kv_kernel,
        grid=(NL, R // tm),
        in_specs=[pl.BlockSpec((tm, K), lambda l, i: (i, 0)),
                  pl.BlockSpec((None, 1, K), lambda l, i: (l, 0, 0)),
                  pl.BlockSpec((None, K, 2 * MEM_W), lambda l, i: (l, 0, 0))],
        out_specs=(pl.BlockSpec((None, tm, MEM_W), lambda l, i: (l, i, 0)),
                   pl.BlockSpec((None, tm, MEM_W), lambda l, i: (l, i, 0))),
        out_shape=(out, out),
        compiler_params=_cparams(("parallel", "parallel")),
        name="mem_kv",
    )(mem, g, w)


def _final_norm_kernel(x_ref, g_ref, o_ref):
    o_ref[...] = _rms(x_ref[...], g_ref[...])


def final_norm(x, g, tm):
    T, K = x.shape
    return pl.pallas_call(
        _final_norm_kernel,
        grid=(T // tm,),
        in_specs=[pl.BlockSpec((tm, K), lambda i: (i, 0)), pl.BlockSpec((1, K), lambda i: (0, 0))],
        out_specs=pl.BlockSpec((tm, K), lambda i: (i, 0)),
        out_shape=jax.ShapeDtypeStruct((T, K), f32),
        compiler_params=_cparams(("parallel",)),
        name="final_norm",
    )(x, g)


def _mem_attend(q, k, v):
    tq = q.shape[0]
    nh = MEM_W // HEAD_DIM
    lane = lax.broadcasted_iota(jnp.int32, (tq, MEM_W), 1)
    masks = [(lane >= HEAD_DIM * h) & (lane < HEAD_DIM * (h + 1)) for h in range(nh)]
    qs = jnp.concatenate([jnp.where(m, q, 0.0) for m in masks], axis=0).astype(bf16)
    s = _nt(qs, k) * (HEAD_DIM ** -0.5)
    p = jnp.exp(s - jnp.max(s, axis=-1, keepdims=True))
    l = jnp.sum(p, axis=-1, keepdims=True)
    o = _mm(p.astype(bf16), v) / l
    out = jnp.zeros((tq, MEM_W), f32)
    for h in range(nh):
        out = jnp.where(masks[h], o[h * tq:(h + 1) * tq], out)
    return out


def _mem_attn_prompt_kernel(q_ref, k_ref, v_ref, o_ref):
    o_ref[...] = _mem_attend(q_ref[...], k_ref[...].astype(bf16), v_ref[...].astype(bf16))


def mem_attn_prompt(P, mk, mv, nb, L, tq):
    nq = L // tq
    return pl.pallas_call(
        _mem_attn_prompt_kernel,
        grid=(nb, nq),
        in_specs=[pl.BlockSpec((tq, MEM_W), lambda b, i: (b * nq + i, 0)),
                  pl.BlockSpec((MEM_LEN, MEM_W), lambda b, i: (b, 0)),
                  pl.BlockSpec((MEM_LEN, MEM_W), lambda b, i: (b, 0))],
        out_specs=pl.BlockSpec((tq, MEM_W), lambda b, i: (b * nq + i, 0)),
        out_shape=jax.ShapeDtypeStruct((nb * L, MEM_W), f32),
        compiler_params=_cparams(("parallel", "parallel")),
        name="mem_attn_prompt",
    )(P, mk, mv)


def _mem_attn_sample_kernel(q_ref, k_ref, v_ref, o_ref, *, bb):
    for b in range(bb):
        rows = slice(b * S_PAD, (b + 1) * S_PAD)
        o_ref[rows, :] = _mem_attend(q_ref[rows, :], k_ref[b].astype(bf16), v_ref[b].astype(bf16))


def mem_attn_sample(P, row0, ck, cv, bb):
    nb = ck.shape[0]
    blk0 = row0 // (bb * S_PAD)
    return pl.pallas_call(
        functools.partial(_mem_attn_sample_kernel, bb=bb),
        grid=(nb // bb,),
        in_specs=[pl.BlockSpec((bb * S_PAD, MEM_W), lambda i: (blk0 + i, 0)),
                  pl.BlockSpec((bb, MEM_LEN, MEM_W), lambda i: (i, 0, 0)),
                  pl.BlockSpec((bb, MEM_LEN, MEM_W), lambda i: (i, 0, 0))],
        out_specs=pl.BlockSpec((bb * S_PAD, MEM_W), lambda i: (i, 0)),
        out_shape=jax.ShapeDtypeStruct((nb * S_PAD, MEM_W), f32),
        compiler_params=_cparams(("parallel",)),
        name="mem_attn_sample",
    )(P, ck, cv)


def _delta_kernel(q_ref, k_ref, v_ref, z_ref, ba_ref, bq_ref, bk_ref, bv_ref, cq_ref, ck_ref, cv_ref,
                  ab_ref, go_ref, s0_ref, o_ref, s_ref, qs_ref, ks_ref, vs_ref, gs_ref, bs_ref,
                  *, L, C, valid):
    h = pl.program_id(1)
    row = lax.broadcasted_iota(jnp.int32, (L, LANES), 0)
    lane = lax.broadcasted_iota(jnp.int32, (L, LANES), 1)
    row8 = lax.broadcasted_iota(jnp.int32, (SUBLANES, LANES), 0)

    def conv(x_ref, b_ref, c_ref):
        x = x_ref[...]
        b8 = b_ref[...]
        c8 = c_ref[...]
        acc = x * c8[3:4, :]
        for s in range(1, 4):
            r = pltpu.roll(x, s, axis=0)
            fix = jnp.where(row8 < s, pltpu.roll(b8, s, axis=0), r[0:SUBLANES])
            r = jnp.concatenate([fix, r[SUBLANES:]], axis=0) if L > SUBLANES else fix
            acc = acc + r * c8[3 - s:4 - s, :]
        return _silu(acc)

    def l2n(x):
        return x * lax.rsqrt(jnp.sum(x * x, axis=-1, keepdims=True) + EPS)

    q = l2n(conv(q_ref, bq_ref, cq_ref)) * (LANES ** -0.5)
    k = l2n(conv(k_ref, bk_ref, ck_ref))
    v = conv(v_ref, bv_ref, cv_ref)

    ba = ba_ref[...]
    ab = ab_ref[...]
    beta_all = _sigmoid(ba)
    g_all = -jnp.exp(ab[0:1, :]) * _softplus(ba + ab[1:2, :])
    beta = jnp.sum(jnp.where(lane == h, beta_all, 0.0), axis=-1, keepdims=True)
    g = jnp.sum(jnp.where(lane == h + A_HEADS, g_all, 0.0), axis=-1, keepdims=True)
    if valid < L:
        live = row < valid
        live1 = lax.broadcasted_iota(jnp.int32, (L, 1), 0) < valid
        k = jnp.where(live, k, 0.0)
        v = jnp.where(live, v, 0.0)
        beta = jnp.where(live1, beta, 0.0)
        g = jnp.where(live1, g, 0.0)
    gc = jnp.broadcast_to(g, (L, LANES))
    rowc = row & (C - 1)
    s = 1
    while s < C:
        gc = gc + jnp.where(rowc >= s, pltpu.roll(gc, s, axis=0), 0.0)
        s *= 2
    qs_ref[...] = q
    ks_ref[...] = k
    vs_ref[...] = v
    gs_ref[...] = gc
    bs_ref[...] = jnp.broadcast_to(beta, (L, LANES))
    s_ref[...] = s0_ref[...]

    ii = lax.broadcasted_iota(jnp.int32, (C, C), 0)
    jj = lax.broadcasted_iota(jnp.int32, (C, C), 1)
    tri = ii >= jj
    strict = ii > jj
    eye = jnp.where(ii == jj, 1.0, 0.0)
    gout = go_ref[...]
    nsteps = int(math.log2(C)) - 1

    def chunk(c, carry):
        r0 = pl.multiple_of(c * C, C)
        rows = pl.ds(r0, C)
        qc = qs_ref[rows, :]
        kc = ks_ref[rows, :]
        vc = vs_ref[rows, :]
        gcc = gs_ref[rows, :]
        bc = bs_ref[rows, :]
        gi = gcc[:, 0:C]
        gj = jnp.sum(jnp.where(ii == jj, gi, 0.0), axis=0, keepdims=True)
        decay = jnp.where(tri, jnp.exp(jnp.where(tri, gi - gj, 0.0)), 0.0)
        kb = kc * bc
        kcb = kc.astype(bf16)
        n = jnp.where(strict, -(_nt(kb.astype(bf16), kcb) * decay), 0.0)
        x = eye + n
        p = n
        for _ in range(nsteps):
            pb = p.astype(bf16)
            p = _mm(pb, pb)
            x = x + _mm(x.astype(bf16), p.astype(bf16))
        xb = x.astype(bf16)
        eg = jnp.exp(gcc)
        u = _mm(xb, (vc * bc).astype(bf16))
        w = _mm(xb, (kb * eg).astype(bf16))
        a = jnp.where(tri, _nt(qc.astype(bf16), kcb) * decay, 0.0)
        glast = gcc[C - 1:C, :]
        qg = qc * eg
        kg = kc * jnp.exp(glast - gcc)
        S = s_ref[...]
        Sb = S.astype(bf16)
        v_new = u - _mm(w.astype(bf16), Sb)
        vnb = v_new.astype(bf16)
        o = _mm(qg.astype(bf16), Sb) + _mm(a.astype(bf16), vnb)
        s_ref[...] = S * jnp.exp(glast) + _tn(kg.astype(bf16), vnb)
        o_ref[rows, :] = _rms(o, gout) * _silu(z_ref[rows, :])
        return carry

    lax.fori_loop(0, L // C, chunk, 0)


def delta_mixer(P, row0, nb, L, C, valid, buf8, conv8, ab8, gout, s0):
    rb0 = row0 // L
    qc0, kc0, vc0, zc0, bac = 2, 8, 14, 20, 26
    tile = lambda c0: pl.BlockSpec((L, LANES), lambda b, h: (rb0 + b, c0 + h))
    bufspec = lambda c0: pl.BlockSpec((None, SUBLANES, LANES), lambda b, h: (b, 0, c0 + h))
    cwspec = lambda c0: pl.BlockSpec((SUBLANES, LANES), lambda b, h: (0, c0 + h))
    return pl.pallas_call(
        functools.partial(_delta_kernel, L=L, C=C, valid=valid),
        grid=(nb, A_HEADS),
        in_specs=[tile(qc0), tile(kc0), tile(vc0), tile(zc0),
                  pl.BlockSpec((L, LANES), lambda b, h: (rb0 + b, bac)),
                  bufspec(0), bufspec(A_HEADS), bufspec(2 * A_HEADS),
                  cwspec(0), cwspec(A_HEADS), cwspec(2 * A_HEADS),
                  pl.BlockSpec((SUBLANES, LANES), lambda b, h: (0, 0)),
                  pl.BlockSpec((1, LANES), lambda b, h: (0, 0)),
                  pl.BlockSpec((None, None, LANES, LANES), lambda b, h: (b, h, 0, 0))],
        out_specs=(pl.BlockSpec((L, LANES), lambda b, h: (b, h)),
                   pl.BlockSpec((None, None, LANES, LANES), lambda b, h: (b, h, 0, 0))),
        out_shape=(jax.ShapeDtypeStruct((nb * L, A_W), f32),
                   jax.ShapeDtypeStruct((nb, A_HEADS, LANES, LANES), f32)),
        scratch_shapes=[pltpu.VMEM((L, LANES), f32)] * 5,
        compiler_params=_cparams(("parallel", "parallel")),
        name="delta_mixer",
    )(P, P, P, P, P, buf8, buf8, buf8, conv8, conv8, conv8, ab8, gout, s0)


def _attn_residual(h_ref, mo_ref, om_ref, wo1_ref, wo2_ref):
    return (h_ref[...] + _mm(mo_ref[...].astype(bf16), wo1_ref[...])
            + _mm(om_ref[...].astype(bf16), wo2_ref[...]))


def _ffn_kernel(h_ref, mo_ref, om_ref, wo1_ref, wo2_ref, g_ref, wg_ref, wu_ref, wd_ref, o_ref, xn_ref, acc_ref):
    k = pl.program_id(1)

    @pl.when(k == 0)
    def _():
        h1 = _attn_residual(h_ref, mo_ref, om_ref, wo1_ref, wo2_ref)
        acc_ref[...] = h1
        xn_ref[...] = _rms(h1, g_ref[...]).astype(bf16)

    xn = xn_ref[...]
    a = _silu(_mm(xn, wg_ref[...])) * _mm(xn, wu_ref[...])
    acc_ref[...] += _mm(a.astype(bf16), wd_ref[...])

    @pl.when(k == pl.num_programs(1) - 1)
    def _():
        o_ref[...] = acc_ref[...]


def outproj_ffn(H, mo, om, w_out, g, w_gu, w_down, tm, tf):
    T, K = H.shape
    FF = w_down.shape[0]
    nk = FF // tf
    mw, ow = mo.shape[1], om.shape[1]
    return pl.pallas_call(
        _ffn_kernel,
        grid=(T // tm, nk),
        in_specs=[pl.BlockSpec((tm, K), lambda i, k: (i, 0)),
                  pl.BlockSpec((tm, mw), lambda i, k: (i, 0)),
                  pl.BlockSpec((tm, ow), lambda i, k: (i, 0)),
                  pl.BlockSpec((mw, K), lambda i, k: (0, 0)),
                  pl.BlockSpec((ow, K), lambda i, k: (mw // ow, 0)),
                  pl.BlockSpec((1, K), lambda i, k: (0, 0)),
                  pl.BlockSpec((K, tf), lambda i, k: (0, k)),
                  pl.BlockSpec((K, tf), lambda i, k: (0, nk + k)),
                  pl.BlockSpec((tf, K), lambda i, k: (k, 0))],
        out_specs=pl.BlockSpec((tm, K), lambda i, k: (i, 0)),
        out_shape=jax.ShapeDtypeStruct((T, K), f32),
        scratch_shapes=[pltpu.VMEM((tm, K), bf16), pltpu.VMEM((tm, K), f32)],
        compiler_params=_cparams(("parallel", "arbitrary")),
        name="outproj_ffn",
    )(H, mo, om, w_out, w_out, g, w_gu, w_gu, w_down)


def _moe_kernel(h_ref, mo_ref, om_ref, wo1_ref, wo2_ref, g_ref, wr_ref, wg_ref, wu_ref, wd_ref,
                o_ref, xn_ref, acc_ref, gate_ref):
    e = pl.program_id(1)
    k = pl.program_id(2)
    tm = h_ref.shape[0]

    @pl.when((e == 0) & (k == 0))
    def _():
        h1 = _attn_residual(h_ref, mo_ref, om_ref, wo1_ref, wo2_ref)
        acc_ref[...] = h1
        xn = _rms(h1, g_ref[...])
        xn_ref[...] = xn.astype(bf16)
        lane = lax.broadcasted_iota(jnp.int32, (tm, LANES), 1)
        logits = jnp.dot(xn, wr_ref[...], preferred_element_type=f32, precision=lax.Precision.HIGHEST)
        logits = jnp.where(lane < N_EXPERTS, logits, -jnp.inf)
        m1 = jnp.max(logits, axis=-1, keepdims=True)
        i1 = jnp.min(jnp.where(logits == m1, lane, LANES), axis=-1, keepdims=True)
        rest = jnp.where(lane == i1, -jnp.inf, logits)
        m2 = jnp.max(rest, axis=-1, keepdims=True)
        i2 = jnp.min(jnp.where(rest == m2, lane, LANES), axis=-1, keepdims=True)
        e2 = jnp.exp(m2 - m1)
        w1 = 1.0 / (1.0 + e2)
        w2 = e2 / (1.0 + e2)
        gate_ref[...] = jnp.where(lane == i1, w1, 0.0) + jnp.where(lane == i2, w2, 0.0)

    xn = xn_ref[...]
    a = _silu(_mm(xn, wg_ref[...])) * _mm(xn, wu_ref[...])
    y = _mm(a.astype(bf16), wd_ref[...])
    lane = lax.broadcasted_iota(jnp.int32, (tm, LANES), 1)
    ge = jnp.sum(jnp.where(lane == e, gate_ref[...], 0.0), axis=-1, keepdims=True)
    acc_ref[...] += ge * y

    @pl.when((e == pl.num_programs(1) - 1) & (k == pl.num_programs(2) - 1))
    def _():
        o_ref[...] = acc_ref[...]


def outproj_moe(H, mo, om, w_out, g, w_router, w_gu, w_down, tm, tf):
    T, K = H.shape
    NE, FF = w_down.shape[0], w_down.shape[1]
    nk = FF // tf
    mw, ow = mo.shape[1], om.shape[1]
    return pl.pallas_call(
        _moe_kernel,
        grid=(T // tm, NE, nk),
        in_specs=[pl.BlockSpec((tm, K), lambda i, e, k: (i, 0)),
                  pl.BlockSpec((tm, mw), lambda i, e, k: (i, 0)),
                  pl.BlockSpec((tm, ow), lambda i, e, k: (i, 0)),
                  pl.BlockSpec((mw, K), lambda i, e, k: (0, 0)),
                  pl.BlockSpec((ow, K), lambda i, e, k: (mw // ow, 0)),
                  pl.BlockSpec((1, K), lambda i, e, k: (0, 0)),
                  pl.BlockSpec((K, LANES), lambda i, e, k: (0, 0)),
                  pl.BlockSpec((None, K, tf), lambda i, e, k: (e, 0, k)),
                  pl.BlockSpec((None, K, tf), lambda i, e, k: (e, 0, nk + k)),
                  pl.BlockSpec((None, tf, K), lambda i, e, k: (e, k, 0))],
        out_specs=pl.BlockSpec((tm, K), lambda i, e, k: (i, 0)),
        out_shape=jax.ShapeDtypeStruct((T, K), f32),
        scratch_shapes=[pltpu.VMEM((tm, K), bf16), pltpu.VMEM((tm, K), f32), pltpu.VMEM((tm, LANES), f32)],
        compiler_params=_cparams(("parallel", "arbitrary", "arbitrary")),
        name="outproj_moe",
    )(H, mo, om, w_out, w_out, g, w_router, w_gu, w_gu, w_down)


def _shared_kv_kernel(x_ref, g_ref, w_ref, cos_ref, sin_ref, ck_ref, cv_ref, sk_ref, sv_ref, wk_ref, wv_ref):
    y = _mm(_rms(x_ref[...], g_ref[...]).astype(bf16), w_ref[...])
    cos = cos_ref[...]
    sin = sin_ref[...]
    ck_ref[...] = y[:, 0 * LANES:1 * LANES]
    cv_ref[...] = y[:, 1 * LANES:2 * LANES]
    sk_ref[...] = _rope_lanes(y[:, 2 * LANES:3 * LANES], cos, sin)
    sv_ref[...] = y[:, 3 * LANES:4 * LANES]
    wk_ref[...] = _rope_lanes(y[:, 4 * LANES:5 * LANES], cos, sin)
    wv_ref[...] = y[:, 5 * LANES:6 * LANES]


def shared_kv(H, g, w, cos, sin, tm):
    T, K = H.shape
    row = pl.BlockSpec((tm, LANES), lambda i: (i, 0))
    out = jax.ShapeDtypeStruct((T, LANES), f32)
    return pl.pallas_call(
        _shared_kv_kernel,
        grid=(T // tm,),
        in_specs=[pl.BlockSpec((tm, K), lambda i: (i, 0)), pl.BlockSpec((1, K), lambda i: (0, 0)),
                  pl.BlockSpec((K, 6 * LANES), lambda i: (0, 0)), row, row],
        out_specs=(row,) * 6,
        out_shape=(out,) * 6,
        compiler_params=_cparams(("parallel",)),
        name="shared_kv",
    )(H, g, w, cos, sin)


def _compress_kernel(*refs, npages):
    nin = 2 * npages
    k_strips = jnp.concatenate([r[...] for r in refs[0:npages]], axis=0) if npages > 1 else refs[0][...]
    v_strips = jnp.concatenate([r[...] for r in refs[npages:nin]], axis=0) if npages > 1 else refs[npages][...]
    (pek_ref, pev_ref, w1k_ref, w1v_ref, w2k_ref, w2v_ref, cos_ref, sin_ref, ck_ref, cv_ref) = refs[nin:]
    n = k_strips.shape[0]
    half = 2 * CMP_HID
    row = lax.broadcasted_iota(jnp.int32, (n, LANES), 0)

    def tokens(strips, pe_ref, w1_ref, w2_ref):
        top = _mm((strips + pe_ref[0:1, :]).astype(bf16), w1_ref[:, 0:half])
        bot = _mm((strips + pe_ref[1:2, :]).astype(bf16), w1_ref[:, half:2 * half])
        hid = top + pltpu.roll(bot, n - 1, axis=0)
        out = _mm(_silu(hid).astype(bf16), w2_ref[...])
        return jnp.where(row < n - 1, out, 0.0)

    ck_ref[...] = _rope_lanes(tokens(k_strips, pek_ref, w1k_ref, w2k_ref), cos_ref[...], sin_ref[...])
    cv_ref[...] = tokens(v_strips, pev_ref, w1v_ref, w2v_ref)


def compress(k_src, v_src, k_maps, v_maps, strip_rows, nb, consts, prefetch=None):
    npages = len(k_maps)
    n = npages * strip_rows
    pek, pev, w1k, w1v, w2k, w2v, cos, sin = consts
    flat = 16 * LANES
    const2 = lambda shape: pl.BlockSpec(shape, lambda b, *_: (0, 0))
    in_specs = ([pl.BlockSpec((None, strip_rows, flat), m) for m in k_maps]
                + [pl.BlockSpec((None, strip_rows, flat), m) for m in v_maps]
                + [const2((2, flat)), const2((2, flat)), const2((flat, 4 * CMP_HID)), const2((flat, 4 * CMP_HID)),
                   const2((2 * CMP_HID, LANES)), const2((2 * CMP_HID, LANES)), const2((n, LANES)), const2((n, LANES))])
    out_spec = pl.BlockSpec((None, n, LANES), lambda b, *_: (b, 0, 0))
    out = jax.ShapeDtypeStruct((nb, n, LANES), f32)
    grid_spec = pltpu.PrefetchScalarGridSpec(
        num_scalar_prefetch=0 if prefetch is None else 1, grid=(nb,),
        in_specs=in_specs, out_specs=(out_spec, out_spec))
    args = ([] if prefetch is None else [prefetch]) + [k_src] * npages + [v_src] * npages + [pek, pev, w1k, w1v, w2k, w2v, cos, sin]
    kern = functools.partial(_compress_kernel, npages=npages)
    if prefetch is not None:
        kern = functools.partial(_drop_first, kern)
    return pl.pallas_call(
        kern, grid_spec=grid_spec, out_shape=(out, out),
        compiler_params=_cparams(("parallel",)), name="compress",
    )(*args)


def _drop_first(fn, _prefetch_ref, *refs):
    return fn(*refs)


def _nsa_queries(p, cos, sin):
    tq = p.shape[0]
    lane = lax.broadcasted_iota(jnp.int32, (tq, LANES), 1)
    stacks = []
    for grp in range(2):
        rows = []
        for hh in range(B_HEADS // 2):
            head = grp * (B_HEADS // 2) + hh
            blk = MEM_W // LANES + head // 2
            x = p[:, blk * LANES:(blk + 1) * LANES]
            x = _rope_lanes(x, cos, sin) * (HEAD_DIM ** -0.5)
            if head % 2 != grp:
                x = pltpu.roll(x, HEAD_DIM, axis=1)
            keep = (lane >= grp * HEAD_DIM) & (lane < (grp + 1) * HEAD_DIM)
            rows.append(jnp.where(keep, x, 0.0))
        stacks.append(jnp.concatenate(rows, axis=0).astype(bf16))
    return stacks


def _masked_softmax_parts(s, mask):
    sm = jnp.where(mask, s, NEG)
    m = jnp.max(sm, axis=-1, keepdims=True)
    p = jnp.where(mask, jnp.exp(sm - m), 0.0)
    return p, jnp.sum(p, axis=-1, keepdims=True)


def _safe_div(o, l):
    return jnp.where(l > 0.0, o / jnp.where(l > 0.0, l, 1.0), 0.0)


def _split3(x):
    hi = x.astype(bf16)
    r1 = x - hi.astype(f32)
    mid = r1.astype(bf16)
    lo = (r1 - mid.astype(f32)).astype(bf16)
    return hi, mid, lo


def _select_blocks(psum, qpos, m_ref, nsel_pad):
    tq = psum.shape[0]
    mb = m_ref[...]
    hi, mid, lo = _split3(psum)
    imp = _mm(hi, mb) + _mm(mid, mb) + _mm(lo, mb)
    lane = lax.broadcasted_iota(jnp.int32, (tq, LANES), 1)
    cur = qpos // SEL_BLK
    forced = (lane == 0) | (lane == cur) | (lane == cur - 1)
    valid = lane * SEL_BLK <= qpos
    imp = jnp.where(forced, 1e9, jnp.where(valid, imp, -1e9))
    rank = jnp.zeros((tq, LANES), f32)
    for j in range(nsel_pad):
        col = imp[:, j:j + 1]
        ahead = jnp.where(col > imp, 1.0, jnp.where(col == imp, jnp.where(lane > j, 1.0, 0.0), 0.0))
        rank = rank + ahead
    return jnp.where((rank < SEL_TOP) & (lane < nsel_pad), 1.0, 0.0)


def _take_group(x, grp):
    return x[:, grp * HEAD_DIM:(grp + 1) * HEAD_DIM]


def _assemble_heads(outs, tq):
    cols = []
    for grp in range(2):
        for hh in range(B_HEADS // 2):
            cols.append(outs[grp][hh * tq:(hh + 1) * tq, :])
    return jnp.concatenate(cols, axis=-1)


def _gate_stack(p, grp, tq):
    gl = _sigmoid(p[:, (MEM_W + B_W):(MEM_W + B_W) + LANES])
    cols = [[], [], []]
    for hh in range(B_HEADS // 2):
        base = (grp * (B_HEADS // 2) + hh) * 3
        for j in range(3):
            cols[j].append(gl[:, base + j:base + j + 1])
    return [jnp.concatenate(c, axis=0) for c in cols]


def _nsa_prompt_kernel(p_ref, cos_ref, sin_ref, ck_ref, cv_ref, sk_ref, sv_ref, wk_ref, wv_ref, m_ref, e_ref,
                       o_ref, *, tq, kc, nsel):
    qb = pl.program_id(1)
    s0 = qb * tq
    nh = B_HEADS // 2
    R = nh * tq
    p = p_ref[...]
    qstacks = _nsa_queries(p, cos_ref[...], sin_ref[...])
    qpos1 = s0 + lax.broadcasted_iota(jnp.int32, (tq, 1), 0)
    qposR = jnp.concatenate([qpos1] * nh, axis=0)
    ckb = ck_ref[...].astype(bf16)
    cvb = cv_ref[...].astype(bf16)
    ncmp_lane = lax.broadcasted_iota(jnp.int32, (R, LANES), 1)
    cpos = ncmp_lane * CMP_STRIDE + (CMP_BLK - 1)
    n_cmp = ck_ref.shape[0] - 1
    cmask = (cpos <= qposR) & (ncmp_lane < n_cmp)
    wstart = pl.multiple_of(jnp.maximum(s0 - WINDOW, 0), tq)
    wlen = WINDOW + tq
    wkb = wk_ref[pl.ds(wstart, wlen), :].astype(bf16)
    wvb = wv_ref[pl.ds(wstart, wlen), :].astype(bf16)
    wpos = wstart + lax.broadcasted_iota(jnp.int32, (R, wlen), 1)
    dist = qposR - wpos
    wmask = (dist >= 0) & (dist < WINDOW)
    nchunks = (s0 + tq + kc - 1) // kc
    outs = []
    for grp in range(2):
        qs = qstacks[grp]
        gates = _gate_stack(p, grp, tq)
        pc, lc = _masked_softmax_parts(_nt(qs, ckb), cmask)
        pcn = _safe_div(pc, lc)
        o_c = _take_group(_mm(pcn.astype(bf16), cvb), grp)
        psum = pcn[0:tq]
        for hh in range(1, nh):
            psum = psum + pcn[hh * tq:(hh + 1) * tq]
        sel = _select_blocks(psum, qpos1, m_ref, nsel).astype(bf16)

        def body(c, carry):
            m_run, l_run, acc = carry
            k0 = pl.multiple_of(c * kc, kc)
            kb = sk_ref[pl.ds(k0, kc), :].astype(bf16)
            vb = sv_ref[pl.ds(k0, kc), :].astype(bf16)
            s = _nt(qs, kb)
            selk = _mm(sel, e_ref[:, pl.ds(k0, kc)])
            selR = jnp.concatenate([selk] * nh, axis=0)
            kpos = k0 + lax.broadcasted_iota(jnp.int32, (R, kc), 1)
            sm = jnp.where(kpos <= qposR, jnp.where(selR > 0.5, s, NEG), NEG)
            m_new = jnp.maximum(m_run, jnp.max(sm, axis=-1, keepdims=True))
            alpha = jnp.exp(m_run - m_new)
            pe = jnp.exp(sm - m_new)
            l_new = alpha * l_run + jnp.sum(pe, axis=-1, keepdims=True)
            acc_new = alpha * acc + _mm(pe.astype(bf16), vb)
            return m_new, l_new, acc_new

        init = (jnp.full((R, 1), NEG, f32), jnp.zeros((R, 1), f32), jnp.zeros((R, LANES), f32))
        m_run, l_run, acc = lax.fori_loop(0, nchunks, body, init)
        o_s = _take_group(acc / l_run, grp)
        pw, lw = _masked_softmax_parts(_nt(qs, wkb), wmask)
        o_w = _take_group(_mm(pw.astype(bf16), wvb) / lw, grp)
        outs.append(gates[0] * o_c + gates[1] * o_s + gates[2] * o_w)
    o_ref[...] = _assemble_heads(outs, tq)


def nsa_prompt(P, cos, sin, ck, cv, sk, sv, wk, wv, m_mat, e_mat, nb, L, tq, kc):
    nq = L // tq
    nsel = L // SEL_BLK
    seq = pl.BlockSpec((L, LANES), lambda b, i: (b, 0))
    cmp_spec = pl.BlockSpec((None, LANES, LANES), lambda b, i: (b, 0, 0))
    return pl.pallas_call(
        functools.partial(_nsa_prompt_kernel, tq=tq, kc=kc, nsel=nsel),
        grid=(nb, nq),
        in_specs=[pl.BlockSpec((tq, B_IN_PAD), lambda b, i: (b * nq + i, 0)),
                  pl.BlockSpec((tq, LANES), lambda b, i: (i, 0)),
                  pl.BlockSpec((tq, LANES), lambda b, i: (i, 0)),
                  cmp_spec, cmp_spec, seq, seq, seq, seq,
                  pl.BlockSpec((LANES, LANES), lambda b, i: (0, 0)),
                  pl.BlockSpec((LANES, L), lambda b, i: (0, 0))],
        out_specs=pl.BlockSpec((tq, B_W), lambda b, i: (b * nq + i, 0)),
        out_shape=jax.ShapeDtypeStruct((nb * L, B_W), f32),
        compiler_params=_cparams(("parallel", "parallel")),
        name="nsa_prompt",
    )(P, cos, sin, ck, cv, sk, sv, wk, wv, m_mat, e_mat)


def _nsa_sample_kernel(*refs, npages, past):
    pt_ref = refs[0]
    del pt_ref
    refs = refs[1:]
    kpages = refs[0:npages]
    vpages = refs[npages:2 * npages]
    (p_ref, cos_ref, sin_ref, ck_ref, cv_ref, nsk_ref, nsv_ref, cwk_ref, cwv_ref, nwk_ref, nwv_ref,
     m_ref, e_ref, o_ref) = refs[2 * npages:]
    tq = S_PAD
    nh = B_HEADS // 2
    R = nh * tq
    p = p_ref[...]
    qstacks = _nsa_queries(p, cos_ref[...], sin_ref[...])
    t1 = lax.broadcasted_iota(jnp.int32, (tq, 1), 0)
    qpos1 = past + t1
    tR = jnp.concatenate([t1] * nh, axis=0)
    qposR = past + tR
    ckb = ck_ref[...].astype(bf16)
    cvb = cv_ref[...].astype(bf16)
    lane = lax.broadcasted_iota(jnp.int32, (R, LANES), 1)
    cmask = ((lane * CMP_STRIDE + (CMP_BLK - 1)) <= qposR) & (lane < ck_ref.shape[0] - 1)
    skb = jnp.concatenate([r[...] for r in kpages], axis=0).astype(bf16)
    svb = jnp.concatenate([r[...] for r in vpages], axis=0).astype(bf16)
    nskb = nsk_ref[...].astype(bf16)
    nsvb = nsv_ref[...].astype(bf16)
    cwkb = cwk_ref[...].astype(bf16)
    cwvb = cwv_ref[...].astype(bf16)
    nwkb = nwk_ref[...].astype(bf16)
    nwvb = nwv_ref[...].astype(bf16)
    wb = cwk_ref.shape[0]
    new_ok = lax.broadcasted_iota(jnp.int32, (R, tq), 1) <= tR
    wdist = qposR - (past - wb + lax.broadcasted_iota(jnp.int32, (R, wb), 1))
    wmask = (wdist >= 0) & (wdist < WINDOW)
    nsel = past // SEL_BLK + 1
    outs = []
    for grp in range(2):
        qs = qstacks[grp]
        gates = _gate_stack(p, grp, tq)
        pc, lc = _masked_softmax_parts(_nt(qs, ckb), cmask)
        pcn = _safe_div(pc, lc)
        o_c = _take_group(_mm(pcn.astype(bf16), cvb), grp)
        psum = pcn[0:tq]
        for hh in range(1, nh):
            psum = psum + pcn[hh * tq:(hh + 1) * tq]
        sel = _select_blocks(psum, qpos1, m_ref, nsel)
        selk = _mm(sel.astype(bf16), e_ref[...])
        selR = jnp.concatenate([selk] * nh, axis=0) > 0.5
        sel_new = jnp.concatenate([sel[:, past // SEL_BLK:past // SEL_BLK + 1]] * nh, axis=0) > 0.5
        s_old = jnp.where(selR, _nt(qs, skb), NEG)
        s_new = jnp.where(new_ok, jnp.where(sel_new, _nt(qs, nskb), NEG), NEG)
        m = jnp.maximum(jnp.max(s_old, axis=-1, keepdims=True), jnp.max(s_new, axis=-1, keepdims=True))
        p_old = jnp.exp(s_old - m)
        p_new = jnp.exp(s_new - m)
        l = jnp.sum(p_old, axis=-1, keepdims=True) + jnp.sum(p_new, axis=-1, keepdims=True)
        o_s = _take_group((_mm(p_old.astype(bf16), svb) + _mm(p_new.astype(bf16), nsvb)) / l, grp)
        w_old = jnp.where(wmask, _nt(qs, cwkb), NEG)
        w_new = jnp.where(new_ok, _nt(qs, nwkb), NEG)
        mw = jnp.maximum(jnp.max(w_old, axis=-1, keepdims=True), jnp.max(w_new, axis=-1, keepdims=True))
        pw_old = jnp.exp(w_old - mw)
        pw_new = jnp.exp(w_new - mw)
        lw = jnp.sum(pw_old, axis=-1, keepdims=True) + jnp.sum(pw_new, axis=-1, keepdims=True)
        o_w = _take_group((_mm(pw_old.astype(bf16), cwvb) + _mm(pw_new.astype(bf16), nwvb)) / lw, grp)
        outs.append(gates[0] * o_c + gates[1] * o_s + gates[2] * o_w)
    o_ref[...] = _assemble_heads(outs, tq)


def nsa_sample(P, row0, page_table, pool_k, pool_v, cos, sin, ck, cv, nsk, nsv, cwk, cwv, nwk, nwv, m_mat, e_mat):
    nb = cwk.shape[0]
    npages = page_table.shape[0] // nb
    past = npages * PAGE
    rb0 = row0 // S_PAD
    page = lambda j: pl.BlockSpec((None, PAGE, LANES), lambda b, pt: (pt[b * npages + j], 0, 0))
    new_rows = pl.BlockSpec((S_PAD, LANES), lambda b, pt: (rb0 + b, 0))
    per_seq = lambda n: pl.BlockSpec((None, n, LANES), lambda b, pt: (b, 0, 0))
    in_specs = ([page(j) for j in range(npages)] + [page(j) for j in range(npages)]
                + [pl.BlockSpec((S_PAD, B_IN_PAD), lambda b, pt: (rb0 + b, 0)),
                   pl.BlockSpec((S_PAD, LANES), lambda b, pt: (0, 0)),
                   pl.BlockSpec((S_PAD, LANES), lambda b, pt: (0, 0)),
                   per_seq(LANES), per_seq(LANES), new_rows, new_rows,
                   per_seq(cwk.shape[1]), per_seq(cwk.shape[1]), new_rows, new_rows,
                   pl.BlockSpec((LANES, LANES), lambda b, pt: (0, 0)),
                   pl.BlockSpec((LANES, past), lambda b, pt: (0, 0))])
    grid_spec = pltpu.PrefetchScalarGridSpec(
        num_scalar_prefetch=1, grid=(nb,), in_specs=in_specs,
        out_specs=pl.BlockSpec((S_PAD, B_W), lambda b, pt: (b, 0)))
    return pl.pallas_call(
        functools.partial(_nsa_sample_kernel, npages=npages, past=past),
        grid_spec=grid_spec,
        out_shape=jax.ShapeDtypeStruct((nb * S_PAD, B_W), f32),
        compiler_params=_cparams(("parallel",)),
        name="nsa_sample",
    )(page_table, *([pool_k] * npages), *([pool_v] * npages), P, cos, sin, ck, cv, nsk, nsv, cwk, cwv, nwk, nwv,
      m_mat, e_mat)


def _rope_tables(pos):
    half = HEAD_DIM // 2
    inv = ROPE_THETA ** (-jnp.arange(half, dtype=f32) / half)
    ang = pos.astype(f32)[:, None] * inv[None, :]
    cos = jnp.tile(jnp.cos(ang), (1, LANES // half))
    sin = jnp.tile(jnp.sin(ang), (1, LANES // half))
    sign = jnp.where((jnp.arange(LANES) % HEAD_DIM) < half, -1.0, 1.0).astype(f32)
    return cos, sin * sign[None, :]


def _cmp_to_sel(ncmp, nsel):
    per = SEL_BLK // CMP_STRIDE
    sub = CMP_BLK // CMP_STRIDE
    i = jnp.arange(LANES)[:, None]
    j = jnp.arange(LANES)[None, :]
    m = jnp.zeros((LANES, LANES), f32)
    for r in range(sub):
        m = m + (((i + r) // per) == j).astype(f32)
    m = jnp.where((i < ncmp) & (j < nsel), m / sub, 0.0)
    return m.astype(bf16)


def _block_expand(nkeys):
    j = jnp.arange(LANES)[:, None]
    k = jnp.arange(nkeys)[None, :]
    return ((k // SEL_BLK) == j).astype(bf16)


def _compress_consts(pe, w1, w2):
    w1r = w1.reshape(CMP_BLK, HEAD_DIM, CMP_HID)
    eye2 = jnp.eye(2, dtype=f32)
    def expand(w):
        return jnp.einsum('ldc,gh->lgdhc', w, eye2).reshape(16 * LANES, 2 * CMP_HID)
    w1s = jnp.concatenate([expand(w1r[:16]), expand(w1r[16:])], axis=1).astype(bf16)
    pes = jnp.stack([jnp.broadcast_to(pe[:16, None, :], (16, 2, HEAD_DIM)).reshape(-1),
                     jnp.broadcast_to(pe[16:, None, :], (16, 2, HEAD_DIM)).reshape(-1)])
    w2s = jnp.einsum('cd,gh->gchd', w2, eye2).reshape(2 * CMP_HID, LANES).astype(bf16)
    return pes, w1s, w2s


def kernel(x_prompt, x_sample, mem_prompt, state_conv, state_ssm, cache_cmp_k, cache_cmp_v, cache_sel_k, cache_sel_v, cache_win_k, cache_win_v, cache_mem_k, cache_mem_v, page_table, norm_mix, norm_ffn, norm_mem, w_mem_kv, w_in_a, conv_w_a, a_log, dt_bias, norm_out_a, w_out_a, w_in_b, w_out_b, norm_kv, w_kv_shared, cmp_pe_k, cmp_w1_k, cmp_w2_k, cmp_pe_v, cmp_w1_v, cmp_w2_v, w_gu_dense, w_down_dense, w_router, w_gu_exp, w_down_exp, norm_final):
    bp, lp, d = x_prompt.shape
    bs, ls, _ = x_sample.shape
    depth = norm_mix.shape[0]
    n_a = w_in_a.shape[0]
    past = page_table.shape[1] * PAGE
    tp = bp * lp
    T = tp + bs * S_PAD
    tm = 512

    xs = jnp.pad(x_sample, ((0, 0), (0, S_PAD - ls), (0, 0)))
    H = jnp.concatenate([x_prompt.reshape(tp, d), xs.reshape(bs * S_PAD, d)], axis=0)

    w_in_a_b = jnp.pad(w_in_a, ((0, 0), (0, 0), (0, A_IN_PAD - w_in_a.shape[2]))).astype(bf16)
    w_in_b_b = jnp.pad(w_in_b, ((0, 0), (0, 0), (0, B_IN_PAD - w_in_b.shape[2]))).astype(bf16)
    w_out_a_b = w_out_a.astype(bf16)
    w_out_b_b = w_out_b.astype(bf16)
    w_gu_dense_b = w_gu_dense.astype(bf16)
    w_down_dense_b = w_down_dense.astype(bf16)
    w_gu_exp_b = w_gu_exp.astype(bf16)
    w_down_exp_b = w_down_exp.astype(bf16)
    w_router_p = jnp.pad(w_router, ((0, 0), (0, 0), (0, LANES - N_EXPERTS)))

    mem_k, mem_v = mem_kv(mem_prompt.reshape(bp * MEM_LEN, d), norm_mem.reshape(depth, 1, d),
                          w_mem_kv.astype(bf16), tm)
    cmk = cache_mem_k.reshape(depth, bs, MEM_LEN, MEM_W)
    cmv = cache_mem_v.reshape(depth, bs, MEM_LEN, MEM_W)

    pos_p = jnp.arange(lp, dtype=jnp.int32)
    pos_s = past + jnp.arange(S_PAD, dtype=jnp.int32)
    cos_p, sin_p = _rope_tables(pos_p)
    cos_s, sin_s = _rope_tables(pos_s)

    conv_p, ssm_p, conv_s, ssm_s = [], [], [], []
    kv = None
    for layer in range(depth):
        is_a = layer < n_a
        g_mix = norm_mix[layer].reshape(1, d)
        if is_a:
            P = norm_matmul(H, g_mix, w_in_a_b[layer], tm, A_IN_PAD // 3)
        else:
            lb = layer - n_a
            if lb == 0:
                cos_all = jnp.concatenate([jnp.tile(cos_p, (bp, 1)), jnp.tile(cos_s, (bs, 1))], axis=0)
                sin_all = jnp.concatenate([jnp.tile(sin_p, (bp, 1)), jnp.tile(sin_s, (bs, 1))], axis=0)
                kv = shared_kv(H, norm_kv.reshape(1, d), w_kv_shared.astype(bf16), cos_all, sin_all, tm)
                ncmp = (lp - CMP_BLK) // CMP_STRIDE + 1
                cpos = jnp.arange(LANES, dtype=jnp.int32) * CMP_STRIDE + CMP_BLK - 1
                cos_c, sin_c = _rope_tables(cpos)
                pek, w1k, w2k = _compress_consts(cmp_pe_k, cmp_w1_k, cmp_w2_k)
                pev, w1v, w2v = _compress_consts(cmp_pe_v, cmp_w1_v, cmp_w2_v)
                consts = (pek, pev, w1k, w1v, w2k, w2v, cos_c, sin_c)
                strips = lp // 16
                ck_p, cv_p = compress(kv[0][:tp].reshape(bp, strips, 16 * LANES),
                                      kv[1][:tp].reshape(bp, strips, 16 * LANES),
                                      [lambda b: (b, 0, 0)], [lambda b: (b, 0, 0)], strips, bp, consts)
                npages = page_table.shape[1]
                pt_flat = page_table.reshape(-1)
                pool_strips = PAGE // 16
                pmap = lambda j: (lambda b, pt: (pt[b * npages + j], 0, 0))
                ck_s, cv_s = compress(cache_cmp_k.reshape(-1, pool_strips, 16 * LANES),
                                      cache_cmp_v.reshape(-1, pool_strips, 16 * LANES),
                                      [pmap(j) for j in range(npages)], [pmap(j) for j in range(npages)],
                                      pool_strips, bs, consts, prefetch=pt_flat)
                m_p = _cmp_to_sel(ncmp, lp // SEL_BLK)
                e_p = _block_expand(lp)
                ncmp_s = (past + ls - CMP_BLK) // CMP_STRIDE + 1
                m_s = _cmp_to_sel(ncmp_s, -(-(past + ls) // SEL_BLK))
                e_s = _block_expand(past)
                pool_sk = cache_sel_k.reshape(-1, PAGE, LANES)
                pool_sv = cache_sel_v.reshape(-1, PAGE, LANES)
                cwk = cache_win_k.reshape(bs, -1, LANES)
                cwv = cache_win_v.reshape(bs, -1, LANES)
            P = norm_matmul(H, g_mix, w_in_b_b[lb], tm, B_IN_PAD)
        om_p = mem_attn_prompt(P, mem_k[layer], mem_v[layer], bp, lp, 512)
        om_s = mem_attn_sample(P, tp, cmk[layer], cmv[layer], 8)
        om = jnp.concatenate([om_p, om_s], axis=0)
        if is_a:
            conv8 = jnp.pad(conv_w_a[layer], ((0, SUBLANES - conv_w_a.shape[1]), (0, 0)))
            ab8 = jnp.zeros((SUBLANES, LANES), f32)
            ab8 = ab8.at[0, A_HEADS:2 * A_HEADS].set(a_log[layer]).at[1, A_HEADS:2 * A_HEADS].set(dt_bias[layer])
            gout = norm_out_a[layer].reshape(1, LANES)
            zero_buf = jnp.zeros((bp, SUBLANES, CONV_DIM), f32)
            zero_state = jnp.zeros((bp, A_HEADS, LANES, LANES), f32)
            mo_p, sp = delta_mixer(P, 0, bp, lp, 64, lp, zero_buf, conv8, ab8, gout, zero_state)
            buf_s = jnp.pad(state_conv[layer], ((0, 0), (SUBLANES - state_conv.shape[2], 0), (0, 0)))
            mo_s, ss = delta_mixer(P, tp, bs, S_PAD, S_PAD, ls, buf_s, conv8, ab8, gout, state_ssm[layer])
            conv_p.append(P[:tp].reshape(bp, lp, -1)[:, lp - 3:, MEM_W:MEM_W + CONV_DIM])
            conv_s.append(P[tp:].reshape(bs, S_PAD, -1)[:, ls - 3:ls, MEM_W:MEM_W + CONV_DIM])
            ssm_p.append(sp)
            ssm_s.append(ss)
            w_out = w_out_a_b[layer]
        else:
            mo_p = nsa_prompt(P, cos_p, sin_p, ck_p, cv_p, kv[2], kv[3], kv[4], kv[5], m_p, e_p, bp, lp, Q_BLOCK, 512)
            mo_s = nsa_sample(P, tp, pt_flat, pool_sk, pool_sv, cos_s, sin_s, ck_s, cv_s, kv[2], kv[3], cwk, cwv,
                              kv[4], kv[5], m_s, e_s)
            w_out = w_out_b_b[lb]
        mo = jnp.concatenate([mo_p, mo_s], axis=0)
        g_ffn = norm_ffn[layer].reshape(1, d)
        if layer % 2 == 0:
            H = outproj_ffn(H, mo, om, w_out, g_ffn, w_gu_dense_b[layer // 2], w_down_dense_b[layer // 2], tm, D_FF // 2)
        else:
            H = outproj_moe(H, mo, om, w_out, g_ffn, w_router_p[layer // 2], w_gu_exp_b[layer // 2],
                            w_down_exp_b[layer // 2], tm, D_FF_EXPERT // 7)
    Y = final_norm(H, norm_final.reshape(1, d), tm)

    def rows_p(a):
        return a[:tp].reshape(bp, lp, 2, HEAD_DIM)

    def rows_s(a):
        return a[tp:].reshape(bs, S_PAD, 2, HEAD_DIM)[:, :ls]

    wlen = min(WINDOW, lp)
    return (Y[:tp].reshape(bp, lp, d), Y[tp:].reshape(bs, S_PAD, d)[:, :ls],
            jnp.stack(conv_p), jnp.stack(ssm_p),
            rows_p(kv[0]), rows_p(kv[1]), rows_p(kv[2]), rows_p(kv[3]),
            rows_p(kv[4])[:, lp - wlen:], rows_p(kv[5])[:, lp - wlen:],
            mem_k.reshape(depth, bp, MEM_LEN, 4, HEAD_DIM), mem_v.reshape(depth, bp, MEM_LEN, 4, HEAD_DIM),
            jnp.stack(conv_s), jnp.stack(ssm_s),
            rows_s(kv[0]), rows_s(kv[1]), rows_s(kv[2]), rows_s(kv[3]),
            jnp.concatenate([cache_win_k[:, ls:], rows_s(kv[4])], axis=1),
            jnp.concatenate([cache_win_v[:, ls:], rows_s(kv[5])], axis=1))
```

```python
import functools
import math

import jax
import jax.numpy as jnp
from jax import lax
from jax.experimental import pallas as pl
from jax.experimental.pallas import tpu as pltpu

f32 = jnp.float32
bf16 = jnp.bfloat16

EPS = 1e-6
NEG = -1e30
ROPE_THETA = 10000.0
HEAD_DIM = 64
LANES = 128
SUBLANES = 8
VMEM_LIMIT = 48 * 1024 * 1024

D_MODEL = 1024
MEM_LEN = 256
MEM_W = 256
A_HEADS = 6
A_W = 768
CONV_DIM = 2304
A_IN_PAD = 3456
A_MEMQ_BLK = 12
A_BA_BLK = 26
B_HEADS = 12
B_W = 768
B_IN_PAD = 1152
CMP_BLK = 32
CMP_STRIDE = 16
CMP_HID = 256
SEL_BLK = 64
SEL_TOP = 16
WINDOW = 512
Q_BLOCK = 128
D_FF = 2816
N_EXPERTS = 8
D_FF_EXPERT = 3584
PAGE = 128
S_PAD = 8


def _cparams(sem):
    return pltpu.CompilerParams(dimension_semantics=sem, vmem_limit_bytes=VMEM_LIMIT)


def _nt(a, b):
    return lax.dot_general(a, b, (((1,), (1,)), ((), ())), preferred_element_type=f32)


def _tn(a, b):
    return lax.dot_general(a, b, (((0,), (0,)), ((), ())), preferred_element_type=f32)


def _mm(a, b):
    return jnp.dot(a, b, preferred_element_type=f32)


def _rms(x, g):
    return (x * lax.rsqrt(jnp.mean(x * x, axis=-1, keepdims=True) + EPS)) * g


def _sigmoid(x):
    return 1.0 / (1.0 + jnp.exp(-x))


def _silu(x):
    return x * _sigmoid(x)


def _softplus(x):
    return jnp.maximum(x, 0.0) + jnp.log(1.0 + jnp.exp(-jnp.abs(x)))


def _rope_lanes(x, cos, sin_signed):
    lane = lax.broadcasted_iota(jnp.int32, x.shape, x.ndim - 1)
    lo = (lane % HEAD_DIM) < (HEAD_DIM // 2)
    partner = jnp.where(lo, pltpu.roll(x, LANES - HEAD_DIM // 2, axis=x.ndim - 1),
                        pltpu.roll(x, HEAD_DIM // 2, axis=x.ndim - 1))
    return x * cos + partner * sin_signed


def _norm_matmul_kernel(x_ref, g_ref, w_ref, o_ref, xn_ref):
    @pl.when(pl.program_id(1) == 0)
    def _():
        xn_ref[...] = _rms(x_ref[...], g_ref[...]).astype(bf16)

    o_ref[...] = _mm(xn_ref[...], w_ref[...])


def norm_matmul(x, g, w, tm, tn):
    T, K = x.shape
    N = w.shape[1]
    return pl.pallas_call(
        _norm_matmul_kernel,
        grid=(T // tm, N // tn),
        in_specs=[pl.BlockSpec((tm, K), lambda i, j: (i, 0)),
                  pl.BlockSpec((1, K), lambda i, j: (0, 0)),
                  pl.BlockSpec((K, tn), lambda i, j: (0, j))],
        out_specs=pl.BlockSpec((tm, tn), lambda i, j: (i, j)),
        out_shape=jax.ShapeDtypeStruct((T, N), f32),
        scratch_shapes=[pltpu.VMEM((tm, K), bf16)],
        compiler_params=_cparams(("parallel", "arbitrary")),
        name="norm_matmul",
    )(x, g, w)


def _mem_kv_kernel(x_ref, g_ref, w_ref, k_ref, v_ref):
    y = _mm(_rms(x_ref[...], g_ref[...]).astype(bf16), w_ref[...])
    k_ref[...] = y[:, :MEM_W]
    v_ref[...] = y[:, MEM_W:]


def mem_kv(mem, g, w, tm):
    R, K = mem.shape
    NL = w.shape[0]
    out = jax.ShapeDtypeStruct((NL, R, MEM_W), f32)
    return pl.pallas_call(
        _mem_kv_kernel,
        grid=(NL, R // tm),
        in_specs=[pl.BlockSpec((tm, K), lambda l, i: (i, 0)),
                  pl.BlockSpec((None, 1, K), lambda l, i: (l, 0, 0)),
                  pl.BlockSpec((None, K, 2 * MEM_W), lambda l, i: (l, 0, 0))],
        out_specs=(pl.BlockSpec((None, tm, MEM_W), lambda l, i: (l, i, 0)),
                   pl.BlockSpec((None, tm, MEM_W), lambda l, i: (l, i, 0))),
        out_shape=(out, out),
        compiler_params=_cparams(("parallel", "parallel")),
        name="mem_kv",
    )(mem, g, w)


def _final_norm_kernel(x_ref, g_ref, o_ref):
    o_ref[...] = _rms(x_ref[...], g_ref[...])


def final_norm(x, g, tm):
    T, K = x.shape
    return pl.pallas_call(
        _final_norm_kernel,
        grid=(T // tm,),
        in_specs=[pl.BlockSpec((tm, K), lambda i: (i, 0)), pl.BlockSpec((1, K), lambda i: (0, 0))],
        out_specs=pl.BlockSpec((tm, K), lambda i: (i, 0)),
        out_shape=jax.ShapeDtypeStruct((T, K), f32),
        compiler_params=_cparams(("parallel",)),
        name="final_norm",
    )(x, g)


def _mem_attend(q, k, v):
    tq = q.shape[0]
    nh = MEM_W // HEAD_DIM
    lane = lax.broadcasted_iota(jnp.int32, (tq, MEM_W), 1)
    masks = [(lane >= HEAD_DIM * h) & (lane < HEAD_DIM * (h + 1)) for h in range(nh)]
    qs = jnp.concatenate([jnp.where(m, q, 0.0) for m in masks], axis=0).astype(bf16)
    s = _nt(qs, k) * (HEAD_DIM ** -0.5)
    p = jnp.exp(s - jnp.max(s, axis=-1, keepdims=True))
    l = jnp.sum(p, axis=-1, keepdims=True)
    o = _mm(p.astype(bf16), v) / l
    out = jnp.zeros((tq, MEM_W), f32)
    for h in range(nh):
        out = jnp.where(masks[h], o[h * tq:(h + 1) * tq], out)
    return out


def _mem_attn_prompt_kernel(q_ref, k_ref, v_ref, o_ref):
    o_ref[...] = _mem_attend(q_ref[...], k_ref[...].astype(bf16), v_ref[...].astype(bf16))


def mem_attn_prompt(P, qblk, mk, mv, nb, L, tq):
    nq = L // tq
    return pl.pallas_call(
        _mem_attn_prompt_kernel,
        grid=(nb, nq),
        in_specs=[pl.BlockSpec((tq, MEM_W), lambda b, i: (b * nq + i, qblk)),
                  pl.BlockSpec((MEM_LEN, MEM_W), lambda b, i: (b, 0)),
                  pl.BlockSpec((MEM_LEN, MEM_W), lambda b, i: (b, 0))],
        out_specs=pl.BlockSpec((tq, MEM_W), lambda b, i: (b * nq + i, 0)),
        out_shape=jax.ShapeDtypeStruct((nb * L, MEM_W), f32),
        compiler_params=_cparams(("parallel", "parallel")),
        name="mem_attn_prompt",
    )(P, mk, mv)


def _mem_attn_sample_kernel(q_ref, k_ref, v_ref, o_ref, *, bb):
    for b in range(bb):
        rows = slice(b * S_PAD, (b + 1) * S_PAD)
        o_ref[rows, :] = _mem_attend(q_ref[rows, :], k_ref[b].astype(bf16), v_ref[b].astype(bf16))


def mem_attn_sample(P, qblk, row0, ck, cv, bb):
    nb = ck.shape[0]
    blk0 = row0 // (bb * S_PAD)
    return pl.pallas_call(
        functools.partial(_mem_attn_sample_kernel, bb=bb),
        grid=(nb // bb,),
        in_specs=[pl.BlockSpec((bb * S_PAD, MEM_W), lambda i: (blk0 + i, qblk)),
                  pl.BlockSpec((bb, MEM_LEN, MEM_W), lambda i: (i, 0, 0)),
                  pl.BlockSpec((bb, MEM_LEN, MEM_W), lambda i: (i, 0, 0))],
        out_specs=pl.BlockSpec((bb * S_PAD, MEM_W), lambda i: (i, 0)),
        out_shape=jax.ShapeDtypeStruct((nb * S_PAD, MEM_W), f32),
        compiler_params=_cparams(("parallel",)),
        name="mem_attn_sample",
    )(P, ck, cv)


def _delta_kernel(q_ref, k_ref, v_ref, z_ref, ba_ref, buf_ref, cw_ref, ab_ref, go_ref, s0_ref,
                  o_ref, sout_ref, qs_ref, ks_ref, vs_ref, gs_ref, bs_ref, tail_ref, s_ref,
                  *, TL, C, valid):
    t = pl.program_id(1)

    @pl.when(t == 0)
    def _():
        tail_ref[...] = buf_ref[...]
        s_ref[...] = s0_ref[...]

    row8 = lax.broadcasted_iota(jnp.int32, (SUBLANES, A_W), 0)

    def conv(x_ref, c0):
        x = x_ref[...]
        b8 = tail_ref[:, c0:c0 + A_W]
        c8 = cw_ref[:, c0:c0 + A_W]
        acc = x * c8[3:4, :]
        for s in range(1, 4):
            r = pltpu.roll(x, s, axis=0)
            fix = jnp.where(row8 < s, pltpu.roll(b8, s, axis=0), r[0:SUBLANES])
            r = jnp.concatenate([fix, r[SUBLANES:]], axis=0) if TL > SUBLANES else fix
            acc = acc + r * c8[3 - s:4 - s, :]
        tail_ref[:, c0:c0 + A_W] = x[TL - SUBLANES:TL]
        return _silu(acc)

    q = conv(q_ref, 0)
    k = conv(k_ref, A_W)
    v = conv(v_ref, 2 * A_W)
    ba = ba_ref[...]
    ab = ab_ref[...]
    beta_all = _sigmoid(ba)
    g_all = -jnp.exp(ab[0:1, :]) * _softplus(ba + ab[1:2, :])
    if valid < TL:
        live = lax.broadcasted_iota(jnp.int32, (TL, A_W), 0) < valid
        live1 = lax.broadcasted_iota(jnp.int32, (TL, LANES), 0) < valid
        k = jnp.where(live, k, 0.0)
        v = jnp.where(live, v, 0.0)
        beta_all = jnp.where(live1, beta_all, 0.0)
        g_all = jnp.where(live1, g_all, 0.0)
    rowc = lax.broadcasted_iota(jnp.int32, (TL, LANES), 0) & (C - 1)
    gc_all = g_all
    s = 1
    while s < C:
        gc_all = gc_all + jnp.where(rowc >= s, pltpu.roll(gc_all, s, axis=0), 0.0)
        s *= 2
    vs_ref[...] = v
    for h in range(A_HEADS):
        sl = slice(h * LANES, (h + 1) * LANES)
        qh = q[:, sl]
        kh = k[:, sl]
        qs_ref[:, sl] = qh * lax.rsqrt(jnp.sum(qh * qh, axis=-1, keepdims=True) + EPS) * (LANES ** -0.5)
        ks_ref[:, sl] = kh * lax.rsqrt(jnp.sum(kh * kh, axis=-1, keepdims=True) + EPS)
        gs_ref[:, sl] = jnp.broadcast_to(gc_all[:, A_HEADS + h:A_HEADS + h + 1], (TL, LANES))
        bs_ref[:, sl] = jnp.broadcast_to(beta_all[:, h:h + 1], (TL, LANES))

    ii = lax.broadcasted_iota(jnp.int32, (C, C), 0)
    jj = lax.broadcasted_iota(jnp.int32, (C, C), 1)
    tri = ii >= jj
    strict = ii > jj
    eye = jnp.where(ii == jj, 1.0, 0.0)
    gout = go_ref[...]
    nsteps = int(math.log2(C)) - 1

    heads = range(A_HEADS)
    lanes_of = [slice(h * LANES, (h + 1) * LANES) for h in heads]

    def chunk(c, carry):
        rows = pl.ds(pl.multiple_of(c * C, C), C)
        qc = [qs_ref[rows, sl] for sl in lanes_of]
        kc = [ks_ref[rows, sl] for sl in lanes_of]
        vc = [vs_ref[rows, sl] for sl in lanes_of]
        gcc = [gs_ref[rows, sl] for sl in lanes_of]
        bc = [bs_ref[rows, sl] for sl in lanes_of]
        kb = [kc[h] * bc[h] for h in heads]
        kq = [_nt(jnp.concatenate([kb[h], qc[h]], axis=0).astype(bf16), kc[h].astype(bf16)) for h in heads]
        decay = []
        for h in heads:
            gi = gcc[h][:, 0:C]
            gj = jnp.sum(jnp.where(ii == jj, gi, 0.0), axis=0, keepdims=True)
            decay.append(jnp.where(tri, jnp.exp(jnp.where(tri, gi - gj, 0.0)), 0.0))
        n = [jnp.where(strict, -(kq[h][0:C] * decay[h]), 0.0) for h in heads]
        a = [jnp.where(tri, kq[h][C:2 * C] * decay[h], 0.0) for h in heads]
        x = [eye + n[h] for h in heads]
        p = n
        for _ in range(nsteps):
            pb = [p[h].astype(bf16) for h in heads]
            p = [_mm(pb[h], pb[h]) for h in heads]
            x = [x[h] + _mm(x[h].astype(bf16), p[h].astype(bf16)) for h in heads]
        eg = [jnp.exp(gcc[h]) for h in heads]
        uw = [_mm(x[h].astype(bf16), jnp.concatenate([vc[h] * bc[h], kb[h] * eg[h]], axis=1).astype(bf16))
              for h in heads]
        S = [s_ref[h] for h in heads]
        Sb = [S[h].astype(bf16) for h in heads]
        wq = [_mm(jnp.concatenate([uw[h][:, LANES:], qc[h] * eg[h]], axis=0).astype(bf16), Sb[h])
              for h in heads]
        vnb = [(uw[h][:, :LANES] - wq[h][0:C]).astype(bf16) for h in heads]
        av = [_mm(a[h].astype(bf16), vnb[h]) for h in heads]
        glast = [gcc[h][C - 1:C, :] for h in heads]
        kv_new = [_tn((kc[h] * jnp.exp(glast[h] - gcc[h])).astype(bf16), vnb[h]) for h in heads]
        for h in heads:
            s_ref[h] = S[h] * jnp.exp(glast[h]) + kv_new[h]
            o_ref[rows, lanes_of[h]] = _rms(wq[h][C:2 * C] + av[h], gout) * _silu(z_ref[rows, lanes_of[h]])
        return carry

    if TL == C:
        chunk(0, 0)
    else:
        lax.fori_loop(0, TL // C, chunk, 0)

    @pl.when(t == pl.num_programs(1) - 1)
    def _():
        sout_ref[...] = s_ref[...]


def delta_mixer(P, row0, nb, L, TL, C, valid, buf8, conv8, ab8, gout, s0):
    nt = L // TL
    rb0 = row0 // TL
    tile = lambda c: pl.BlockSpec((TL, A_W), lambda b, t: (rb0 + b * nt + t, c))
    state = pl.BlockSpec((None, A_HEADS, LANES, LANES), lambda b, t: (b, 0, 0, 0))
    return pl.pallas_call(
        functools.partial(_delta_kernel, TL=TL, C=C, valid=valid),
        grid=(nb, nt),
        in_specs=[tile(0), tile(1), tile(2), tile(3),
                  pl.BlockSpec((TL, LANES), lambda b, t: (rb0 + b * nt + t, A_BA_BLK)),
                  pl.BlockSpec((None, SUBLANES, CONV_DIM), lambda b, t: (b, 0, 0)),
                  pl.BlockSpec((SUBLANES, CONV_DIM), lambda b, t: (0, 0)),
                  pl.BlockSpec((SUBLANES, LANES), lambda b, t: (0, 0)),
                  pl.BlockSpec((1, LANES), lambda b, t: (0, 0)),
                  state],
        out_specs=(pl.BlockSpec((TL, A_W), lambda b, t: (b * nt + t, 0)), state),
        out_shape=(jax.ShapeDtypeStruct((nb * L, A_W), f32),
                   jax.ShapeDtypeStruct((nb, A_HEADS, LANES, LANES), f32)),
        scratch_shapes=[pltpu.VMEM((TL, A_W), f32)] * 5
        + [pltpu.VMEM((SUBLANES, CONV_DIM), f32), pltpu.VMEM((A_HEADS, LANES, LANES), f32)],
        compiler_params=_cparams(("parallel", "arbitrary")),
        name="delta_mixer",
    )(P, P, P, P, P, buf8, conv8, ab8, gout, s0)


def _attn_residual(h_ref, mo_ref, om_ref, wo1_ref, wo2_ref):
    return (h_ref[...] + _mm(mo_ref[...].astype(bf16), wo1_ref[...])
            + _mm(om_ref[...].astype(bf16), wo2_ref[...]))


def _ffn_kernel(h_ref, mo_ref, om_ref, wo1_ref, wo2_ref, g_ref, wg_ref, wu_ref, wd_ref, o_ref, xn_ref, acc_ref):
    k = pl.program_id(1)

    @pl.when(k == 0)
    def _():
        h1 = _attn_residual(h_ref, mo_ref, om_ref, wo1_ref, wo2_ref)
        acc_ref[...] = h1
        xn_ref[...] = _rms(h1, g_ref[...]).astype(bf16)

    xn = xn_ref[...]
    a = _silu(_mm(xn, wg_ref[...])) * _mm(xn, wu_ref[...])
    acc_ref[...] += _mm(a.astype(bf16), wd_ref[...])

    @pl.when(k == pl.num_programs(1) - 1)
    def _():
        o_ref[...] = acc_ref[...]


def _outproj_specs(tm, K, mw, ow, imap, cmap):
    return [pl.BlockSpec((tm, K), imap), pl.BlockSpec((tm, mw), imap), pl.BlockSpec((tm, ow), imap),
            pl.BlockSpec((mw, K), cmap(0)), pl.BlockSpec((ow, K), cmap(mw // ow)), pl.BlockSpec((1, K), cmap(0))]


def outproj_ffn(H, mo, om, w_out, g, w_gu, w_down, tm, tf):
    T, K = H.shape
    FF = w_down.shape[0]
    nk = FF // tf
    mw, ow = mo.shape[1], om.shape[1]
    specs = _outproj_specs(tm, K, mw, ow, lambda i, k: (i, 0), lambda r: (lambda i, k: (r, 0)))
    return pl.pallas_call(
        _ffn_kernel,
        grid=(T // tm, nk),
        in_specs=specs + [pl.BlockSpec((K, tf), lambda i, k: (0, k)),
                          pl.BlockSpec((K, tf), lambda i, k: (0, nk + k)),
                          pl.BlockSpec((tf, K), lambda i, k: (k, 0))],
        out_specs=pl.BlockSpec((tm, K), lambda i, k: (i, 0)),
        out_shape=jax.ShapeDtypeStruct((T, K), f32),
        scratch_shapes=[pltpu.VMEM((tm, K), bf16), pltpu.VMEM((tm, K), f32)],
        compiler_params=_cparams(("parallel", "arbitrary")),
        name="outproj_ffn",
    )(H, mo, om, w_out, w_out, g, w_gu, w_gu, w_down)


def _router_kernel(h_ref, mo_ref, om_ref, wo1_ref, wo2_ref, g_ref, wr_ref, h1_ref, xn_ref, r_ref, cnt_ref):
    i = pl.program_id(0)
    tm = h_ref.shape[0]

    @pl.when(i == 0)
    def _():
        cnt_ref[...] = jnp.zeros_like(cnt_ref)

    h1 = _attn_residual(h_ref, mo_ref, om_ref, wo1_ref, wo2_ref)
    h1_ref[...] = h1
    xn = _rms(h1, g_ref[...])
    xn_ref[...] = xn.astype(bf16)
    lane = lax.broadcasted_iota(jnp.int32, (tm, LANES), 1)
    logits = jnp.dot(xn, wr_ref[...], preferred_element_type=f32, precision=lax.Precision.HIGHEST)
    logits = jnp.where(lane < N_EXPERTS, logits, -jnp.inf)
    m1 = jnp.max(logits, axis=-1, keepdims=True)
    i1 = jnp.min(jnp.where(logits == m1, lane, LANES), axis=-1, keepdims=True)
    rest = jnp.where(lane == i1, -jnp.inf, logits)
    m2 = jnp.max(rest, axis=-1, keepdims=True)
    i2 = jnp.min(jnp.where(rest == m2, lane, LANES), axis=-1, keepdims=True)
    e2 = jnp.exp(m2 - m1)
    w1 = 1.0 / (1.0 + e2)
    w2 = e2 / (1.0 + e2)
    hot = jnp.where((lane == i1) | (lane == i2), 1.0, 0.0)
    ri = lax.broadcasted_iota(jnp.int32, (tm, tm), 0)
    ci = lax.broadcasted_iota(jnp.int32, (tm, tm), 1)
    before = jnp.where(ri > ci, 1.0, 0.0).astype(bf16)
    cum = _mm(before, hot.astype(bf16)) + cnt_ref[...]
    r1 = jnp.sum(jnp.where(lane == i1, cum, 0.0), axis=-1, keepdims=True)
    r2 = jnp.sum(jnp.where(lane == i2, cum, 0.0), axis=-1, keepdims=True)
    cnt_ref[...] += jnp.sum(hot, axis=0, keepdims=True)
    cols = (i1.astype(f32), i2.astype(f32), w1, w2, r1, r2)
    r = jnp.zeros((tm, LANES), f32)
    for c, val in enumerate(cols):
        r = jnp.where(lane == c, val, r)
    r_ref[...] = r


def outproj_router(H, mo, om, w_out, g, w_router, tm):
    T, K = H.shape
    mw, ow = mo.shape[1], om.shape[1]
    specs = _outproj_specs(tm, K, mw, ow, lambda i: (i, 0), lambda r: (lambda i: (r, 0)))
    return pl.pallas_call(
        _router_kernel,
        grid=(T // tm,),
        in_specs=specs + [pl.BlockSpec((K, LANES), lambda i: (0, 0))],
        out_specs=(pl.BlockSpec((tm, K), lambda i: (i, 0)), pl.BlockSpec((tm, K), lambda i: (i, 0)),
                   pl.BlockSpec((tm, LANES), lambda i: (i, 0))),
        out_shape=(jax.ShapeDtypeStruct((T, K), f32), jax.ShapeDtypeStruct((T, K), bf16),
                   jax.ShapeDtypeStruct((T, LANES), f32)),
        scratch_shapes=[pltpu.VMEM((1, LANES), f32)],
        compiler_params=_cparams(("arbitrary",)),
        name="outproj_router",
    )(H, mo, om, w_out, w_out, g, w_router)


def _moe_group_kernel(te_ref, nu_ref, x_ref, sw_ref, wg_ref, wu_ref, wd_ref, o_ref, acc_ref):
    del te_ref
    i = pl.program_id(0)
    k = pl.program_id(1)

    @pl.when(k == 0)
    def _():
        acc_ref[...] = jnp.zeros_like(acc_ref)

    @pl.when(i < nu_ref[0])
    def _():
        x = x_ref[...]
        a = _silu(_mm(x, wg_ref[...])) * _mm(x, wu_ref[...])
        acc_ref[...] += _mm(a.astype(bf16), wd_ref[...])

    @pl.when(k == pl.num_programs(1) - 1)
    def _():
        o_ref[...] = acc_ref[...] * sw_ref[...]


def moe_grouped(xg, sw, tile_e, n_used, w_gu, w_down, tm, tf):
    NP, K = xg.shape
    FF = w_down.shape[1]
    nk = FF // tf

    def kk(i, k, nu):
        return jnp.where(i < nu[0], k, nk - 1)

    grid_spec = pltpu.PrefetchScalarGridSpec(
        num_scalar_prefetch=2, grid=(NP // tm, nk),
        in_specs=[pl.BlockSpec((tm, K), lambda i, k, te, nu: (i, 0)),
                  pl.BlockSpec((tm, 1), lambda i, k, te, nu: (i, 0)),
                  pl.BlockSpec((None, K, tf), lambda i, k, te, nu: (te[i], 0, kk(i, k, nu))),
                  pl.BlockSpec((None, K, tf), lambda i, k, te, nu: (te[i], 0, nk + kk(i, k, nu))),
                  pl.BlockSpec((None, tf, K), lambda i, k, te, nu: (te[i], kk(i, k, nu), 0))],
        out_specs=pl.BlockSpec((tm, K), lambda i, k, te, nu: (i, 0)),
        scratch_shapes=[pltpu.VMEM((tm, K), f32)])
    return pl.pallas_call(
        _moe_group_kernel, grid_spec=grid_spec,
        out_shape=jax.ShapeDtypeStruct((NP, K), f32),
        compiler_params=_cparams(("parallel", "arbitrary")),
        name="moe_grouped",
    )(tile_e, n_used, xg, sw, w_gu, w_gu, w_down)


def _dispatch(r, tm):
    T = r.shape[0]
    A = 2 * T
    ntiles = A // tm + N_EXPERTS
    NP = ntiles * tm
    e_flat = r[:, 0:2].astype(jnp.int32).reshape(A)
    w_flat = r[:, 2:4].reshape(A)
    rank = r[:, 4:6].astype(jnp.int32)
    counts = jnp.sum((e_flat[:, None] == jnp.arange(N_EXPERTS)[None, :]).astype(jnp.int32), axis=0)
    pcounts = (counts + tm - 1) // tm * tm
    pends = jnp.cumsum(pcounts)
    pstarts = pends - pcounts
    starts = jnp.cumsum(counts) - counts
    n_used = pends[-1] // tm
    a_sorted = jnp.sort(e_flat * A + jnp.arange(A, dtype=jnp.int32)) % A
    tiles = jnp.arange(ntiles, dtype=jnp.int32)
    tile_e = jnp.minimum(jnp.searchsorted(pends, tiles * tm, side='right'), N_EXPERTS - 1).astype(jnp.int32)
    tile_e = jnp.where(tiles < n_used, tile_e, tile_e[jnp.maximum(n_used - 1, 0)])
    slot = jnp.arange(NP, dtype=jnp.int32)
    e_s = tile_e[slot // tm]
    rr = slot - pstarts[e_s]
    ok = (rr < counts[e_s]) & (slot < pends[-1])
    a_s = a_sorted[jnp.clip(starts[e_s] + rr, 0, A - 1)]
    src = a_s // 2
    sw = jnp.where(ok, w_flat[a_s], 0.0).reshape(NP, 1)
    slots = pstarts[r[:, 0:2].astype(jnp.int32)] + rank
    return src, sw, tile_e, n_used.reshape(1).astype(jnp.int32), slots


def outproj_moe(H, mo, om, w_out, g, w_router, w_gu, w_down, tm, tf):
    h1, xn, r = outproj_router(H, mo, om, w_out, g, w_router, tm)
    src, sw, tile_e, n_used, slots = _dispatch(r, tm)
    yg = moe_grouped(jnp.take(xn, src, axis=0), sw, tile_e, n_used, w_gu, w_down, tm, tf)
    return h1 + jnp.take(yg, slots[:, 0], axis=0) + jnp.take(yg, slots[:, 1], axis=0)


def _shared_kv_kernel(x_ref, g_ref, w_ref, cos_ref, sin_ref, ck_ref, cv_ref, sk_ref, sv_ref, wk_ref, wv_ref):
    y = _mm(_rms(x_ref[...], g_ref[...]).astype(bf16), w_ref[...])
    cos = cos_ref[...]
    sin = sin_ref[...]
    ck_ref[...] = y[:, 0 * LANES:1 * LANES]
    cv_ref[...] = y[:, 1 * LANES:2 * LANES]
    sk_ref[...] = _rope_lanes(y[:, 2 * LANES:3 * LANES], cos, sin)
    sv_ref[...] = y[:, 3 * LANES:4 * LANES]
    wk_ref[...] = _rope_lanes(y[:, 4 * LANES:5 * LANES], cos, sin)
    wv_ref[...] = y[:, 5 * LANES:6 * LANES]


def shared_kv(H, g, w, cos, sin, tm):
    T, K = H.shape
    row = pl.BlockSpec((tm, LANES), lambda i: (i, 0))
    out = jax.ShapeDtypeStruct((T, LANES), f32)
    return pl.pallas_call(
        _shared_kv_kernel,
        grid=(T // tm,),
        in_specs=[pl.BlockSpec((tm, K), lambda i: (i, 0)), pl.BlockSpec((1, K), lambda i: (0, 0)),
                  pl.BlockSpec((K, 6 * LANES), lambda i: (0, 0)), row, row],
        out_specs=(row,) * 6,
        out_shape=(out,) * 6,
        compiler_params=_cparams(("parallel",)),
        name="shared_kv",
    )(H, g, w, cos, sin)


def _compress_kernel(*refs, npages):
    nin = 2 * npages
    k_strips = jnp.concatenate([r[...] for r in refs[0:npages]], axis=0) if npages > 1 else refs[0][...]
    v_strips = jnp.concatenate([r[...] for r in refs[npages:nin]], axis=0) if npages > 1 else refs[npages][...]
    (pek_ref, pev_ref, w1k_ref, w1v_ref, w2k_ref, w2v_ref, cos_ref, sin_ref, ck_ref, cv_ref) = refs[nin:]
    n = k_strips.shape[0]
    half = 2 * CMP_HID
    row = lax.broadcasted_iota(jnp.int32, (n, LANES), 0)

    def tokens(strips, pe_ref, w1_ref, w2_ref):
        top = _mm((strips + pe_ref[0:1, :]).astype(bf16), w1_ref[:, 0:half])
        bot = _mm((strips + pe_ref[1:2, :]).astype(bf16), w1_ref[:, half:2 * half])
        hid = top + pltpu.roll(bot, n - 1, axis=0)
        out = _mm(_silu(hid).astype(bf16), w2_ref[...])
        return jnp.where(row < n - 1, out, 0.0)

    ck_ref[...] = _rope_lanes(tokens(k_strips, pek_ref, w1k_ref, w2k_ref), cos_ref[...], sin_ref[...])
    cv_ref[...] = tokens(v_strips, pev_ref, w1v_ref, w2v_ref)


def _drop_first(fn, _prefetch_ref, *refs):
    return fn(*refs)


def compress(k_src, v_src, k_maps, v_maps, strip_rows, nb, consts, prefetch=None):
    npages = len(k_maps)
    n = npages * strip_rows
    pek, pev, w1k, w1v, w2k, w2v, cos, sin = consts
    flat = 16 * LANES
    const2 = lambda shape: pl.BlockSpec(shape, lambda b, *_: (0, 0))
    in_specs = ([pl.BlockSpec((None, strip_rows, flat), m) for m in k_maps]
                + [pl.BlockSpec((None, strip_rows, flat), m) for m in v_maps]
                + [const2((2, flat)), const2((2, flat)), const2((flat, 4 * CMP_HID)), const2((flat, 4 * CMP_HID)),
                   const2((2 * CMP_HID, LANES)), const2((2 * CMP_HID, LANES)), const2((n, LANES)), const2((n, LANES))])
    out_spec = pl.BlockSpec((None, n, LANES), lambda b, *_: (b, 0, 0))
    out = jax.ShapeDtypeStruct((nb, n, LANES), f32)
    grid_spec = pltpu.PrefetchScalarGridSpec(
        num_scalar_prefetch=0 if prefetch is None else 1, grid=(nb,),
        in_specs=in_specs, out_specs=(out_spec, out_spec))
    args = ([] if prefetch is None else [prefetch]) + [k_src] * npages + [v_src] * npages + [pek, pev, w1k, w1v, w2k, w2v, cos, sin]
    kern = functools.partial(_compress_kernel, npages=npages)
    if prefetch is not None:
        kern = functools.partial(_drop_first, kern)
    return pl.pallas_call(
        kern, grid_spec=grid_spec, out_shape=(out, out),
        compiler_params=_cparams(("parallel",)), name="compress",
    )(*args)


def _nsa_queries(p, cos, sin):
    tq = p.shape[0]
    lane = lax.broadcasted_iota(jnp.int32, (tq, LANES), 1)
    stacks = []
    for grp in range(2):
        rows = []
        for hh in range(B_HEADS // 2):
            head = grp * (B_HEADS // 2) + hh
            blk = MEM_W // LANES + head // 2
            x = p[:, blk * LANES:(blk + 1) * LANES]
            x = _rope_lanes(x, cos, sin) * (HEAD_DIM ** -0.5)
            if head % 2 != grp:
                x = pltpu.roll(x, HEAD_DIM, axis=1)
            keep = (lane >= grp * HEAD_DIM) & (lane < (grp + 1) * HEAD_DIM)
            rows.append(jnp.where(keep, x, 0.0))
        stacks.append(jnp.concatenate(rows, axis=0).astype(bf16))
    return stacks


def _masked_softmax_parts(s, mask):
    sm = jnp.where(mask, s, NEG)
    m = jnp.max(sm, axis=-1, keepdims=True)
    p = jnp.where(mask, jnp.exp(sm - m), 0.0)
    return p, jnp.sum(p, axis=-1, keepdims=True)


def _safe_div(o, l):
    return jnp.where(l > 0.0, o / jnp.where(l > 0.0, l, 1.0), 0.0)


def _split3(x):
    hi = x.astype(bf16)
    r1 = x - hi.astype(f32)
    mid = r1.astype(bf16)
    lo = (r1 - mid.astype(f32)).astype(bf16)
    return hi, mid, lo


def _select_blocks(psum, qpos, m_ref, nsel_pad):
    tq = psum.shape[0]
    mb = m_ref[...]
    hi, mid, lo = _split3(psum)
    imp = _mm(hi, mb) + _mm(mid, mb) + _mm(lo, mb)
    lane = lax.broadcasted_iota(jnp.int32, (tq, LANES), 1)
    cur = qpos // SEL_BLK
    forced = (lane == 0) | (lane == cur) | (lane == cur - 1)
    valid = lane * SEL_BLK <= qpos
    imp = jnp.where(forced, 1e9, jnp.where(valid, imp, -1e9))
    rank = jnp.zeros((tq, LANES), f32)
    for j in range(nsel_pad):
        col = imp[:, j:j + 1]
        ahead = jnp.where(col > imp, 1.0, jnp.where(col == imp, jnp.where(lane > j, 1.0, 0.0), 0.0))
        rank = rank + ahead
    return jnp.where((rank < SEL_TOP) & (lane < nsel_pad), 1.0, 0.0)


def _take_group(x, grp):
    return x[:, grp * HEAD_DIM:(grp + 1) * HEAD_DIM]


def _assemble_heads(outs, tq):
    cols = []
    for grp in range(2):
        for hh in range(B_HEADS // 2):
            cols.append(outs[grp][hh * tq:(hh + 1) * tq, :])
    return jnp.concatenate(cols, axis=-1)


def _gate_stack(p, grp, tq):
    gl = _sigmoid(p[:, (MEM_W + B_W):(MEM_W + B_W) + LANES])
    cols = [[], [], []]
    for hh in range(B_HEADS // 2):
        base = (grp * (B_HEADS // 2) + hh) * 3
        for j in range(3):
            cols[j].append(gl[:, base + j:base + j + 1])
    return [jnp.concatenate(c, axis=0) for c in cols]


def _nsa_prompt_kernel(p_ref, cos_ref, sin_ref, ck_ref, cv_ref, sk_ref, sv_ref, wk_ref, wv_ref, m_ref, e_ref,
                       o_ref, *, tq, kc, nsel):
    qb = pl.program_id(1)
    s0 = qb * tq
    nh = B_HEADS // 2
    R = nh * tq
    p = p_ref[...]
    qstacks = _nsa_queries(p, cos_ref[...], sin_ref[...])
    qpos1 = s0 + lax.broadcasted_iota(jnp.int32, (tq, 1), 0)
    qposR = jnp.concatenate([qpos1] * nh, axis=0)
    ckb = ck_ref[...].astype(bf16)
    cvb = cv_ref[...].astype(bf16)
    ncmp_lane = lax.broadcasted_iota(jnp.int32, (R, LANES), 1)
    cpos = ncmp_lane * CMP_STRIDE + (CMP_BLK - 1)
    n_cmp = ck_ref.shape[0] - 1
    cmask = (cpos <= qposR) & (ncmp_lane < n_cmp)
    wstart = pl.multiple_of(jnp.maximum(s0 - WINDOW, 0), tq)
    wlen = WINDOW + tq
    wkb = wk_ref[pl.ds(wstart, wlen), :].astype(bf16)
    wvb = wv_ref[pl.ds(wstart, wlen), :].astype(bf16)
    wpos = wstart + lax.broadcasted_iota(jnp.int32, (R, wlen), 1)
    dist = qposR - wpos
    wmask = (dist >= 0) & (dist < WINDOW)
    nchunks = (s0 + tq + kc - 1) // kc
    outs = []
    for grp in range(2):
        qs = qstacks[grp]
        gates = _gate_stack(p, grp, tq)
        pc, lc = _masked_softmax_parts(_nt(qs, ckb), cmask)
        pcn = _safe_div(pc, lc)
        o_c = _take_group(_mm(pcn.astype(bf16), cvb), grp)
        psum = pcn[0:tq]
        for hh in range(1, nh):
            psum = psum + pcn[hh * tq:(hh + 1) * tq]
        sel = _select_blocks(psum, qpos1, m_ref, nsel).astype(bf16)

        def body(c, carry):
            m_run, l_run, acc = carry
            k0 = pl.multiple_of(c * kc, kc)
            kb = sk_ref[pl.ds(k0, kc), :].astype(bf16)
            vb = sv_ref[pl.ds(k0, kc), :].astype(bf16)
            s = _nt(qs, kb)
            selk = _mm(sel, e_ref[:, pl.ds(k0, kc)])
            selR = jnp.concatenate([selk] * nh, axis=0)
            kpos = k0 + lax.broadcasted_iota(jnp.int32, (R, kc), 1)
            sm = jnp.where(kpos <= qposR, jnp.where(selR > 0.5, s, NEG), NEG)
            m_new = jnp.maximum(m_run, jnp.max(sm, axis=-1, keepdims=True))
            alpha = jnp.exp(m_run - m_new)
            pe = jnp.exp(sm - m_new)
            l_new = alpha * l_run + jnp.sum(pe, axis=-1, keepdims=True)
            acc_new = alpha * acc + _mm(pe.astype(bf16), vb)
            return m_new, l_new, acc_new

        init = (jnp.full((R, 1), NEG, f32), jnp.zeros((R, 1), f32), jnp.zeros((R, LANES), f32))
        m_run, l_run, acc = lax.fori_loop(0, nchunks, body, init)
        o_s = _take_group(acc / l_run, grp)
        pw, lw = _masked_softmax_parts(_nt(qs, wkb), wmask)
        o_w = _take_group(_mm(pw.astype(bf16), wvb) / lw, grp)
        outs.append(gates[0] * o_c + gates[1] * o_s + gates[2] * o_w)
    o_ref[...] = _assemble_heads(outs, tq)


def nsa_prompt(P, cos, sin, ck, cv, sk, sv, wk, wv, m_mat, e_mat, nb, L, tq, kc):
    nq = L // tq
    nsel = L // SEL_BLK
    seq = pl.BlockSpec((L, LANES), lambda b, i: (b, 0))
    cmp_spec = pl.BlockSpec((None, LANES, LANES), lambda b, i: (b, 0, 0))
    return pl.pallas_call(
        functools.partial(_nsa_prompt_kernel, tq=tq, kc=kc, nsel=nsel),
        grid=(nb, nq),
        in_specs=[pl.BlockSpec((tq, B_IN_PAD), lambda b, i: (b * nq + i, 0)),
                  pl.BlockSpec((tq, LANES), lambda b, i: (i, 0)),
                  pl.BlockSpec((tq, LANES), lambda b, i: (i, 0)),
                  cmp_spec, cmp_spec, seq, seq, seq, seq,
                  pl.BlockSpec((LANES, LANES), lambda b, i: (0, 0)),
                  pl.BlockSpec((LANES, L), lambda b, i: (0, 0))],
        out_specs=pl.BlockSpec((tq, B_W), lambda b, i: (b * nq + i, 0)),
        out_shape=jax.ShapeDtypeStruct((nb * L, B_W), f32),
        compiler_params=_cparams(("parallel", "parallel")),
        name="nsa_prompt",
    )(P, cos, sin, ck, cv, sk, sv, wk, wv, m_mat, e_mat)


def _nsa_sample_kernel(*refs, npages, past):
    refs = refs[1:]
    kpages = refs[0:npages]
    vpages = refs[npages:2 * npages]
    (p_ref, cos_ref, sin_ref, ck_ref, cv_ref, nsk_ref, nsv_ref, cwk_ref, cwv_ref, nwk_ref, nwv_ref,
     m_ref, e_ref, o_ref) = refs[2 * npages:]
    tq = S_PAD
    nh = B_HEADS // 2
    R = nh * tq
    p = p_ref[...]
    qstacks = _nsa_queries(p, cos_ref[...], sin_ref[...])
    t1 = lax.broadcasted_iota(jnp.int32, (tq, 1), 0)
    qpos1 = past + t1
    tR = jnp.concatenate([t1] * nh, axis=0)
    qposR = past + tR
    ckb = ck_ref[...].astype(bf16)
    cvb = cv_ref[...].astype(bf16)
    lane = lax.broadcasted_iota(jnp.int32, (R, LANES), 1)
    cmask = ((lane * CMP_STRIDE + (CMP_BLK - 1)) <= qposR) & (lane < ck_ref.shape[0] - 1)
    skb = jnp.concatenate([r[...] for r in kpages], axis=0).astype(bf16)
    svb = jnp.concatenate([r[...] for r in vpages], axis=0).astype(bf16)
    nskb = nsk_ref[...].astype(bf16)
    nsvb = nsv_ref[...].astype(bf16)
    cwkb = cwk_ref[...].astype(bf16)
    cwvb = cwv_ref[...].astype(bf16)
    nwkb = nwk_ref[...].astype(bf16)
    nwvb = nwv_ref[...].astype(bf16)
    wb = cwk_ref.shape[0]
    new_ok = lax.broadcasted_iota(jnp.int32, (R, tq), 1) <= tR
    wdist = qposR - (past - wb + lax.broadcasted_iota(jnp.int32, (R, wb), 1))
    wmask = (wdist >= 0) & (wdist < WINDOW)
    nsel = past // SEL_BLK + 1
    outs = []
    for grp in range(2):
        qs = qstacks[grp]
        gates = _gate_stack(p, grp, tq)
        pc, lc = _masked_softmax_parts(_nt(qs, ckb), cmask)
        pcn = _safe_div(pc, lc)
        o_c = _take_group(_mm(pcn.astype(bf16), cvb), grp)
        psum = pcn[0:tq]
        for hh in range(1, nh):
            psum = psum + pcn[hh * tq:(hh + 1) * tq]
        sel = _select_blocks(psum, qpos1, m_ref, nsel)
        selk = _mm(sel.astype(bf16), e_ref[...])
        selR = jnp.concatenate([selk] * nh, axis=0) > 0.5
        sel_new = jnp.concatenate([sel[:, past // SEL_BLK:past // SEL_BLK + 1]] * nh, axis=0) > 0.5
        s_old = jnp.where(selR, _nt(qs, skb), NEG)
        s_new = jnp.where(new_ok, jnp.where(sel_new, _nt(qs, nskb), NEG), NEG)
        m = jnp.maximum(jnp.max(s_old, axis=-1, keepdims=True), jnp.max(s_new, axis=-1, keepdims=True))
        p_old = jnp.exp(s_old - m)
        p_new = jnp.exp(s_new - m)
        l = jnp.sum(p_old, axis=-1, keepdims=True) + jnp.sum(p_new, axis=-1, keepdims=True)
        o_s = _take_group((_mm(p_old.astype(bf16), svb) + _mm(p_new.astype(bf16), nsvb)) / l, grp)
        w_old = jnp.where(wmask, _nt(qs, cwkb), NEG)
        w_new = jnp.where(new_ok, _nt(qs, nwkb), NEG)
        mw = jnp.maximum(jnp.max(w_old, axis=-1, keepdims=True), jnp.max(w_new, axis=-1, keepdims=True))
        pw_old = jnp.exp(w_old - mw)
        pw_new = jnp.exp(w_new - mw)
        lw = jnp.sum(pw_old, axis=-1, keepdims=True) + jnp.sum(pw_new, axis=-1, keepdims=True)
        o_w = _take_group((_mm(pw_old.astype(bf16), cwvb) + _mm(pw_new.astype(bf16), nwvb)) / lw, grp)
        outs.append(gates[0] * o_c + gates[1] * o_s + gates[2] * o_w)
    o_ref[...] = _assemble_heads(outs, tq)


def nsa_sample(P, row0, page_table, pool_k, pool_v, cos, sin, ck, cv, nsk, nsv, cwk, cwv, nwk, nwv, m_mat, e_mat):
    nb = cwk.shape[0]
    npages = page_table.shape[0] // nb
    past = npages * PAGE
    rb0 = row0 // S_PAD
    page = lambda j: pl.BlockSpec((None, PAGE, LANES), lambda b, pt: (pt[b * npages + j], 0, 0))
    new_rows = pl.BlockSpec((S_PAD, LANES), lambda b, pt: (rb0 + b, 0))
    per_seq = lambda n: pl.BlockSpec((None, n, LANES), lambda b, pt: (b, 0, 0))
    in_specs = ([page(j) for j in range(npages)] + [page(j) for j in range(npages)]
                + [pl.BlockSpec((S_PAD, B_IN_PAD), lambda b, pt: (rb0 + b, 0)),
                   pl.BlockSpec((S_PAD, LANES), lambda b, pt: (0, 0)),
                   pl.BlockSpec((S_PAD, LANES), lambda b, pt: (0, 0)),
                   per_seq(LANES), per_seq(LANES), new_rows, new_rows,
                   per_seq(cwk.shape[1]), per_seq(cwk.shape[1]), new_rows, new_rows,
                   pl.BlockSpec((LANES, LANES), lambda b, pt: (0, 0)),
                   pl.BlockSpec((LANES, past), lambda b, pt: (0, 0))])
    grid_spec = pltpu.PrefetchScalarGridSpec(
        num_scalar_prefetch=1, grid=(nb,), in_specs=in_specs,
        out_specs=pl.BlockSpec((S_PAD, B_W), lambda b, pt: (b, 0)))
    return pl.pallas_call(
        functools.partial(_nsa_sample_kernel, npages=npages, past=past),
        grid_spec=grid_spec,
        out_shape=jax.ShapeDtypeStruct((nb * S_PAD, B_W), f32),
        compiler_params=_cparams(("parallel",)),
        name="nsa_sample",
    )(page_table, *([pool_k] * npages), *([pool_v] * npages), P, cos, sin, ck, cv, nsk, nsv, cwk, cwv, nwk, nwv,
      m_mat, e_mat)


def _rope_tables(pos):
    half = HEAD_DIM // 2
    inv = ROPE_THETA ** (-jnp.arange(half, dtype=f32) / half)
    ang = pos.astype(f32)[:, None] * inv[None, :]
    cos = jnp.tile(jnp.cos(ang), (1, LANES // half))
    sin = jnp.tile(jnp.sin(ang), (1, LANES // half))
    sign = jnp.where((jnp.arange(LANES) % HEAD_DIM) < half, -1.0, 1.0).astype(f32)
    return cos, sin * sign[None, :]


def _cmp_to_sel(ncmp, nsel):
    per = SEL_BLK // CMP_STRIDE
    sub = CMP_BLK // CMP_STRIDE
    i = jnp.arange(LANES)[:, None]
    j = jnp.arange(LANES)[None, :]
    m = jnp.zeros((LANES, LANES), f32)
    for r in range(sub):
        m = m + (((i + r) // per) == j).astype(f32)
    m = jnp.where((i < ncmp) & (j < nsel), m / sub, 0.0)
    return m.astype(bf16)


def _block_expand(nkeys):
    j = jnp.arange(LANES)[:, None]
    k = jnp.arange(nkeys)[None, :]
    return ((k // SEL_BLK) == j).astype(bf16)


def _compress_consts(pe, w1, w2):
    w1r = w1.reshape(CMP_BLK, HEAD_DIM, CMP_HID)
    eye2 = jnp.eye(2, dtype=f32)

    def expand(w):
        return jnp.einsum('ldc,gh->lgdhc', w, eye2).reshape(16 * LANES, 2 * CMP_HID)

    w1s = jnp.concatenate([expand(w1r[:16]), expand(w1r[16:])], axis=1).astype(bf16)
    pes = jnp.stack([jnp.broadcast_to(pe[:16, None, :], (16, 2, HEAD_DIM)).reshape(-1),
                     jnp.broadcast_to(pe[16:, None, :], (16, 2, HEAD_DIM)).reshape(-1)])
    w2s = jnp.einsum('cd,gh->gchd', w2, eye2).reshape(2 * CMP_HID, LANES).astype(bf16)
    return pes, w1s, w2s


def kernel(x_prompt, x_sample, mem_prompt, state_conv, state_ssm, cache_cmp_k, cache_cmp_v, cache_sel_k, cache_sel_v, cache_win_k, cache_win_v, cache_mem_k, cache_mem_v, page_table, norm_mix, norm_ffn, norm_mem, w_mem_kv, w_in_a, conv_w_a, a_log, dt_bias, norm_out_a, w_out_a, w_in_b, w_out_b, norm_kv, w_kv_shared, cmp_pe_k, cmp_w1_k, cmp_w2_k, cmp_pe_v, cmp_w1_v, cmp_w2_v, w_gu_dense, w_down_dense, w_router, w_gu_exp, w_down_exp, norm_final):
    bp, lp, d = x_prompt.shape
    bs, ls, _ = x_sample.shape
    depth = norm_mix.shape[0]
    n_a = w_in_a.shape[0]
    past = page_table.shape[1] * PAGE
    tp = bp * lp
    T = tp + bs * S_PAD
    tm = 512

    xs = jnp.pad(x_sample, ((0, 0), (0, S_PAD - ls), (0, 0)))
    H = jnp.concatenate([x_prompt.reshape(tp, d), xs.reshape(bs * S_PAD, d)], axis=0)

    qkvz = MEM_W + CONV_DIM + A_W
    w_in_a_b = jnp.concatenate(
        [w_in_a[:, :, MEM_W:qkvz], w_in_a[:, :, :MEM_W], w_in_a[:, :, qkvz:],
         jnp.zeros(w_in_a.shape[:2] + (A_IN_PAD - w_in_a.shape[2],), f32)], axis=2).astype(bf16)
    w_in_b_b = jnp.pad(w_in_b, ((0, 0), (0, 0), (0, B_IN_PAD - w_in_b.shape[2]))).astype(bf16)
    w_out_a_b = w_out_a.astype(bf16)
    w_out_b_b = w_out_b.astype(bf16)
    w_gu_dense_b = w_gu_dense.astype(bf16)
    w_down_dense_b = w_down_dense.astype(bf16)
    w_gu_exp_b = w_gu_exp.astype(bf16)
    w_down_exp_b = w_down_exp.astype(bf16)
    w_router_p = jnp.pad(w_router, ((0, 0), (0, 0), (0, LANES - N_EXPERTS)))

    mem_k, mem_v = mem_kv(mem_prompt.reshape(bp * MEM_LEN, d), norm_mem.reshape(depth, 1, d),
                          w_mem_kv.astype(bf16), tm)
    cmk = cache_mem_k.reshape(depth, bs, MEM_LEN, MEM_W)
    cmv = cache_mem_v.reshape(depth, bs, MEM_LEN, MEM_W)

    pos_p = jnp.arange(lp, dtype=jnp.int32)
    pos_s = past + jnp.arange(S_PAD, dtype=jnp.int32)
    cos_p, sin_p = _rope_tables(pos_p)
    cos_s, sin_s = _rope_tables(pos_s)

    conv_p, ssm_p, conv_s, ssm_s = [], [], [], []
    kv = None
    for layer in range(depth):
        is_a = layer < n_a
        g_mix = norm_mix[layer].reshape(1, d)
        if is_a:
            P = norm_matmul(H, g_mix, w_in_a_b[layer], tm, A_IN_PAD // 3)
            qblk = A_MEMQ_BLK
        else:
            lb = layer - n_a
            qblk = 0
            if lb == 0:
                cos_all = jnp.concatenate([jnp.tile(cos_p, (bp, 1)), jnp.tile(cos_s, (bs, 1))], axis=0)
                sin_all = jnp.concatenate([jnp.tile(sin_p, (bp, 1)), jnp.tile(sin_s, (bs, 1))], axis=0)
                kv = shared_kv(H, norm_kv.reshape(1, d), w_kv_shared.astype(bf16), cos_all, sin_all, tm)
                ncmp = (lp - CMP_BLK) // CMP_STRIDE + 1
                cpos = jnp.arange(LANES, dtype=jnp.int32) * CMP_STRIDE + CMP_BLK - 1
                cos_c, sin_c = _rope_tables(cpos)
                pek, w1k, w2k = _compress_consts(cmp_pe_k, cmp_w1_k, cmp_w2_k)
                pev, w1v, w2v = _compress_consts(cmp_pe_v, cmp_w1_v, cmp_w2_v)
                consts = (pek, pev, w1k, w1v, w2k, w2v, cos_c, sin_c)
                strips = lp // 16
                ck_p, cv_p = compress(kv[0][:tp].reshape(bp, strips, 16 * LANES),
                                      kv[1][:tp].reshape(bp, strips, 16 * LANES),
                                      [lambda b: (b, 0, 0)], [lambda b: (b, 0, 0)], strips, bp, consts)
                npages = page_table.shape[1]
                pt_flat = page_table.reshape(-1)
                pool_strips = PAGE // 16
                pmap = lambda j: (lambda b, pt: (pt[b * npages + j], 0, 0))
                ck_s, cv_s = compress(cache_cmp_k.reshape(-1, pool_strips, 16 * LANES),
                                      cache_cmp_v.reshape(-1, pool_strips, 16 * LANES),
                                      [pmap(j) for j in range(npages)], [pmap(j) for j in range(npages)],
                                      pool_strips, bs, consts, prefetch=pt_flat)
                m_p = _cmp_to_sel(ncmp, lp // SEL_BLK)
                e_p = _block_expand(lp)
                ncmp_s = (past + ls - CMP_BLK) // CMP_STRIDE + 1
                m_s = _cmp_to_sel(ncmp_s, -(-(past + ls) // SEL_BLK))
                e_s = _block_expand(past)
                pool_sk = cache_sel_k.reshape(-1, PAGE, LANES)
                pool_sv = cache_sel_v.reshape(-1, PAGE, LANES)
                cwk = cache_win_k.reshape(bs, -1, LANES)
                cwv = cache_win_v.reshape(bs, -1, LANES)
            P = norm_matmul(H, g_mix, w_in_b_b[lb], tm, B_IN_PAD)
        om_p = mem_attn_prompt(P, qblk, mem_k[layer], mem_v[layer], bp, lp, 512)
        om_s = mem_attn_sample(P, qblk, tp, cmk[layer], cmv[layer], 8)
        om = jnp.concatenate([om_p, om_s], axis=0)
        if is_a:
            conv8 = jnp.pad(conv_w_a[layer], ((0, SUBLANES - conv_w_a.shape[1]), (0, 0)))
            ab8 = jnp.zeros((SUBLANES, LANES), f32)
            ab8 = ab8.at[0, A_HEADS:2 * A_HEADS].set(a_log[layer]).at[1, A_HEADS:2 * A_HEADS].set(dt_bias[layer])
            gout = norm_out_a[layer].reshape(1, LANES)
            zero_buf = jnp.zeros((bp, SUBLANES, CONV_DIM), f32)
            zero_state = jnp.zeros((bp, A_HEADS, LANES, LANES), f32)
            mo_p, sp = delta_mixer(P, 0, bp, lp, 512, 64, 512, zero_buf, conv8, ab8, gout, zero_state)
            buf_s = jnp.pad(state_conv[layer], ((0, 0), (SUBLANES - state_conv.shape[2], 0), (0, 0)))
            mo_s, ss = delta_mixer(P, tp, bs, S_PAD, S_PAD, S_PAD, ls, buf_s, conv8, ab8, gout, state_ssm[layer])
            conv_p.append(jnp.stack([P[b * lp + lp - 3:(b + 1) * lp, :CONV_DIM] for b in range(bp)]))
            conv_s.append(P[tp:, :CONV_DIM].reshape(bs, S_PAD, CONV_DIM)[:, ls - 3:ls])
            ssm_p.append(sp)
            ssm_s.append(ss)
            w_out = w_out_a_b[layer]
        else:
            mo_p = nsa_prompt(P, cos_p, sin_p, ck_p, cv_p, kv[2], kv[3], kv[4], kv[5], m_p, e_p, bp, lp, Q_BLOCK, 512)
            mo_s = nsa_sample(P, tp, pt_flat, pool_sk, pool_sv, cos_s, sin_s, ck_s, cv_s, kv[2], kv[3], cwk, cwv,
                              kv[4], kv[5], m_s, e_s)
            w_out = w_out_b_b[lb]
        mo = jnp.concatenate([mo_p, mo_s], axis=0)
        g_ffn = norm_ffn[layer].reshape(1, d)
        if layer % 2 == 0:
            H = outproj_ffn(H, mo, om, w_out, g_ffn, w_gu_dense_b[layer // 2], w_down_dense_b[layer // 2], tm, D_FF // 2)
        else:
            H = outproj_moe(H, mo, om, w_out, g_ffn, w_router_p[layer // 2], w_gu_exp_b[layer // 2],
                            w_down_exp_b[layer // 2], tm, D_FF_EXPERT // 2)
    Y = final_norm(H, norm_final.reshape(1, d), tm)

    def rows_p(a):
        return a[:tp].reshape(bp, lp, 2, HEAD_DIM)

    def rows_s(a):
        return a[tp:].reshape(bs, S_PAD, 2, HEAD_DIM)[:, :ls]

    wlen = min(WINDOW, lp)
    return (Y[:tp].reshape(bp, lp, d), Y[tp:].reshape(bs, S_PAD, d)[:, :ls],
            jnp.stack(conv_p), jnp.stack(ssm_p),
            rows_p(kv[0]), rows_p(kv[1]), rows_p(kv[2]), rows_p(kv[3]),
            rows_p(kv[4])[:, lp - wlen:], rows_p(kv[5])[:, lp - wlen:],
            mem_k.reshape(depth, bp, MEM_LEN, 4, HEAD_DIM), mem_v.reshape(depth, bp, MEM_LEN, 4, HEAD_DIM),
            jnp.stack(conv_s), jnp.stack(ssm_s),
            rows_s(kv[0]), rows_s(kv[1]), rows_s(kv[2]), rows_s(kv[3]),
            jnp.concatenate([cache_win_k[:, ls:], rows_s(kv[4])], axis=1),
            jnp.concatenate([cache_win_v[:, ls:], rows_s(kv[5])], axis=1))
```

```python
import functools
import math

import jax
import jax.numpy as jnp
from jax import lax
from jax.experimental import pallas as pl
from jax.experimental.pallas import tpu as pltpu

f32 = jnp.float32
bf16 = jnp.bfloat16

EPS = 1e-6
NEG = -1e30
ROPE_THETA = 10000.0
HEAD_DIM = 64
LANES = 128
SUBLANES = 8
VMEM_LIMIT = 48 * 1024 * 1024

D_MODEL = 1024
MEM_LEN = 256
MEM_W = 256
A_HEADS = 6
A_W = 768
CONV_DIM = 2304
A_IN_PAD = 3456
A_MEMQ_BLK = 12
A_BA_BLK = 26
B_HEADS = 12
HPG = 6
B_W = 768
B_IN_PAD = 1152
CMP_BLK = 32
CMP_STRIDE = 16
CMP_HID = 256
SEL_BLK = 64
SEL_TOP = 16
WINDOW = 512
Q_BLOCK = 128
D_FF = 2816
N_EXPERTS = 8
D_FF_EXPERT = 3584
PAGE = 128
S_PAD = 8


def _cparams(sem):
    return pltpu.CompilerParams(dimension_semantics=sem, vmem_limit_bytes=VMEM_LIMIT)


def _nt(a, b):
    return lax.dot_general(a, b, (((1,), (1,)), ((), ())), preferred_element_type=f32)


def _tn(a, b):
    return lax.dot_general(a, b, (((0,), (0,)), ((), ())), preferred_element_type=f32)


def _mm(a, b):
    return jnp.dot(a, b, preferred_element_type=f32)


def _rms(x, g):
    return (x * lax.rsqrt(jnp.mean(x * x, axis=-1, keepdims=True) + EPS)) * g


def _sigmoid(x):
    return 1.0 / (1.0 + jnp.exp(-x))


def _silu(x):
    return x * _sigmoid(x)


def _softplus(x):
    return jnp.maximum(x, 0.0) + jnp.log(1.0 + jnp.exp(-jnp.abs(x)))


def _rope_lanes(x, cos, sin_signed):
    lane = lax.broadcasted_iota(jnp.int32, x.shape, x.ndim - 1)
    lo = (lane % HEAD_DIM) < (HEAD_DIM // 2)
    partner = jnp.where(lo, pltpu.roll(x, LANES - HEAD_DIM // 2, axis=x.ndim - 1),
                        pltpu.roll(x, HEAD_DIM // 2, axis=x.ndim - 1))
    return x * cos + partner * sin_signed


def _norm_matmul_kernel(x_ref, g_ref, w_ref, o_ref, xn_ref):
    @pl.when(pl.program_id(1) == 0)
    def _():
        xn_ref[...] = _rms(x_ref[...], g_ref[...]).astype(bf16)

    o_ref[...] = _mm(xn_ref[...], w_ref[...])


def norm_matmul(x, g, w, tm, tn):
    T, K = x.shape
    N = w.shape[1]
    return pl.pallas_call(
        _norm_matmul_kernel,
        grid=(T // tm, N // tn),
        in_specs=[pl.BlockSpec((tm, K), lambda i, j: (i, 0)),
                  pl.BlockSpec((1, K), lambda i, j: (0, 0)),
                  pl.BlockSpec((K, tn), lambda i, j: (0, j))],
        out_specs=pl.BlockSpec((tm, tn), lambda i, j: (i, j)),
        out_shape=jax.ShapeDtypeStruct((T, N), f32),
        scratch_shapes=[pltpu.VMEM((tm, K), bf16)],
        compiler_params=_cparams(("parallel", "arbitrary")),
        name="norm_matmul",
    )(x, g, w)


def _mem_kv_kernel(x_ref, g_ref, w_ref, k_ref, v_ref):
    y = _mm(_rms(x_ref[...], g_ref[...]).astype(bf16), w_ref[...])
    k_ref[...] = y[:, :MEM_W]
    v_ref[...] = y[:, MEM_W:]


def mem_kv(mem, g, w, tm):
    R, K = mem.shape
    NL = w.shape[0]
    out = jax.ShapeDtypeStruct((NL, R, MEM_W), f32)
    return pl.pallas_call(
        _mem_kv_kernel,
        grid=(NL, R // tm),
        in_specs=[pl.BlockSpec((tm, K), lambda l, i: (i, 0)),
                  pl.BlockSpec((None, 1, K), lambda l, i: (l, 0, 0)),
                  pl.BlockSpec((None, K, 2 * MEM_W), lambda l, i: (l, 0, 0))],
        out_specs=(pl.BlockSpec((None, tm, MEM_W), lambda l, i: (l, i, 0)),
                   pl.BlockSpec((None, tm, MEM_W), lambda l, i: (l, i, 0))),
        out_shape=(out, out),
        compiler_params=_cparams(("parallel", "parallel")),
        name="mem_kv",
    )(mem, g, w)


def _final_norm_kernel(x_ref, g_ref, o_ref):
    o_ref[...] = _rms(x_ref[...], g_ref[...])


def final_norm(x, g, tm):
    T, K = x.shape
    return pl.pallas_call(
        _final_norm_kernel,
        grid=(T // tm,),
        in_specs=[pl.BlockSpec((tm, K), lambda i: (i, 0)), pl.BlockSpec((1, K), lambda i: (0, 0))],
        out_specs=pl.BlockSpec((tm, K), lambda i: (i, 0)),
        out_shape=jax.ShapeDtypeStruct((T, K), f32),
        compiler_params=_cparams(("parallel",)),
        name="final_norm",
    )(x, g)


def _mem_attend(q, k, v):
    tq = q.shape[0]
    nh = MEM_W // HEAD_DIM
    lane = lax.broadcasted_iota(jnp.int32, (tq, MEM_W), 1)
    masks = [(lane >= HEAD_DIM * h) & (lane < HEAD_DIM * (h + 1)) for h in range(nh)]
    qs = jnp.concatenate([jnp.where(m, q, 0.0) for m in masks], axis=0).astype(bf16)
    s = _nt(qs, k) * (HEAD_DIM ** -0.5)
    p = jnp.exp(s - jnp.max(s, axis=-1, keepdims=True))
    l = jnp.sum(p, axis=-1, keepdims=True)
    o = _mm(p.astype(bf16), v) / l
    out = jnp.zeros((tq, MEM_W), f32)
    for h in range(nh):
        out = jnp.where(masks[h], o[h * tq:(h + 1) * tq], out)
    return out


def _mem_attn_prompt_kernel(q_ref, k_ref, v_ref, o_ref):
    o_ref[...] = _mem_attend(q_ref[...], k_ref[...].astype(bf16), v_ref[...].astype(bf16))


def mem_attn_prompt(P, qblk, mk, mv, nb, L, tq):
    nq = L // tq
    return pl.pallas_call(
        _mem_attn_prompt_kernel,
        grid=(nb, nq),
        in_specs=[pl.BlockSpec((tq, MEM_W), lambda b, i: (b * nq + i, qblk)),
                  pl.BlockSpec((MEM_LEN, MEM_W), lambda b, i: (b, 0)),
                  pl.BlockSpec((MEM_LEN, MEM_W), lambda b, i: (b, 0))],
        out_specs=pl.BlockSpec((tq, MEM_W), lambda b, i: (b * nq + i, 0)),
        out_shape=jax.ShapeDtypeStruct((nb * L, MEM_W), f32),
        compiler_params=_cparams(("parallel", "parallel")),
        name="mem_attn_prompt",
    )(P, mk, mv)


def _mem_attn_sample_kernel(q_ref, k_ref, v_ref, o_ref, *, bb):
    for b in range(bb):
        rows = slice(b * S_PAD, (b + 1) * S_PAD)
        o_ref[rows, :] = _mem_attend(q_ref[rows, :], k_ref[b].astype(bf16), v_ref[b].astype(bf16))


def mem_attn_sample(P, qblk, row0, ck, cv, bb):
    nb = ck.shape[0]
    blk0 = row0 // (bb * S_PAD)
    return pl.pallas_call(
        functools.partial(_mem_attn_sample_kernel, bb=bb),
        grid=(nb // bb,),
        in_specs=[pl.BlockSpec((bb * S_PAD, MEM_W), lambda i: (blk0 + i, qblk)),
                  pl.BlockSpec((bb, MEM_LEN, MEM_W), lambda i: (i, 0, 0)),
                  pl.BlockSpec((bb, MEM_LEN, MEM_W), lambda i: (i, 0, 0))],
        out_specs=pl.BlockSpec((bb * S_PAD, MEM_W), lambda i: (i, 0)),
        out_shape=jax.ShapeDtypeStruct((nb * S_PAD, MEM_W), f32),
        compiler_params=_cparams(("parallel",)),
        name="mem_attn_sample",
    )(P, ck, cv)


def _delta_kernel(q_ref, k_ref, v_ref, z_ref, ba_ref, buf_ref, cw_ref, ab_ref, go_ref, s0_ref,
                  o_ref, sout_ref, qs_ref, ks_ref, vs_ref, gs_ref, bs_ref, tail_ref, s_ref,
                  *, TL, C, valid):
    t = pl.program_id(1)

    @pl.when(t == 0)
    def _():
        tail_ref[...] = buf_ref[...]
        s_ref[...] = s0_ref[...]

    row8 = lax.broadcasted_iota(jnp.int32, (SUBLANES, A_W), 0)

    def conv(x_ref, c0):
        x = x_ref[...]
        b8 = tail_ref[:, c0:c0 + A_W]
        c8 = cw_ref[:, c0:c0 + A_W]
        acc = x * c8[3:4, :]
        for s in range(1, 4):
            r = pltpu.roll(x, s, axis=0)
            fix = jnp.where(row8 < s, pltpu.roll(b8, s, axis=0), r[0:SUBLANES])
            r = jnp.concatenate([fix, r[SUBLANES:]], axis=0) if TL > SUBLANES else fix
            acc = acc + r * c8[3 - s:4 - s, :]
        tail_ref[:, c0:c0 + A_W] = x[TL - SUBLANES:TL]
        return _silu(acc)

    q = conv(q_ref, 0)
    k = conv(k_ref, A_W)
    v = conv(v_ref, 2 * A_W)
    ba = ba_ref[...]
    ab = ab_ref[...]
    beta_all = _sigmoid(ba)
    g_all = -jnp.exp(ab[0:1, :]) * _softplus(ba + ab[1:2, :])
    if valid < TL:
        live = lax.broadcasted_iota(jnp.int32, (TL, A_W), 0) < valid
        live1 = lax.broadcasted_iota(jnp.int32, (TL, LANES), 0) < valid
        k = jnp.where(live, k, 0.0)
        v = jnp.where(live, v, 0.0)
        beta_all = jnp.where(live1, beta_all, 0.0)
        g_all = jnp.where(live1, g_all, 0.0)
    rowc = lax.broadcasted_iota(jnp.int32, (TL, LANES), 0) & (C - 1)
    gc_all = g_all
    s = 1
    while s < C:
        gc_all = gc_all + jnp.where(rowc >= s, pltpu.roll(gc_all, s, axis=0), 0.0)
        s *= 2
    vs_ref[...] = v
    for h in range(A_HEADS):
        sl = slice(h * LANES, (h + 1) * LANES)
        qh = q[:, sl]
        kh = k[:, sl]
        qs_ref[:, sl] = qh * lax.rsqrt(jnp.sum(qh * qh, axis=-1, keepdims=True) + EPS) * (LANES ** -0.5)
        ks_ref[:, sl] = kh * lax.rsqrt(jnp.sum(kh * kh, axis=-1, keepdims=True) + EPS)
        gs_ref[:, sl] = jnp.broadcast_to(gc_all[:, A_HEADS + h:A_HEADS + h + 1], (TL, LANES))
        bs_ref[:, sl] = jnp.broadcast_to(beta_all[:, h:h + 1], (TL, LANES))

    ii = lax.broadcasted_iota(jnp.int32, (C, C), 0)
    jj = lax.broadcasted_iota(jnp.int32, (C, C), 1)
    tri = ii >= jj
    strict = ii > jj
    eye = jnp.where(ii == jj, 1.0, 0.0)
    gout = go_ref[...]
    nsteps = int(math.log2(C)) - 1

    heads = range(A_HEADS)
    lanes_of = [slice(h * LANES, (h + 1) * LANES) for h in heads]

    def chunk(c, carry):
        rows = pl.ds(pl.multiple_of(c * C, C), C)
        qc = [qs_ref[rows, sl] for sl in lanes_of]
        kc = [ks_ref[rows, sl] for sl in lanes_of]
        vc = [vs_ref[rows, sl] for sl in lanes_of]
        gcc = [gs_ref[rows, sl] for sl in lanes_of]
        bc = [bs_ref[rows, sl] for sl in lanes_of]
        kb = [kc[h] * bc[h] for h in heads]
        kq = [_nt(jnp.concatenate([kb[h], qc[h]], axis=0).astype(bf16), kc[h].astype(bf16)) for h in heads]
        decay = []
        for h in heads:
            gi = gcc[h][:, 0:C]
            gj = jnp.sum(jnp.where(ii == jj, gi, 0.0), axis=0, keepdims=True)
            decay.append(jnp.where(tri, jnp.exp(jnp.where(tri, gi - gj, 0.0)), 0.0))
        n = [jnp.where(strict, -(kq[h][0:C] * decay[h]), 0.0) for h in heads]
        a = [jnp.where(tri, kq[h][C:2 * C] * decay[h], 0.0) for h in heads]
        x = [eye + n[h] for h in heads]
        p = n
        for _ in range(nsteps):
            pb = [p[h].astype(bf16) for h in heads]
            p = [_mm(pb[h], pb[h]) for h in heads]
            x = [x[h] + _mm(x[h].astype(bf16), p[h].astype(bf16)) for h in heads]
        eg = [jnp.exp(gcc[h]) for h in heads]
        uw = [_mm(x[h].astype(bf16), jnp.concatenate([vc[h] * bc[h], kb[h] * eg[h]], axis=1).astype(bf16))
              for h in heads]
        S = [s_ref[h] for h in heads]
        Sb = [S[h].astype(bf16) for h in heads]
        wq = [_mm(jnp.concatenate([uw[h][:, LANES:], qc[h] * eg[h]], axis=0).astype(bf16), Sb[h])
              for h in heads]
        vnb = [(uw[h][:, :LANES] - wq[h][0:C]).astype(bf16) for h in heads]
        av = [_mm(a[h].astype(bf16), vnb[h]) for h in heads]
        glast = [gcc[h][C - 1:C, :] for h in heads]
        kv_new = [_tn((kc[h] * jnp.exp(glast[h] - gcc[h])).astype(bf16), vnb[h]) for h in heads]
        for h in heads:
            s_ref[h] = S[h] * jnp.exp(glast[h]) + kv_new[h]
            o_ref[rows, lanes_of[h]] = _rms(wq[h][C:2 * C] + av[h], gout) * _silu(z_ref[rows, lanes_of[h]])
        return carry

    if TL == C:
        chunk(0, 0)
    else:
        lax.fori_loop(0, TL // C, chunk, 0)

    @pl.when(t == pl.num_programs(1) - 1)
    def _():
        sout_ref[...] = s_ref[...]


def delta_mixer(P, row0, nb, L, TL, C, valid, buf8, conv8, ab8, gout, s0):
    nt = L // TL
    rb0 = row0 // TL
    tile = lambda c: pl.BlockSpec((TL, A_W), lambda b, t: (rb0 + b * nt + t, c))
    state = pl.BlockSpec((None, A_HEADS, LANES, LANES), lambda b, t: (b, 0, 0, 0))
    return pl.pallas_call(
        functools.partial(_delta_kernel, TL=TL, C=C, valid=valid),
        grid=(nb, nt),
        in_specs=[tile(0), tile(1), tile(2), tile(3),
                  pl.BlockSpec((TL, LANES), lambda b, t: (rb0 + b * nt + t, A_BA_BLK)),
                  pl.BlockSpec((None, SUBLANES, CONV_DIM), lambda b, t: (b, 0, 0)),
                  pl.BlockSpec((SUBLANES, CONV_DIM), lambda b, t: (0, 0)),
                  pl.BlockSpec((SUBLANES, LANES), lambda b, t: (0, 0)),
                  pl.BlockSpec((1, LANES), lambda b, t: (0, 0)),
                  state],
        out_specs=(pl.BlockSpec((TL, A_W), lambda b, t: (b * nt + t, 0)), state),
        out_shape=(jax.ShapeDtypeStruct((nb * L, A_W), f32),
                   jax.ShapeDtypeStruct((nb, A_HEADS, LANES, LANES), f32)),
        scratch_shapes=[pltpu.VMEM((TL, A_W), f32)] * 5
        + [pltpu.VMEM((SUBLANES, CONV_DIM), f32), pltpu.VMEM((A_HEADS, LANES, LANES), f32)],
        compiler_params=_cparams(("parallel", "arbitrary")),
        name="delta_mixer",
    )(P, P, P, P, P, buf8, conv8, ab8, gout, s0)


def _attn_residual(h_ref, mo_ref, om_ref, wo1_ref, wo2_ref):
    return (h_ref[...] + _mm(mo_ref[...].astype(bf16), wo1_ref[...])
            + _mm(om_ref[...].astype(bf16), wo2_ref[...]))


def _ffn_kernel(h_ref, mo_ref, om_ref, wo1_ref, wo2_ref, g_ref, wg_ref, wu_ref, wd_ref, o_ref, xn_ref, acc_ref):
    k = pl.program_id(1)

    @pl.when(k == 0)
    def _():
        h1 = _attn_residual(h_ref, mo_ref, om_ref, wo1_ref, wo2_ref)
        acc_ref[...] = h1
        xn_ref[...] = _rms(h1, g_ref[...]).astype(bf16)

    xn = xn_ref[...]
    a = _silu(_mm(xn, wg_ref[...])) * _mm(xn, wu_ref[...])
    acc_ref[...] += _mm(a.astype(bf16), wd_ref[...])

    @pl.when(k == pl.num_programs(1) - 1)
    def _():
        o_ref[...] = acc_ref[...]


def _outproj_specs(tm, K, mw, ow, imap, cmap):
    return [pl.BlockSpec((tm, K), imap), pl.BlockSpec((tm, mw), imap), pl.BlockSpec((tm, ow), imap),
            pl.BlockSpec((mw, K), cmap(0)), pl.BlockSpec((ow, K), cmap(mw // ow)), pl.BlockSpec((1, K), cmap(0))]


def outproj_ffn(H, mo, om, w_out, g, w_gu, w_down, tm, tf):
    T, K = H.shape
    FF = w_down.shape[0]
    nk = FF // tf
    mw, ow = mo.shape[1], om.shape[1]
    specs = _outproj_specs(tm, K, mw, ow, lambda i, k: (i, 0), lambda r: (lambda i, k: (r, 0)))
    return pl.pallas_call(
        _ffn_kernel,
        grid=(T // tm, nk),
        in_specs=specs + [pl.BlockSpec((K, tf), lambda i, k: (0, k)),
                          pl.BlockSpec((K, tf), lambda i, k: (0, nk + k)),
                          pl.BlockSpec((tf, K), lambda i, k: (k, 0))],
        out_specs=pl.BlockSpec((tm, K), lambda i, k: (i, 0)),
        out_shape=jax.ShapeDtypeStruct((T, K), f32),
        scratch_shapes=[pltpu.VMEM((tm, K), bf16), pltpu.VMEM((tm, K), f32)],
        compiler_params=_cparams(("parallel", "arbitrary")),
        name="outproj_ffn",
    )(H, mo, om, w_out, w_out, g, w_gu, w_gu, w_down)


def _router_kernel(h_ref, mo_ref, om_ref, wo1_ref, wo2_ref, g_ref, wr_ref, h1_ref, xn_ref, r_ref, cnt_ref):
    i = pl.program_id(0)
    tm = h_ref.shape[0]

    @pl.when(i == 0)
    def _():
        cnt_ref[...] = jnp.zeros_like(cnt_ref)

    h1 = _attn_residual(h_ref, mo_ref, om_ref, wo1_ref, wo2_ref)
    h1_ref[...] = h1
    xn = _rms(h1, g_ref[...])
    xn_ref[...] = xn.astype(bf16)
    lane = lax.broadcasted_iota(jnp.int32, (tm, LANES), 1)
    xh = xn.astype(bf16)
    xl = (xn - xh.astype(f32)).astype(bf16)
    wr = wr_ref[...]
    wh = wr.astype(bf16)
    wl = (wr - wh.astype(f32)).astype(bf16)
    logits = _mm(xh, wh) + _mm(xh, wl) + _mm(xl, wh)
    logits = jnp.where(lane < N_EXPERTS, logits, -jnp.inf)
    m1 = jnp.max(logits, axis=-1, keepdims=True)
    i1 = jnp.min(jnp.where(logits == m1, lane, LANES), axis=-1, keepdims=True)
    rest = jnp.where(lane == i1, -jnp.inf, logits)
    m2 = jnp.max(rest, axis=-1, keepdims=True)
    i2 = jnp.min(jnp.where(rest == m2, lane, LANES), axis=-1, keepdims=True)
    e2 = jnp.exp(m2 - m1)
    w1 = 1.0 / (1.0 + e2)
    w2 = e2 / (1.0 + e2)
    hot = jnp.where((lane == i1) | (lane == i2), 1.0, 0.0)
    ri = lax.broadcasted_iota(jnp.int32, (tm, tm), 0)
    ci = lax.broadcasted_iota(jnp.int32, (tm, tm), 1)
    before = jnp.where(ri > ci, 1.0, 0.0).astype(bf16)
    cum = _mm(before, hot.astype(bf16)) + cnt_ref[...]
    r1 = jnp.sum(jnp.where(lane == i1, cum, 0.0), axis=-1, keepdims=True)
    r2 = jnp.sum(jnp.where(lane == i2, cum, 0.0), axis=-1, keepdims=True)
    cnt_ref[...] += jnp.sum(hot, axis=0, keepdims=True)
    cols = (i1.astype(f32), i2.astype(f32), w1, w2, r1, r2)
    r = jnp.zeros((tm, LANES), f32)
    for c, val in enumerate(cols):
        r = jnp.where(lane == c, val, r)
    r_ref[...] = r


def outproj_router(H, mo, om, w_out, g, w_router, tm):
    T, K = H.shape
    mw, ow = mo.shape[1], om.shape[1]
    specs = _outproj_specs(tm, K, mw, ow, lambda i: (i, 0), lambda r: (lambda i: (r, 0)))
    return pl.pallas_call(
        _router_kernel,
        grid=(T // tm,),
        in_specs=specs + [pl.BlockSpec((K, LANES), lambda i: (0, 0))],
        out_specs=(pl.BlockSpec((tm, K), lambda i: (i, 0)), pl.BlockSpec((tm, K), lambda i: (i, 0)),
                   pl.BlockSpec((tm, LANES), lambda i: (i, 0))),
        out_shape=(jax.ShapeDtypeStruct((T, K), f32), jax.ShapeDtypeStruct((T, K), bf16),
                   jax.ShapeDtypeStruct((T, LANES), f32)),
        scratch_shapes=[pltpu.VMEM((1, LANES), f32)],
        compiler_params=_cparams(("arbitrary",)),
        name="outproj_router",
    )(H, mo, om, w_out, w_out, g, w_router)


def _moe_group_kernel(te_ref, nu_ref, x_ref, sw_ref, wg_ref, wu_ref, wd_ref, o_ref, acc_ref):
    del te_ref
    i = pl.program_id(0)
    k = pl.program_id(1)

    @pl.when(k == 0)
    def _():
        acc_ref[...] = jnp.zeros_like(acc_ref)

    @pl.when(i < nu_ref[0])
    def _():
        x = x_ref[...]
        a = _silu(_mm(x, wg_ref[...])) * _mm(x, wu_ref[...])
        acc_ref[...] += _mm(a.astype(bf16), wd_ref[...])

    @pl.when(k == pl.num_programs(1) - 1)
    def _():
        o_ref[...] = acc_ref[...] * sw_ref[...]


def moe_grouped(xg, sw, tile_e, n_used, w_gu, w_down, tm, tf):
    NP, K = xg.shape
    FF = w_down.shape[1]
    nk = FF // tf

    def kk(i, k, nu):
        return jnp.where(i < nu[0], k, nk - 1)

    grid_spec = pltpu.PrefetchScalarGridSpec(
        num_scalar_prefetch=2, grid=(NP // tm, nk),
        in_specs=[pl.BlockSpec((tm, K), lambda i, k, te, nu: (i, 0)),
                  pl.BlockSpec((tm, 1), lambda i, k, te, nu: (i, 0)),
                  pl.BlockSpec((None, K, tf), lambda i, k, te, nu: (te[i], 0, kk(i, k, nu))),
                  pl.BlockSpec((None, K, tf), lambda i, k, te, nu: (te[i], 0, nk + kk(i, k, nu))),
                  pl.BlockSpec((None, tf, K), lambda i, k, te, nu: (te[i], kk(i, k, nu), 0))],
        out_specs=pl.BlockSpec((tm, K), lambda i, k, te, nu: (i, 0)),
        scratch_shapes=[pltpu.VMEM((tm, K), f32)])
    return pl.pallas_call(
        _moe_group_kernel, grid_spec=grid_spec,
        out_shape=jax.ShapeDtypeStruct((NP, K), f32),
        compiler_params=_cparams(("parallel", "arbitrary")),
        name="moe_grouped",
    )(tile_e, n_used, xg, sw, w_gu, w_gu, w_down)


def _dispatch(r, tm):
    T = r.shape[0]
    A = 2 * T
    ntiles = A // tm + N_EXPERTS
    NP = ntiles * tm
    e_flat = r[:, 0:2].astype(jnp.int32).reshape(A)
    w_flat = r[:, 2:4].reshape(A)
    rank = r[:, 4:6].astype(jnp.int32)
    counts = jnp.sum((e_flat[:, None] == jnp.arange(N_EXPERTS)[None, :]).astype(jnp.int32), axis=0)
    pcounts = (counts + tm - 1) // tm * tm
    pends = jnp.cumsum(pcounts)
    pstarts = pends - pcounts
    starts = jnp.cumsum(counts) - counts
    n_used = pends[-1] // tm
    a_sorted = jnp.sort(e_flat * A + jnp.arange(A, dtype=jnp.int32)) % A
    tiles = jnp.arange(ntiles, dtype=jnp.int32)
    tile_e = jnp.minimum(jnp.searchsorted(pends, tiles * tm, side='right'), N_EXPERTS - 1).astype(jnp.int32)
    tile_e = jnp.where(tiles < n_used, tile_e, tile_e[jnp.maximum(n_used - 1, 0)])
    slot = jnp.arange(NP, dtype=jnp.int32)
    e_s = tile_e[slot // tm]
    rr = slot - pstarts[e_s]
    ok = (rr < counts[e_s]) & (slot < pends[-1])
    a_s = a_sorted[jnp.clip(starts[e_s] + rr, 0, A - 1)]
    src = a_s // 2
    sw = jnp.where(ok, w_flat[a_s], 0.0).reshape(NP, 1)
    slots = pstarts[r[:, 0:2].astype(jnp.int32)] + rank
    return src, sw, tile_e, n_used.reshape(1).astype(jnp.int32), slots


def outproj_moe(H, mo, om, w_out, g, w_router, w_gu, w_down, tm, tf):
    h1, xn, r = outproj_router(H, mo, om, w_out, g, w_router, tm)
    src, sw, tile_e, n_used, slots = _dispatch(r, tm)
    yg = moe_grouped(jnp.take(xn, src, axis=0), sw, tile_e, n_used, w_gu, w_down, tm, tf)
    return h1 + jnp.take(yg, slots[:, 0], axis=0) + jnp.take(yg, slots[:, 1], axis=0)


def _shared_kv_kernel(x_ref, g_ref, w_ref, cos_ref, sin_ref, ck_ref, cv_ref, sk_ref, sv_ref, wk_ref, wv_ref):
    y = _mm(_rms(x_ref[...], g_ref[...]).astype(bf16), w_ref[...])
    cos = cos_ref[...]
    sin = sin_ref[...]
    ck_ref[...] = y[:, 0 * LANES:1 * LANES]
    cv_ref[...] = y[:, 1 * LANES:2 * LANES]
    sk_ref[...] = _rope_lanes(y[:, 2 * LANES:3 * LANES], cos, sin)
    sv_ref[...] = y[:, 3 * LANES:4 * LANES]
    wk_ref[...] = _rope_lanes(y[:, 4 * LANES:5 * LANES], cos, sin)
    wv_ref[...] = y[:, 5 * LANES:6 * LANES]


def shared_kv(H, g, w, cos, sin, tm):
    T, K = H.shape
    row = pl.BlockSpec((tm, LANES), lambda i: (i, 0))
    out = jax.ShapeDtypeStruct((T, LANES), f32)
    return pl.pallas_call(
        _shared_kv_kernel,
        grid=(T // tm,),
        in_specs=[pl.BlockSpec((tm, K), lambda i: (i, 0)), pl.BlockSpec((1, K), lambda i: (0, 0)),
                  pl.BlockSpec((K, 6 * LANES), lambda i: (0, 0)), row, row],
        out_specs=(row,) * 6,
        out_shape=(out,) * 6,
        compiler_params=_cparams(("parallel",)),
        name="shared_kv",
    )(H, g, w, cos, sin)


def _compress_kernel(*refs, npages):
    nin = 2 * npages
    k_strips = jnp.concatenate([r[...] for r in refs[0:npages]], axis=0) if npages > 1 else refs[0][...]
    v_strips = jnp.concatenate([r[...] for r in refs[npages:nin]], axis=0) if npages > 1 else refs[npages][...]
    (pek_ref, pev_ref, w1k_ref, w1v_ref, w2k_ref, w2v_ref, cos_ref, sin_ref, ck_ref, cv_ref) = refs[nin:]
    n = k_strips.shape[0]
    half = 2 * CMP_HID
    row = lax.broadcasted_iota(jnp.int32, (n, LANES), 0)

    def tokens(strips, pe_ref, w1_ref, w2_ref):
        top = _mm((strips + pe_ref[0:1, :]).astype(bf16), w1_ref[:, 0:half])
        bot = _mm((strips + pe_ref[1:2, :]).astype(bf16), w1_ref[:, half:2 * half])
        hid = top + pltpu.roll(bot, n - 1, axis=0)
        out = _mm(_silu(hid).astype(bf16), w2_ref[...])
        return jnp.where(row < n - 1, out, 0.0)

    ck_ref[...] = _rope_lanes(tokens(k_strips, pek_ref, w1k_ref, w2k_ref), cos_ref[...], sin_ref[...])
    cv_ref[...] = tokens(v_strips, pev_ref, w1v_ref, w2v_ref)


def _drop_first(fn, _prefetch_ref, *refs):
    return fn(*refs)


def compress(k_src, v_src, k_maps, v_maps, strip_rows, nb, consts, prefetch=None):
    npages = len(k_maps)
    n = npages * strip_rows
    pek, pev, w1k, w1v, w2k, w2v, cos, sin = consts
    flat = 16 * LANES
    const2 = lambda shape: pl.BlockSpec(shape, lambda b, *_: (0, 0))
    in_specs = ([pl.BlockSpec((None, strip_rows, flat), m) for m in k_maps]
                + [pl.BlockSpec((None, strip_rows, flat), m) for m in v_maps]
                + [const2((2, flat)), const2((2, flat)), const2((flat, 4 * CMP_HID)), const2((flat, 4 * CMP_HID)),
                   const2((2 * CMP_HID, LANES)), const2((2 * CMP_HID, LANES)), const2((n, LANES)), const2((n, LANES))])
    out_spec = pl.BlockSpec((None, n, LANES), lambda b, *_: (b, 0, 0))
    out = jax.ShapeDtypeStruct((nb, n, LANES), f32)
    grid_spec = pltpu.PrefetchScalarGridSpec(
        num_scalar_prefetch=0 if prefetch is None else 1, grid=(nb,),
        in_specs=in_specs, out_specs=(out_spec, out_spec))
    args = ([] if prefetch is None else [prefetch]) + [k_src] * npages + [v_src] * npages + [pek, pev, w1k, w1v, w2k, w2v, cos, sin]
    kern = functools.partial(_compress_kernel, npages=npages)
    if prefetch is not None:
        kern = functools.partial(_drop_first, kern)
    return pl.pallas_call(
        kern, grid_spec=grid_spec, out_shape=(out, out),
        compiler_params=_cparams(("parallel",)), name="compress",
    )(*args)


def _nsa_queries(p, cos, sin):
    tq = p.shape[0]
    lane = lax.broadcasted_iota(jnp.int32, (tq, LANES), 1)
    rows = []
    for head in range(B_HEADS):
        grp = head // HPG
        blk = MEM_W // LANES + head // 2
        x = _rope_lanes(p[:, blk * LANES:(blk + 1) * LANES], cos, sin) * (HEAD_DIM ** -0.5)
        if head % 2 != grp:
            x = pltpu.roll(x, HEAD_DIM, axis=1)
        keep = (lane >= grp * HEAD_DIM) & (lane < (grp + 1) * HEAD_DIM)
        rows.append(jnp.where(keep, x, 0.0))
    return jnp.concatenate(rows, axis=0)


def _masked_softmax_parts(s, mask):
    sm = jnp.where(mask, s, NEG)
    m = jnp.max(sm, axis=-1, keepdims=True)
    p = jnp.where(mask, jnp.exp(sm - m), 0.0)
    return p, jnp.sum(p, axis=-1, keepdims=True)


def _safe_div(o, l):
    return jnp.where(l > 0.0, o / jnp.where(l > 0.0, l, 1.0), 0.0)


def _split3(x):
    hi = x.astype(bf16)
    r1 = x - hi.astype(f32)
    mid = r1.astype(bf16)
    lo = (r1 - mid.astype(f32)).astype(bf16)
    return hi, mid, lo


def _select_bias(pcn, qpos_row, past_blocks, mt_ref, eye_ref, nsel):
    tq = qpos_row.shape[1]
    nr = -(-nsel // SUBLANES) * SUBLANES
    mt = mt_ref[0:nr, :]
    blk = lax.broadcasted_iota(jnp.int32, (nr, tq), 0)
    lane = lax.broadcasted_iota(jnp.int32, (tq, LANES), 1)
    cur = qpos_row // SEL_BLK
    forced = (blk == 0) | (blk == cur) | (blk == cur - 1)
    valid = blk * SEL_BLK <= qpos_row
    stacks, picks = [], []
    for grp in range(2):
        psum = pcn[grp * HPG * tq:(grp * HPG + 1) * tq]
        for hh in range(1, HPG):
            psum = psum + pcn[(grp * HPG + hh) * tq:(grp * HPG + hh + 1) * tq]
        hi, mid, lo = _split3(psum)
        imp = _nt(mt, hi) + _nt(mt, mid) + _nt(mt, lo)
        imp = jnp.where(forced, 1e9, jnp.where(valid, imp, -1e9))
        rank = jnp.zeros((nr, tq), f32)
        for j in range(nsel):
            rj = imp[j:j + 1, :]
            rank = rank + jnp.where(rj > imp, 1.0, jnp.where(rj == imp, jnp.where(blk > j, 1.0, 0.0), 0.0))
        sel_t = jnp.where((rank < SEL_TOP) & (blk < nsel), 1.0, 0.0).astype(bf16)
        pick = _tn(sel_t, eye_ref[0:nr, :])
        picks.append(pick)
        stacks += [jnp.where((pick > 0.5) & (lane < past_blocks), 0.0, NEG)] * HPG
    return jnp.concatenate(stacks, axis=0), picks


def _nsa_combine(p, o_c, o_s, o_w):
    tq = p.shape[0]
    gl = _sigmoid(p[:, (MEM_W + B_W):(MEM_W + B_W) + LANES])

    def gate(j):
        return jnp.concatenate([gl[:, 3 * h + j:3 * h + j + 1] for h in range(B_HEADS)], axis=0)

    comb = gate(0) * o_c + gate(1) * o_s + gate(2) * o_w
    lane = lax.broadcasted_iota(jnp.int32, (tq, LANES), 1)
    cols = []
    for c in range(B_HEADS // 2):
        lo = comb[2 * c * tq:(2 * c + 1) * tq]
        hi = comb[(2 * c + 1) * tq:(2 * c + 2) * tq]
        if (2 * c) // HPG == 1:
            lo = pltpu.roll(lo, HEAD_DIM, axis=1)
        if (2 * c + 1) // HPG == 0:
            hi = pltpu.roll(hi, HEAD_DIM, axis=1)
        cols.append(jnp.where(lane < HEAD_DIM, lo, hi))
    return jnp.concatenate(cols, axis=1)


def _nsa_prompt_kernel(p_ref, cos_ref, sin_ref, ck_ref, cv_ref, sk_ref, sv_ref, wk_ref, wv_ref, mt_ref, et_ref,
                       eye_ref, o_ref, *, tq, kc, nsel):
    qb = pl.program_id(1)
    s0 = qb * tq
    R = B_HEADS * tq
    p = p_ref[...]
    qf = _nsa_queries(p, cos_ref[...], sin_ref[...])
    qb16 = qf.astype(bf16)
    ii = lax.broadcasted_iota(jnp.int32, (tq, tq), 0)
    jj = lax.broadcasted_iota(jnp.int32, (tq, tq), 1)
    qposR = s0 + jnp.concatenate([lax.broadcasted_iota(jnp.int32, (tq, 1), 0)] * B_HEADS, axis=0)
    lane = lax.broadcasted_iota(jnp.int32, (R, LANES), 1)
    cmask = ((lane * CMP_STRIDE + (CMP_BLK - 1)) <= qposR) & (lane < ck_ref.shape[0] - 1)
    pc, lc = _masked_softmax_parts(_nt(qb16, ck_ref[...].astype(bf16)), cmask)
    pcn = _safe_div(pc, lc)
    o_c = _mm(pcn.astype(bf16), cv_ref[...].astype(bf16))
    qpos_row = s0 + lax.broadcasted_iota(jnp.int32, (1, tq), 1)
    bias, _ = _select_bias(pcn, qpos_row, s0 // SEL_BLK, mt_ref, eye_ref, nsel)
    qsel = jnp.concatenate([qb16, bias.astype(bf16)], axis=1)
    sd = _nt(qb16, sk_ref[pl.ds(s0, tq), :].astype(bf16)).reshape(B_HEADS, tq, tq)
    sd = jnp.where((ii >= jj)[None], sd, NEG).reshape(R, tq)
    m0 = jnp.max(sd, axis=-1, keepdims=True)
    pd = jnp.exp(sd - m0)
    init = (m0, jnp.sum(pd, axis=-1, keepdims=True), _mm(pd.astype(bf16), sv_ref[pl.ds(s0, tq), :].astype(bf16)))

    def body(c, carry):
        m_run, l_run, acc = carry
        k0 = pl.multiple_of(c * kc, kc)
        ka = jnp.concatenate([sk_ref[pl.ds(k0, kc), :].astype(bf16), et_ref[pl.ds(k0, kc), :]], axis=1)
        s = _nt(qsel, ka)
        m_new = jnp.maximum(m_run, jnp.max(s, axis=-1, keepdims=True))
        alpha = jnp.exp(m_run - m_new)
        pe = jnp.exp(s - m_new)
        l_new = alpha * l_run + jnp.sum(pe, axis=-1, keepdims=True)
        acc_new = alpha * acc + _mm(pe.astype(bf16), sv_ref[pl.ds(k0, kc), :].astype(bf16))
        return m_new, l_new, acc_new

    _, l_run, acc = lax.fori_loop(0, (s0 + kc - 1) // kc, body, init)
    o_s = acc / l_run
    wlen = WINDOW + tq
    wrow = lax.broadcasted_iota(jnp.int32, (wlen, LANES), 0)
    wlane = lax.broadcasted_iota(jnp.int32, (wlen, LANES), 1)
    is_pad = jnp.where((wrow + (s0 - WINDOW) < 0) & (wlane == 0), 1.0, 0.0).astype(bf16)
    kw = jnp.concatenate([wk_ref[pl.ds(s0, wlen), :].astype(bf16), is_pad], axis=1)
    qwin = jnp.concatenate([qb16, jnp.where(lane == 0, NEG, 0.0).astype(bf16)], axis=1)
    sw = _nt(qwin, kw).reshape(B_HEADS, tq, wlen)
    far = jnp.where((jj > ii)[None], sw[:, :, 0:tq], NEG)
    near = jnp.where((jj <= ii)[None], sw[:, :, wlen - tq:wlen], NEG)
    sw = jnp.concatenate([far, sw[:, :, tq:wlen - tq], near], axis=2).reshape(R, wlen)
    pw = jnp.exp(sw - jnp.max(sw, axis=-1, keepdims=True))
    o_w = _mm(pw.astype(bf16), wv_ref[pl.ds(s0, wlen), :].astype(bf16)) / jnp.sum(pw, axis=-1, keepdims=True)
    o_ref[...] = _nsa_combine(p, o_c, o_s, o_w)


def nsa_prompt(P, cos, sin, ck, cv, sk, sv, wk_pad, wv_pad, mt, et, eye, nb, L, tq, kc):
    nq = L // tq
    nsel = L // SEL_BLK
    seq = pl.BlockSpec((L, LANES), lambda b, i: (b, 0))
    win = pl.BlockSpec((None, WINDOW + L, LANES), lambda b, i: (b, 0, 0))
    cmp_spec = pl.BlockSpec((None, LANES, LANES), lambda b, i: (b, 0, 0))
    const = lambda shape: pl.BlockSpec(shape, lambda b, i: (0, 0))
    return pl.pallas_call(
        functools.partial(_nsa_prompt_kernel, tq=tq, kc=kc, nsel=nsel),
        grid=(nb, nq),
        in_specs=[pl.BlockSpec((tq, B_IN_PAD), lambda b, i: (b * nq + i, 0)),
                  pl.BlockSpec((tq, LANES), lambda b, i: (i, 0)),
                  pl.BlockSpec((tq, LANES), lambda b, i: (i, 0)),
                  cmp_spec, cmp_spec, seq, seq, win, win,
                  const((LANES, LANES)), const((L, LANES)), const((LANES, LANES))],
        out_specs=pl.BlockSpec((tq, B_W), lambda b, i: (b * nq + i, 0)),
        out_shape=jax.ShapeDtypeStruct((nb * L, B_W), f32),
        compiler_params=_cparams(("parallel", "parallel")),
        name="nsa_prompt",
    )(P, cos, sin, ck, cv, sk, sv, wk_pad, wv_pad, mt, et, eye)


def _nsa_sample_kernel(*refs, npages, past):
    refs = refs[1:]
    kpages = refs[0:npages]
    vpages = refs[npages:2 * npages]
    (p_ref, cos_ref, sin_ref, ck_ref, cv_ref, nsk_ref, nsv_ref, cwk_ref, cwv_ref, nwk_ref, nwv_ref,
     mt_ref, et_ref, eye_ref, o_ref) = refs[2 * npages:]
    tq = S_PAD
    R = B_HEADS * tq
    p = p_ref[...]
    qb16 = _nsa_queries(p, cos_ref[...], sin_ref[...]).astype(bf16)
    tR = jnp.concatenate([lax.broadcasted_iota(jnp.int32, (tq, 1), 0)] * B_HEADS, axis=0)
    qposR = past + tR
    lane = lax.broadcasted_iota(jnp.int32, (R, LANES), 1)
    cmask = ((lane * CMP_STRIDE + (CMP_BLK - 1)) <= qposR) & (lane < ck_ref.shape[0] - 1)
    pc, lc = _masked_softmax_parts(_nt(qb16, ck_ref[...].astype(bf16)), cmask)
    pcn = _safe_div(pc, lc)
    o_c = _mm(pcn.astype(bf16), cv_ref[...].astype(bf16))
    qpos_row = past + lax.broadcasted_iota(jnp.int32, (1, tq), 1)
    new_blk = past // SEL_BLK
    bias, picks = _select_bias(pcn, qpos_row, new_blk, mt_ref, eye_ref, new_blk + 1)
    new_ok = lax.broadcasted_iota(jnp.int32, (R, tq), 1) <= tR
    ka = jnp.concatenate([jnp.concatenate([r[...] for r in kpages], axis=0).astype(bf16), et_ref[...]], axis=1)
    s_old = _nt(jnp.concatenate([qb16, bias.astype(bf16)], axis=1), ka)
    new_kept = jnp.concatenate([picks[h // HPG][:, new_blk:new_blk + 1] for h in range(B_HEADS)], axis=0) > 0.5
    s_new = jnp.where(new_ok, jnp.where(new_kept, _nt(qb16, nsk_ref[...].astype(bf16)), NEG), NEG)
    m = jnp.maximum(jnp.max(s_old, axis=-1, keepdims=True), jnp.max(s_new, axis=-1, keepdims=True))
    p_old = jnp.exp(s_old - m)
    p_new = jnp.exp(s_new - m)
    l = jnp.sum(p_old, axis=-1, keepdims=True) + jnp.sum(p_new, axis=-1, keepdims=True)
    svb = jnp.concatenate([r[...] for r in vpages], axis=0).astype(bf16)
    o_s = (_mm(p_old.astype(bf16), svb) + _mm(p_new.astype(bf16), nsv_ref[...].astype(bf16))) / l
    wb = cwk_ref.shape[0]
    wdist = qposR - (past - wb + lax.broadcasted_iota(jnp.int32, (R, wb), 1))
    w_old = jnp.where((wdist >= 0) & (wdist < WINDOW), _nt(qb16, cwk_ref[...].astype(bf16)), NEG)
    w_new = jnp.where(new_ok, _nt(qb16, nwk_ref[...].astype(bf16)), NEG)
    mw = jnp.maximum(jnp.max(w_old, axis=-1, keepdims=True), jnp.max(w_new, axis=-1, keepdims=True))
    pw_old = jnp.exp(w_old - mw)
    pw_new = jnp.exp(w_new - mw)
    lw = jnp.sum(pw_old, axis=-1, keepdims=True) + jnp.sum(pw_new, axis=-1, keepdims=True)
    o_w = (_mm(pw_old.astype(bf16), cwv_ref[...].astype(bf16)) + _mm(pw_new.astype(bf16), nwv_ref[...].astype(bf16))) / lw
    o_ref[...] = _nsa_combine(p, o_c, o_s, o_w)


def nsa_sample(P, row0, page_table, pool_k, pool_v, cos, sin, ck, cv, nsk, nsv, cwk, cwv, nwk, nwv, mt, et, eye):
    nb = cwk.shape[0]
    npages = page_table.shape[0] // nb
    past = npages * PAGE
    rb0 = row0 // S_PAD
    page = lambda j: pl.BlockSpec((None, PAGE, LANES), lambda b, pt: (pt[b * npages + j], 0, 0))
    new_rows = pl.BlockSpec((S_PAD, LANES), lambda b, pt: (rb0 + b, 0))
    per_seq = lambda n: pl.BlockSpec((None, n, LANES), lambda b, pt: (b, 0, 0))
    in_specs = ([page(j) for j in range(npages)] + [page(j) for j in range(npages)]
                + [pl.BlockSpec((S_PAD, B_IN_PAD), lambda b, pt: (rb0 + b, 0)),
                   pl.BlockSpec((S_PAD, LANES), lambda b, pt: (0, 0)),
                   pl.BlockSpec((S_PAD, LANES), lambda b, pt: (0, 0)),
                   per_seq(LANES), per_seq(LANES), new_rows, new_rows,
                   per_seq(cwk.shape[1]), per_seq(cwk.shape[1]), new_rows, new_rows,
                   pl.BlockSpec((LANES, LANES), lambda b, pt: (0, 0)),
                   pl.BlockSpec((past, LANES), lambda b, pt: (0, 0)),
                   pl.BlockSpec((LANES, LANES), lambda b, pt: (0, 0))])
    grid_spec = pltpu.PrefetchScalarGridSpec(
        num_scalar_prefetch=1, grid=(nb,), in_specs=in_specs,
        out_specs=pl.BlockSpec((S_PAD, B_W), lambda b, pt: (b, 0)))
    return pl.pallas_call(
        functools.partial(_nsa_sample_kernel, npages=npages, past=past),
        grid_spec=grid_spec,
        out_shape=jax.ShapeDtypeStruct((nb * S_PAD, B_W), f32),
        compiler_params=_cparams(("parallel",)),
        name="nsa_sample",
    )(page_table, *([pool_k] * npages), *([pool_v] * npages), P, cos, sin, ck, cv, nsk, nsv, cwk, cwv, nwk, nwv,
      mt, et, eye)


def _rope_tables(pos):
    half = HEAD_DIM // 2
    inv = ROPE_THETA ** (-jnp.arange(half, dtype=f32) / half)
    ang = pos.astype(f32)[:, None] * inv[None, :]
    cos = jnp.tile(jnp.cos(ang), (1, LANES // half))
    sin = jnp.tile(jnp.sin(ang), (1, LANES // half))
    sign = jnp.where((jnp.arange(LANES) % HEAD_DIM) < half, -1.0, 1.0).astype(f32)
    return cos, sin * sign[None, :]


def _cmp_to_sel(ncmp, nsel):
    per = SEL_BLK // CMP_STRIDE
    sub = CMP_BLK // CMP_STRIDE
    i = jnp.arange(LANES)[:, None]
    j = jnp.arange(LANES)[None, :]
    m = jnp.zeros((LANES, LANES), f32)
    for r in range(sub):
        m = m + (((i + r) // per) == j).astype(f32)
    m = jnp.where((i < ncmp) & (j < nsel), m / sub, 0.0)
    return m.astype(bf16)


def _block_expand(nkeys):
    j = jnp.arange(LANES)[:, None]
    k = jnp.arange(nkeys)[None, :]
    return ((k // SEL_BLK) == j).astype(bf16)


def _compress_consts(pe, w1, w2):
    w1r = w1.reshape(CMP_BLK, HEAD_DIM, CMP_HID)
    eye2 = jnp.eye(2, dtype=f32)

    def expand(w):
        return jnp.einsum('ldc,gh->lgdhc', w, eye2).reshape(16 * LANES, 2 * CMP_HID)

    w1s = jnp.concatenate([expand(w1r[:16]), expand(w1r[16:])], axis=1).astype(bf16)
    pes = jnp.stack([jnp.broadcast_to(pe[:16, None, :], (16, 2, HEAD_DIM)).reshape(-1),
                     jnp.broadcast_to(pe[16:, None, :], (16, 2, HEAD_DIM)).reshape(-1)])
    w2s = jnp.einsum('cd,gh->gchd', w2, eye2).reshape(2 * CMP_HID, LANES).astype(bf16)
    return pes, w1s, w2s


def kernel(x_prompt, x_sample, mem_prompt, state_conv, state_ssm, cache_cmp_k, cache_cmp_v, cache_sel_k, cache_sel_v, cache_win_k, cache_win_v, cache_mem_k, cache_mem_v, page_table, norm_mix, norm_ffn, norm_mem, w_mem_kv, w_in_a, conv_w_a, a_log, dt_bias, norm_out_a, w_out_a, w_in_b, w_out_b, norm_kv, w_kv_shared, cmp_pe_k, cmp_w1_k, cmp_w2_k, cmp_pe_v, cmp_w1_v, cmp_w2_v, w_gu_dense, w_down_dense, w_router, w_gu_exp, w_down_exp, norm_final):
    bp, lp, d = x_prompt.shape
    bs, ls, _ = x_sample.shape
    depth = norm_mix.shape[0]
    n_a = w_in_a.shape[0]
    past = page_table.shape[1] * PAGE
    tp = bp * lp
    T = tp + bs * S_PAD
    tm = 512

    xs = jnp.pad(x_sample, ((0, 0), (0, S_PAD - ls), (0, 0)))
    H = jnp.concatenate([x_prompt.reshape(tp, d), xs.reshape(bs * S_PAD, d)], axis=0)

    qkvz = MEM_W + CONV_DIM + A_W
    w_in_a_b = jnp.concatenate(
        [w_in_a[:, :, MEM_W:qkvz], w_in_a[:, :, :MEM_W], w_in_a[:, :, qkvz:],
         jnp.zeros(w_in_a.shape[:2] + (A_IN_PAD - w_in_a.shape[2],), f32)], axis=2).astype(bf16)
    w_in_b_b = jnp.pad(w_in_b, ((0, 0), (0, 0), (0, B_IN_PAD - w_in_b.shape[2]))).astype(bf16)
    w_out_a_b = w_out_a.astype(bf16)
    w_out_b_b = w_out_b.astype(bf16)
    w_gu_dense_b = w_gu_dense.astype(bf16)
    w_down_dense_b = w_down_dense.astype(bf16)
    w_gu_exp_b = w_gu_exp.astype(bf16)
    w_down_exp_b = w_down_exp.astype(bf16)
    w_router_p = jnp.pad(w_router, ((0, 0), (0, 0), (0, LANES - N_EXPERTS)))

    mem_k, mem_v = mem_kv(mem_prompt.reshape(bp * MEM_LEN, d), norm_mem.reshape(depth, 1, d),
                          w_mem_kv.astype(bf16), tm)
    cmk = cache_mem_k.reshape(depth, bs, MEM_LEN, MEM_W)
    cmv = cache_mem_v.reshape(depth, bs, MEM_LEN, MEM_W)

    pos_p = jnp.arange(lp, dtype=jnp.int32)
    pos_s = past + jnp.arange(S_PAD, dtype=jnp.int32)
    cos_p, sin_p = _rope_tables(pos_p)
    cos_s, sin_s = _rope_tables(pos_s)

    conv_p, ssm_p, conv_s, ssm_s = [], [], [], []
    kv = None
    for layer in range(depth):
        is_a = layer < n_a
        g_mix = norm_mix[layer].reshape(1, d)
        if is_a:
            P = norm_matmul(H, g_mix, w_in_a_b[layer], tm, A_IN_PAD // 3)
            qblk = A_MEMQ_BLK
        else:
            lb = layer - n_a
            qblk = 0
            if lb == 0:
                cos_all = jnp.concatenate([jnp.tile(cos_p, (bp, 1)), jnp.tile(cos_s, (bs, 1))], axis=0)
                sin_all = jnp.concatenate([jnp.tile(sin_p, (bp, 1)), jnp.tile(sin_s, (bs, 1))], axis=0)
                kv = shared_kv(H, norm_kv.reshape(1, d), w_kv_shared.astype(bf16), cos_all, sin_all, tm)
                ncmp = (lp - CMP_BLK) // CMP_STRIDE + 1
                cpos = jnp.arange(LANES, dtype=jnp.int32) * CMP_STRIDE + CMP_BLK - 1
                cos_c, sin_c = _rope_tables(cpos)
                pek, w1k, w2k = _compress_consts(cmp_pe_k, cmp_w1_k, cmp_w2_k)
                pev, w1v, w2v = _compress_consts(cmp_pe_v, cmp_w1_v, cmp_w2_v)
                consts = (pek, pev, w1k, w1v, w2k, w2v, cos_c, sin_c)
                strips = lp // 16
                ck_p, cv_p = compress(kv[0][:tp].reshape(bp, strips, 16 * LANES),
                                      kv[1][:tp].reshape(bp, strips, 16 * LANES),
                                      [lambda b: (b, 0, 0)], [lambda b: (b, 0, 0)], strips, bp, consts)
                npages = page_table.shape[1]
                pt_flat = page_table.reshape(-1)
                pool_strips = PAGE // 16
                pmap = lambda j: (lambda b, pt: (pt[b * npages + j], 0, 0))
                ck_s, cv_s = compress(cache_cmp_k.reshape(-1, pool_strips, 16 * LANES),
                                      cache_cmp_v.reshape(-1, pool_strips, 16 * LANES),
                                      [pmap(j) for j in range(npages)], [pmap(j) for j in range(npages)],
                                      pool_strips, bs, consts, prefetch=pt_flat)
                mt_p = _cmp_to_sel(ncmp, lp // SEL_BLK).T
                et_p = _block_expand(lp).T
                ncmp_s = (past + ls - CMP_BLK) // CMP_STRIDE + 1
                mt_s = _cmp_to_sel(ncmp_s, -(-(past + ls) // SEL_BLK)).T
                et_s = _block_expand(past).T
                eye = jnp.eye(LANES, dtype=bf16)
                wk_pad = jnp.pad(kv[4][:tp].reshape(bp, lp, LANES), ((0, 0), (WINDOW, 0), (0, 0)))
                wv_pad = jnp.pad(kv[5][:tp].reshape(bp, lp, LANES), ((0, 0), (WINDOW, 0), (0, 0)))
                pool_sk = cache_sel_k.reshape(-1, PAGE, LANES)
                pool_sv = cache_sel_v.reshape(-1, PAGE, LANES)
                cwk = cache_win_k.reshape(bs, -1, LANES)
                cwv = cache_win_v.reshape(bs, -1, LANES)
            P = norm_matmul(H, g_mix, w_in_b_b[lb], tm, B_IN_PAD)
        om_p = mem_attn_prompt(P, qblk, mem_k[layer], mem_v[layer], bp, lp, 512)
        om_s = mem_attn_sample(P, qblk, tp, cmk[layer], cmv[layer], 8)
        om = jnp.concatenate([om_p, om_s], axis=0)
        if is_a:
            conv8 = jnp.pad(conv_w_a[layer], ((0, SUBLANES - conv_w_a.shape[1]), (0, 0)))
            ab8 = jnp.zeros((SUBLANES, LANES), f32)
            ab8 = ab8.at[0, A_HEADS:2 * A_HEADS].set(a_log[layer]).at[1, A_HEADS:2 * A_HEADS].set(dt_bias[layer])
            gout = norm_out_a[layer].reshape(1, LANES)
            zero_buf = jnp.zeros((bp, SUBLANES, CONV_DIM), f32)
            zero_state = jnp.zeros((bp, A_HEADS, LANES, LANES), f32)
            mo_p, sp = delta_mixer(P, 0, bp, lp, 512, 64, 512, zero_buf, conv8, ab8, gout, zero_state)
            buf_s = jnp.pad(state_conv[layer], ((0, 0), (SUBLANES - state_conv.shape[2], 0), (0, 0)))
            mo_s, ss = delta_mixer(P, tp, bs, S_PAD, S_PAD, S_PAD, ls, buf_s, conv8, ab8, gout, state_ssm[layer])
            conv_p.append(jnp.stack([P[b * lp + lp - 3:(b + 1) * lp, :CONV_DIM] for b in range(bp)]))
            conv_s.append(P[tp:, :CONV_DIM].reshape(bs, S_PAD, CONV_DIM)[:, ls - 3:ls])
            ssm_p.append(sp)
            ssm_s.append(ss)
            w_out = w_out_a_b[layer]
        else:
            mo_p = nsa_prompt(P, cos_p, sin_p, ck_p, cv_p, kv[2], kv[3], wk_pad, wv_pad, mt_p, et_p, eye,
                              bp, lp, Q_BLOCK, 512)
            mo_s = nsa_sample(P, tp, pt_flat, pool_sk, pool_sv, cos_s, sin_s, ck_s, cv_s, kv[2], kv[3], cwk, cwv,
                              kv[4], kv[5], mt_s, et_s, eye)
            w_out = w_out_b_b[lb]
        mo = jnp.concatenate([mo_p, mo_s], axis=0)
        g_ffn = norm_ffn[layer].reshape(1, d)
        if layer % 2 == 0:
            H = outproj_ffn(H, mo, om, w_out, g_ffn, w_gu_dense_b[layer // 2], w_down_dense_b[layer // 2], tm, D_FF // 2)
        else:
            H = outproj_moe(H, mo, om, w_out, g_ffn, w_router_p[layer // 2], w_gu_exp_b[layer // 2],
                            w_down_exp_b[layer // 2], tm, D_FF_EXPERT // 2)
    Y = final_norm(H, norm_final.reshape(1, d), tm)

    def rows_p(a):
        return a[:tp].reshape(bp, lp, 2, HEAD_DIM)

    def rows_s(a):
        return a[tp:].reshape(bs, S_PAD, 2, HEAD_DIM)[:, :ls]

    wlen = min(WINDOW, lp)
    return (Y[:tp].reshape(bp, lp, d), Y[tp:].reshape(bs, S_PAD, d)[:, :ls],
            jnp.stack(conv_p), jnp.stack(ssm_p),
            rows_p(kv[0]), rows_p(kv[1]), rows_p(kv[2]), rows_p(kv[3]),
            rows_p(kv[4])[:, lp - wlen:], rows_p(kv[5])[:, lp - wlen:],
            mem_k.reshape(depth, bp, MEM_LEN, 4, HEAD_DIM), mem_v.reshape(depth, bp, MEM_LEN, 4, HEAD_DIM),
            jnp.stack(conv_s), jnp.stack(ssm_s),
            rows_s(kv[0]), rows_s(kv[1]), rows_s(kv[2]), rows_s(kv[3]),
            jnp.concatenate([cache_win_k[:, ls:], rows_s(kv[4])], axis=1),
            jnp.concatenate([cache_win_v[:, ls:], rows_s(kv[5])], axis=1))
```

```python
import functools
import math

import jax
import jax.numpy as jnp
from jax import lax
from jax.experimental import pallas as pl
from jax.experimental.pallas import tpu as pltpu

f32 = jnp.float32
bf16 = jnp.bfloat16

EPS = 1e-6
NEG = -1e30
ROPE_THETA = 10000.0
HEAD_DIM = 64
LANES = 128
SUBLANES = 8
VMEM_LIMIT = 48 * 1024 * 1024

D_MODEL = 1024
MEM_LEN = 256
MEM_W = 256
A_HEADS = 6
A_W = 768
CONV_DIM = 2304
A_IN_PAD = 3456
A_MEMQ_BLK = 12
A_BA_BLK = 26
B_HEADS = 12
HPG = 6
B_W = 768
B_IN_PAD = 1152
CMP_BLK = 32
CMP_STRIDE = 16
CMP_HID = 256
SEL_BLK = 64
SEL_TOP = 16
WINDOW = 512
Q_BLOCK = 128
D_FF = 2816
N_EXPERTS = 8
D_FF_EXPERT = 3584
PAGE = 128
S_PAD = 8


def _cparams(sem):
    return pltpu.CompilerParams(dimension_semantics=sem, vmem_limit_bytes=VMEM_LIMIT)


def _nt(a, b):
    return lax.dot_general(a, b, (((1,), (1,)), ((), ())), preferred_element_type=f32)


def _tn(a, b):
    return lax.dot_general(a, b, (((0,), (0,)), ((), ())), preferred_element_type=f32)


def _mm(a, b):
    return jnp.dot(a, b, preferred_element_type=f32)


def _rms(x, g):
    return (x * lax.rsqrt(jnp.mean(x * x, axis=-1, keepdims=True) + EPS)) * g


def _sigmoid(x):
    return 1.0 / (1.0 + jnp.exp(-x))


def _silu(x):
    return x * _sigmoid(x)


def _softplus(x):
    return jnp.maximum(x, 0.0) + jnp.log(1.0 + jnp.exp(-jnp.abs(x)))


def _rope_lanes(x, cos, sin_signed):
    lane = lax.broadcasted_iota(jnp.int32, x.shape, x.ndim - 1)
    lo = (lane % HEAD_DIM) < (HEAD_DIM // 2)
    partner = jnp.where(lo, pltpu.roll(x, LANES - HEAD_DIM // 2, axis=x.ndim - 1),
                        pltpu.roll(x, HEAD_DIM // 2, axis=x.ndim - 1))
    return x * cos + partner * sin_signed


def _norm_matmul_kernel(x_ref, g_ref, w_ref, o_ref, xn_ref):
    @pl.when(pl.program_id(1) == 0)
    def _():
        xn_ref[...] = _rms(x_ref[...], g_ref[...]).astype(bf16)

    o_ref[...] = _mm(xn_ref[...], w_ref[...])


def norm_matmul(x, g, w, layer, tm, tn):
    T, K = x.shape
    N = w.shape[2]
    return pl.pallas_call(
        _norm_matmul_kernel,
        grid=(T // tm, N // tn),
        in_specs=[pl.BlockSpec((tm, K), lambda i, j: (i, 0)),
                  pl.BlockSpec((1, K), lambda i, j: (0, 0)),
                  pl.BlockSpec((None, K, tn), lambda i, j: (layer, 0, j))],
        out_specs=pl.BlockSpec((tm, tn), lambda i, j: (i, j)),
        out_shape=jax.ShapeDtypeStruct((T, N), f32),
        scratch_shapes=[pltpu.VMEM((tm, K), bf16)],
        compiler_params=_cparams(("parallel", "arbitrary")),
        name="norm_matmul",
    )(x, g, w)


def _mem_kv_kernel(x_ref, g_ref, w_ref, k_ref, v_ref):
    y = _mm(_rms(x_ref[...], g_ref[...]).astype(bf16), w_ref[...])
    k_ref[...] = y[:, :MEM_W]
    v_ref[...] = y[:, MEM_W:]


def mem_kv(mem, g, w, tm):
    R, K = mem.shape
    NL = w.shape[0]
    out = jax.ShapeDtypeStruct((NL, R, MEM_W), f32)
    return pl.pallas_call(
        _mem_kv_kernel,
        grid=(NL, R // tm),
        in_specs=[pl.BlockSpec((tm, K), lambda l, i: (i, 0)),
                  pl.BlockSpec((None, 1, K), lambda l, i: (l, 0, 0)),
                  pl.BlockSpec((None, K, 2 * MEM_W), lambda l, i: (l, 0, 0))],
        out_specs=(pl.BlockSpec((None, tm, MEM_W), lambda l, i: (l, i, 0)),
                   pl.BlockSpec((None, tm, MEM_W), lambda l, i: (l, i, 0))),
        out_shape=(out, out),
        compiler_params=_cparams(("parallel", "parallel")),
        name="mem_kv",
    )(mem, g, w)


def _final_norm_kernel(x_ref, g_ref, o_ref):
    o_ref[...] = _rms(x_ref[...], g_ref[...])


def final_norm(x, g, tm):
    T, K = x.shape
    return pl.pallas_call(
        _final_norm_kernel,
        grid=(T // tm,),
        in_specs=[pl.BlockSpec((tm, K), lambda i: (i, 0)), pl.BlockSpec((1, K), lambda i: (0, 0))],
        out_specs=pl.BlockSpec((tm, K), lambda i: (i, 0)),
        out_shape=jax.ShapeDtypeStruct((T, K), f32),
        compiler_params=_cparams(("parallel",)),
        name="final_norm",
    )(x, g)


def _mem_attend(q, k, v, transposed=False):
    tq = q.shape[0]
    nh = MEM_W // HEAD_DIM
    lane = lax.broadcasted_iota(jnp.int32, (tq, MEM_W), 1)
    masks = [(lane >= HEAD_DIM * h) & (lane < HEAD_DIM * (h + 1)) for h in range(nh)]
    qs = jnp.concatenate([jnp.where(m, q, 0.0) for m in masks], axis=0).astype(bf16)
    s = (_mm(qs, k) if transposed else _nt(qs, k)) * (HEAD_DIM ** -0.5)
    p = jnp.exp(s - jnp.max(s, axis=-1, keepdims=True))
    l = jnp.sum(p, axis=-1, keepdims=True)
    pb = p.astype(bf16)
    o = (_nt(pb, v) if transposed else _mm(pb, v)) / l
    out = jnp.zeros((tq, MEM_W), f32)
    for h in range(nh):
        out = jnp.where(masks[h], o[h * tq:(h + 1) * tq], out)
    return out


def _mem_attn_prompt_kernel(q_ref, k_ref, v_ref, o_ref):
    o_ref[...] = _mem_attend(q_ref[...], k_ref[...].astype(bf16), v_ref[...].astype(bf16))


def mem_attn_prompt(P, qblk, mk, mv, layer, nb, L, tq):
    nq = L // tq
    return pl.pallas_call(
        _mem_attn_prompt_kernel,
        grid=(nb, nq),
        in_specs=[pl.BlockSpec((tq, MEM_W), lambda b, i: (b * nq + i, qblk)),
                  pl.BlockSpec((None, MEM_LEN, MEM_W), lambda b, i: (layer, b, 0)),
                  pl.BlockSpec((None, MEM_LEN, MEM_W), lambda b, i: (layer, b, 0))],
        out_specs=pl.BlockSpec((tq, MEM_W), lambda b, i: (b * nq + i, 0)),
        out_shape=jax.ShapeDtypeStruct((nb * L, MEM_W), f32),
        compiler_params=_cparams(("parallel", "parallel")),
        name="mem_attn_prompt",
    )(P, mk, mv)


def _mem_attn_sample_kernel(q_ref, k_ref, v_ref, o_ref, *, bb):
    for b in range(bb):
        rows = slice(b * S_PAD, (b + 1) * S_PAD)
        o_ref[rows, :] = _mem_attend(q_ref[rows, :], k_ref[b].astype(bf16), v_ref[b].astype(bf16), transposed=True)


def mem_attn_sample(P, qblk, row0, ckt, cvt, layer, bb):
    nb = ckt.shape[1]
    blk0 = row0 // (bb * S_PAD)
    return pl.pallas_call(
        functools.partial(_mem_attn_sample_kernel, bb=bb),
        grid=(nb // bb,),
        in_specs=[pl.BlockSpec((bb * S_PAD, MEM_W), lambda i: (blk0 + i, qblk)),
                  pl.BlockSpec((None, bb, MEM_W, MEM_LEN), lambda i: (layer, i, 0, 0)),
                  pl.BlockSpec((None, bb, MEM_W, MEM_LEN), lambda i: (layer, i, 0, 0))],
        out_specs=pl.BlockSpec((bb * S_PAD, MEM_W), lambda i: (i, 0)),
        out_shape=jax.ShapeDtypeStruct((nb * S_PAD, MEM_W), f32),
        compiler_params=_cparams(("parallel",)),
        name="mem_attn_sample",
    )(P, ckt, cvt)


def _delta_kernel(q_ref, k_ref, v_ref, z_ref, ba_ref, buf_ref, cw_ref, ab_ref, go_ref, s0_ref,
                  o_ref, sout_ref, qs_ref, ks_ref, vs_ref, gs_ref, bs_ref, tail_ref, s_ref,
                  *, TL, C, valid):
    t = pl.program_id(1)

    @pl.when(t == 0)
    def _():
        tail_ref[...] = buf_ref[...]
        s_ref[...] = s0_ref[...]

    row8 = lax.broadcasted_iota(jnp.int32, (SUBLANES, A_W), 0)

    def conv(x_ref, c0):
        x = x_ref[...]
        b8 = tail_ref[:, c0:c0 + A_W]
        c8 = cw_ref[:, c0:c0 + A_W]
        acc = x * c8[3:4, :]
        for s in range(1, 4):
            r = pltpu.roll(x, s, axis=0)
            fix = jnp.where(row8 < s, pltpu.roll(b8, s, axis=0), r[0:SUBLANES])
            r = jnp.concatenate([fix, r[SUBLANES:]], axis=0) if TL > SUBLANES else fix
            acc = acc + r * c8[3 - s:4 - s, :]
        tail_ref[:, c0:c0 + A_W] = x[TL - SUBLANES:TL]
        return _silu(acc)

    q = conv(q_ref, 0)
    k = conv(k_ref, A_W)
    v = conv(v_ref, 2 * A_W)
    ba = ba_ref[...]
    ab = ab_ref[...]
    beta_all = _sigmoid(ba)
    g_all = -jnp.exp(ab[0:1, :]) * _softplus(ba + ab[1:2, :])
    if valid < TL:
        live = lax.broadcasted_iota(jnp.int32, (TL, A_W), 0) < valid
        live1 = lax.broadcasted_iota(jnp.int32, (TL, LANES), 0) < valid
        k = jnp.where(live, k, 0.0)
        v = jnp.where(live, v, 0.0)
        beta_all = jnp.where(live1, beta_all, 0.0)
        g_all = jnp.where(live1, g_all, 0.0)
    rowc = lax.broadcasted_iota(jnp.int32, (TL, LANES), 0) & (C - 1)
    gc_all = g_all
    s = 1
    while s < C:
        gc_all = gc_all + jnp.where(rowc >= s, pltpu.roll(gc_all, s, axis=0), 0.0)
        s *= 2
    vs_ref[...] = v
    for h in range(A_HEADS):
        sl = slice(h * LANES, (h + 1) * LANES)
        qh = q[:, sl]
        kh = k[:, sl]
        qs_ref[:, sl] = qh * lax.rsqrt(jnp.sum(qh * qh, axis=-1, keepdims=True) + EPS) * (LANES ** -0.5)
        ks_ref[:, sl] = kh * lax.rsqrt(jnp.sum(kh * kh, axis=-1, keepdims=True) + EPS)
        gs_ref[:, sl] = jnp.broadcast_to(gc_all[:, A_HEADS + h:A_HEADS + h + 1], (TL, LANES))
        bs_ref[:, sl] = jnp.broadcast_to(beta_all[:, h:h + 1], (TL, LANES))

    ii = lax.broadcasted_iota(jnp.int32, (C, C), 0)
    jj = lax.broadcasted_iota(jnp.int32, (C, C), 1)
    tri = ii >= jj
    strict = ii > jj
    eye = jnp.where(ii == jj, 1.0, 0.0)
    gout = go_ref[...]
    nsteps = int(math.log2(C)) - 1

    heads = range(A_HEADS)
    lanes_of = [slice(h * LANES, (h + 1) * LANES) for h in heads]

    def chunk(c, carry):
        rows = pl.ds(pl.multiple_of(c * C, C), C)
        qc = [qs_ref[rows, sl] for sl in lanes_of]
        kc = [ks_ref[rows, sl] for sl in lanes_of]
        vc = [vs_ref[rows, sl] for sl in lanes_of]
        gcc = [gs_ref[rows, sl] for sl in lanes_of]
        bc = [bs_ref[rows, sl] for sl in lanes_of]
        kb = [kc[h] * bc[h] for h in heads]
        kq = [_nt(jnp.concatenate([kb[h], qc[h]], axis=0).astype(bf16), kc[h].astype(bf16)) for h in heads]
        decay = []
        for h in heads:
            gi = gcc[h][:, 0:C]
            gj = jnp.sum(jnp.where(ii == jj, gi, 0.0), axis=0, keepdims=True)
            decay.append(jnp.where(tri, jnp.exp(jnp.where(tri, gi - gj, 0.0)), 0.0))
        n = [jnp.where(strict, -(kq[h][0:C] * decay[h]), 0.0) for h in heads]
        a = [jnp.where(tri, kq[h][C:2 * C] * decay[h], 0.0) for h in heads]
        x = [eye + n[h] for h in heads]
        p = n
        for _ in range(nsteps):
            pb = [p[h].astype(bf16) for h in heads]
            p = [_mm(pb[h], pb[h]) for h in heads]
            x = [x[h] + _mm(x[h].astype(bf16), p[h].astype(bf16)) for h in heads]
        eg = [jnp.exp(gcc[h]) for h in heads]
        uw = [_mm(x[h].astype(bf16), jnp.concatenate([vc[h] * bc[h], kb[h] * eg[h]], axis=1).astype(bf16))
              for h in heads]
        S = [s_ref[h] for h in heads]
        Sb = [S[h].astype(bf16) for h in heads]
        wq = [_mm(jnp.concatenate([uw[h][:, LANES:], qc[h] * eg[h]], axis=0).astype(bf16), Sb[h])
              for h in heads]
        vnb = [(uw[h][:, :LANES] - wq[h][0:C]).astype(bf16) for h in heads]
        av = [_mm(a[h].astype(bf16), vnb[h]) for h in heads]
        glast = [gcc[h][C - 1:C, :] for h in heads]
        kv_new = [_tn((kc[h] * jnp.exp(glast[h] - gcc[h])).astype(bf16), vnb[h]) for h in heads]
        for h in heads:
            s_ref[h] = S[h] * jnp.exp(glast[h]) + kv_new[h]
            o_ref[rows, lanes_of[h]] = _rms(wq[h][C:2 * C] + av[h], gout) * _silu(z_ref[rows, lanes_of[h]])
        return carry

    if TL == C:
        chunk(0, 0)
    else:
        lax.fori_loop(0, TL // C, chunk, 0)

    @pl.when(t == pl.num_programs(1) - 1)
    def _():
        sout_ref[...] = s_ref[...]


def delta_mixer(P, row0, nb, L, TL, C, valid, buf8, conv8, ab8, gout, s0, layer):
    nt = L // TL
    rb0 = row0 // TL
    tile = lambda c: pl.BlockSpec((TL, A_W), lambda b, t: (rb0 + b * nt + t, c))
    state = pl.BlockSpec((None, A_HEADS, LANES, LANES), lambda b, t: (b, 0, 0, 0))
    state_in = pl.BlockSpec((None, None, A_HEADS, LANES, LANES), lambda b, t: (layer, b, 0, 0, 0))
    return pl.pallas_call(
        functools.partial(_delta_kernel, TL=TL, C=C, valid=valid),
        grid=(nb, nt),
        in_specs=[tile(0), tile(1), tile(2), tile(3),
                  pl.BlockSpec((TL, LANES), lambda b, t: (rb0 + b * nt + t, A_BA_BLK)),
                  pl.BlockSpec((None, SUBLANES, CONV_DIM), lambda b, t: (b, 0, 0)),
                  pl.BlockSpec((SUBLANES, CONV_DIM), lambda b, t: (0, 0)),
                  pl.BlockSpec((SUBLANES, LANES), lambda b, t: (0, 0)),
                  pl.BlockSpec((1, LANES), lambda b, t: (0, 0)),
                  state_in],
        out_specs=(pl.BlockSpec((TL, A_W), lambda b, t: (b * nt + t, 0)), state),
        out_shape=(jax.ShapeDtypeStruct((nb * L, A_W), f32),
                   jax.ShapeDtypeStruct((nb, A_HEADS, LANES, LANES), f32)),
        scratch_shapes=[pltpu.VMEM((TL, A_W), f32)] * 5
        + [pltpu.VMEM((SUBLANES, CONV_DIM), f32), pltpu.VMEM((A_HEADS, LANES, LANES), f32)],
        compiler_params=_cparams(("parallel", "arbitrary")),
        name="delta_mixer",
    )(P, P, P, P, P, buf8, conv8, ab8, gout, s0)


def _attn_residual(refs, ntp):
    h_ref, mop_ref, mos_ref, omp_ref, oms_ref, wo1_ref, wo2_ref = refs
    is_p = pl.program_id(0) < ntp
    mo = jnp.where(is_p, mop_ref[...], mos_ref[...])
    om = jnp.where(is_p, omp_ref[...], oms_ref[...])
    return h_ref[...] + _mm(mo.astype(bf16), wo1_ref[...]) + _mm(om.astype(bf16), wo2_ref[...])


def _ffn_kernel(*refs, ntp):
    g_ref, wg_ref, wu_ref, wd_ref, o_ref, xn_ref, acc_ref = refs[7:]
    k = pl.program_id(1)

    @pl.when(k == 0)
    def _():
        h1 = _attn_residual(refs[:7], ntp)
        acc_ref[...] = h1
        xn_ref[...] = _rms(h1, g_ref[...]).astype(bf16)

    xn = xn_ref[...]
    a = _silu(_mm(xn, wg_ref[...])) * _mm(xn, wu_ref[...])
    acc_ref[...] += _mm(a.astype(bf16), wd_ref[...])

    @pl.when(k == pl.num_programs(1) - 1)
    def _():
        o_ref[...] = acc_ref[...]


def _outproj_specs(tm, K, mw, ow, ntp, layer):
    prow = lambda i, *_: (jnp.minimum(i, ntp - 1), 0)
    srow = lambda i, *_: (jnp.maximum(i - ntp, 0), 0)
    return [pl.BlockSpec((tm, K), lambda i, *_: (i, 0)),
            pl.BlockSpec((tm, mw), prow), pl.BlockSpec((tm, mw), srow),
            pl.BlockSpec((tm, ow), prow), pl.BlockSpec((tm, ow), srow),
            pl.BlockSpec((None, mw, K), lambda i, *_: (layer, 0, 0)),
            pl.BlockSpec((None, ow, K), lambda i, *_: (layer, mw // ow, 0)),
            pl.BlockSpec((1, K), lambda i, *_: (0, 0))]


def outproj_ffn(H, mo, om, w_out, g, w_gu, w_down, layer, ffn_layer, tm, tf):
    T, K = H.shape
    FF = w_down.shape[1]
    nk = FF // tf
    mw, ow = mo[0].shape[1], om[0].shape[1]
    ntp = mo[0].shape[0] // tm
    return pl.pallas_call(
        functools.partial(_ffn_kernel, ntp=ntp),
        grid=(T // tm, nk),
        in_specs=_outproj_specs(tm, K, mw, ow, ntp, layer)
        + [pl.BlockSpec((None, K, tf), lambda i, k: (ffn_layer, 0, k)),
           pl.BlockSpec((None, K, tf), lambda i, k: (ffn_layer, 0, nk + k)),
           pl.BlockSpec((None, tf, K), lambda i, k: (ffn_layer, k, 0))],
        out_specs=pl.BlockSpec((tm, K), lambda i, k: (i, 0)),
        out_shape=jax.ShapeDtypeStruct((T, K), f32),
        scratch_shapes=[pltpu.VMEM((tm, K), bf16), pltpu.VMEM((tm, K), f32)],
        compiler_params=_cparams(("parallel", "arbitrary")),
        name="outproj_ffn",
    )(H, mo[0], mo[1], om[0], om[1], w_out, w_out, g, w_gu, w_gu, w_down)


def _router_kernel(*refs, ntp):
    g_ref, wr_ref, h1_ref, xn_ref, r_ref, cnt_ref = refs[7:]
    i = pl.program_id(0)
    tm = h1_ref.shape[0]

    @pl.when(i == 0)
    def _():
        cnt_ref[...] = jnp.zeros_like(cnt_ref)

    h1 = _attn_residual(refs[:7], ntp)
    h1_ref[...] = h1
    xn = _rms(h1, g_ref[...])
    xn_ref[...] = xn.astype(bf16)
    lane = lax.broadcasted_iota(jnp.int32, (tm, LANES), 1)
    xh = xn.astype(bf16)
    xl = (xn - xh.astype(f32)).astype(bf16)
    wr = wr_ref[...]
    wh = wr.astype(bf16)
    wl = (wr - wh.astype(f32)).astype(bf16)
    logits = _mm(xh, wh) + _mm(xh, wl) + _mm(xl, wh)
    logits = jnp.where(lane < N_EXPERTS, logits, -jnp.inf)
    m1 = jnp.max(logits, axis=-1, keepdims=True)
    i1 = jnp.min(jnp.where(logits == m1, lane, LANES), axis=-1, keepdims=True)
    rest = jnp.where(lane == i1, -jnp.inf, logits)
    m2 = jnp.max(rest, axis=-1, keepdims=True)
    i2 = jnp.min(jnp.where(rest == m2, lane, LANES), axis=-1, keepdims=True)
    e2 = jnp.exp(m2 - m1)
    w1 = 1.0 / (1.0 + e2)
    w2 = e2 / (1.0 + e2)
    hot = jnp.where((lane == i1) | (lane == i2), 1.0, 0.0)
    ri = lax.broadcasted_iota(jnp.int32, (tm, tm), 0)
    ci = lax.broadcasted_iota(jnp.int32, (tm, tm), 1)
    before = jnp.where(ri > ci, 1.0, 0.0).astype(bf16)
    cum = _mm(before, hot.astype(bf16)) + cnt_ref[...]
    r1 = jnp.sum(jnp.where(lane == i1, cum, 0.0), axis=-1, keepdims=True)
    r2 = jnp.sum(jnp.where(lane == i2, cum, 0.0), axis=-1, keepdims=True)
    cnt_ref[...] += jnp.sum(hot, axis=0, keepdims=True)
    cols = (i1.astype(f32), i2.astype(f32), w1, w2, r1, r2)
    r = jnp.zeros((tm, LANES), f32)
    for c, val in enumerate(cols):
        r = jnp.where(lane == c, val, r)
    r_ref[...] = r


def outproj_router(H, mo, om, w_out, g, w_router, layer, moe_layer, tm):
    T, K = H.shape
    mw, ow = mo[0].shape[1], om[0].shape[1]
    ntp = mo[0].shape[0] // tm
    return pl.pallas_call(
        functools.partial(_router_kernel, ntp=ntp),
        grid=(T // tm,),
        in_specs=_outproj_specs(tm, K, mw, ow, ntp, layer)
        + [pl.BlockSpec((None, K, LANES), lambda i: (moe_layer, 0, 0))],
        out_specs=(pl.BlockSpec((tm, K), lambda i: (i, 0)), pl.BlockSpec((tm, K), lambda i: (i, 0)),
                   pl.BlockSpec((tm, LANES), lambda i: (i, 0))),
        out_shape=(jax.ShapeDtypeStruct((T, K), f32), jax.ShapeDtypeStruct((T, K), bf16),
                   jax.ShapeDtypeStruct((T, LANES), f32)),
        scratch_shapes=[pltpu.VMEM((1, LANES), f32)],
        compiler_params=_cparams(("arbitrary",)),
        name="outproj_router",
    )(H, mo[0], mo[1], om[0], om[1], w_out, w_out, g, w_router)


def _moe_group_kernel(te_ref, nu_ref, x_ref, sw_ref, wg_ref, wu_ref, wd_ref, o_ref, acc_ref):
    del te_ref
    i = pl.program_id(0)
    k = pl.program_id(1)

    @pl.when(k == 0)
    def _():
        acc_ref[...] = jnp.zeros_like(acc_ref)

    @pl.when(i < nu_ref[0])
    def _():
        x = x_ref[...]
        a = _silu(_mm(x, wg_ref[...])) * _mm(x, wu_ref[...])
        acc_ref[...] += _mm(a.astype(bf16), wd_ref[...])

    @pl.when(k == pl.num_programs(1) - 1)
    def _():
        o_ref[...] = acc_ref[...] * sw_ref[...]


def moe_grouped(xg, sw, tile_e, n_used, w_gu, w_down, moe_layer, tm, tf):
    NP, K = xg.shape
    FF = w_down.shape[2]
    nk = FF // tf

    def kk(i, k, nu):
        return jnp.where(i < nu[0], k, nk - 1)

    grid_spec = pltpu.PrefetchScalarGridSpec(
        num_scalar_prefetch=2, grid=(NP // tm, nk),
        in_specs=[pl.BlockSpec((tm, K), lambda i, k, te, nu: (i, 0)),
                  pl.BlockSpec((tm, 1), lambda i, k, te, nu: (i, 0)),
                  pl.BlockSpec((None, None, K, tf), lambda i, k, te, nu: (moe_layer, te[i], 0, kk(i, k, nu))),
                  pl.BlockSpec((None, None, K, tf), lambda i, k, te, nu: (moe_layer, te[i], 0, nk + kk(i, k, nu))),
                  pl.BlockSpec((None, None, tf, K), lambda i, k, te, nu: (moe_layer, te[i], kk(i, k, nu), 0))],
        out_specs=pl.BlockSpec((tm, K), lambda i, k, te, nu: (i, 0)),
        scratch_shapes=[pltpu.VMEM((tm, K), f32)])
    return pl.pallas_call(
        _moe_group_kernel, grid_spec=grid_spec,
        out_shape=jax.ShapeDtypeStruct((NP, K), f32),
        compiler_params=_cparams(("parallel", "arbitrary")),
        name="moe_grouped",
    )(tile_e, n_used, xg, sw, w_gu, w_gu, w_down)


def _dispatch(r, tm):
    T = r.shape[0]
    A = 2 * T
    ntiles = A // tm + N_EXPERTS
    NP = ntiles * tm
    e_flat = r[:, 0:2].astype(jnp.int32).reshape(A)
    w_flat = r[:, 2:4].reshape(A)
    rank = r[:, 4:6].astype(jnp.int32)
    counts = jnp.sum((e_flat[:, None] == jnp.arange(N_EXPERTS)[None, :]).astype(jnp.int32), axis=0)
    pcounts = (counts + tm - 1) // tm * tm
    pends = jnp.cumsum(pcounts)
    pstarts = pends - pcounts
    starts = jnp.cumsum(counts) - counts
    n_used = pends[-1] // tm
    a_sorted = jnp.sort(e_flat * A + jnp.arange(A, dtype=jnp.int32)) % A
    tiles = jnp.arange(ntiles, dtype=jnp.int32)
    tile_e = jnp.minimum(jnp.searchsorted(pends, tiles * tm, side='right'), N_EXPERTS - 1).astype(jnp.int32)
    tile_e = jnp.where(tiles < n_used, tile_e, tile_e[jnp.maximum(n_used - 1, 0)])
    slot = jnp.arange(NP, dtype=jnp.int32)
    e_s = tile_e[slot // tm]
    rr = slot - pstarts[e_s]
    ok = (rr < counts[e_s]) & (slot < pends[-1])
    a_s = a_sorted[jnp.clip(starts[e_s] + rr, 0, A - 1)]
    src = a_s // 2
    sw = jnp.where(ok, w_flat[a_s], 0.0).reshape(NP, 1)
    slots = pstarts[r[:, 0:2].astype(jnp.int32)] + rank
    return src, sw, tile_e, n_used.reshape(1).astype(jnp.int32), slots


def outproj_moe(H, mo, om, w_out, g, w_router, w_gu, w_down, layer, moe_layer, tm, tf):
    h1, xn, r = outproj_router(H, mo, om, w_out, g, w_router, layer, moe_layer, tm)
    src, sw, tile_e, n_used, slots = _dispatch(r, tm)
    yg = moe_grouped(jnp.take(xn, src, axis=0), sw, tile_e, n_used, w_gu, w_down, moe_layer, tm, tf)
    return h1 + jnp.take(yg, slots[:, 0], axis=0) + jnp.take(yg, slots[:, 1], axis=0)


def _shared_kv_kernel(x_ref, g_ref, w_ref, cos_ref, sin_ref, ck_ref, cv_ref, sk_ref, sv_ref, wk_ref, wv_ref):
    y = _mm(_rms(x_ref[...], g_ref[...]).astype(bf16), w_ref[...])
    cos = cos_ref[...]
    sin = sin_ref[...]
    ck_ref[...] = y[:, 0 * LANES:1 * LANES]
    cv_ref[...] = y[:, 1 * LANES:2 * LANES]
    sk_ref[...] = _rope_lanes(y[:, 2 * LANES:3 * LANES], cos, sin)
    sv_ref[...] = y[:, 3 * LANES:4 * LANES]
    wk_ref[...] = _rope_lanes(y[:, 4 * LANES:5 * LANES], cos, sin)
    wv_ref[...] = y[:, 5 * LANES:6 * LANES]


def shared_kv(H, g, w, cos, sin, tm):
    T, K = H.shape
    row = pl.BlockSpec((tm, LANES), lambda i: (i, 0))
    out = jax.ShapeDtypeStruct((T, LANES), f32)
    return pl.pallas_call(
        _shared_kv_kernel,
        grid=(T // tm,),
        in_specs=[pl.BlockSpec((tm, K), lambda i: (i, 0)), pl.BlockSpec((1, K), lambda i: (0, 0)),
                  pl.BlockSpec((K, 6 * LANES), lambda i: (0, 0)), row, row],
        out_specs=(row,) * 6,
        out_shape=(out,) * 6,
        compiler_params=_cparams(("parallel",)),
        name="shared_kv",
    )(H, g, w, cos, sin)


def _compress_kernel(*refs, npages):
    nin = 2 * npages
    k_strips = jnp.concatenate([r[...] for r in refs[0:npages]], axis=0) if npages > 1 else refs[0][...]
    v_strips = jnp.concatenate([r[...] for r in refs[npages:nin]], axis=0) if npages > 1 else refs[npages][...]
    (pek_ref, pev_ref, w1k_ref, w1v_ref, w2k_ref, w2v_ref, cos_ref, sin_ref, ck_ref, cv_ref) = refs[nin:]
    n = k_strips.shape[0]
    half = 2 * CMP_HID
    row = lax.broadcasted_iota(jnp.int32, (n, LANES), 0)

    def tokens(strips, pe_ref, w1_ref, w2_ref):
        top = _mm((strips + pe_ref[0:1, :]).astype(bf16), w1_ref[:, 0:half])
        bot = _mm((strips + pe_ref[1:2, :]).astype(bf16), w1_ref[:, half:2 * half])
        hid = top + pltpu.roll(bot, n - 1, axis=0)
        out = _mm(_silu(hid).astype(bf16), w2_ref[...])
        return jnp.where(row < n - 1, out, 0.0)

    ck_ref[...] = _rope_lanes(tokens(k_strips, pek_ref, w1k_ref, w2k_ref), cos_ref[...], sin_ref[...])
    cv_ref[...] = tokens(v_strips, pev_ref, w1v_ref, w2v_ref)


def _drop_first(fn, _prefetch_ref, *refs):
    return fn(*refs)


def compress(k_src, v_src, k_maps, v_maps, strip_rows, nb, consts, prefetch=None):
    npages = len(k_maps)
    n = npages * strip_rows
    pek, pev, w1k, w1v, w2k, w2v, cos, sin = consts
    flat = 16 * LANES
    const2 = lambda shape: pl.BlockSpec(shape, lambda b, *_: (0, 0))
    in_specs = ([pl.BlockSpec((None, strip_rows, flat), m) for m in k_maps]
                + [pl.BlockSpec((None, strip_rows, flat), m) for m in v_maps]
                + [const2((2, flat)), const2((2, flat)), const2((flat, 4 * CMP_HID)), const2((flat, 4 * CMP_HID)),
                   const2((2 * CMP_HID, LANES)), const2((2 * CMP_HID, LANES)), const2((n, LANES)), const2((n, LANES))])
    out_spec = pl.BlockSpec((None, n, LANES), lambda b, *_: (b, 0, 0))
    out = jax.ShapeDtypeStruct((nb, n, LANES), f32)
    grid_spec = pltpu.PrefetchScalarGridSpec(
        num_scalar_prefetch=0 if prefetch is None else 1, grid=(nb,),
        in_specs=in_specs, out_specs=(out_spec, out_spec))
    args = ([] if prefetch is None else [prefetch]) + [k_src] * npages + [v_src] * npages + [pek, pev, w1k, w1v, w2k, w2v, cos, sin]
    kern = functools.partial(_compress_kernel, npages=npages)
    if prefetch is not None:
        kern = functools.partial(_drop_first, kern)
    return pl.pallas_call(
        kern, grid_spec=grid_spec, out_shape=(out, out),
        compiler_params=_cparams(("parallel",)), name="compress",
    )(*args)


def _nsa_queries(p, cos, sin):
    tq = p.shape[0]
    lane = lax.broadcasted_iota(jnp.int32, (tq, LANES), 1)
    rows = []
    for head in range(B_HEADS):
        grp = head // HPG
        blk = MEM_W // LANES + head // 2
        x = _rope_lanes(p[:, blk * LANES:(blk + 1) * LANES], cos, sin) * (HEAD_DIM ** -0.5)
        if head % 2 != grp:
            x = pltpu.roll(x, HEAD_DIM, axis=1)
        keep = (lane >= grp * HEAD_DIM) & (lane < (grp + 1) * HEAD_DIM)
        rows.append(jnp.where(keep, x, 0.0))
    return jnp.concatenate(rows, axis=0)


def _masked_softmax_parts(s, mask):
    sm = jnp.where(mask, s, NEG)
    m = jnp.max(sm, axis=-1, keepdims=True)
    p = jnp.where(mask, jnp.exp(sm - m), 0.0)
    return p, jnp.sum(p, axis=-1, keepdims=True)


def _safe_div(o, l):
    return jnp.where(l > 0.0, o / jnp.where(l > 0.0, l, 1.0), 0.0)


def _split3(x):
    hi = x.astype(bf16)
    r1 = x - hi.astype(f32)
    mid = r1.astype(bf16)
    lo = (r1 - mid.astype(f32)).astype(bf16)
    return hi, mid, lo


def _select_bias(pcn, qpos_row, past_blocks, mt_ref, eye_ref, nsel):
    tq = qpos_row.shape[1]
    nr = -(-nsel // SUBLANES) * SUBLANES
    mt = mt_ref[0:nr, :]
    blk = lax.broadcasted_iota(jnp.int32, (nr, tq), 0)
    lane = lax.broadcasted_iota(jnp.int32, (tq, LANES), 1)
    cur = qpos_row // SEL_BLK
    forced = (blk == 0) | (blk == cur) | (blk == cur - 1)
    valid = blk * SEL_BLK <= qpos_row
    stacks, picks = [], []
    for grp in range(2):
        psum = pcn[grp * HPG * tq:(grp * HPG + 1) * tq]
        for hh in range(1, HPG):
            psum = psum + pcn[(grp * HPG + hh) * tq:(grp * HPG + hh + 1) * tq]
        hi, mid, lo = _split3(psum)
        imp = _nt(mt, hi) + _nt(mt, mid) + _nt(mt, lo)
        imp = jnp.where(forced, 1e9, jnp.where(valid, imp, -1e9))
        rank = jnp.zeros((nr, tq), f32)
        for j in range(nsel):
            rj = imp[j:j + 1, :]
            rank = rank + jnp.where(rj > imp, 1.0, jnp.where(rj == imp, jnp.where(blk > j, 1.0, 0.0), 0.0))
        sel_t = jnp.where((rank < SEL_TOP) & (blk < nsel), 1.0, 0.0).astype(bf16)
        pick = _tn(sel_t, eye_ref[0:nr, :])
        picks.append(pick)
        stacks += [jnp.where((pick > 0.5) & (lane < past_blocks), 0.0, NEG)] * HPG
    return jnp.concatenate(stacks, axis=0), picks


def _nsa_combine(p, o_c, o_s, o_w):
    tq = p.shape[0]
    gl = _sigmoid(p[:, (MEM_W + B_W):(MEM_W + B_W) + LANES])

    def gate(j):
        return jnp.concatenate([gl[:, 3 * h + j:3 * h + j + 1] for h in range(B_HEADS)], axis=0)

    comb = gate(0) * o_c + gate(1) * o_s + gate(2) * o_w
    lane = lax.broadcasted_iota(jnp.int32, (tq, LANES), 1)
    cols = []
    for c in range(B_HEADS // 2):
        lo = comb[2 * c * tq:(2 * c + 1) * tq]
        hi = comb[(2 * c + 1) * tq:(2 * c + 2) * tq]
        if (2 * c) // HPG == 1:
            lo = pltpu.roll(lo, HEAD_DIM, axis=1)
        if (2 * c + 1) // HPG == 0:
            hi = pltpu.roll(hi, HEAD_DIM, axis=1)
        cols.append(jnp.where(lane < HEAD_DIM, lo, hi))
    return jnp.concatenate(cols, axis=1)


def _nsa_prompt_kernel(p_ref, cos_ref, sin_ref, ck_ref, cv_ref, sk_ref, sv_ref, wk_ref, wv_ref, mt_ref, et_ref,
                       eye_ref, o_ref, *, tq, kc, nsel):
    qb = pl.program_id(1)
    s0 = qb * tq
    R = B_HEADS * tq
    p = p_ref[...]
    qf = _nsa_queries(p, cos_ref[...], sin_ref[...])
    qb16 = qf.astype(bf16)
    ii = lax.broadcasted_iota(jnp.int32, (tq, tq), 0)
    jj = lax.broadcasted_iota(jnp.int32, (tq, tq), 1)
    qposR = s0 + jnp.concatenate([lax.broadcasted_iota(jnp.int32, (tq, 1), 0)] * B_HEADS, axis=0)
    lane = lax.broadcasted_iota(jnp.int32, (R, LANES), 1)
    cmask = ((lane * CMP_STRIDE + (CMP_BLK - 1)) <= qposR) & (lane < ck_ref.shape[0] - 1)
    pc, lc = _masked_softmax_parts(_nt(qb16, ck_ref[...].astype(bf16)), cmask)
    pcn = _safe_div(pc, lc)
    o_c = _mm(pcn.astype(bf16), cv_ref[...].astype(bf16))
    qpos_row = s0 + lax.broadcasted_iota(jnp.int32, (1, tq), 1)
    bias, _ = _select_bias(pcn, qpos_row, s0 // SEL_BLK, mt_ref, eye_ref, nsel)
    qsel = jnp.concatenate([qb16, bias.astype(bf16)], axis=1)
    sd = _nt(qb16, sk_ref[pl.ds(s0, tq), :].astype(bf16)).reshape(B_HEADS, tq, tq)
    sd = jnp.where((ii >= jj)[None], sd, NEG).reshape(R, tq)
    m0 = jnp.max(sd, axis=-1, keepdims=True)
    pd = jnp.exp(sd - m0)
    init = (m0, jnp.sum(pd, axis=-1, keepdims=True), _mm(pd.astype(bf16), sv_ref[pl.ds(s0, tq), :].astype(bf16)))

    def body(c, carry):
        m_run, l_run, acc = carry
        k0 = pl.multiple_of(c * kc, kc)
        ka = jnp.concatenate([sk_ref[pl.ds(k0, kc), :].astype(bf16), et_ref[pl.ds(k0, kc), :]], axis=1)
        s = _nt(qsel, ka)
        m_new = jnp.maximum(m_run, jnp.max(s, axis=-1, keepdims=True))
        alpha = jnp.exp(m_run - m_new)
        pe = jnp.exp(s - m_new)
        l_new = alpha * l_run + jnp.sum(pe, axis=-1, keepdims=True)
        acc_new = alpha * acc + _mm(pe.astype(bf16), sv_ref[pl.ds(k0, kc), :].astype(bf16))
        return m_new, l_new, acc_new

    _, l_run, acc = lax.fori_loop(0, (s0 + kc - 1) // kc, body, init)
    o_s = acc / l_run
    wlen = WINDOW + tq
    wrow = lax.broadcasted_iota(jnp.int32, (wlen, LANES), 0)
    wlane = lax.broadcasted_iota(jnp.int32, (wlen, LANES), 1)
    is_pad = jnp.where((wrow + (s0 - WINDOW) < 0) & (wlane == 0), 1.0, 0.0).astype(bf16)
    kw = jnp.concatenate([wk_ref[pl.ds(s0, wlen), :].astype(bf16), is_pad], axis=1)
    qwin = jnp.concatenate([qb16, jnp.where(lane == 0, NEG, 0.0).astype(bf16)], axis=1)
    sw = _nt(qwin, kw).reshape(B_HEADS, tq, wlen)
    far = jnp.where((jj > ii)[None], sw[:, :, 0:tq], NEG)
    near = jnp.where((jj <= ii)[None], sw[:, :, wlen - tq:wlen], NEG)
    sw = jnp.concatenate([far, sw[:, :, tq:wlen - tq], near], axis=2).reshape(R, wlen)
    pw = jnp.exp(sw - jnp.max(sw, axis=-1, keepdims=True))
    o_w = _mm(pw.astype(bf16), wv_ref[pl.ds(s0, wlen), :].astype(bf16)) / jnp.sum(pw, axis=-1, keepdims=True)
    o_ref[...] = _nsa_combine(p, o_c, o_s, o_w)


def nsa_prompt(P, cos, sin, ck, cv, sk, sv, wk_pad, wv_pad, mt, et, eye, nb, L, tq, kc):
    nq = L // tq
    nsel = L // SEL_BLK
    seq = pl.BlockSpec((L, LANES), lambda b, i: (b, 0))
    win = pl.BlockSpec((None, WINDOW + L, LANES), lambda b, i: (b, 0, 0))
    cmp_spec = pl.BlockSpec((None, LANES, LANES), lambda b, i: (b, 0, 0))
    const = lambda shape: pl.BlockSpec(shape, lambda b, i: (0, 0))
    return pl.pallas_call(
        functools.partial(_nsa_prompt_kernel, tq=tq, kc=kc, nsel=nsel),
        grid=(nb, nq),
        in_specs=[pl.BlockSpec((tq, B_IN_PAD), lambda b, i: (b * nq + i, 0)),
                  pl.BlockSpec((tq, LANES), lambda b, i: (i, 0)),
                  pl.BlockSpec((tq, LANES), lambda b, i: (i, 0)),
                  cmp_spec, cmp_spec, seq, seq, win, win,
                  const((LANES, LANES)), const((L, LANES)), const((LANES, LANES))],
        out_specs=pl.BlockSpec((tq, B_W), lambda b, i: (b * nq + i, 0)),
        out_shape=jax.ShapeDtypeStruct((nb * L, B_W), f32),
        compiler_params=_cparams(("parallel", "parallel")),
        name="nsa_prompt",
    )(P, cos, sin, ck, cv, sk, sv, wk_pad, wv_pad, mt, et, eye)


def _nsa_sample_kernel(*refs, npages, past):
    refs = refs[1:]
    kpages = refs[0:npages]
    vpages = refs[npages:2 * npages]
    (p_ref, cos_ref, sin_ref, ck_ref, cv_ref, nsk_ref, nsv_ref, cwk_ref, cwv_ref, nwk_ref, nwv_ref,
     mt_ref, et_ref, eye_ref, o_ref) = refs[2 * npages:]
    tq = S_PAD
    R = B_HEADS * tq
    p = p_ref[...]
    qb16 = _nsa_queries(p, cos_ref[...], sin_ref[...]).astype(bf16)
    tR = jnp.concatenate([lax.broadcasted_iota(jnp.int32, (tq, 1), 0)] * B_HEADS, axis=0)
    qposR = past + tR
    lane = lax.broadcasted_iota(jnp.int32, (R, LANES), 1)
    cmask = ((lane * CMP_STRIDE + (CMP_BLK - 1)) <= qposR) & (lane < ck_ref.shape[0] - 1)
    pc, lc = _masked_softmax_parts(_nt(qb16, ck_ref[...].astype(bf16)), cmask)
    pcn = _safe_div(pc, lc)
    o_c = _mm(pcn.astype(bf16), cv_ref[...].astype(bf16))
    qpos_row = past + lax.broadcasted_iota(jnp.int32, (1, tq), 1)
    new_blk = past // SEL_BLK
    bias, picks = _select_bias(pcn, qpos_row, new_blk, mt_ref, eye_ref, new_blk + 1)
    new_ok = lax.broadcasted_iota(jnp.int32, (R, tq), 1) <= tR
    ka = jnp.concatenate([jnp.concatenate([r[...] for r in kpages], axis=1).astype(bf16), et_ref[...]], axis=0)
    s_old = _mm(jnp.concatenate([qb16, bias.astype(bf16)], axis=1), ka)
    new_kept = jnp.concatenate([picks[h // HPG][:, new_blk:new_blk + 1] for h in range(B_HEADS)], axis=0) > 0.5
    s_new = jnp.where(new_ok, jnp.where(new_kept, _nt(qb16, nsk_ref[...].astype(bf16)), NEG), NEG)
    m = jnp.maximum(jnp.max(s_old, axis=-1, keepdims=True), jnp.max(s_new, axis=-1, keepdims=True))
    p_old = jnp.exp(s_old - m)
    p_new = jnp.exp(s_new - m)
    l = jnp.sum(p_old, axis=-1, keepdims=True) + jnp.sum(p_new, axis=-1, keepdims=True)
    svb = jnp.concatenate([r[...] for r in vpages], axis=1).astype(bf16)
    o_s = (_nt(p_old.astype(bf16), svb) + _mm(p_new.astype(bf16), nsv_ref[...].astype(bf16))) / l
    wb = cwk_ref.shape[1]
    wdist = qposR - (past - wb + lax.broadcasted_iota(jnp.int32, (R, wb), 1))
    w_old = jnp.where((wdist >= 0) & (wdist < WINDOW), _mm(qb16, cwk_ref[...].astype(bf16)), NEG)
    w_new = jnp.where(new_ok, _nt(qb16, nwk_ref[...].astype(bf16)), NEG)
    mw = jnp.maximum(jnp.max(w_old, axis=-1, keepdims=True), jnp.max(w_new, axis=-1, keepdims=True))
    pw_old = jnp.exp(w_old - mw)
    pw_new = jnp.exp(w_new - mw)
    lw = jnp.sum(pw_old, axis=-1, keepdims=True) + jnp.sum(pw_new, axis=-1, keepdims=True)
    o_w = (_nt(pw_old.astype(bf16), cwv_ref[...].astype(bf16)) + _mm(pw_new.astype(bf16), nwv_ref[...].astype(bf16))) / lw
    o_ref[...] = _nsa_combine(p, o_c, o_s, o_w)


def nsa_sample(P, row0, page_table, pool_k, pool_v, cos, sin, ck, cv, nsk, nsv, cwk, cwv, nwk, nwv, mt, et, eye):
    nb = cwk.shape[0]
    npages = page_table.shape[0] // nb
    past = npages * PAGE
    rb0 = row0 // S_PAD
    page = lambda j: pl.BlockSpec((None, LANES, PAGE), lambda b, pt: (pt[b * npages + j], 0, 0))
    new_rows = pl.BlockSpec((S_PAD, LANES), lambda b, pt: (rb0 + b, 0))
    per_seq = lambda n: pl.BlockSpec((None, LANES, n), lambda b, pt: (b, 0, 0))
    in_specs = ([page(j) for j in range(npages)] + [page(j) for j in range(npages)]
                + [pl.BlockSpec((S_PAD, B_IN_PAD), lambda b, pt: (rb0 + b, 0)),
                   pl.BlockSpec((S_PAD, LANES), lambda b, pt: (0, 0)),
                   pl.BlockSpec((S_PAD, LANES), lambda b, pt: (0, 0)),
                   per_seq(LANES), per_seq(LANES), new_rows, new_rows,
                   per_seq(cwk.shape[2]), per_seq(cwk.shape[2]), new_rows, new_rows,
                   pl.BlockSpec((LANES, LANES), lambda b, pt: (0, 0)),
                   pl.BlockSpec((LANES, past), lambda b, pt: (0, 0)),
                   pl.BlockSpec((LANES, LANES), lambda b, pt: (0, 0))])
    grid_spec = pltpu.PrefetchScalarGridSpec(
        num_scalar_prefetch=1, grid=(nb,), in_specs=in_specs,
        out_specs=pl.BlockSpec((S_PAD, B_W), lambda b, pt: (b, 0)))
    return pl.pallas_call(
        functools.partial(_nsa_sample_kernel, npages=npages, past=past),
        grid_spec=grid_spec,
        out_shape=jax.ShapeDtypeStruct((nb * S_PAD, B_W), f32),
        compiler_params=_cparams(("parallel",)),
        name="nsa_sample",
    )(page_table, *([pool_k] * npages), *([pool_v] * npages), P, cos, sin, ck, cv, nsk, nsv, cwk, cwv, nwk, nwv,
      mt, et, eye)


def _rope_tables(pos):
    half = HEAD_DIM // 2
    inv = ROPE_THETA ** (-jnp.arange(half, dtype=f32) / half)
    ang = pos.astype(f32)[:, None] * inv[None, :]
    cos = jnp.tile(jnp.cos(ang), (1, LANES // half))
    sin = jnp.tile(jnp.sin(ang), (1, LANES // half))
    sign = jnp.where((jnp.arange(LANES) % HEAD_DIM) < half, -1.0, 1.0).astype(f32)
    return cos, sin * sign[None, :]


def _cmp_to_sel(ncmp, nsel):
    per = SEL_BLK // CMP_STRIDE
    sub = CMP_BLK // CMP_STRIDE
    i = jnp.arange(LANES)[:, None]
    j = jnp.arange(LANES)[None, :]
    m = jnp.zeros((LANES, LANES), f32)
    for r in range(sub):
        m = m + (((i + r) // per) == j).astype(f32)
    m = jnp.where((i < ncmp) & (j < nsel), m / sub, 0.0)
    return m.astype(bf16)


def _block_expand(nkeys):
    j = jnp.arange(LANES)[:, None]
    k = jnp.arange(nkeys)[None, :]
    return ((k // SEL_BLK) == j).astype(bf16)


def _compress_consts(pe, w1, w2):
    w1r = w1.reshape(CMP_BLK, HEAD_DIM, CMP_HID)
    eye2 = jnp.eye(2, dtype=f32)

    def expand(w):
        return jnp.einsum('ldc,gh->lgdhc', w, eye2).reshape(16 * LANES, 2 * CMP_HID)

    w1s = jnp.concatenate([expand(w1r[:16]), expand(w1r[16:])], axis=1).astype(bf16)
    pes = jnp.stack([jnp.broadcast_to(pe[:16, None, :], (16, 2, HEAD_DIM)).reshape(-1),
                     jnp.broadcast_to(pe[16:, None, :], (16, 2, HEAD_DIM)).reshape(-1)])
    w2s = jnp.einsum('cd,gh->gchd', w2, eye2).reshape(2 * CMP_HID, LANES).astype(bf16)
    return pes, w1s, w2s


def kernel(x_prompt, x_sample, mem_prompt, state_conv, state_ssm, cache_cmp_k, cache_cmp_v, cache_sel_k, cache_sel_v, cache_win_k, cache_win_v, cache_mem_k, cache_mem_v, page_table, norm_mix, norm_ffn, norm_mem, w_mem_kv, w_in_a, conv_w_a, a_log, dt_bias, norm_out_a, w_out_a, w_in_b, w_out_b, norm_kv, w_kv_shared, cmp_pe_k, cmp_w1_k, cmp_w2_k, cmp_pe_v, cmp_w1_v, cmp_w2_v, w_gu_dense, w_down_dense, w_router, w_gu_exp, w_down_exp, norm_final):
    bp, lp, d = x_prompt.shape
    bs, ls, _ = x_sample.shape
    depth = norm_mix.shape[0]
    n_a = w_in_a.shape[0]
    past = page_table.shape[1] * PAGE
    tp = bp * lp
    T = tp + bs * S_PAD
    tm = 512

    xs = jnp.pad(x_sample, ((0, 0), (0, S_PAD - ls), (0, 0)))
    H = jnp.concatenate([x_prompt.reshape(tp, d), xs.reshape(bs * S_PAD, d)], axis=0)

    qkvz = MEM_W + CONV_DIM + A_W
    w_in_a_b = jnp.concatenate(
        [w_in_a[:, :, MEM_W:qkvz], w_in_a[:, :, :MEM_W], w_in_a[:, :, qkvz:],
         jnp.zeros(w_in_a.shape[:2] + (A_IN_PAD - w_in_a.shape[2],), f32)], axis=2).astype(bf16)
    w_in_b_b = jnp.pad(w_in_b, ((0, 0), (0, 0), (0, B_IN_PAD - w_in_b.shape[2]))).astype(bf16)
    w_out_a_b = w_out_a.astype(bf16)
    w_out_b_b = w_out_b.astype(bf16)
    w_gu_dense_b = w_gu_dense.astype(bf16)
    w_down_dense_b = w_down_dense.astype(bf16)
    w_gu_exp_b = w_gu_exp.astype(bf16)
    w_down_exp_b = w_down_exp.astype(bf16)
    w_router_p = jnp.pad(w_router, ((0, 0), (0, 0), (0, LANES - N_EXPERTS)))

    mem_k, mem_v = mem_kv(mem_prompt.reshape(bp * MEM_LEN, d), norm_mem.reshape(depth, 1, d),
                          w_mem_kv.astype(bf16), tm)
    cmk = jnp.transpose(cache_mem_k, (0, 1, 3, 4, 2)).reshape(depth, bs, MEM_W, MEM_LEN)
    cmv = jnp.transpose(cache_mem_v, (0, 1, 3, 4, 2)).reshape(depth, bs, MEM_W, MEM_LEN)
    zero_buf = jnp.zeros((bp, SUBLANES, CONV_DIM), f32)
    zero_state = jnp.zeros((1, bp, A_HEADS, LANES, LANES), f32)

    pos_p = jnp.arange(lp, dtype=jnp.int32)
    pos_s = past + jnp.arange(S_PAD, dtype=jnp.int32)
    cos_p, sin_p = _rope_tables(pos_p)
    cos_s, sin_s = _rope_tables(pos_s)

    conv_p, ssm_p, conv_s, ssm_s = [], [], [], []
    kv = None
    for layer in range(depth):
        is_a = layer < n_a
        g_mix = norm_mix[layer].reshape(1, d)
        if is_a:
            P = norm_matmul(H, g_mix, w_in_a_b, layer, tm, A_IN_PAD // 3)
            qblk = A_MEMQ_BLK
        else:
            lb = layer - n_a
            qblk = 0
            if lb == 0:
                cos_all = jnp.concatenate([jnp.tile(cos_p, (bp, 1)), jnp.tile(cos_s, (bs, 1))], axis=0)
                sin_all = jnp.concatenate([jnp.tile(sin_p, (bp, 1)), jnp.tile(sin_s, (bs, 1))], axis=0)
                kv = shared_kv(H, norm_kv.reshape(1, d), w_kv_shared.astype(bf16), cos_all, sin_all, tm)
                ncmp = (lp - CMP_BLK) // CMP_STRIDE + 1
                cpos = jnp.arange(LANES, dtype=jnp.int32) * CMP_STRIDE + CMP_BLK - 1
                cos_c, sin_c = _rope_tables(cpos)
                pek, w1k, w2k = _compress_consts(cmp_pe_k, cmp_w1_k, cmp_w2_k)
                pev, w1v, w2v = _compress_consts(cmp_pe_v, cmp_w1_v, cmp_w2_v)
                consts = (pek, pev, w1k, w1v, w2k, w2v, cos_c, sin_c)
                strips = lp // 16
                ck_p, cv_p = compress(kv[0][:tp].reshape(bp, strips, 16 * LANES),
                                      kv[1][:tp].reshape(bp, strips, 16 * LANES),
                                      [lambda b: (b, 0, 0)], [lambda b: (b, 0, 0)], strips, bp, consts)
                npages = page_table.shape[1]
                pt_flat = page_table.reshape(-1)
                pool_strips = PAGE // 16
                pmap = lambda j: (lambda b, pt: (pt[b * npages + j], 0, 0))
                ck_s, cv_s = compress(cache_cmp_k.reshape(-1, pool_strips, 16 * LANES),
                                      cache_cmp_v.reshape(-1, pool_strips, 16 * LANES),
                                      [pmap(j) for j in range(npages)], [pmap(j) for j in range(npages)],
                                      pool_strips, bs, consts, prefetch=pt_flat)
                mt_p = _cmp_to_sel(ncmp, lp // SEL_BLK).T
                et_p = _block_expand(lp).T
                ncmp_s = (past + ls - CMP_BLK) // CMP_STRIDE + 1
                mt_s = _cmp_to_sel(ncmp_s, -(-(past + ls) // SEL_BLK)).T
                e_s = _block_expand(past)
                eye = jnp.eye(LANES, dtype=bf16)
                wk_pad = jnp.pad(kv[4][:tp].reshape(bp, lp, LANES), ((0, 0), (WINDOW, 0), (0, 0)))
                wv_pad = jnp.pad(kv[5][:tp].reshape(bp, lp, LANES), ((0, 0), (WINDOW, 0), (0, 0)))
                pool_sk = jnp.transpose(cache_sel_k, (0, 2, 3, 1)).reshape(-1, LANES, PAGE)
                pool_sv = jnp.transpose(cache_sel_v, (0, 2, 3, 1)).reshape(-1, LANES, PAGE)
                cwk = jnp.transpose(cache_win_k, (0, 2, 3, 1)).reshape(bs, LANES, -1)
                cwv = jnp.transpose(cache_win_v, (0, 2, 3, 1)).reshape(bs, LANES, -1)
            P = norm_matmul(H, g_mix, w_in_b_b, lb, tm, B_IN_PAD)
        om = (mem_attn_prompt(P, qblk, mem_k, mem_v, layer, bp, lp, 512),
              mem_attn_sample(P, qblk, tp, cmk, cmv, layer, 8))
        if is_a:
            conv8 = jnp.pad(conv_w_a[layer], ((0, SUBLANES - conv_w_a.shape[1]), (0, 0)))
            ab8 = jnp.zeros((SUBLANES, LANES), f32)
            ab8 = ab8.at[0, A_HEADS:2 * A_HEADS].set(a_log[layer]).at[1, A_HEADS:2 * A_HEADS].set(dt_bias[layer])
            gout = norm_out_a[layer].reshape(1, LANES)
            mo_p, sp = delta_mixer(P, 0, bp, lp, 512, 64, 512, zero_buf, conv8, ab8, gout, zero_state, 0)
            buf_s = jnp.pad(state_conv[layer], ((0, 0), (SUBLANES - state_conv.shape[2], 0), (0, 0)))
            mo_s, ss = delta_mixer(P, tp, bs, S_PAD, S_PAD, S_PAD, ls, buf_s, conv8, ab8, gout, state_ssm, layer)
            conv_p.append(jnp.stack([P[b * lp + lp - 3:(b + 1) * lp, :CONV_DIM] for b in range(bp)]))
            conv_s.append(P[tp:, :CONV_DIM].reshape(bs, S_PAD, CONV_DIM)[:, ls - 3:ls])
            ssm_p.append(sp)
            ssm_s.append(ss)
            w_out, wl = w_out_a_b, layer
        else:
            mo_p = nsa_prompt(P, cos_p, sin_p, ck_p, cv_p, kv[2], kv[3], wk_pad, wv_pad, mt_p, et_p, eye,
                              bp, lp, Q_BLOCK, 512)
            mo_s = nsa_sample(P, tp, pt_flat, pool_sk, pool_sv, cos_s, sin_s, ck_s, cv_s, kv[2], kv[3], cwk, cwv,
                              kv[4], kv[5], mt_s, e_s, eye)
            w_out, wl = w_out_b_b, lb
        mo = (mo_p, mo_s)
        g_ffn = norm_ffn[layer].reshape(1, d)
        if layer % 2 == 0:
            H = outproj_ffn(H, mo, om, w_out, g_ffn, w_gu_dense_b, w_down_dense_b, wl, layer // 2, tm, D_FF // 2)
        else:
            H = outproj_moe(H, mo, om, w_out, g_ffn, w_router_p, w_gu_exp_b, w_down_exp_b, wl, layer // 2,
                            tm, D_FF_EXPERT // 2)
    Y = final_norm(H, norm_final.reshape(1, d), tm)

    def rows_p(a):
        return a[:tp].reshape(bp, lp, 2, HEAD_DIM)

    def rows_s(a):
        return a[tp:].reshape(bs, S_PAD, 2, HEAD_DIM)[:, :ls]

    wlen = min(WINDOW, lp)
    return (Y[:tp].reshape(bp, lp, d), Y[tp:].reshape(bs, S_PAD, d)[:, :ls],
            jnp.stack(conv_p), jnp.stack(ssm_p),
            rows_p(kv[0]), rows_p(kv[1]), rows_p(kv[2]), rows_p(kv[3]),
            rows_p(kv[4])[:, lp - wlen:], rows_p(kv[5])[:, lp - wlen:],
            mem_k.reshape(depth, bp, MEM_LEN, 4, HEAD_DIM), mem_v.reshape(depth, bp, MEM_LEN, 4, HEAD_DIM),
            jnp.stack(conv_s), jnp.stack(ssm_s),
            rows_s(kv[0]), rows_s(kv[1]), rows_s(kv[2]), rows_s(kv[3]),
            jnp.concatenate([cache_win_k[:, ls:], rows_s(kv[4])], axis=1),
            jnp.concatenate([cache_win_v[:, ls:], rows_s(kv[5])], axis=1))
```

```python
import functools
import math

import jax
import jax.numpy as jnp
from jax import lax
from jax.experimental import pallas as pl
from jax.experimental.pallas import tpu as pltpu

f32 = jnp.float32
bf16 = jnp.bfloat16

EPS = 1e-6
NEG = -1e30
ROPE_THETA = 10000.0
HEAD_DIM = 64
LANES = 128
SUBLANES = 8
VMEM_LIMIT = 48 * 1024 * 1024

D_MODEL = 1024
MEM_LEN = 256
MEM_W = 256
A_HEADS = 6
A_W = 768
CONV_DIM = 2304
A_IN_PAD = 3456
A_MEMQ_BLK = 12
A_BA_BLK = 26
B_HEADS = 12
HPG = 6
B_W = 768
B_IN_PAD = 1152
CMP_BLK = 32
CMP_STRIDE = 16
CMP_HID = 256
SEL_BLK = 64
SEL_TOP = 16
WINDOW = 512
Q_BLOCK = 128
D_FF = 2816
N_EXPERTS = 8
D_FF_EXPERT = 3584
PAGE = 128
S_PAD = 8


def _cparams(sem):
    return pltpu.CompilerParams(dimension_semantics=sem, vmem_limit_bytes=VMEM_LIMIT)


def _nt(a, b):
    return lax.dot_general(a, b, (((1,), (1,)), ((), ())), preferred_element_type=f32)


def _tn(a, b):
    return lax.dot_general(a, b, (((0,), (0,)), ((), ())), preferred_element_type=f32)


def _mm(a, b):
    return jnp.dot(a, b, preferred_element_type=f32)


def _rms(x, g):
    return (x * lax.rsqrt(jnp.mean(x * x, axis=-1, keepdims=True) + EPS)) * g


def _sigmoid(x):
    return 1.0 / (1.0 + jnp.exp(-x))


def _silu(x):
    return x * _sigmoid(x)


def _softplus(x):
    return jnp.maximum(x, 0.0) + jnp.log(1.0 + jnp.exp(-jnp.abs(x)))


def _rope_lanes(x, cos, sin_signed):
    lane = lax.broadcasted_iota(jnp.int32, x.shape, x.ndim - 1)
    lo = (lane % HEAD_DIM) < (HEAD_DIM // 2)
    partner = jnp.where(lo, pltpu.roll(x, LANES - HEAD_DIM // 2, axis=x.ndim - 1),
                        pltpu.roll(x, HEAD_DIM // 2, axis=x.ndim - 1))
    return x * cos + partner * sin_signed


def _norm_matmul_kernel(x_ref, g_ref, w_ref, o_ref, xn_ref):
    @pl.when(pl.program_id(1) == 0)
    def _():
        xn_ref[...] = _rms(x_ref[...], g_ref[...]).astype(bf16)

    o_ref[...] = _mm(xn_ref[...], w_ref[...])


def norm_matmul(x, g, w, layer, tm, tn):
    T, K = x.shape
    N = w.shape[2]
    return pl.pallas_call(
        _norm_matmul_kernel,
        grid=(T // tm, N // tn),
        in_specs=[pl.BlockSpec((tm, K), lambda i, j: (i, 0)),
                  pl.BlockSpec((1, K), lambda i, j: (0, 0)),
                  pl.BlockSpec((None, K, tn), lambda i, j: (layer, 0, j))],
        out_specs=pl.BlockSpec((tm, tn), lambda i, j: (i, j)),
        out_shape=jax.ShapeDtypeStruct((T, N), f32),
        scratch_shapes=[pltpu.VMEM((tm, K), bf16)],
        compiler_params=_cparams(("parallel", "arbitrary")),
        name="norm_matmul",
    )(x, g, w)


def _mem_kv_kernel(x_ref, g_ref, w_ref, k_ref, v_ref):
    y = _mm(_rms(x_ref[...], g_ref[...]).astype(bf16), w_ref[...])
    k_ref[...] = y[:, :MEM_W]
    v_ref[...] = y[:, MEM_W:]


def mem_kv(mem, g, w, tm):
    R, K = mem.shape
    NL = w.shape[0]
    out = jax.ShapeDtypeStruct((NL, R, MEM_W), f32)
    return pl.pallas_call(
        _mem_kv_kernel,
        grid=(NL, R // tm),
        in_specs=[pl.BlockSpec((tm, K), lambda l, i: (i, 0)),
                  pl.BlockSpec((None, 1, K), lambda l, i: (l, 0, 0)),
                  pl.BlockSpec((None, K, 2 * MEM_W), lambda l, i: (l, 0, 0))],
        out_specs=(pl.BlockSpec((None, tm, MEM_W), lambda l, i: (l, i, 0)),
                   pl.BlockSpec((None, tm, MEM_W), lambda l, i: (l, i, 0))),
        out_shape=(out, out),
        compiler_params=_cparams(("parallel", "parallel")),
        name="mem_kv",
    )(mem, g, w)


def _final_norm_kernel(x_ref, g_ref, o_ref):
    o_ref[...] = _rms(x_ref[...], g_ref[...])


def final_norm(x, g, tm):
    T, K = x.shape
    return pl.pallas_call(
        _final_norm_kernel,
        grid=(T // tm,),
        in_specs=[pl.BlockSpec((tm, K), lambda i: (i, 0)), pl.BlockSpec((1, K), lambda i: (0, 0))],
        out_specs=pl.BlockSpec((tm, K), lambda i: (i, 0)),
        out_shape=jax.ShapeDtypeStruct((T, K), f32),
        compiler_params=_cparams(("parallel",)),
        name="final_norm",
    )(x, g)


def _mem_attend(q, k, v, transposed=False):
    tq = q.shape[0]
    nh = MEM_W // HEAD_DIM
    lane = lax.broadcasted_iota(jnp.int32, (tq, MEM_W), 1)
    masks = [(lane >= HEAD_DIM * h) & (lane < HEAD_DIM * (h + 1)) for h in range(nh)]
    qs = jnp.concatenate([jnp.where(m, q, 0.0) for m in masks], axis=0).astype(bf16)
    s = (_mm(qs, k) if transposed else _nt(qs, k)) * (HEAD_DIM ** -0.5)
    p = jnp.exp(s - jnp.max(s, axis=-1, keepdims=True))
    l = jnp.sum(p, axis=-1, keepdims=True)
    pb = p.astype(bf16)
    o = (_nt(pb, v) if transposed else _mm(pb, v)) / l
    out = jnp.zeros((tq, MEM_W), f32)
    for h in range(nh):
        out = jnp.where(masks[h], o[h * tq:(h + 1) * tq], out)
    return out


def _mem_attn_prompt_kernel(q_ref, k_ref, v_ref, o_ref):
    o_ref[...] = _mem_attend(q_ref[...], k_ref[...].astype(bf16), v_ref[...].astype(bf16))


def mem_attn_prompt(P, qblk, mk, mv, layer, nb, L, tq):
    nq = L // tq
    return pl.pallas_call(
        _mem_attn_prompt_kernel,
        grid=(nb, nq),
        in_specs=[pl.BlockSpec((tq, MEM_W), lambda b, i: (b * nq + i, qblk)),
                  pl.BlockSpec((None, MEM_LEN, MEM_W), lambda b, i: (layer, b, 0)),
                  pl.BlockSpec((None, MEM_LEN, MEM_W), lambda b, i: (layer, b, 0))],
        out_specs=pl.BlockSpec((tq, MEM_W), lambda b, i: (b * nq + i, 0)),
        out_shape=jax.ShapeDtypeStruct((nb * L, MEM_W), f32),
        compiler_params=_cparams(("parallel", "parallel")),
        name="mem_attn_prompt",
    )(P, mk, mv)


def _mem_attn_sample_kernel(q_ref, k_ref, v_ref, o_ref, *, bb):
    for b in range(bb):
        rows = slice(b * S_PAD, (b + 1) * S_PAD)
        o_ref[rows, :] = _mem_attend(q_ref[rows, :], k_ref[b].astype(bf16), v_ref[b].astype(bf16), transposed=True)


def mem_attn_sample(P, qblk, row0, ckt, cvt, layer, bb):
    nb = ckt.shape[1]
    blk0 = row0 // (bb * S_PAD)
    return pl.pallas_call(
        functools.partial(_mem_attn_sample_kernel, bb=bb),
        grid=(nb // bb,),
        in_specs=[pl.BlockSpec((bb * S_PAD, MEM_W), lambda i: (blk0 + i, qblk)),
                  pl.BlockSpec((None, bb, MEM_W, MEM_LEN), lambda i: (layer, i, 0, 0)),
                  pl.BlockSpec((None, bb, MEM_W, MEM_LEN), lambda i: (layer, i, 0, 0))],
        out_specs=pl.BlockSpec((bb * S_PAD, MEM_W), lambda i: (i, 0)),
        out_shape=jax.ShapeDtypeStruct((nb * S_PAD, MEM_W), f32),
        compiler_params=_cparams(("parallel",)),
        name="mem_attn_sample",
    )(P, ckt, cvt)


def _delta_kernel(q_ref, k_ref, v_ref, z_ref, ba_ref, buf_ref, cw_ref, ab_ref, go_ref, s0_ref,
                  o_ref, sout_ref, qs_ref, ks_ref, vs_ref, gs_ref, bs_ref, tail_ref, s_ref,
                  *, TL, C, valid):
    t = pl.program_id(1)

    @pl.when(t == 0)
    def _():
        tail_ref[...] = buf_ref[...]
        s_ref[...] = s0_ref[...]

    row8 = lax.broadcasted_iota(jnp.int32, (SUBLANES, A_W), 0)

    def conv(x_ref, c0):
        x = x_ref[...]
        b8 = tail_ref[:, c0:c0 + A_W]
        c8 = cw_ref[:, c0:c0 + A_W]
        acc = x * c8[3:4, :]
        for s in range(1, 4):
            r = pltpu.roll(x, s, axis=0)
            fix = jnp.where(row8 < s, pltpu.roll(b8, s, axis=0), r[0:SUBLANES])
            r = jnp.concatenate([fix, r[SUBLANES:]], axis=0) if TL > SUBLANES else fix
            acc = acc + r * c8[3 - s:4 - s, :]
        tail_ref[:, c0:c0 + A_W] = x[TL - SUBLANES:TL]
        return _silu(acc)

    q = conv(q_ref, 0)
    k = conv(k_ref, A_W)
    v = conv(v_ref, 2 * A_W)
    ba = ba_ref[...]
    ab = ab_ref[...]
    beta_all = _sigmoid(ba)
    g_all = -jnp.exp(ab[0:1, :]) * _softplus(ba + ab[1:2, :])
    if valid < TL:
        live = lax.broadcasted_iota(jnp.int32, (TL, A_W), 0) < valid
        live1 = lax.broadcasted_iota(jnp.int32, (TL, LANES), 0) < valid
        k = jnp.where(live, k, 0.0)
        v = jnp.where(live, v, 0.0)
        beta_all = jnp.where(live1, beta_all, 0.0)
        g_all = jnp.where(live1, g_all, 0.0)
    rowc = lax.broadcasted_iota(jnp.int32, (TL, LANES), 0) & (C - 1)
    gc_all = g_all
    s = 1
    while s < C:
        gc_all = gc_all + jnp.where(rowc >= s, pltpu.roll(gc_all, s, axis=0), 0.0)
        s *= 2
    vs_ref[...] = v
    for h in range(A_HEADS):
        sl = slice(h * LANES, (h + 1) * LANES)
        qh = q[:, sl]
        kh = k[:, sl]
        qs_ref[:, sl] = qh * lax.rsqrt(jnp.sum(qh * qh, axis=-1, keepdims=True) + EPS) * (LANES ** -0.5)
        ks_ref[:, sl] = kh * lax.rsqrt(jnp.sum(kh * kh, axis=-1, keepdims=True) + EPS)
        gs_ref[:, sl] = jnp.broadcast_to(gc_all[:, A_HEADS + h:A_HEADS + h + 1], (TL, LANES))
        bs_ref[:, sl] = jnp.broadcast_to(beta_all[:, h:h + 1], (TL, LANES))

    ii = lax.broadcasted_iota(jnp.int32, (C, C), 0)
    jj = lax.broadcasted_iota(jnp.int32, (C, C), 1)
    tri = ii >= jj
    strict = ii > jj
    eye = jnp.where(ii == jj, 1.0, 0.0)
    gout = go_ref[...]
    nsteps = int(math.log2(C)) - 1

    heads = range(A_HEADS)
    lanes_of = [slice(h * LANES, (h + 1) * LANES) for h in heads]

    cpi = 2 if TL // C >= 2 else 1

    def chunks(c, carry):
        rows_of = [pl.ds(pl.multiple_of((c * cpi + i) * C, C), C) for i in range(cpi)]
        units = [(rows, h) for rows in rows_of for h in heads]
        U = range(len(units))
        qc = [qs_ref[rows, lanes_of[h]] for rows, h in units]
        kc = [ks_ref[rows, lanes_of[h]] for rows, h in units]
        vc = [vs_ref[rows, lanes_of[h]] for rows, h in units]
        gcc = [gs_ref[rows, lanes_of[h]] for rows, h in units]
        bc = [bs_ref[rows, lanes_of[h]] for rows, h in units]
        kb = [kc[u] * bc[u] for u in U]
        kq = [_nt(jnp.concatenate([kb[u], qc[u]], axis=0).astype(bf16), kc[u].astype(bf16)) for u in U]
        decay = []
        for u in U:
            gi = gcc[u][:, 0:C]
            gj = jnp.sum(jnp.where(ii == jj, gi, 0.0), axis=0, keepdims=True)
            decay.append(jnp.where(tri, jnp.exp(jnp.where(tri, gi - gj, 0.0)), 0.0))
        n = [jnp.where(strict, -(kq[u][0:C] * decay[u]), 0.0) for u in U]
        a = [jnp.where(tri, kq[u][C:2 * C] * decay[u], 0.0).astype(bf16) for u in U]
        x = [eye + n[u] for u in U]
        p = n
        for _ in range(nsteps):
            pb = [p[u].astype(bf16) for u in U]
            p = [_mm(pb[u], pb[u]) for u in U]
            x = [x[u] + _mm(x[u].astype(bf16), p[u].astype(bf16)) for u in U]
        eg = [jnp.exp(gcc[u]) for u in U]
        uw = [_mm(x[u].astype(bf16), jnp.concatenate([vc[u] * bc[u], kb[u] * eg[u]], axis=1).astype(bf16))
              for u in U]
        wqin = [jnp.concatenate([uw[u][:, LANES:], qc[u] * eg[u]], axis=0).astype(bf16) for u in U]
        glast = [gcc[u][C - 1:C, :] for u in U]
        kg = [(kc[u] * jnp.exp(glast[u] - gcc[u])).astype(bf16) for u in U]
        for i, rows in enumerate(rows_of):
            us = [i * A_HEADS + h for h in heads]
            S = [s_ref[h] for h in heads]
            wq = [_mm(wqin[u], S[h].astype(bf16)) for h, u in zip(heads, us)]
            vnb = [(uw[u][:, :LANES] - wq[h][0:C]).astype(bf16) for h, u in zip(heads, us)]
            av = [_mm(a[u], vnb[h]) for h, u in zip(heads, us)]
            kv_new = [_tn(kg[u], vnb[h]) for h, u in zip(heads, us)]
            for h, u in zip(heads, us):
                s_ref[h] = S[h] * jnp.exp(glast[u]) + kv_new[h]
                o_ref[rows, lanes_of[h]] = _rms(wq[h][C:2 * C] + av[h], gout) * _silu(z_ref[rows, lanes_of[h]])
        return carry

    if TL // C == cpi:
        chunks(0, 0)
    else:
        lax.fori_loop(0, TL // (C * cpi), chunks, 0)

    @pl.when(t == pl.num_programs(1) - 1)
    def _():
        sout_ref[...] = s_ref[...]


def delta_mixer(P, row0, nb, L, TL, C, valid, buf8, conv8, ab8, gout, s0, layer):
    nt = L // TL
    rb0 = row0 // TL
    tile = lambda c: pl.BlockSpec((TL, A_W), lambda b, t: (rb0 + b * nt + t, c))
    state = pl.BlockSpec((None, A_HEADS, LANES, LANES), lambda b, t: (b, 0, 0, 0))
    state_in = pl.BlockSpec((None, None, A_HEADS, LANES, LANES), lambda b, t: (layer, b, 0, 0, 0))
    return pl.pallas_call(
        functools.partial(_delta_kernel, TL=TL, C=C, valid=valid),
        grid=(nb, nt),
        in_specs=[tile(0), tile(1), tile(2), tile(3),
                  pl.BlockSpec((TL, LANES), lambda b, t: (rb0 + b * nt + t, A_BA_BLK)),
                  pl.BlockSpec((None, SUBLANES, CONV_DIM), lambda b, t: (b, 0, 0)),
                  pl.BlockSpec((SUBLANES, CONV_DIM), lambda b, t: (0, 0)),
                  pl.BlockSpec((SUBLANES, LANES), lambda b, t: (0, 0)),
                  pl.BlockSpec((1, LANES), lambda b, t: (0, 0)),
                  state_in],
        out_specs=(pl.BlockSpec((TL, A_W), lambda b, t: (b * nt + t, 0)), state),
        out_shape=(jax.ShapeDtypeStruct((nb * L, A_W), f32),
                   jax.ShapeDtypeStruct((nb, A_HEADS, LANES, LANES), f32)),
        scratch_shapes=[pltpu.VMEM((TL, A_W), f32)] * 5
        + [pltpu.VMEM((SUBLANES, CONV_DIM), f32), pltpu.VMEM((A_HEADS, LANES, LANES), f32)],
        compiler_params=_cparams(("parallel", "arbitrary")),
        name="delta_mixer",
    )(P, P, P, P, P, buf8, conv8, ab8, gout, s0)


def _attn_residual(refs, ntp):
    h_ref, mop_ref, mos_ref, omp_ref, oms_ref, wo1_ref, wo2_ref = refs
    is_p = pl.program_id(0) < ntp
    mo = jnp.where(is_p, mop_ref[...], mos_ref[...])
    om = jnp.where(is_p, omp_ref[...], oms_ref[...])
    return h_ref[...] + _mm(mo.astype(bf16), wo1_ref[...]) + _mm(om.astype(bf16), wo2_ref[...])


def _ffn_kernel(*refs, ntp):
    g_ref, wg_ref, wu_ref, wd_ref, o_ref, xn_ref, acc_ref = refs[7:]
    k = pl.program_id(1)

    @pl.when(k == 0)
    def _():
        h1 = _attn_residual(refs[:7], ntp)
        acc_ref[...] = h1
        xn_ref[...] = _rms(h1, g_ref[...]).astype(bf16)

    xn = xn_ref[...]
    a = _silu(_mm(xn, wg_ref[...])) * _mm(xn, wu_ref[...])
    acc_ref[...] += _mm(a.astype(bf16), wd_ref[...])

    @pl.when(k == pl.num_programs(1) - 1)
    def _():
        o_ref[...] = acc_ref[...]


def _outproj_specs(tm, K, mw, ow, ntp, layer):
    prow = lambda i, *_: (jnp.minimum(i, ntp - 1), 0)
    srow = lambda i, *_: (jnp.maximum(i - ntp, 0), 0)
    return [pl.BlockSpec((tm, K), lambda i, *_: (i, 0)),
            pl.BlockSpec((tm, mw), prow), pl.BlockSpec((tm, mw), srow),
            pl.BlockSpec((tm, ow), prow), pl.BlockSpec((tm, ow), srow),
            pl.BlockSpec((None, mw, K), lambda i, *_: (layer, 0, 0)),
            pl.BlockSpec((None, ow, K), lambda i, *_: (layer, mw // ow, 0)),
            pl.BlockSpec((1, K), lambda i, *_: (0, 0))]


def outproj_ffn(H, mo, om, w_out, g, w_gu, w_down, layer, ffn_layer, tm, tf):
    T, K = H.shape
    FF = w_down.shape[1]
    nk = FF // tf
    mw, ow = mo[0].shape[1], om[0].shape[1]
    ntp = mo[0].shape[0] // tm
    return pl.pallas_call(
        functools.partial(_ffn_kernel, ntp=ntp),
        grid=(T // tm, nk),
        in_specs=_outproj_specs(tm, K, mw, ow, ntp, layer)
        + [pl.BlockSpec((None, K, tf), lambda i, k: (ffn_layer, 0, k)),
           pl.BlockSpec((None, K, tf), lambda i, k: (ffn_layer, 0, nk + k)),
           pl.BlockSpec((None, tf, K), lambda i, k: (ffn_layer, k, 0))],
        out_specs=pl.BlockSpec((tm, K), lambda i, k: (i, 0)),
        out_shape=jax.ShapeDtypeStruct((T, K), f32),
        scratch_shapes=[pltpu.VMEM((tm, K), bf16), pltpu.VMEM((tm, K), f32)],
        compiler_params=_cparams(("parallel", "arbitrary")),
        name="outproj_ffn",
    )(H, mo[0], mo[1], om[0], om[1], w_out, w_out, g, w_gu, w_gu, w_down)


def _router_kernel(*refs, ntp):
    g_ref, wr_ref, h1_ref, xn_ref, r_ref, cnt_ref = refs[7:]
    i = pl.program_id(0)
    tm = h1_ref.shape[0]

    @pl.when(i == 0)
    def _():
        cnt_ref[...] = jnp.zeros_like(cnt_ref)

    h1 = _attn_residual(refs[:7], ntp)
    h1_ref[...] = h1
    xn = _rms(h1, g_ref[...])
    xn_ref[...] = xn.astype(bf16)
    lane = lax.broadcasted_iota(jnp.int32, (tm, LANES), 1)
    xh = xn.astype(bf16)
    xl = (xn - xh.astype(f32)).astype(bf16)
    wr = wr_ref[...]
    wh = wr.astype(bf16)
    wl = (wr - wh.astype(f32)).astype(bf16)
    logits = _mm(xh, wh) + _mm(xh, wl) + _mm(xl, wh)
    logits = jnp.where(lane < N_EXPERTS, logits, -jnp.inf)
    m1 = jnp.max(logits, axis=-1, keepdims=True)
    i1 = jnp.min(jnp.where(logits == m1, lane, LANES), axis=-1, keepdims=True)
    rest = jnp.where(lane == i1, -jnp.inf, logits)
    m2 = jnp.max(rest, axis=-1, keepdims=True)
    i2 = jnp.min(jnp.where(rest == m2, lane, LANES), axis=-1, keepdims=True)
    e2 = jnp.exp(m2 - m1)
    w1 = 1.0 / (1.0 + e2)
    w2 = e2 / (1.0 + e2)
    hot = jnp.where((lane == i1) | (lane == i2), 1.0, 0.0)
    ri = lax.broadcasted_iota(jnp.int32, (tm, tm), 0)
    ci = lax.broadcasted_iota(jnp.int32, (tm, tm), 1)
    before = jnp.where(ri > ci, 1.0, 0.0).astype(bf16)
    cum = _mm(before, hot.astype(bf16)) + cnt_ref[...]
    r1 = jnp.sum(jnp.where(lane == i1, cum, 0.0), axis=-1, keepdims=True)
    r2 = jnp.sum(jnp.where(lane == i2, cum, 0.0), axis=-1, keepdims=True)
    cnt_ref[...] += jnp.sum(hot, axis=0, keepdims=True)
    cols = (i1.astype(f32), i2.astype(f32), w1, w2, r1, r2)
    r = jnp.zeros((tm, LANES), f32)
    for c, val in enumerate(cols):
        r = jnp.where(lane == c, val, r)
    r_ref[...] = r


def outproj_router(H, mo, om, w_out, g, w_router, layer, moe_layer, tm):
    T, K = H.shape
    mw, ow = mo[0].shape[1], om[0].shape[1]
    ntp = mo[0].shape[0] // tm
    return pl.pallas_call(
        functools.partial(_router_kernel, ntp=ntp),
        grid=(T // tm,),
        in_specs=_outproj_specs(tm, K, mw, ow, ntp, layer)
        + [pl.BlockSpec((None, K, LANES), lambda i: (moe_layer, 0, 0))],
        out_specs=(pl.BlockSpec((tm, K), lambda i: (i, 0)), pl.BlockSpec((tm, K), lambda i: (i, 0)),
                   pl.BlockSpec((tm, LANES), lambda i: (i, 0))),
        out_shape=(jax.ShapeDtypeStruct((T, K), f32), jax.ShapeDtypeStruct((T, K), bf16),
                   jax.ShapeDtypeStruct((T, LANES), f32)),
        scratch_shapes=[pltpu.VMEM((1, LANES), f32)],
        compiler_params=_cparams(("arbitrary",)),
        name="outproj_router",
    )(H, mo[0], mo[1], om[0], om[1], w_out, w_out, g, w_router)


def _moe_group_kernel(te_ref, nu_ref, x_ref, sw_ref, wg_ref, wu_ref, wd_ref, o_ref, acc_ref):
    del te_ref
    i = pl.program_id(0)
    k = pl.program_id(1)

    @pl.when(k == 0)
    def _():
        acc_ref[...] = jnp.zeros_like(acc_ref)

    @pl.when(i < nu_ref[0])
    def _():
        x = x_ref[...]
        a = _silu(_mm(x, wg_ref[...])) * _mm(x, wu_ref[...])
        acc_ref[...] += _mm(a.astype(bf16), wd_ref[...])

    @pl.when(k == pl.num_programs(1) - 1)
    def _():
        o_ref[...] = (acc_ref[...] * sw_ref[...]).astype(o_ref.dtype)


def moe_grouped(xg, sw, tile_e, n_used, w_gu, w_down, moe_layer, tm, tf):
    NP, K = xg.shape
    FF = w_down.shape[2]
    nk = FF // tf

    def kk(i, k, nu):
        return jnp.where(i < nu[0], k, nk - 1)

    grid_spec = pltpu.PrefetchScalarGridSpec(
        num_scalar_prefetch=2, grid=(NP // tm, nk),
        in_specs=[pl.BlockSpec((tm, K), lambda i, k, te, nu: (i, 0)),
                  pl.BlockSpec((tm, 1), lambda i, k, te, nu: (i, 0)),
                  pl.BlockSpec((None, None, K, tf), lambda i, k, te, nu: (moe_layer, te[i], 0, kk(i, k, nu))),
                  pl.BlockSpec((None, None, K, tf), lambda i, k, te, nu: (moe_layer, te[i], 0, nk + kk(i, k, nu))),
                  pl.BlockSpec((None, None, tf, K), lambda i, k, te, nu: (moe_layer, te[i], kk(i, k, nu), 0))],
        out_specs=pl.BlockSpec((tm, K), lambda i, k, te, nu: (i, 0)),
        scratch_shapes=[pltpu.VMEM((tm, K), f32)])
    return pl.pallas_call(
        _moe_group_kernel, grid_spec=grid_spec,
        out_shape=jax.ShapeDtypeStruct((NP, K), bf16),
        compiler_params=_cparams(("parallel", "arbitrary")),
        name="moe_grouped",
    )(tile_e, n_used, xg, sw, w_gu, w_gu, w_down)


def _dispatch(r, tm):
    T = r.shape[0]
    A = 2 * T
    ntiles = A // tm + N_EXPERTS
    NP = ntiles * tm
    e_flat = r[:, 0:2].astype(jnp.int32).reshape(A)
    w_flat = r[:, 2:4].reshape(A)
    rank = r[:, 4:6].astype(jnp.int32)
    counts = jnp.sum((e_flat[:, None] == jnp.arange(N_EXPERTS)[None, :]).astype(jnp.int32), axis=0)
    pcounts = (counts + tm - 1) // tm * tm
    pends = jnp.cumsum(pcounts)
    pstarts = pends - pcounts
    starts = jnp.cumsum(counts) - counts
    n_used = pends[-1] // tm
    a_sorted = jnp.sort(e_flat * A + jnp.arange(A, dtype=jnp.int32)) % A
    tiles = jnp.arange(ntiles, dtype=jnp.int32)
    tile_e = jnp.minimum(jnp.searchsorted(pends, tiles * tm, side='right'), N_EXPERTS - 1).astype(jnp.int32)
    tile_e = jnp.where(tiles < n_used, tile_e, tile_e[jnp.maximum(n_used - 1, 0)])
    slot = jnp.arange(NP, dtype=jnp.int32)
    e_s = tile_e[slot // tm]
    rr = slot - pstarts[e_s]
    ok = (rr < counts[e_s]) & (slot < pends[-1])
    a_s = a_sorted[jnp.clip(starts[e_s] + rr, 0, A - 1)]
    src = a_s // 2
    sw = jnp.where(ok, w_flat[a_s], 0.0).reshape(NP, 1)
    slots = pstarts[r[:, 0:2].astype(jnp.int32)] + rank
    return src, sw, tile_e, n_used.reshape(1).astype(jnp.int32), slots


def outproj_moe(H, mo, om, w_out, g, w_router, w_gu, w_down, layer, moe_layer, tm, tf):
    h1, xn, r = outproj_router(H, mo, om, w_out, g, w_router, layer, moe_layer, tm)
    src, sw, tile_e, n_used, slots = _dispatch(r, tm)
    yg = moe_grouped(jnp.take(xn, src, axis=0), sw, tile_e, n_used, w_gu, w_down, moe_layer, tm, tf)
    return h1 + jnp.take(yg, slots[:, 0], axis=0).astype(f32) + jnp.take(yg, slots[:, 1], axis=0).astype(f32)


def _shared_kv_kernel(x_ref, g_ref, w_ref, cos_ref, sin_ref, ck_ref, cv_ref, sk_ref, sv_ref, wk_ref, wv_ref):
    y = _mm(_rms(x_ref[...], g_ref[...]).astype(bf16), w_ref[...])
    cos = cos_ref[...]
    sin = sin_ref[...]
    ck_ref[...] = y[:, 0 * LANES:1 * LANES]
    cv_ref[...] = y[:, 1 * LANES:2 * LANES]
    sk_ref[...] = _rope_lanes(y[:, 2 * LANES:3 * LANES], cos, sin)
    sv_ref[...] = y[:, 3 * LANES:4 * LANES]
    wk_ref[...] = _rope_lanes(y[:, 4 * LANES:5 * LANES], cos, sin)
    wv_ref[...] = y[:, 5 * LANES:6 * LANES]


def shared_kv(H, g, w, cos, sin, tm):
    T, K = H.shape
    row = pl.BlockSpec((tm, LANES), lambda i: (i, 0))
    out = jax.ShapeDtypeStruct((T, LANES), f32)
    return pl.pallas_call(
        _shared_kv_kernel,
        grid=(T // tm,),
        in_specs=[pl.BlockSpec((tm, K), lambda i: (i, 0)), pl.BlockSpec((1, K), lambda i: (0, 0)),
                  pl.BlockSpec((K, 6 * LANES), lambda i: (0, 0)), row, row],
        out_specs=(row,) * 6,
        out_shape=(out,) * 6,
        compiler_params=_cparams(("parallel",)),
        name="shared_kv",
    )(H, g, w, cos, sin)


def _compress_kernel(*refs, npages):
    nin = 2 * npages
    k_strips = jnp.concatenate([r[...] for r in refs[0:npages]], axis=0) if npages > 1 else refs[0][...]
    v_strips = jnp.concatenate([r[...] for r in refs[npages:nin]], axis=0) if npages > 1 else refs[npages][...]
    (pek_ref, pev_ref, w1k_ref, w1v_ref, w2k_ref, w2v_ref, cos_ref, sin_ref, ck_ref, cv_ref) = refs[nin:]
    n = k_strips.shape[0]
    half = 2 * CMP_HID
    row = lax.broadcasted_iota(jnp.int32, (n, LANES), 0)

    def tokens(strips, pe_ref, w1_ref, w2_ref):
        top = _mm((strips + pe_ref[0:1, :]).astype(bf16), w1_ref[:, 0:half])
        bot = _mm((strips + pe_ref[1:2, :]).astype(bf16), w1_ref[:, half:2 * half])
        hid = top + pltpu.roll(bot, n - 1, axis=0)
        out = _mm(_silu(hid).astype(bf16), w2_ref[...])
        return jnp.where(row < n - 1, out, 0.0)

    ck_ref[...] = _rope_lanes(tokens(k_strips, pek_ref, w1k_ref, w2k_ref), cos_ref[...], sin_ref[...])
    cv_ref[...] = tokens(v_strips, pev_ref, w1v_ref, w2v_ref)


def _drop_first(fn, _prefetch_ref, *refs):
    return fn(*refs)


def compress(k_src, v_src, k_maps, v_maps, strip_rows, nb, consts, prefetch=None):
    npages = len(k_maps)
    n = npages * strip_rows
    pek, pev, w1k, w1v, w2k, w2v, cos, sin = consts
    flat = 16 * LANES
    const2 = lambda shape: pl.BlockSpec(shape, lambda b, *_: (0, 0))
    in_specs = ([pl.BlockSpec((None, strip_rows, flat), m) for m in k_maps]
                + [pl.BlockSpec((None, strip_rows, flat), m) for m in v_maps]
                + [const2((2, flat)), const2((2, flat)), const2((flat, 4 * CMP_HID)), const2((flat, 4 * CMP_HID)),
                   const2((2 * CMP_HID, LANES)), const2((2 * CMP_HID, LANES)), const2((n, LANES)), const2((n, LANES))])
    out_spec = pl.BlockSpec((None, n, LANES), lambda b, *_: (b, 0, 0))
    out = jax.ShapeDtypeStruct((nb, n, LANES), f32)
    grid_spec = pltpu.PrefetchScalarGridSpec(
        num_scalar_prefetch=0 if prefetch is None else 1, grid=(nb,),
        in_specs=in_specs, out_specs=(out_spec, out_spec))
    args = ([] if prefetch is None else [prefetch]) + [k_src] * npages + [v_src] * npages + [pek, pev, w1k, w1v, w2k, w2v, cos, sin]
    kern = functools.partial(_compress_kernel, npages=npages)
    if prefetch is not None:
        kern = functools.partial(_drop_first, kern)
    return pl.pallas_call(
        kern, grid_spec=grid_spec, out_shape=(out, out),
        compiler_params=_cparams(("parallel",)), name="compress",
    )(*args)


def _nsa_queries(p, cos, sin):
    tq = p.shape[0]
    lane = lax.broadcasted_iota(jnp.int32, (tq, LANES), 1)
    rows = []
    for head in range(B_HEADS):
        grp = head // HPG
        blk = MEM_W // LANES + head // 2
        x = _rope_lanes(p[:, blk * LANES:(blk + 1) * LANES], cos, sin) * (HEAD_DIM ** -0.5)
        if head % 2 != grp:
            x = pltpu.roll(x, HEAD_DIM, axis=1)
        keep = (lane >= grp * HEAD_DIM) & (lane < (grp + 1) * HEAD_DIM)
        rows.append(jnp.where(keep, x, 0.0))
    return jnp.concatenate(rows, axis=0)


def _masked_softmax_parts(s, mask):
    sm = jnp.where(mask, s, NEG)
    m = jnp.max(sm, axis=-1, keepdims=True)
    p = jnp.where(mask, jnp.exp(sm - m), 0.0)
    return p, jnp.sum(p, axis=-1, keepdims=True)


def _safe_div(o, l):
    return jnp.where(l > 0.0, o / jnp.where(l > 0.0, l, 1.0), 0.0)


def _split3(x):
    hi = x.astype(bf16)
    r1 = x - hi.astype(f32)
    mid = r1.astype(bf16)
    lo = (r1 - mid.astype(f32)).astype(bf16)
    return hi, mid, lo


def _select_bias(pcn, qpos_row, past_blocks, mt_ref, eye_ref, nsel):
    tq = qpos_row.shape[1]
    nr = -(-nsel // SUBLANES) * SUBLANES
    mt = mt_ref[0:nr, :]
    blk = lax.broadcasted_iota(jnp.int32, (nr, tq), 0)
    lane = lax.broadcasted_iota(jnp.int32, (tq, LANES), 1)
    cur = qpos_row // SEL_BLK
    forced = (blk == 0) | (blk == cur) | (blk == cur - 1)
    valid = blk * SEL_BLK <= qpos_row
    stacks, picks = [], []
    for grp in range(2):
        psum = pcn[grp * HPG * tq:(grp * HPG + 1) * tq]
        for hh in range(1, HPG):
            psum = psum + pcn[(grp * HPG + hh) * tq:(grp * HPG + hh + 1) * tq]
        hi, mid, lo = _split3(psum)
        imp = _nt(mt, hi) + _nt(mt, mid) + _nt(mt, lo)
        imp = jnp.where(forced, 1e9, jnp.where(valid, imp, -1e9))
        rank = jnp.zeros((nr, tq), f32)
        for j in range(nsel):
            rj = imp[j:j + 1, :]
            rank = rank + jnp.where(rj > imp, 1.0, jnp.where(rj == imp, jnp.where(blk > j, 1.0, 0.0), 0.0))
        sel_t = jnp.where((rank < SEL_TOP) & (blk < nsel), 1.0, 0.0).astype(bf16)
        pick = _tn(sel_t, eye_ref[0:nr, :])
        picks.append(pick)
        stacks += [jnp.where((pick > 0.5) & (lane < past_blocks), 0.0, NEG)] * HPG
    return jnp.concatenate(stacks, axis=0), picks


def _nsa_combine(p, o_c, o_s, o_w):
    tq = p.shape[0]
    gl = _sigmoid(p[:, (MEM_W + B_W):(MEM_W + B_W) + LANES])

    def gate(j):
        return jnp.concatenate([gl[:, 3 * h + j:3 * h + j + 1] for h in range(B_HEADS)], axis=0)

    comb = gate(0) * o_c + gate(1) * o_s + gate(2) * o_w
    lane = lax.broadcasted_iota(jnp.int32, (tq, LANES), 1)
    cols = []
    for c in range(B_HEADS // 2):
        lo = comb[2 * c * tq:(2 * c + 1) * tq]
        hi = comb[(2 * c + 1) * tq:(2 * c + 2) * tq]
        if (2 * c) // HPG == 1:
            lo = pltpu.roll(lo, HEAD_DIM, axis=1)
        if (2 * c + 1) // HPG == 0:
            hi = pltpu.roll(hi, HEAD_DIM, axis=1)
        cols.append(jnp.where(lane < HEAD_DIM, lo, hi))
    return jnp.concatenate(cols, axis=1)


def _nsa_prompt_kernel(p_ref, cos_ref, sin_ref, ck_ref, cv_ref, sk_ref, sv_ref, wk_ref, wv_ref, mt_ref, et_ref,
                       eye_ref, o_ref, *, tq, kc, nsel):
    qb = pl.program_id(1)
    s0 = qb * tq
    R = B_HEADS * tq
    p = p_ref[...]
    qf = _nsa_queries(p, cos_ref[...], sin_ref[...])
    qb16 = qf.astype(bf16)
    ii = lax.broadcasted_iota(jnp.int32, (tq, tq), 0)
    jj = lax.broadcasted_iota(jnp.int32, (tq, tq), 1)
    qposR = s0 + jnp.concatenate([lax.broadcasted_iota(jnp.int32, (tq, 1), 0)] * B_HEADS, axis=0)
    lane = lax.broadcasted_iota(jnp.int32, (R, LANES), 1)
    cmask = ((lane * CMP_STRIDE + (CMP_BLK - 1)) <= qposR) & (lane < ck_ref.shape[0] - 1)
    pc, lc = _masked_softmax_parts(_nt(qb16, ck_ref[...].astype(bf16)), cmask)
    pcn = _safe_div(pc, lc)
    o_c = _mm(pcn.astype(bf16), cv_ref[...].astype(bf16))
    qpos_row = s0 + lax.broadcasted_iota(jnp.int32, (1, tq), 1)
    bias, _ = _select_bias(pcn, qpos_row, s0 // SEL_BLK, mt_ref, eye_ref, nsel)
    qsel = jnp.concatenate([qb16, bias.astype(bf16)], axis=1)
    sd = _nt(qb16, sk_ref[pl.ds(s0, tq), :].astype(bf16)).reshape(B_HEADS, tq, tq)
    sd = jnp.where((ii >= jj)[None], sd, NEG).reshape(R, tq)
    m0 = jnp.max(sd, axis=-1, keepdims=True)
    pd = jnp.exp(sd - m0)
    init = (m0, jnp.sum(pd, axis=-1, keepdims=True), _mm(pd.astype(bf16), sv_ref[pl.ds(s0, tq), :].astype(bf16)))

    def body(c, carry):
        m_run, l_run, acc = carry
        k0 = pl.multiple_of(c * kc, kc)
        ka = jnp.concatenate([sk_ref[pl.ds(k0, kc), :].astype(bf16), et_ref[pl.ds(k0, kc), :]], axis=1)
        s = _nt(qsel, ka)
        m_new = jnp.maximum(m_run, jnp.max(s, axis=-1, keepdims=True))
        alpha = jnp.exp(m_run - m_new)
        pe = jnp.exp(s - m_new)
        l_new = alpha * l_run + jnp.sum(pe, axis=-1, keepdims=True)
        acc_new = alpha * acc + _mm(pe.astype(bf16), sv_ref[pl.ds(k0, kc), :].astype(bf16))
        return m_new, l_new, acc_new

    _, l_run, acc = lax.fori_loop(0, (s0 + kc - 1) // kc, body, init)
    o_s = acc / l_run
    wlen = WINDOW + tq
    wrow = lax.broadcasted_iota(jnp.int32, (wlen, LANES), 0)
    wlane = lax.broadcasted_iota(jnp.int32, (wlen, LANES), 1)
    is_pad = jnp.where((wrow + (s0 - WINDOW) < 0) & (wlane == 0), 1.0, 0.0).astype(bf16)
    kw = jnp.concatenate([wk_ref[pl.ds(s0, wlen), :].astype(bf16), is_pad], axis=1)
    qwin = jnp.concatenate([qb16, jnp.where(lane == 0, NEG, 0.0).astype(bf16)], axis=1)
    sw = _nt(qwin, kw).reshape(B_HEADS, tq, wlen)
    far = jnp.where((jj > ii)[None], sw[:, :, 0:tq], NEG)
    near = jnp.where((jj <= ii)[None], sw[:, :, wlen - tq:wlen], NEG)
    sw = jnp.concatenate([far, sw[:, :, tq:wlen - tq], near], axis=2).reshape(R, wlen)
    pw = jnp.exp(sw - jnp.max(sw, axis=-1, keepdims=True))
    o_w = _mm(pw.astype(bf16), wv_ref[pl.ds(s0, wlen), :].astype(bf16)) / jnp.sum(pw, axis=-1, keepdims=True)
    o_ref[...] = _nsa_combine(p, o_c, o_s, o_w)


def nsa_prompt(P, cos, sin, ck, cv, sk, sv, wk_pad, wv_pad, mt, et, eye, nb, L, tq, kc):
    nq = L // tq
    nsel = L // SEL_BLK
    seq = pl.BlockSpec((L, LANES), lambda b, i: (b, 0))
    win = pl.BlockSpec((None, WINDOW + L, LANES), lambda b, i: (b, 0, 0))
    cmp_spec = pl.BlockSpec((None, LANES, LANES), lambda b, i: (b, 0, 0))
    const = lambda shape: pl.BlockSpec(shape, lambda b, i: (0, 0))
    return pl.pallas_call(
        functools.partial(_nsa_prompt_kernel, tq=tq, kc=kc, nsel=nsel),
        grid=(nb, nq),
        in_specs=[pl.BlockSpec((tq, B_IN_PAD), lambda b, i: (b * nq + i, 0)),
                  pl.BlockSpec((tq, LANES), lambda b, i: (i, 0)),
                  pl.BlockSpec((tq, LANES), lambda b, i: (i, 0)),
                  cmp_spec, cmp_spec, seq, seq, win, win,
                  const((LANES, LANES)), const((L, LANES)), const((LANES, LANES))],
        out_specs=pl.BlockSpec((tq, B_W), lambda b, i: (b * nq + i, 0)),
        out_shape=jax.ShapeDtypeStruct((nb * L, B_W), f32),
        compiler_params=_cparams(("parallel", "parallel")),
        name="nsa_prompt",
    )(P, cos, sin, ck, cv, sk, sv, wk_pad, wv_pad, mt, et, eye)


def _nsa_sample_kernel(*refs, npages, past):
    refs = refs[1:]
    kpages = refs[0:npages]
    vpages = refs[npages:2 * npages]
    (p_ref, cos_ref, sin_ref, ck_ref, cv_ref, nsk_ref, nsv_ref, cwk_ref, cwv_ref, nwk_ref, nwv_ref,
     mt_ref, et_ref, eye_ref, o_ref) = refs[2 * npages:]
    tq = S_PAD
    R = B_HEADS * tq
    p = p_ref[...]
    qb16 = _nsa_queries(p, cos_ref[...], sin_ref[...]).astype(bf16)
    tR = jnp.concatenate([lax.broadcasted_iota(jnp.int32, (tq, 1), 0)] * B_HEADS, axis=0)
    qposR = past + tR
    lane = lax.broadcasted_iota(jnp.int32, (R, LANES), 1)
    cmask = ((lane * CMP_STRIDE + (CMP_BLK - 1)) <= qposR) & (lane < ck_ref.shape[0] - 1)
    pc, lc = _masked_softmax_parts(_nt(qb16, ck_ref[...].astype(bf16)), cmask)
    pcn = _safe_div(pc, lc)
    o_c = _mm(pcn.astype(bf16), cv_ref[...].astype(bf16))
    qpos_row = past + lax.broadcasted_iota(jnp.int32, (1, tq), 1)
    new_blk = past // SEL_BLK
    bias, picks = _select_bias(pcn, qpos_row, new_blk, mt_ref, eye_ref, new_blk + 1)
    new_ok = lax.broadcasted_iota(jnp.int32, (R, tq), 1) <= tR
    ka = jnp.concatenate([jnp.concatenate([r[...] for r in kpages], axis=1).astype(bf16), et_ref[...]], axis=0)
    s_old = _mm(jnp.concatenate([qb16, bias.astype(bf16)], axis=1), ka)
    new_kept = jnp.concatenate([picks[h // HPG][:, new_blk:new_blk + 1] for h in range(B_HEADS)], axis=0) > 0.5
    s_new = jnp.where(new_ok, jnp.where(new_kept, _nt(qb16, nsk_ref[...].astype(bf16)), NEG), NEG)
    m = jnp.maximum(jnp.max(s_old, axis=-1, keepdims=True), jnp.max(s_new, axis=-1, keepdims=True))
    p_old = jnp.exp(s_old - m)
    p_new = jnp.exp(s_new - m)
    l = jnp.sum(p_old, axis=-1, keepdims=True) + jnp.sum(p_new, axis=-1, keepdims=True)
    svb = jnp.concatenate([r[...] for r in vpages], axis=1).astype(bf16)
    o_s = (_nt(p_old.astype(bf16), svb) + _mm(p_new.astype(bf16), nsv_ref[...].astype(bf16))) / l
    wb = cwk_ref.shape[1]
    wdist = qposR - (past - wb + lax.broadcasted_iota(jnp.int32, (R, wb), 1))
    w_old = jnp.where((wdist >= 0) & (wdist < WINDOW), _mm(qb16, cwk_ref[...].astype(bf16)), NEG)
    w_new = jnp.where(new_ok, _nt(qb16, nwk_ref[...].astype(bf16)), NEG)
    mw = jnp.maximum(jnp.max(w_old, axis=-1, keepdims=True), jnp.max(w_new, axis=-1, keepdims=True))
    pw_old = jnp.exp(w_old - mw)
    pw_new = jnp.exp(w_new - mw)
    lw = jnp.sum(pw_old, axis=-1, keepdims=True) + jnp.sum(pw_new, axis=-1, keepdims=True)
    o_w = (_nt(pw_old.astype(bf16), cwv_ref[...].astype(bf16)) + _mm(pw_new.astype(bf16), nwv_ref[...].astype(bf16))) / lw
    o_ref[...] = _nsa_combine(p, o_c, o_s, o_w)


def nsa_sample(P, row0, page_table, pool_k, pool_v, cos, sin, ck, cv, nsk, nsv, cwk, cwv, nwk, nwv, mt, et, eye):
    nb = cwk.shape[0]
    npages = page_table.shape[0] // nb
    past = npages * PAGE
    rb0 = row0 // S_PAD
    page = lambda j: pl.BlockSpec((None, LANES, PAGE), lambda b, pt: (pt[b * npages + j], 0, 0))
    new_rows = pl.BlockSpec((S_PAD, LANES), lambda b, pt: (rb0 + b, 0))
    per_seq = lambda n: pl.BlockSpec((None, LANES, n), lambda b, pt: (b, 0, 0))
    in_specs = ([page(j) for j in range(npages)] + [page(j) for j in range(npages)]
                + [pl.BlockSpec((S_PAD, B_IN_PAD), lambda b, pt: (rb0 + b, 0)),
                   pl.BlockSpec((S_PAD, LANES), lambda b, pt: (0, 0)),
                   pl.BlockSpec((S_PAD, LANES), lambda b, pt: (0, 0)),
                   per_seq(LANES), per_seq(LANES), new_rows, new_rows,
                   per_seq(cwk.shape[2]), per_seq(cwk.shape[2]), new_rows, new_rows,
                   pl.BlockSpec((LANES, LANES), lambda b, pt: (0, 0)),
                   pl.BlockSpec((LANES, past), lambda b, pt: (0, 0)),
                   pl.BlockSpec((LANES, LANES), lambda b, pt: (0, 0))])
    grid_spec = pltpu.PrefetchScalarGridSpec(
        num_scalar_prefetch=1, grid=(nb,), in_specs=in_specs,
        out_specs=pl.BlockSpec((S_PAD, B_W), lambda b, pt: (b, 0)))
    return pl.pallas_call(
        functools.partial(_nsa_sample_kernel, npages=npages, past=past),
        grid_spec=grid_spec,
        out_shape=jax.ShapeDtypeStruct((nb * S_PAD, B_W), f32),
        compiler_params=_cparams(("parallel",)),
        name="nsa_sample",
    )(page_table, *([pool_k] * npages), *([pool_v] * npages), P, cos, sin, ck, cv, nsk, nsv, cwk, cwv, nwk, nwv,
      mt, et, eye)


def _rope_tables(pos):
    half = HEAD_DIM // 2
    inv = ROPE_THETA ** (-jnp.arange(half, dtype=f32) / half)
    ang = pos.astype(f32)[:, None] * inv[None, :]
    cos = jnp.tile(jnp.cos(ang), (1, LANES // half))
    sin = jnp.tile(jnp.sin(ang), (1, LANES // half))
    sign = jnp.where((jnp.arange(LANES) % HEAD_DIM) < half, -1.0, 1.0).astype(f32)
    return cos, sin * sign[None, :]


def _cmp_to_sel(ncmp, nsel):
    per = SEL_BLK // CMP_STRIDE
    sub = CMP_BLK // CMP_STRIDE
    i = jnp.arange(LANES)[:, None]
    j = jnp.arange(LANES)[None, :]
    m = jnp.zeros((LANES, LANES), f32)
    for r in range(sub):
        m = m + (((i + r) // per) == j).astype(f32)
    m = jnp.where((i < ncmp) & (j < nsel), m / sub, 0.0)
    return m.astype(bf16)


def _block_expand(nkeys):
    j = jnp.arange(LANES)[:, None]
    k = jnp.arange(nkeys)[None, :]
    return ((k // SEL_BLK) == j).astype(bf16)


def _compress_consts(pe, w1, w2):
    w1r = w1.reshape(CMP_BLK, HEAD_DIM, CMP_HID)
    eye2 = jnp.eye(2, dtype=f32)

    def expand(w):
        return jnp.einsum('ldc,gh->lgdhc', w, eye2).reshape(16 * LANES, 2 * CMP_HID)

    w1s = jnp.concatenate([expand(w1r[:16]), expand(w1r[16:])], axis=1).astype(bf16)
    pes = jnp.stack([jnp.broadcast_to(pe[:16, None, :], (16, 2, HEAD_DIM)).reshape(-1),
                     jnp.broadcast_to(pe[16:, None, :], (16, 2, HEAD_DIM)).reshape(-1)])
    w2s = jnp.einsum('cd,gh->gchd', w2, eye2).reshape(2 * CMP_HID, LANES).astype(bf16)
    return pes, w1s, w2s


def kernel(x_prompt, x_sample, mem_prompt, state_conv, state_ssm, cache_cmp_k, cache_cmp_v, cache_sel_k, cache_sel_v, cache_win_k, cache_win_v, cache_mem_k, cache_mem_v, page_table, norm_mix, norm_ffn, norm_mem, w_mem_kv, w_in_a, conv_w_a, a_log, dt_bias, norm_out_a, w_out_a, w_in_b, w_out_b, norm_kv, w_kv_shared, cmp_pe_k, cmp_w1_k, cmp_w2_k, cmp_pe_v, cmp_w1_v, cmp_w2_v, w_gu_dense, w_down_dense, w_router, w_gu_exp, w_down_exp, norm_final):
    bp, lp, d = x_prompt.shape
    bs, ls, _ = x_sample.shape
    depth = norm_mix.shape[0]
    n_a = w_in_a.shape[0]
    past = page_table.shape[1] * PAGE
    tp = bp * lp
    T = tp + bs * S_PAD
    tm = 512

    xs = jnp.pad(x_sample, ((0, 0), (0, S_PAD - ls), (0, 0)))
    H = jnp.concatenate([x_prompt.reshape(tp, d), xs.reshape(bs * S_PAD, d)], axis=0)

    qkvz = MEM_W + CONV_DIM + A_W
    w_in_a_b = jnp.concatenate(
        [w_in_a[:, :, MEM_W:qkvz], w_in_a[:, :, :MEM_W], w_in_a[:, :, qkvz:],
         jnp.zeros(w_in_a.shape[:2] + (A_IN_PAD - w_in_a.shape[2],), f32)], axis=2).astype(bf16)
    w_in_b_b = jnp.pad(w_in_b, ((0, 0), (0, 0), (0, B_IN_PAD - w_in_b.shape[2]))).astype(bf16)
    w_out_a_b = w_out_a.astype(bf16)
    w_out_b_b = w_out_b.astype(bf16)
    w_gu_dense_b = w_gu_dense.astype(bf16)
    w_down_dense_b = w_down_dense.astype(bf16)
    w_gu_exp_b = w_gu_exp.astype(bf16)
    w_down_exp_b = w_down_exp.astype(bf16)
    w_router_p = jnp.pad(w_router, ((0, 0), (0, 0), (0, LANES - N_EXPERTS)))

    mem_k, mem_v = mem_kv(mem_prompt.reshape(bp * MEM_LEN, d), norm_mem.reshape(depth, 1, d),
                          w_mem_kv.astype(bf16), tm)
    cmk = jnp.transpose(cache_mem_k, (0, 1, 3, 4, 2)).reshape(depth, bs, MEM_W, MEM_LEN)
    cmv = jnp.transpose(cache_mem_v, (0, 1, 3, 4, 2)).reshape(depth, bs, MEM_W, MEM_LEN)
    zero_buf = jnp.zeros((bp, SUBLANES, CONV_DIM), f32)
    zero_state = jnp.zeros((1, bp, A_HEADS, LANES, LANES), f32)

    pos_p = jnp.arange(lp, dtype=jnp.int32)
    pos_s = past + jnp.arange(S_PAD, dtype=jnp.int32)
    cos_p, sin_p = _rope_tables(pos_p)
    cos_s, sin_s = _rope_tables(pos_s)

    conv_p, ssm_p, conv_s, ssm_s = [], [], [], []
    kv = None
    for layer in range(depth):
        is_a = layer < n_a
        g_mix = norm_mix[layer].reshape(1, d)
        if is_a:
            P = norm_matmul(H, g_mix, w_in_a_b, layer, tm, A_IN_PAD // 3)
            qblk = A_MEMQ_BLK
        else:
            lb = layer - n_a
            qblk = 0
            if lb == 0:
                cos_all = jnp.concatenate([jnp.tile(cos_p, (bp, 1)), jnp.tile(cos_s, (bs, 1))], axis=0)
                sin_all = jnp.concatenate([jnp.tile(sin_p, (bp, 1)), jnp.tile(sin_s, (bs, 1))], axis=0)
                kv = shared_kv(H, norm_kv.reshape(1, d), w_kv_shared.astype(bf16), cos_all, sin_all, tm)
                ncmp = (lp - CMP_BLK) // CMP_STRIDE + 1
                cpos = jnp.arange(LANES, dtype=jnp.int32) * CMP_STRIDE + CMP_BLK - 1
                cos_c, sin_c = _rope_tables(cpos)
                pek, w1k, w2k = _compress_consts(cmp_pe_k, cmp_w1_k, cmp_w2_k)
                pev, w1v, w2v = _compress_consts(cmp_pe_v, cmp_w1_v, cmp_w2_v)
                consts = (pek, pev, w1k, w1v, w2k, w2v, cos_c, sin_c)
                strips = lp // 16
                ck_p, cv_p = compress(kv[0][:tp].reshape(bp, strips, 16 * LANES),
                                      kv[1][:tp].reshape(bp, strips, 16 * LANES),
                                      [lambda b: (b, 0, 0)], [lambda b: (b, 0, 0)], strips, bp, consts)
                npages = page_table.shape[1]
                pt_flat = page_table.reshape(-1)
                pool_strips = PAGE // 16
                pmap = lambda j: (lambda b, pt: (pt[b * npages + j], 0, 0))
                ck_s, cv_s = compress(cache_cmp_k.reshape(-1, pool_strips, 16 * LANES),
                                      cache_cmp_v.reshape(-1, pool_strips, 16 * LANES),
                                      [pmap(j) for j in range(npages)], [pmap(j) for j in range(npages)],
                                      pool_strips, bs, consts, prefetch=pt_flat)
                mt_p = _cmp_to_sel(ncmp, lp // SEL_BLK).T
                et_p = _block_expand(lp).T
                ncmp_s = (past + ls - CMP_BLK) // CMP_STRIDE + 1
                mt_s = _cmp_to_sel(ncmp_s, -(-(past + ls) // SEL_BLK)).T
                e_s = _block_expand(past)
                eye = jnp.eye(LANES, dtype=bf16)
                wk_pad = jnp.pad(kv[4][:tp].reshape(bp, lp, LANES), ((0, 0), (WINDOW, 0), (0, 0)))
                wv_pad = jnp.pad(kv[5][:tp].reshape(bp, lp, LANES), ((0, 0), (WINDOW, 0), (0, 0)))
                pool_sk = jnp.transpose(cache_sel_k, (0, 2, 3, 1)).reshape(-1, LANES, PAGE)
                pool_sv = jnp.transpose(cache_sel_v, (0, 2, 3, 1)).reshape(-1, LANES, PAGE)
                cwk = jnp.transpose(cache_win_k, (0, 2, 3, 1)).reshape(bs, LANES, -1)
                cwv = jnp.transpose(cache_win_v, (0, 2, 3, 1)).reshape(bs, LANES, -1)
            P = norm_matmul(H, g_mix, w_in_b_b, lb, tm, B_IN_PAD)
        om = (mem_attn_prompt(P, qblk, mem_k, mem_v, layer, bp, lp, 512),
              mem_attn_sample(P, qblk, tp, cmk, cmv, layer, 8))
        if is_a:
            conv8 = jnp.pad(conv_w_a[layer], ((0, SUBLANES - conv_w_a.shape[1]), (0, 0)))
            ab8 = jnp.zeros((SUBLANES, LANES), f32)
            ab8 = ab8.at[0, A_HEADS:2 * A_HEADS].set(a_log[layer]).at[1, A_HEADS:2 * A_HEADS].set(dt_bias[layer])
            gout = norm_out_a[layer].reshape(1, LANES)
            mo_p, sp = delta_mixer(P, 0, bp, lp, 512, 64, 512, zero_buf, conv8, ab8, gout, zero_state, 0)
            buf_s = jnp.pad(state_conv[layer], ((0, 0), (SUBLANES - state_conv.shape[2], 0), (0, 0)))
            mo_s, ss = delta_mixer(P, tp, bs, S_PAD, S_PAD, S_PAD, ls, buf_s, conv8, ab8, gout, state_ssm, layer)
            conv_p.append(jnp.stack([P[b * lp + lp - 3:(b + 1) * lp, :CONV_DIM] for b in range(bp)]))
            conv_s.append(P[tp:, :CONV_DIM].reshape(bs, S_PAD, CONV_DIM)[:, ls - 3:ls])
            ssm_p.append(sp)
            ssm_s.append(ss)
            w_out, wl = w_out_a_b, layer
        else:
            mo_p = nsa_prompt(P, cos_p, sin_p, ck_p, cv_p, kv[2], kv[3], wk_pad, wv_pad, mt_p, et_p, eye,
                              bp, lp, Q_BLOCK, 512)
            mo_s = nsa_sample(P, tp, pt_flat, pool_sk, pool_sv, cos_s, sin_s, ck_s, cv_s, kv[2], kv[3], cwk, cwv,
                              kv[4], kv[5], mt_s, e_s, eye)
            w_out, wl = w_out_b_b, lb
        mo = (mo_p, mo_s)
        g_ffn = norm_ffn[layer].reshape(1, d)
        if layer % 2 == 0:
            H = outproj_ffn(H, mo, om, w_out, g_ffn, w_gu_dense_b, w_down_dense_b, wl, layer // 2, tm, D_FF // 2)
        else:
            H = outproj_moe(H, mo, om, w_out, g_ffn, w_router_p, w_gu_exp_b, w_down_exp_b, wl, layer // 2,
                            tm, D_FF_EXPERT // 2)
    Y = final_norm(H, norm_final.reshape(1, d), tm)

    def rows_p(a):
        return a[:tp].reshape(bp, lp, 2, HEAD_DIM)

    def rows_s(a):
        return a[tp:].reshape(bs, S_PAD, 2, HEAD_DIM)[:, :ls]

    wlen = min(WINDOW, lp)
    return (Y[:tp].reshape(bp, lp, d), Y[tp:].reshape(bs, S_PAD, d)[:, :ls],
            jnp.stack(conv_p), jnp.stack(ssm_p),
            rows_p(kv[0]), rows_p(kv[1]), rows_p(kv[2]), rows_p(kv[3]),
            rows_p(kv[4])[:, lp - wlen:], rows_p(kv[5])[:, lp - wlen:],
            mem_k.reshape(depth, bp, MEM_LEN, 4, HEAD_DIM), mem_v.reshape(depth, bp, MEM_LEN, 4, HEAD_DIM),
            jnp.stack(conv_s), jnp.stack(ssm_s),
            rows_s(kv[0]), rows_s(kv[1]), rows_s(kv[2]), rows_s(kv[3]),
            jnp.concatenate([cache_win_k[:, ls:], rows_s(kv[4])], axis=1),
            jnp.concatenate([cache_win_v[:, ls:], rows_s(kv[5])], axis=1))
```

```python
import functools
import math

import jax
import jax.numpy as jnp
from jax import lax
from jax.experimental import pallas as pl
from jax.experimental.pallas import tpu as pltpu

f32 = jnp.float32
bf16 = jnp.bfloat16

EPS = 1e-6
NEG = -1e30
ROPE_THETA = 10000.0
HEAD_DIM = 64
LANES = 128
SUBLANES = 8
VMEM_LIMIT = 48 * 1024 * 1024

D_MODEL = 1024
MEM_LEN = 256
MEM_W = 256
A_HEADS = 6
A_W = 768
CONV_DIM = 2304
A_IN_PAD = 3456
A_MEMQ_BLK = 12
A_BA_BLK = 26
B_HEADS = 12
HPG = 6
B_W = 768
B_IN_PAD = 1152
CMP_BLK = 32
CMP_STRIDE = 16
CMP_HID = 256
SEL_BLK = 64
SEL_TOP = 16
WINDOW = 512
Q_BLOCK = 128
D_FF = 2816
N_EXPERTS = 8
D_FF_EXPERT = 3584
PAGE = 128
S_PAD = 8
COMBINE_ALIGN = 16


def _cparams(sem):
    return pltpu.CompilerParams(dimension_semantics=sem, vmem_limit_bytes=VMEM_LIMIT)


def _nt(a, b):
    return lax.dot_general(a, b, (((1,), (1,)), ((), ())), preferred_element_type=f32)


def _tn(a, b):
    return lax.dot_general(a, b, (((0,), (0,)), ((), ())), preferred_element_type=f32)


def _mm(a, b):
    return jnp.dot(a, b, preferred_element_type=f32)


def _rms(x, g):
    return (x * lax.rsqrt(jnp.mean(x * x, axis=-1, keepdims=True) + EPS)) * g


def _sigmoid(x):
    return 1.0 / (1.0 + jnp.exp(-x))


def _silu(x):
    return x * _sigmoid(x)


def _softplus(x):
    return jnp.maximum(x, 0.0) + jnp.log(1.0 + jnp.exp(-jnp.abs(x)))


def _rope_lanes(x, cos, sin_signed):
    lane = lax.broadcasted_iota(jnp.int32, x.shape, x.ndim - 1)
    lo = (lane % HEAD_DIM) < (HEAD_DIM // 2)
    partner = jnp.where(lo, pltpu.roll(x, LANES - HEAD_DIM // 2, axis=x.ndim - 1),
                        pltpu.roll(x, HEAD_DIM // 2, axis=x.ndim - 1))
    return x * cos + partner * sin_signed


def _norm_matmul_kernel(x_ref, g_ref, w_ref, o_ref, xn_ref):
    @pl.when(pl.program_id(1) == 0)
    def _():
        xn_ref[...] = _rms(x_ref[...], g_ref[...]).astype(bf16)

    o_ref[...] = _mm(xn_ref[...], w_ref[...])


def norm_matmul(x, g, w, layer, tm, tn):
    T, K = x.shape
    N = w.shape[2]
    return pl.pallas_call(
        _norm_matmul_kernel,
        grid=(T // tm, N // tn),
        in_specs=[pl.BlockSpec((tm, K), lambda i, j: (i, 0)),
                  pl.BlockSpec((1, K), lambda i, j: (0, 0)),
                  pl.BlockSpec((None, K, tn), lambda i, j: (layer, 0, j))],
        out_specs=pl.BlockSpec((tm, tn), lambda i, j: (i, j)),
        out_shape=jax.ShapeDtypeStruct((T, N), f32),
        scratch_shapes=[pltpu.VMEM((tm, K), bf16)],
        compiler_params=_cparams(("parallel", "arbitrary")),
        name="norm_matmul",
    )(x, g, w)


def _mem_kv_kernel(x_ref, g_ref, w_ref, k_ref, v_ref):
    y = _mm(_rms(x_ref[...], g_ref[...]).astype(bf16), w_ref[...])
    k_ref[...] = y[:, :MEM_W]
    v_ref[...] = y[:, MEM_W:]


def mem_kv(mem, g, w, tm):
    R, K = mem.shape
    NL = w.shape[0]
    out = jax.ShapeDtypeStruct((NL, R, MEM_W), f32)
    return pl.pallas_call(
        _mem_kv_kernel,
        grid=(NL, R // tm),
        in_specs=[pl.BlockSpec((tm, K), lambda l, i: (i, 0)),
                  pl.BlockSpec((None, 1, K), lambda l, i: (l, 0, 0)),
                  pl.BlockSpec((None, K, 2 * MEM_W), lambda l, i: (l, 0, 0))],
        out_specs=(pl.BlockSpec((None, tm, MEM_W), lambda l, i: (l, i, 0)),
                   pl.BlockSpec((None, tm, MEM_W), lambda l, i: (l, i, 0))),
        out_shape=(out, out),
        compiler_params=_cparams(("parallel", "parallel")),
        name="mem_kv",
    )(mem, g, w)


def _final_norm_kernel(x_ref, g_ref, o_ref):
    o_ref[...] = _rms(x_ref[...], g_ref[...])


def final_norm(x, g, tm):
    T, K = x.shape
    return pl.pallas_call(
        _final_norm_kernel,
        grid=(T // tm,),
        in_specs=[pl.BlockSpec((tm, K), lambda i: (i, 0)), pl.BlockSpec((1, K), lambda i: (0, 0))],
        out_specs=pl.BlockSpec((tm, K), lambda i: (i, 0)),
        out_shape=jax.ShapeDtypeStruct((T, K), f32),
        compiler_params=_cparams(("parallel",)),
        name="final_norm",
    )(x, g)


def _mem_attend(q, k, v, transposed=False):
    tq = q.shape[0]
    nh = MEM_W // HEAD_DIM
    lane = lax.broadcasted_iota(jnp.int32, (tq, MEM_W), 1)
    masks = [(lane >= HEAD_DIM * h) & (lane < HEAD_DIM * (h + 1)) for h in range(nh)]
    qs = jnp.concatenate([jnp.where(m, q, 0.0) for m in masks], axis=0).astype(bf16)
    s = (_mm(qs, k) if transposed else _nt(qs, k)) * (HEAD_DIM ** -0.5)
    p = jnp.exp(s - jnp.max(s, axis=-1, keepdims=True))
    l = jnp.sum(p, axis=-1, keepdims=True)
    pb = p.astype(bf16)
    o = (_nt(pb, v) if transposed else _mm(pb, v)) / l
    out = jnp.zeros((tq, MEM_W), f32)
    for h in range(nh):
        out = jnp.where(masks[h], o[h * tq:(h + 1) * tq], out)
    return out


def _mem_attn_prompt_kernel(q_ref, k_ref, v_ref, o_ref):
    o_ref[...] = _mem_attend(q_ref[...], k_ref[...].astype(bf16), v_ref[...].astype(bf16))


def mem_attn_prompt(P, qblk, mk, mv, layer, nb, L, tq):
    nq = L // tq
    return pl.pallas_call(
        _mem_attn_prompt_kernel,
        grid=(nb, nq),
        in_specs=[pl.BlockSpec((tq, MEM_W), lambda b, i: (b * nq + i, qblk)),
                  pl.BlockSpec((None, MEM_LEN, MEM_W), lambda b, i: (layer, b, 0)),
                  pl.BlockSpec((None, MEM_LEN, MEM_W), lambda b, i: (layer, b, 0))],
        out_specs=pl.BlockSpec((tq, MEM_W), lambda b, i: (b * nq + i, 0)),
        out_shape=jax.ShapeDtypeStruct((nb * L, MEM_W), f32),
        compiler_params=_cparams(("parallel", "parallel")),
        name="mem_attn_prompt",
    )(P, mk, mv)


def _mem_attn_sample_kernel(q_ref, k_ref, v_ref, o_ref, *, bb):
    for b in range(bb):
        rows = slice(b * S_PAD, (b + 1) * S_PAD)
        o_ref[rows, :] = _mem_attend(q_ref[rows, :], k_ref[b].astype(bf16), v_ref[b].astype(bf16), transposed=True)


def mem_attn_sample(P, qblk, row0, ckt, cvt, layer, bb):
    nb = ckt.shape[1]
    blk0 = row0 // (bb * S_PAD)
    return pl.pallas_call(
        functools.partial(_mem_attn_sample_kernel, bb=bb),
        grid=(nb // bb,),
        in_specs=[pl.BlockSpec((bb * S_PAD, MEM_W), lambda i: (blk0 + i, qblk)),
                  pl.BlockSpec((None, bb, MEM_W, MEM_LEN), lambda i: (layer, i, 0, 0)),
                  pl.BlockSpec((None, bb, MEM_W, MEM_LEN), lambda i: (layer, i, 0, 0))],
        out_specs=pl.BlockSpec((bb * S_PAD, MEM_W), lambda i: (i, 0)),
        out_shape=jax.ShapeDtypeStruct((nb * S_PAD, MEM_W), f32),
        compiler_params=_cparams(("parallel",)),
        name="mem_attn_sample",
    )(P, ckt, cvt)


def _delta_kernel(q_ref, k_ref, v_ref, z_ref, ba_ref, buf_ref, cw_ref, ab_ref, go_ref, s0_ref,
                  o_ref, sout_ref, qs_ref, ks_ref, vs_ref, gs_ref, bs_ref, tail_ref, s_ref,
                  *, TL, C, valid):
    t = pl.program_id(1)

    @pl.when(t == 0)
    def _():
        tail_ref[...] = buf_ref[...]
        s_ref[...] = s0_ref[...]

    row8 = lax.broadcasted_iota(jnp.int32, (SUBLANES, A_W), 0)

    def conv(x_ref, c0):
        x = x_ref[...]
        b8 = tail_ref[:, c0:c0 + A_W]
        c8 = cw_ref[:, c0:c0 + A_W]
        acc = x * c8[3:4, :]
        for s in range(1, 4):
            r = pltpu.roll(x, s, axis=0)
            fix = jnp.where(row8 < s, pltpu.roll(b8, s, axis=0), r[0:SUBLANES])
            r = jnp.concatenate([fix, r[SUBLANES:]], axis=0) if TL > SUBLANES else fix
            acc = acc + r * c8[3 - s:4 - s, :]
        tail_ref[:, c0:c0 + A_W] = x[TL - SUBLANES:TL]
        return _silu(acc)

    q = conv(q_ref, 0)
    k = conv(k_ref, A_W)
    v = conv(v_ref, 2 * A_W)
    ba = ba_ref[...]
    ab = ab_ref[...]
    beta_all = _sigmoid(ba)
    g_all = -jnp.exp(ab[0:1, :]) * _softplus(ba + ab[1:2, :])
    if valid < TL:
        live = lax.broadcasted_iota(jnp.int32, (TL, A_W), 0) < valid
        live1 = lax.broadcasted_iota(jnp.int32, (TL, LANES), 0) < valid
        k = jnp.where(live, k, 0.0)
        v = jnp.where(live, v, 0.0)
        beta_all = jnp.where(live1, beta_all, 0.0)
        g_all = jnp.where(live1, g_all, 0.0)
    rowc = lax.broadcasted_iota(jnp.int32, (TL, LANES), 0) & (C - 1)
    gc_all = g_all
    s = 1
    while s < C:
        gc_all = gc_all + jnp.where(rowc >= s, pltpu.roll(gc_all, s, axis=0), 0.0)
        s *= 2
    vs_ref[...] = v
    for h in range(A_HEADS):
        sl = slice(h * LANES, (h + 1) * LANES)
        qh = q[:, sl]
        kh = k[:, sl]
        qs_ref[:, sl] = qh * lax.rsqrt(jnp.sum(qh * qh, axis=-1, keepdims=True) + EPS) * (LANES ** -0.5)
        ks_ref[:, sl] = kh * lax.rsqrt(jnp.sum(kh * kh, axis=-1, keepdims=True) + EPS)
        gs_ref[:, sl] = jnp.broadcast_to(gc_all[:, A_HEADS + h:A_HEADS + h + 1], (TL, LANES))
        bs_ref[:, sl] = jnp.broadcast_to(beta_all[:, h:h + 1], (TL, LANES))

    ii = lax.broadcasted_iota(jnp.int32, (C, C), 0)
    jj = lax.broadcasted_iota(jnp.int32, (C, C), 1)
    tri = ii >= jj
    strict = ii > jj
    eye = jnp.where(ii == jj, 1.0, 0.0)
    gout = go_ref[...]
    nsteps = int(math.log2(C)) - 1

    heads = range(A_HEADS)
    lanes_of = [slice(h * LANES, (h + 1) * LANES) for h in heads]

    cpi = 2 if TL // C >= 2 else 1

    def chunks(c, carry):
        rows_of = [pl.ds(pl.multiple_of((c * cpi + i) * C, C), C) for i in range(cpi)]
        units = [(rows, h) for rows in rows_of for h in heads]
        U = range(len(units))
        qc = [qs_ref[rows, lanes_of[h]] for rows, h in units]
        kc = [ks_ref[rows, lanes_of[h]] for rows, h in units]
        vc = [vs_ref[rows, lanes_of[h]] for rows, h in units]
        gcc = [gs_ref[rows, lanes_of[h]] for rows, h in units]
        bc = [bs_ref[rows, lanes_of[h]] for rows, h in units]
        kb = [kc[u] * bc[u] for u in U]
        kq = [_nt(jnp.concatenate([kb[u], qc[u]], axis=0).astype(bf16), kc[u].astype(bf16)) for u in U]
        decay = []
        for u in U:
            gi = gcc[u][:, 0:C]
            gj = jnp.sum(jnp.where(ii == jj, gi, 0.0), axis=0, keepdims=True)
            decay.append(jnp.where(tri, jnp.exp(jnp.where(tri, gi - gj, 0.0)), 0.0))
        n = [jnp.where(strict, -(kq[u][0:C] * decay[u]), 0.0) for u in U]
        a = [jnp.where(tri, kq[u][C:2 * C] * decay[u], 0.0).astype(bf16) for u in U]
        x = [eye + n[u] for u in U]
        p = n
        for _ in range(nsteps):
            pb = [p[u].astype(bf16) for u in U]
            p = [_mm(pb[u], pb[u]) for u in U]
            x = [x[u] + _mm(x[u].astype(bf16), p[u].astype(bf16)) for u in U]
        eg = [jnp.exp(gcc[u]) for u in U]
        uw = [_mm(x[u].astype(bf16), jnp.concatenate([vc[u] * bc[u], kb[u] * eg[u]], axis=1).astype(bf16))
              for u in U]
        wqin = [jnp.concatenate([uw[u][:, LANES:], qc[u] * eg[u]], axis=0).astype(bf16) for u in U]
        glast = [gcc[u][C - 1:C, :] for u in U]
        kg = [(kc[u] * jnp.exp(glast[u] - gcc[u])).astype(bf16) for u in U]
        for i, rows in enumerate(rows_of):
            us = [i * A_HEADS + h for h in heads]
            S = [s_ref[h] for h in heads]
            wq = [_mm(wqin[u], S[h].astype(bf16)) for h, u in zip(heads, us)]
            vnb = [(uw[u][:, :LANES] - wq[h][0:C]).astype(bf16) for h, u in zip(heads, us)]
            av = [_mm(a[u], vnb[h]) for h, u in zip(heads, us)]
            kv_new = [_tn(kg[u], vnb[h]) for h, u in zip(heads, us)]
            for h, u in zip(heads, us):
                s_ref[h] = S[h] * jnp.exp(glast[u]) + kv_new[h]
                o_ref[rows, lanes_of[h]] = _rms(wq[h][C:2 * C] + av[h], gout) * _silu(z_ref[rows, lanes_of[h]])
        return carry

    if TL // C == cpi:
        chunks(0, 0)
    else:
        lax.fori_loop(0, TL // (C * cpi), chunks, 0)

    @pl.when(t == pl.num_programs(1) - 1)
    def _():
        sout_ref[...] = s_ref[...]


def delta_mixer(P, row0, nb, L, TL, C, valid, buf8, conv8, ab8, gout, s0, layer):
    nt = L // TL
    rb0 = row0 // TL
    tile = lambda c: pl.BlockSpec((TL, A_W), lambda b, t: (rb0 + b * nt + t, c))
    state = pl.BlockSpec((None, A_HEADS, LANES, LANES), lambda b, t: (b, 0, 0, 0))
    state_in = pl.BlockSpec((None, None, A_HEADS, LANES, LANES), lambda b, t: (layer, b, 0, 0, 0))
    return pl.pallas_call(
        functools.partial(_delta_kernel, TL=TL, C=C, valid=valid),
        grid=(nb, nt),
        in_specs=[tile(0), tile(1), tile(2), tile(3),
                  pl.BlockSpec((TL, LANES), lambda b, t: (rb0 + b * nt + t, A_BA_BLK)),
                  pl.BlockSpec((None, SUBLANES, CONV_DIM), lambda b, t: (b, 0, 0)),
                  pl.BlockSpec((SUBLANES, CONV_DIM), lambda b, t: (0, 0)),
                  pl.BlockSpec((SUBLANES, LANES), lambda b, t: (0, 0)),
                  pl.BlockSpec((1, LANES), lambda b, t: (0, 0)),
                  state_in],
        out_specs=(pl.BlockSpec((TL, A_W), lambda b, t: (b * nt + t, 0)), state),
        out_shape=(jax.ShapeDtypeStruct((nb * L, A_W), f32),
                   jax.ShapeDtypeStruct((nb, A_HEADS, LANES, LANES), f32)),
        scratch_shapes=[pltpu.VMEM((TL, A_W), f32)] * 5
        + [pltpu.VMEM((SUBLANES, CONV_DIM), f32), pltpu.VMEM((A_HEADS, LANES, LANES), f32)],
        compiler_params=_cparams(("parallel", "arbitrary")),
        name="delta_mixer",
    )(P, P, P, P, P, buf8, conv8, ab8, gout, s0)


def _attn_residual(refs, ntp):
    h_ref, mop_ref, mos_ref, omp_ref, oms_ref, wo1_ref, wo2_ref = refs
    is_p = pl.program_id(0) < ntp
    mo = jnp.where(is_p, mop_ref[...], mos_ref[...])
    om = jnp.where(is_p, omp_ref[...], oms_ref[...])
    return h_ref[...] + _mm(mo.astype(bf16), wo1_ref[...]) + _mm(om.astype(bf16), wo2_ref[...])


def _ffn_kernel(*refs, ntp):
    g_ref, wg_ref, wu_ref, wd_ref, o_ref, xn_ref, acc_ref = refs[7:]
    k = pl.program_id(1)

    @pl.when(k == 0)
    def _():
        h1 = _attn_residual(refs[:7], ntp)
        acc_ref[...] = h1
        xn_ref[...] = _rms(h1, g_ref[...]).astype(bf16)

    xn = xn_ref[...]
    a = _silu(_mm(xn, wg_ref[...])) * _mm(xn, wu_ref[...])
    acc_ref[...] += _mm(a.astype(bf16), wd_ref[...])

    @pl.when(k == pl.num_programs(1) - 1)
    def _():
        o_ref[...] = acc_ref[...]


def _outproj_specs(tm, K, mw, ow, ntp, layer):
    prow = lambda i, *_: (jnp.minimum(i, ntp - 1), 0)
    srow = lambda i, *_: (jnp.maximum(i - ntp, 0), 0)
    return [pl.BlockSpec((tm, K), lambda i, *_: (i, 0)),
            pl.BlockSpec((tm, mw), prow), pl.BlockSpec((tm, mw), srow),
            pl.BlockSpec((tm, ow), prow), pl.BlockSpec((tm, ow), srow),
            pl.BlockSpec((None, mw, K), lambda i, *_: (layer, 0, 0)),
            pl.BlockSpec((None, ow, K), lambda i, *_: (layer, mw // ow, 0)),
            pl.BlockSpec((1, K), lambda i, *_: (0, 0))]


def outproj_ffn(H, mo, om, w_out, g, w_gu, w_down, layer, ffn_layer, tm, tf):
    T, K = H.shape
    FF = w_down.shape[1]
    nk = FF // tf
    mw, ow = mo[0].shape[1], om[0].shape[1]
    ntp = mo[0].shape[0] // tm
    return pl.pallas_call(
        functools.partial(_ffn_kernel, ntp=ntp),
        grid=(T // tm, nk),
        in_specs=_outproj_specs(tm, K, mw, ow, ntp, layer)
        + [pl.BlockSpec((None, K, tf), lambda i, k: (ffn_layer, 0, k)),
           pl.BlockSpec((None, K, tf), lambda i, k: (ffn_layer, 0, nk + k)),
           pl.BlockSpec((None, tf, K), lambda i, k: (ffn_layer, k, 0))],
        out_specs=pl.BlockSpec((tm, K), lambda i, k: (i, 0)),
        out_shape=jax.ShapeDtypeStruct((T, K), f32),
        scratch_shapes=[pltpu.VMEM((tm, K), bf16), pltpu.VMEM((tm, K), f32)],
        compiler_params=_cparams(("parallel", "arbitrary")),
        name="outproj_ffn",
    )(H, mo[0], mo[1], om[0], om[1], w_out, w_out, g, w_gu, w_gu, w_down)


def _router_kernel(*refs, ntp):
    g_ref, wr_ref, h1_ref, xn_ref, r_ref, cnt_ref = refs[7:]
    i = pl.program_id(0)
    tm = h1_ref.shape[0]

    @pl.when(i == 0)
    def _():
        cnt_ref[...] = jnp.zeros_like(cnt_ref)

    h1 = _attn_residual(refs[:7], ntp)
    h1_ref[...] = h1
    xn = _rms(h1, g_ref[...])
    xn_ref[...] = xn.astype(bf16)
    lane = lax.broadcasted_iota(jnp.int32, (tm, LANES), 1)
    xh = xn.astype(bf16)
    xl = (xn - xh.astype(f32)).astype(bf16)
    wr = wr_ref[...]
    wh = wr.astype(bf16)
    wl = (wr - wh.astype(f32)).astype(bf16)
    logits = _mm(xh, wh) + _mm(xh, wl) + _mm(xl, wh)
    logits = jnp.where(lane < N_EXPERTS, logits, -jnp.inf)
    m1 = jnp.max(logits, axis=-1, keepdims=True)
    i1 = jnp.min(jnp.where(logits == m1, lane, LANES), axis=-1, keepdims=True)
    rest = jnp.where(lane == i1, -jnp.inf, logits)
    m2 = jnp.max(rest, axis=-1, keepdims=True)
    i2 = jnp.min(jnp.where(rest == m2, lane, LANES), axis=-1, keepdims=True)
    e2 = jnp.exp(m2 - m1)
    w1 = 1.0 / (1.0 + e2)
    w2 = e2 / (1.0 + e2)
    hot = jnp.where((lane == i1) | (lane == i2), 1.0, 0.0)
    ri = lax.broadcasted_iota(jnp.int32, (tm, tm), 0)
    ci = lax.broadcasted_iota(jnp.int32, (tm, tm), 1)
    before = jnp.where(ri > ci, 1.0, 0.0).astype(bf16)
    cum = _mm(before, hot.astype(bf16)) + cnt_ref[...]
    r1 = jnp.sum(jnp.where(lane == i1, cum, 0.0), axis=-1, keepdims=True)
    r2 = jnp.sum(jnp.where(lane == i2, cum, 0.0), axis=-1, keepdims=True)
    cnt_ref[...] += jnp.sum(hot, axis=0, keepdims=True)
    cols = (i1.astype(f32), i2.astype(f32), w1, w2, r1, r2)
    r = jnp.zeros((tm, LANES), f32)
    for c, val in enumerate(cols):
        r = jnp.where(lane == c, val, r)
    r_ref[...] = r


def outproj_router(H, mo, om, w_out, g, w_router, layer, moe_layer, tm):
    T, K = H.shape
    mw, ow = mo[0].shape[1], om[0].shape[1]
    ntp = mo[0].shape[0] // tm
    return pl.pallas_call(
        functools.partial(_router_kernel, ntp=ntp),
        grid=(T // tm,),
        in_specs=_outproj_specs(tm, K, mw, ow, ntp, layer)
        + [pl.BlockSpec((None, K, LANES), lambda i: (moe_layer, 0, 0))],
        out_specs=(pl.BlockSpec((tm, K), lambda i: (i, 0)), pl.BlockSpec((tm, K), lambda i: (i, 0)),
                   pl.BlockSpec((tm, LANES), lambda i: (i, 0))),
        out_shape=(jax.ShapeDtypeStruct((T, K), f32), jax.ShapeDtypeStruct((T, K), bf16),
                   jax.ShapeDtypeStruct((T, LANES), f32)),
        scratch_shapes=[pltpu.VMEM((1, LANES), f32)],
        compiler_params=_cparams(("arbitrary",)),
        name="outproj_router",
    )(H, mo[0], mo[1], om[0], om[1], w_out, w_out, g, w_router)


def _moe_group_kernel(te_ref, nu_ref, x_ref, sw_ref, wg_ref, wu_ref, wd_ref, o_ref, acc_ref):
    del te_ref
    i = pl.program_id(0)
    k = pl.program_id(1)

    @pl.when(k == 0)
    def _():
        acc_ref[...] = jnp.zeros_like(acc_ref)

    @pl.when(i < nu_ref[0])
    def _():
        x = x_ref[...]
        a = _silu(_mm(x, wg_ref[...])) * _mm(x, wu_ref[...])
        acc_ref[...] += _mm(a.astype(bf16), wd_ref[...])

    @pl.when(k == pl.num_programs(1) - 1)
    def _():
        o_ref[...] = (acc_ref[...] * sw_ref[...]).astype(o_ref.dtype)


def moe_grouped(xg, sw, tile_e, n_used, w_gu, w_down, moe_layer, tm, tf):
    NP, K = xg.shape
    FF = w_down.shape[2]
    nk = FF // tf

    def kk(i, k, nu):
        return jnp.where(i < nu[0], k, nk - 1)

    grid_spec = pltpu.PrefetchScalarGridSpec(
        num_scalar_prefetch=2, grid=(NP // tm, nk),
        in_specs=[pl.BlockSpec((tm, K), lambda i, k, te, nu: (i, 0)),
                  pl.BlockSpec((tm, 1), lambda i, k, te, nu: (i, 0)),
                  pl.BlockSpec((None, None, K, tf), lambda i, k, te, nu: (moe_layer, te[i], 0, kk(i, k, nu))),
                  pl.BlockSpec((None, None, K, tf), lambda i, k, te, nu: (moe_layer, te[i], 0, nk + kk(i, k, nu))),
                  pl.BlockSpec((None, None, tf, K), lambda i, k, te, nu: (moe_layer, te[i], kk(i, k, nu), 0))],
        out_specs=pl.BlockSpec((tm, K), lambda i, k, te, nu: (i, 0)),
        scratch_shapes=[pltpu.VMEM((tm, K), f32)])
    return pl.pallas_call(
        _moe_group_kernel, grid_spec=grid_spec,
        out_shape=jax.ShapeDtypeStruct((NP, K), bf16),
        compiler_params=_cparams(("parallel", "arbitrary")),
        name="moe_grouped",
    )(tile_e, n_used, xg, sw, w_gu, w_gu, w_down)


def _dispatch(r, tm):
    T = r.shape[0]
    A = 2 * T
    ntiles = A // tm + N_EXPERTS
    NP = ntiles * tm
    e_flat = r[:, 0:2].astype(jnp.int32).reshape(A)
    w_flat = r[:, 2:4].reshape(A)
    rank = r[:, 4:6].astype(jnp.int32)
    counts = jnp.sum((e_flat[:, None] == jnp.arange(N_EXPERTS)[None, :]).astype(jnp.int32), axis=0)
    pcounts = (counts + tm - 1) // tm * tm
    pends = jnp.cumsum(pcounts)
    pstarts = pends - pcounts
    starts = jnp.cumsum(counts) - counts
    n_used = pends[-1] // tm
    a_sorted = jnp.sort(e_flat * A + jnp.arange(A, dtype=jnp.int32)) % A
    tiles = jnp.arange(ntiles, dtype=jnp.int32)
    tile_e = jnp.minimum(jnp.searchsorted(pends, tiles * tm, side='right'), N_EXPERTS - 1).astype(jnp.int32)
    tile_e = jnp.where(tiles < n_used, tile_e, tile_e[jnp.maximum(n_used - 1, 0)])
    slot = jnp.arange(NP, dtype=jnp.int32)
    e_s = tile_e[slot // tm]
    rr = slot - pstarts[e_s]
    ok = (rr < counts[e_s]) & (slot < pends[-1])
    a_s = a_sorted[jnp.clip(starts[e_s] + rr, 0, A - 1)]
    src = a_s // 2
    sw = jnp.where(ok, w_flat[a_s], 0.0).reshape(NP, 1)
    picks = r[:, 0:2].astype(jnp.int32)
    slots = pstarts[picks] + rank
    hot = jnp.any(picks[:, :, None] == jnp.arange(N_EXPERTS)[None, None, :], axis=1).astype(jnp.int32)
    tile_cnt = jnp.sum(hot.reshape(T // tm, tm, N_EXPERTS), axis=1)
    lo = pstarts[None, :] + jnp.cumsum(tile_cnt, axis=0) - tile_cnt
    hi = lo + tile_cnt
    base = jnp.minimum(lo // COMBINE_ALIGN, (NP - (tm + COMBINE_ALIGN)) // COMBINE_ALIGN)
    windows = tuple(a.reshape(-1).astype(jnp.int32) for a in (base, lo, hi))
    return src, sw, tile_e, n_used.reshape(1).astype(jnp.int32), slots.astype(jnp.int32), windows


def _combine_kernel(base_ref, lo_ref, hi_ref, h1_ref, slot_ref, win_ref, o_ref, acc_ref):
    i = pl.program_id(0)
    e = pl.program_id(1)
    tm = h1_ref.shape[0]
    W = win_ref.shape[0]

    @pl.when(e == 0)
    def _():
        acc_ref[...] = h1_ref[...]

    idx = i * N_EXPERTS + e
    lo = lo_ref[idx]
    hi = hi_ref[idx]

    @pl.when(hi > lo)
    def _():
        s = slot_ref[...]
        s = jnp.where((s >= lo) & (s < hi), s - base_ref[idx] * COMBINE_ALIGN, -1)
        col = lax.broadcasted_iota(jnp.int32, (tm, W), 1)
        onehot = jnp.where(col == s[:, 0:1], 1.0, jnp.where(col == s[:, 1:2], 1.0, 0.0)).astype(bf16)
        acc_ref[...] += _mm(onehot, win_ref[...])

    @pl.when(e == pl.num_programs(1) - 1)
    def _():
        o_ref[...] = acc_ref[...]


def moe_combine(h1, yg, slots, windows, tm):
    T, K = h1.shape
    W = tm + COMBINE_ALIGN
    grid_spec = pltpu.PrefetchScalarGridSpec(
        num_scalar_prefetch=3, grid=(T // tm, N_EXPERTS),
        in_specs=[pl.BlockSpec((tm, K), lambda i, e, base, lo, hi: (i, 0)),
                  pl.BlockSpec((tm, 2), lambda i, e, base, lo, hi: (i, 0)),
                  pl.BlockSpec((pl.Element(W), pl.Element(K)),
                               lambda i, e, base, lo, hi: (base[i * N_EXPERTS + e] * COMBINE_ALIGN, 0))],
        out_specs=pl.BlockSpec((tm, K), lambda i, e, base, lo, hi: (i, 0)),
        scratch_shapes=[pltpu.VMEM((tm, K), f32)])
    return pl.pallas_call(
        _combine_kernel, grid_spec=grid_spec,
        out_shape=jax.ShapeDtypeStruct((T, K), f32),
        compiler_params=_cparams(("parallel", "arbitrary")),
        name="moe_combine",
    )(*windows, h1, slots, yg)


def outproj_moe(H, mo, om, w_out, g, w_router, w_gu, w_down, layer, moe_layer, tm, tf):
    h1, xn, r = outproj_router(H, mo, om, w_out, g, w_router, layer, moe_layer, tm)
    src, sw, tile_e, n_used, slots, windows = _dispatch(r, tm)
    yg = moe_grouped(jnp.take(xn, src, axis=0), sw, tile_e, n_used, w_gu, w_down, moe_layer, tm, tf)
    return moe_combine(h1, yg, slots, windows, tm)


def _shared_kv_kernel(x_ref, g_ref, w_ref, cos_ref, sin_ref, ck_ref, cv_ref, sk_ref, sv_ref, wk_ref, wv_ref):
    y = _mm(_rms(x_ref[...], g_ref[...]).astype(bf16), w_ref[...])
    cos = cos_ref[...]
    sin = sin_ref[...]
    ck_ref[...] = y[:, 0 * LANES:1 * LANES]
    cv_ref[...] = y[:, 1 * LANES:2 * LANES]
    sk_ref[...] = _rope_lanes(y[:, 2 * LANES:3 * LANES], cos, sin)
    sv_ref[...] = y[:, 3 * LANES:4 * LANES]
    wk_ref[...] = _rope_lanes(y[:, 4 * LANES:5 * LANES], cos, sin)
    wv_ref[...] = y[:, 5 * LANES:6 * LANES]


def shared_kv(H, g, w, cos, sin, tm):
    T, K = H.shape
    row = pl.BlockSpec((tm, LANES), lambda i: (i, 0))
    out = jax.ShapeDtypeStruct((T, LANES), f32)
    return pl.pallas_call(
        _shared_kv_kernel,
        grid=(T // tm,),
        in_specs=[pl.BlockSpec((tm, K), lambda i: (i, 0)), pl.BlockSpec((1, K), lambda i: (0, 0)),
                  pl.BlockSpec((K, 6 * LANES), lambda i: (0, 0)), row, row],
        out_specs=(row,) * 6,
        out_shape=(out,) * 6,
        compiler_params=_cparams(("parallel",)),
        name="shared_kv",
    )(H, g, w, cos, sin)


def _compress_kernel(*refs, npages):
    nin = 2 * npages
    k_strips = jnp.concatenate([r[...] for r in refs[0:npages]], axis=0) if npages > 1 else refs[0][...]
    v_strips = jnp.concatenate([r[...] for r in refs[npages:nin]], axis=0) if npages > 1 else refs[npages][...]
    (pek_ref, pev_ref, w1k_ref, w1v_ref, w2k_ref, w2v_ref, cos_ref, sin_ref, ck_ref, cv_ref) = refs[nin:]
    n = k_strips.shape[0]
    half = 2 * CMP_HID
    row = lax.broadcasted_iota(jnp.int32, (n, LANES), 0)

    def tokens(strips, pe_ref, w1_ref, w2_ref):
        top = _mm((strips + pe_ref[0:1, :]).astype(bf16), w1_ref[:, 0:half])
        bot = _mm((strips + pe_ref[1:2, :]).astype(bf16), w1_ref[:, half:2 * half])
        hid = top + pltpu.roll(bot, n - 1, axis=0)
        out = _mm(_silu(hid).astype(bf16), w2_ref[...])
        return jnp.where(row < n - 1, out, 0.0)

    ck_ref[...] = _rope_lanes(tokens(k_strips, pek_ref, w1k_ref, w2k_ref), cos_ref[...], sin_ref[...])
    cv_ref[...] = tokens(v_strips, pev_ref, w1v_ref, w2v_ref)


def _drop_first(fn, _prefetch_ref, *refs):
    return fn(*refs)


def compress(k_src, v_src, k_maps, v_maps, strip_rows, nb, consts, prefetch=None):
    npages = len(k_maps)
    n = npages * strip_rows
    pek, pev, w1k, w1v, w2k, w2v, cos, sin = consts
    flat = 16 * LANES
    const2 = lambda shape: pl.BlockSpec(shape, lambda b, *_: (0, 0))
    in_specs = ([pl.BlockSpec((None, strip_rows, flat), m) for m in k_maps]
                + [pl.BlockSpec((None, strip_rows, flat), m) for m in v_maps]
                + [const2((2, flat)), const2((2, flat)), const2((flat, 4 * CMP_HID)), const2((flat, 4 * CMP_HID)),
                   const2((2 * CMP_HID, LANES)), const2((2 * CMP_HID, LANES)), const2((n, LANES)), const2((n, LANES))])
    out_spec = pl.BlockSpec((None, n, LANES), lambda b, *_: (b, 0, 0))
    out = jax.ShapeDtypeStruct((nb, n, LANES), f32)
    grid_spec = pltpu.PrefetchScalarGridSpec(
        num_scalar_prefetch=0 if prefetch is None else 1, grid=(nb,),
        in_specs=in_specs, out_specs=(out_spec, out_spec))
    args = ([] if prefetch is None else [prefetch]) + [k_src] * npages + [v_src] * npages + [pek, pev, w1k, w1v, w2k, w2v, cos, sin]
    kern = functools.partial(_compress_kernel, npages=npages)
    if prefetch is not None:
        kern = functools.partial(_drop_first, kern)
    return pl.pallas_call(
        kern, grid_spec=grid_spec, out_shape=(out, out),
        compiler_params=_cparams(("parallel",)), name="compress",
    )(*args)


def _nsa_queries(p, cos, sin):
    tq = p.shape[0]
    lane = lax.broadcasted_iota(jnp.int32, (tq, LANES), 1)
    rows = []
    for head in range(B_HEADS):
        grp = head // HPG
        blk = MEM_W // LANES + head // 2
        x = _rope_lanes(p[:, blk * LANES:(blk + 1) * LANES], cos, sin) * (HEAD_DIM ** -0.5)
        if head % 2 != grp:
            x = pltpu.roll(x, HEAD_DIM, axis=1)
        keep = (lane >= grp * HEAD_DIM) & (lane < (grp + 1) * HEAD_DIM)
        rows.append(jnp.where(keep, x, 0.0))
    return jnp.concatenate(rows, axis=0)


def _masked_softmax_parts(s, mask):
    sm = jnp.where(mask, s, NEG)
    m = jnp.max(sm, axis=-1, keepdims=True)
    p = jnp.where(mask, jnp.exp(sm - m), 0.0)
    return p, jnp.sum(p, axis=-1, keepdims=True)


def _safe_div(o, l):
    return jnp.where(l > 0.0, o / jnp.where(l > 0.0, l, 1.0), 0.0)


def _split3(x):
    hi = x.astype(bf16)
    r1 = x - hi.astype(f32)
    mid = r1.astype(bf16)
    lo = (r1 - mid.astype(f32)).astype(bf16)
    return hi, mid, lo


def _select_bias(pcn, qpos_row, past_blocks, mt_ref, eye_ref, nsel):
    tq = qpos_row.shape[1]
    nr = -(-nsel // SUBLANES) * SUBLANES
    mt = mt_ref[0:nr, :]
    blk = lax.broadcasted_iota(jnp.int32, (nr, tq), 0)
    lane = lax.broadcasted_iota(jnp.int32, (tq, LANES), 1)
    cur = qpos_row // SEL_BLK
    forced = (blk == 0) | (blk == cur) | (blk == cur - 1)
    valid = blk * SEL_BLK <= qpos_row
    stacks, picks = [], []
    for grp in range(2):
        psum = pcn[grp * HPG * tq:(grp * HPG + 1) * tq]
        for hh in range(1, HPG):
            psum = psum + pcn[(grp * HPG + hh) * tq:(grp * HPG + hh + 1) * tq]
        hi, mid, lo = _split3(psum)
        imp = _nt(mt, hi) + _nt(mt, mid) + _nt(mt, lo)
        imp = jnp.where(forced, 1e9, jnp.where(valid, imp, -1e9))
        rank = jnp.zeros((nr, tq), f32)
        for j in range(nsel):
            rj = imp[j:j + 1, :]
            rank = rank + jnp.where(rj > imp, 1.0, jnp.where(rj == imp, jnp.where(blk > j, 1.0, 0.0), 0.0))
        sel_t = jnp.where((rank < SEL_TOP) & (blk < nsel), 1.0, 0.0).astype(bf16)
        pick = _tn(sel_t, eye_ref[0:nr, :])
        picks.append(pick)
        stacks += [jnp.where((pick > 0.5) & (lane < past_blocks), 0.0, NEG)] * HPG
    return jnp.concatenate(stacks, axis=0), picks


def _nsa_combine(p, o_c, o_s, o_w):
    tq = p.shape[0]
    gl = _sigmoid(p[:, (MEM_W + B_W):(MEM_W + B_W) + LANES])

    def gate(j):
        return jnp.concatenate([gl[:, 3 * h + j:3 * h + j + 1] for h in range(B_HEADS)], axis=0)

    comb = gate(0) * o_c + gate(1) * o_s + gate(2) * o_w
    lane = lax.broadcasted_iota(jnp.int32, (tq, LANES), 1)
    cols = []
    for c in range(B_HEADS // 2):
        lo = comb[2 * c * tq:(2 * c + 1) * tq]
        hi = comb[(2 * c + 1) * tq:(2 * c + 2) * tq]
        if (2 * c) // HPG == 1:
            lo = pltpu.roll(lo, HEAD_DIM, axis=1)
        if (2 * c + 1) // HPG == 0:
            hi = pltpu.roll(hi, HEAD_DIM, axis=1)
        cols.append(jnp.where(lane < HEAD_DIM, lo, hi))
    return jnp.concatenate(cols, axis=1)


def _nsa_prompt_kernel(p_ref, cos_ref, sin_ref, ck_ref, cv_ref, sk_ref, sv_ref, wk_ref, wv_ref, mt_ref, et_ref,
                       eye_ref, o_ref, *, tq, kc, nsel):
    qb = pl.program_id(1)
    s0 = qb * tq
    R = B_HEADS * tq
    p = p_ref[...]
    qf = _nsa_queries(p, cos_ref[...], sin_ref[...])
    qb16 = qf.astype(bf16)
    ii = lax.broadcasted_iota(jnp.int32, (tq, tq), 0)
    jj = lax.broadcasted_iota(jnp.int32, (tq, tq), 1)
    qposR = s0 + jnp.concatenate([lax.broadcasted_iota(jnp.int32, (tq, 1), 0)] * B_HEADS, axis=0)
    lane = lax.broadcasted_iota(jnp.int32, (R, LANES), 1)
    cmask = ((lane * CMP_STRIDE + (CMP_BLK - 1)) <= qposR) & (lane < ck_ref.shape[0] - 1)
    pc, lc = _masked_softmax_parts(_nt(qb16, ck_ref[...].astype(bf16)), cmask)
    pcn = _safe_div(pc, lc)
    o_c = _mm(pcn.astype(bf16), cv_ref[...].astype(bf16))
    qpos_row = s0 + lax.broadcasted_iota(jnp.int32, (1, tq), 1)
    bias, _ = _select_bias(pcn, qpos_row, s0 // SEL_BLK, mt_ref, eye_ref, nsel)
    qsel = jnp.concatenate([qb16, bias.astype(bf16)], axis=1)
    sd = _nt(qb16, sk_ref[pl.ds(s0, tq), :].astype(bf16)).reshape(B_HEADS, tq, tq)
    sd = jnp.where((ii >= jj)[None], sd, NEG).reshape(R, tq)
    m0 = jnp.max(sd, axis=-1, keepdims=True)
    pd = jnp.exp(sd - m0)
    init = (m0, jnp.sum(pd, axis=-1, keepdims=True), _mm(pd.astype(bf16), sv_ref[pl.ds(s0, tq), :].astype(bf16)))

    def body(c, carry):
        m_run, l_run, acc = carry
        k0 = pl.multiple_of(c * kc, kc)
        ka = jnp.concatenate([sk_ref[pl.ds(k0, kc), :].astype(bf16), et_ref[pl.ds(k0, kc), :]], axis=1)
        s = _nt(qsel, ka)
        m_new = jnp.maximum(m_run, jnp.max(s, axis=-1, keepdims=True))
        alpha = jnp.exp(m_run - m_new)
        pe = jnp.exp(s - m_new)
        l_new = alpha * l_run + jnp.sum(pe, axis=-1, keepdims=True)
        acc_new = alpha * acc + _mm(pe.astype(bf16), sv_ref[pl.ds(k0, kc), :].astype(bf16))
        return m_new, l_new, acc_new

    _, l_run, acc = lax.fori_loop(0, (s0 + kc - 1) // kc, body, init)
    o_s = acc / l_run
    wlen = WINDOW + tq
    wrow = lax.broadcasted_iota(jnp.int32, (wlen, LANES), 0)
    wlane = lax.broadcasted_iota(jnp.int32, (wlen, LANES), 1)
    is_pad = jnp.where((wrow + (s0 - WINDOW) < 0) & (wlane == 0), 1.0, 0.0).astype(bf16)
    kw = jnp.concatenate([wk_ref[pl.ds(s0, wlen), :].astype(bf16), is_pad], axis=1)
    qwin = jnp.concatenate([qb16, jnp.where(lane == 0, NEG, 0.0).astype(bf16)], axis=1)
    sw = _nt(qwin, kw).reshape(B_HEADS, tq, wlen)
    far = jnp.where((jj > ii)[None], sw[:, :, 0:tq], NEG)
    near = jnp.where((jj <= ii)[None], sw[:, :, wlen - tq:wlen], NEG)
    sw = jnp.concatenate([far, sw[:, :, tq:wlen - tq], near], axis=2).reshape(R, wlen)
    pw = jnp.exp(sw - jnp.max(sw, axis=-1, keepdims=True))
    o_w = _mm(pw.astype(bf16), wv_ref[pl.ds(s0, wlen), :].astype(bf16)) / jnp.sum(pw, axis=-1, keepdims=True)
    o_ref[...] = _nsa_combine(p, o_c, o_s, o_w)


def nsa_prompt(P, cos, sin, ck, cv, sk, sv, wk_pad, wv_pad, mt, et, eye, nb, L, tq, kc):
    nq = L // tq
    nsel = L // SEL_BLK
    seq = pl.BlockSpec((L, LANES), lambda b, i: (b, 0))
    win = pl.BlockSpec((None, WINDOW + L, LANES), lambda b, i: (b, 0, 0))
    cmp_spec = pl.BlockSpec((None, LANES, LANES), lambda b, i: (b, 0, 0))
    const = lambda shape: pl.BlockSpec(shape, lambda b, i: (0, 0))
    return pl.pallas_call(
        functools.partial(_nsa_prompt_kernel, tq=tq, kc=kc, nsel=nsel),
        grid=(nb, nq),
        in_specs=[pl.BlockSpec((tq, B_IN_PAD), lambda b, i: (b * nq + i, 0)),
                  pl.BlockSpec((tq, LANES), lambda b, i: (i, 0)),
                  pl.BlockSpec((tq, LANES), lambda b, i: (i, 0)),
                  cmp_spec, cmp_spec, seq, seq, win, win,
                  const((LANES, LANES)), const((L, LANES)), const((LANES, LANES))],
        out_specs=pl.BlockSpec((tq, B_W), lambda b, i: (b * nq + i, 0)),
        out_shape=jax.ShapeDtypeStruct((nb * L, B_W), f32),
        compiler_params=_cparams(("parallel", "parallel")),
        name="nsa_prompt",
    )(P, cos, sin, ck, cv, sk, sv, wk_pad, wv_pad, mt, et, eye)


def _nsa_sample_kernel(*refs, npages, past):
    refs = refs[1:]
    kpages = refs[0:npages]
    vpages = refs[npages:2 * npages]
    (p_ref, cos_ref, sin_ref, ck_ref, cv_ref, nsk_ref, nsv_ref, cwk_ref, cwv_ref, nwk_ref, nwv_ref,
     mt_ref, et_ref, eye_ref, o_ref) = refs[2 * npages:]
    tq = S_PAD
    R = B_HEADS * tq
    p = p_ref[...]
    qb16 = _nsa_queries(p, cos_ref[...], sin_ref[...]).astype(bf16)
    tR = jnp.concatenate([lax.broadcasted_iota(jnp.int32, (tq, 1), 0)] * B_HEADS, axis=0)
    qposR = past + tR
    lane = lax.broadcasted_iota(jnp.int32, (R, LANES), 1)
    cmask = ((lane * CMP_STRIDE + (CMP_BLK - 1)) <= qposR) & (lane < ck_ref.shape[0] - 1)
    pc, lc = _masked_softmax_parts(_nt(qb16, ck_ref[...].astype(bf16)), cmask)
    pcn = _safe_div(pc, lc)
    o_c = _mm(pcn.astype(bf16), cv_ref[...].astype(bf16))
    qpos_row = past + lax.broadcasted_iota(jnp.int32, (1, tq), 1)
    new_blk = past // SEL_BLK
    bias, picks = _select_bias(pcn, qpos_row, new_blk, mt_ref, eye_ref, new_blk + 1)
    new_ok = lax.broadcasted_iota(jnp.int32, (R, tq), 1) <= tR
    ka = jnp.concatenate([jnp.concatenate([r[...] for r in kpages], axis=1).astype(bf16), et_ref[...]], axis=0)
    s_old = _mm(jnp.concatenate([qb16, bias.astype(bf16)], axis=1), ka)
    new_kept = jnp.concatenate([picks[h // HPG][:, new_blk:new_blk + 1] for h in range(B_HEADS)], axis=0) > 0.5
    s_new = jnp.where(new_ok, jnp.where(new_kept, _nt(qb16, nsk_ref[...].astype(bf16)), NEG), NEG)
    m = jnp.maximum(jnp.max(s_old, axis=-1, keepdims=True), jnp.max(s_new, axis=-1, keepdims=True))
    p_old = jnp.exp(s_old - m)
    p_new = jnp.exp(s_new - m)
    l = jnp.sum(p_old, axis=-1, keepdims=True) + jnp.sum(p_new, axis=-1, keepdims=True)
    svb = jnp.concatenate([r[...] for r in vpages], axis=1).astype(bf16)
    o_s = (_nt(p_old.astype(bf16), svb) + _mm(p_new.astype(bf16), nsv_ref[...].astype(bf16))) / l
    wb = cwk_ref.shape[1]
    wdist = qposR - (past - wb + lax.broadcasted_iota(jnp.int32, (R, wb), 1))
    w_old = jnp.where((wdist >= 0) & (wdist < WINDOW), _mm(qb16, cwk_ref[...].astype(bf16)), NEG)
    w_new = jnp.where(new_ok, _nt(qb16, nwk_ref[...].astype(bf16)), NEG)
    mw = jnp.maximum(jnp.max(w_old, axis=-1, keepdims=True), jnp.max(w_new, axis=-1, keepdims=True))
    pw_old = jnp.exp(w_old - mw)
    pw_new = jnp.exp(w_new - mw)
    lw = jnp.sum(pw_old, axis=-1, keepdims=True) + jnp.sum(pw_new, axis=-1, keepdims=True)
    o_w = (_nt(pw_old.astype(bf16), cwv_ref[...].astype(bf16)) + _mm(pw_new.astype(bf16), nwv_ref[...].astype(bf16))) / lw
    o_ref[...] = _nsa_combine(p, o_c, o_s, o_w)


def nsa_sample(P, row0, page_table, pool_k, pool_v, cos, sin, ck, cv, nsk, nsv, cwk, cwv, nwk, nwv, mt, et, eye):
    nb = cwk.shape[0]
    npages = page_table.shape[0] // nb
    past = npages * PAGE
    rb0 = row0 // S_PAD
    page = lambda j: pl.BlockSpec((None, LANES, PAGE), lambda b, pt: (pt[b * npages + j], 0, 0))
    new_rows = pl.BlockSpec((S_PAD, LANES), lambda b, pt: (rb0 + b, 0))
    per_seq = lambda n: pl.BlockSpec((None, LANES, n), lambda b, pt: (b, 0, 0))
    in_specs = ([page(j) for j in range(npages)] + [page(j) for j in range(npages)]
                + [pl.BlockSpec((S_PAD, B_IN_PAD), lambda b, pt: (rb0 + b, 0)),
                   pl.BlockSpec((S_PAD, LANES), lambda b, pt: (0, 0)),
                   pl.BlockSpec((S_PAD, LANES), lambda b, pt: (0, 0)),
                   per_seq(LANES), per_seq(LANES), new_rows, new_rows,
                   per_seq(cwk.shape[2]), per_seq(cwk.shape[2]), new_rows, new_rows,
                   pl.BlockSpec((LANES, LANES), lambda b, pt: (0, 0)),
                   pl.BlockSpec((LANES, past), lambda b, pt: (0, 0)),
                   pl.BlockSpec((LANES, LANES), lambda b, pt: (0, 0))])
    grid_spec = pltpu.PrefetchScalarGridSpec(
        num_scalar_prefetch=1, grid=(nb,), in_specs=in_specs,
        out_specs=pl.BlockSpec((S_PAD, B_W), lambda b, pt: (b, 0)))
    return pl.pallas_call(
        functools.partial(_nsa_sample_kernel, npages=npages, past=past),
        grid_spec=grid_spec,
        out_shape=jax.ShapeDtypeStruct((nb * S_PAD, B_W), f32),
        compiler_params=_cparams(("parallel",)),
        name="nsa_sample",
    )(page_table, *([pool_k] * npages), *([pool_v] * npages), P, cos, sin, ck, cv, nsk, nsv, cwk, cwv, nwk, nwv,
      mt, et, eye)


def _rope_tables(pos):
    half = HEAD_DIM // 2
    inv = ROPE_THETA ** (-jnp.arange(half, dtype=f32) / half)
    ang = pos.astype(f32)[:, None] * inv[None, :]
    cos = jnp.tile(jnp.cos(ang), (1, LANES // half))
    sin = jnp.tile(jnp.sin(ang), (1, LANES // half))
    sign = jnp.where((jnp.arange(LANES) % HEAD_DIM) < half, -1.0, 1.0).astype(f32)
    return cos, sin * sign[None, :]


def _cmp_to_sel(ncmp, nsel):
    per = SEL_BLK // CMP_STRIDE
    sub = CMP_BLK // CMP_STRIDE
    i = jnp.arange(LANES)[:, None]
    j = jnp.arange(LANES)[None, :]
    m = jnp.zeros((LANES, LANES), f32)
    for r in range(sub):
        m = m + (((i + r) // per) == j).astype(f32)
    m = jnp.where((i < ncmp) & (j < nsel), m / sub, 0.0)
    return m.astype(bf16)


def _block_expand(nkeys):
    j = jnp.arange(LANES)[:, None]
    k = jnp.arange(nkeys)[None, :]
    return ((k // SEL_BLK) == j).astype(bf16)


def _compress_consts(pe, w1, w2):
    w1r = w1.reshape(CMP_BLK, HEAD_DIM, CMP_HID)
    eye2 = jnp.eye(2, dtype=f32)

    def expand(w):
        return jnp.einsum('ldc,gh->lgdhc', w, eye2).reshape(16 * LANES, 2 * CMP_HID)

    w1s = jnp.concatenate([expand(w1r[:16]), expand(w1r[16:])], axis=1).astype(bf16)
    pes = jnp.stack([jnp.broadcast_to(pe[:16, None, :], (16, 2, HEAD_DIM)).reshape(-1),
                     jnp.broadcast_to(pe[16:, None, :], (16, 2, HEAD_DIM)).reshape(-1)])
    w2s = jnp.einsum('cd,gh->gchd', w2, eye2).reshape(2 * CMP_HID, LANES).astype(bf16)
    return pes, w1s, w2s


def kernel(x_prompt, x_sample, mem_prompt, state_conv, state_ssm, cache_cmp_k, cache_cmp_v, cache_sel_k, cache_sel_v, cache_win_k, cache_win_v, cache_mem_k, cache_mem_v, page_table, norm_mix, norm_ffn, norm_mem, w_mem_kv, w_in_a, conv_w_a, a_log, dt_bias, norm_out_a, w_out_a, w_in_b, w_out_b, norm_kv, w_kv_shared, cmp_pe_k, cmp_w1_k, cmp_w2_k, cmp_pe_v, cmp_w1_v, cmp_w2_v, w_gu_dense, w_down_dense, w_router, w_gu_exp, w_down_exp, norm_final):
    bp, lp, d = x_prompt.shape
    bs, ls, _ = x_sample.shape
    depth = norm_mix.shape[0]
    n_a = w_in_a.shape[0]
    past = page_table.shape[1] * PAGE
    tp = bp * lp
    T = tp + bs * S_PAD
    tm = 512

    xs = jnp.pad(x_sample, ((0, 0), (0, S_PAD - ls), (0, 0)))
    H = jnp.concatenate([x_prompt.reshape(tp, d), xs.reshape(bs * S_PAD, d)], axis=0)

    qkvz = MEM_W + CONV_DIM + A_W
    w_in_a_b = jnp.concatenate(
        [w_in_a[:, :, MEM_W:qkvz], w_in_a[:, :, :MEM_W], w_in_a[:, :, qkvz:],
         jnp.zeros(w_in_a.shape[:2] + (A_IN_PAD - w_in_a.shape[2],), f32)], axis=2).astype(bf16)
    w_in_b_b = jnp.pad(w_in_b, ((0, 0), (0, 0), (0, B_IN_PAD - w_in_b.shape[2]))).astype(bf16)
    w_out_a_b = w_out_a.astype(bf16)
    w_out_b_b = w_out_b.astype(bf16)
    w_gu_dense_b = w_gu_dense.astype(bf16)
    w_down_dense_b = w_down_dense.astype(bf16)
    w_gu_exp_b = w_gu_exp.astype(bf16)
    w_down_exp_b = w_down_exp.astype(bf16)
    w_router_p = jnp.pad(w_router, ((0, 0), (0, 0), (0, LANES - N_EXPERTS)))

    mem_k, mem_v = mem_kv(mem_prompt.reshape(bp * MEM_LEN, d), norm_mem.reshape(depth, 1, d),
                          w_mem_kv.astype(bf16), tm)
    cmk = jnp.transpose(cache_mem_k, (0, 1, 3, 4, 2)).reshape(depth, bs, MEM_W, MEM_LEN)
    cmv = jnp.transpose(cache_mem_v, (0, 1, 3, 4, 2)).reshape(depth, bs, MEM_W, MEM_LEN)
    zero_buf = jnp.zeros((bp, SUBLANES, CONV_DIM), f32)
    zero_state = jnp.zeros((1, bp, A_HEADS, LANES, LANES), f32)

    pos_p = jnp.arange(lp, dtype=jnp.int32)
    pos_s = past + jnp.arange(S_PAD, dtype=jnp.int32)
    cos_p, sin_p = _rope_tables(pos_p)
    cos_s, sin_s = _rope_tables(pos_s)

    conv_p, ssm_p, conv_s, ssm_s = [], [], [], []
    kv = None
    for layer in range(depth):
        is_a = layer < n_a
        g_mix = norm_mix[layer].reshape(1, d)
        if is_a:
            P = norm_matmul(H, g_mix, w_in_a_b, layer, tm, A_IN_PAD // 3)
            qblk = A_MEMQ_BLK
        else:
            lb = layer - n_a
            qblk = 0
            if lb == 0:
                cos_all = jnp.concatenate([jnp.tile(cos_p, (bp, 1)), jnp.tile(cos_s, (bs, 1))], axis=0)
                sin_all = jnp.concatenate([jnp.tile(sin_p, (bp, 1)), jnp.tile(sin_s, (bs, 1))], axis=0)
                kv = shared_kv(H, norm_kv.reshape(1, d), w_kv_shared.astype(bf16), cos_all, sin_all, tm)
                ncmp = (lp - CMP_BLK) // CMP_STRIDE + 1
                cpos = jnp.arange(LANES, dtype=jnp.int32) * CMP_STRIDE + CMP_BLK - 1
                cos_c, sin_c = _rope_tables(cpos)
                pek, w1k, w2k = _compress_consts(cmp_pe_k, cmp_w1_k, cmp_w2_k)
                pev, w1v, w2v = _compress_consts(cmp_pe_v, cmp_w1_v, cmp_w2_v)
                consts = (pek, pev, w1k, w1v, w2k, w2v, cos_c, sin_c)
                strips = lp // 16
                ck_p, cv_p = compress(kv[0][:tp].reshape(bp, strips, 16 * LANES),
                                      kv[1][:tp].reshape(bp, strips, 16 * LANES),
                                      [lambda b: (b, 0, 0)], [lambda b: (b, 0, 0)], strips, bp, consts)
                npages = page_table.shape[1]
                pt_flat = page_table.reshape(-1)
                pool_strips = PAGE // 16
                pmap = lambda j: (lambda b, pt: (pt[b * npages + j], 0, 0))
                ck_s, cv_s = compress(cache_cmp_k.reshape(-1, pool_strips, 16 * LANES),
                                      cache_cmp_v.reshape(-1, pool_strips, 16 * LANES),
                                      [pmap(j) for j in range(npages)], [pmap(j) for j in range(npages)],
                                      pool_strips, bs, consts, prefetch=pt_flat)
                mt_p = _cmp_to_sel(ncmp, lp // SEL_BLK).T
                et_p = _block_expand(lp).T
                ncmp_s = (past + ls - CMP_BLK) // CMP_STRIDE + 1
                mt_s = _cmp_to_sel(ncmp_s, -(-(past + ls) // SEL_BLK)).T
                e_s = _block_expand(past)
                eye = jnp.eye(LANES, dtype=bf16)
                wk_pad = jnp.pad(kv[4][:tp].reshape(bp, lp, LANES), ((0, 0), (WINDOW, 0), (0, 0)))
                wv_pad = jnp.pad(kv[5][:tp].reshape(bp, lp, LANES), ((0, 0), (WINDOW, 0), (0, 0)))
                pool_sk = jnp.transpose(cache_sel_k, (0, 2, 3, 1)).reshape(-1, LANES, PAGE)
                pool_sv = jnp.transpose(cache_sel_v, (0, 2, 3, 1)).reshape(-1, LANES, PAGE)
                cwk = jnp.transpose(cache_win_k, (0, 2, 3, 1)).reshape(bs, LANES, -1)
                cwv = jnp.transpose(cache_win_v, (0, 2, 3, 1)).reshape(bs, LANES, -1)
            P = norm_matmul(H, g_mix, w_in_b_b, lb, tm, B_IN_PAD)
        om = (mem_attn_prompt(P, qblk, mem_k, mem_v, layer, bp, lp, 512),
              mem_attn_sample(P, qblk, tp, cmk, cmv, layer, 8))
        if is_a:
            conv8 = jnp.pad(conv_w_a[layer], ((0, SUBLANES - conv_w_a.shape[1]), (0, 0)))
            ab8 = jnp.zeros((SUBLANES, LANES), f32)
            ab8 = ab8.at[0, A_HEADS:2 * A_HEADS].set(a_log[layer]).at[1, A_HEADS:2 * A_HEADS].set(dt_bias[layer])
            gout = norm_out_a[layer].reshape(1, LANES)
            mo_p, sp = delta_mixer(P, 0, bp, lp, 512, 64, 512, zero_buf, conv8, ab8, gout, zero_state, 0)
            buf_s = jnp.pad(state_conv[layer], ((0, 0), (SUBLANES - state_conv.shape[2], 0), (0, 0)))
            mo_s, ss = delta_mixer(P, tp, bs, S_PAD, S_PAD, S_PAD, ls, buf_s, conv8, ab8, gout, state_ssm, layer)
            conv_p.append(jnp.stack([P[b * lp + lp - 3:(b + 1) * lp, :CONV_DIM] for b in range(bp)]))
            conv_s.append(P[tp:, :CONV_DIM].reshape(bs, S_PAD, CONV_DIM)[:, ls - 3:ls])
            ssm_p.append(sp)
            ssm_s.append(ss)
            w_out, wl = w_out_a_b, layer
        else:
            mo_p = nsa_prompt(P, cos_p, sin_p, ck_p, cv_p, kv[2], kv[3], wk_pad, wv_pad, mt_p, et_p, eye,
                              bp, lp, Q_BLOCK, 512)
            mo_s = nsa_sample(P, tp, pt_flat, pool_sk, pool_sv, cos_s, sin_s, ck_s, cv_s, kv[2], kv[3], cwk, cwv,
                              kv[4], kv[5], mt_s, e_s, eye)
            w_out, wl = w_out_b_b, lb
        mo = (mo_p, mo_s)
        g_ffn = norm_ffn[layer].reshape(1, d)
        if layer % 2 == 0:
            H = outproj_ffn(H, mo, om, w_out, g_ffn, w_gu_dense_b, w_down_dense_b, wl, layer // 2, tm, D_FF // 2)
        else:
            H = outproj_moe(H, mo, om, w_out, g_ffn, w_router_p, w_gu_exp_b, w_down_exp_b, wl, layer // 2,
                            tm, D_FF_EXPERT // 2)
    Y = final_norm(H, norm_final.reshape(1, d), tm)

    def rows_p(a):
        return a[:tp].reshape(bp, lp, 2, HEAD_DIM)

    def rows_s(a):
        return a[tp:].reshape(bs, S_PAD, 2, HEAD_DIM)[:, :ls]

    wlen = min(WINDOW, lp)
    return (Y[:tp].reshape(bp, lp, d), Y[tp:].reshape(bs, S_PAD, d)[:, :ls],
            jnp.stack(conv_p), jnp.stack(ssm_p),
            rows_p(kv[0]), rows_p(kv[1]), rows_p(kv[2]), rows_p(kv[3]),
            rows_p(kv[4])[:, lp - wlen:], rows_p(kv[5])[:, lp - wlen:],
            mem_k.reshape(depth, bp, MEM_LEN, 4, HEAD_DIM), mem_v.reshape(depth, bp, MEM_LEN, 4, HEAD_DIM),
            jnp.stack(conv_s), jnp.stack(ssm_s),
            rows_s(kv[0]), rows_s(kv[1]), rows_s(kv[2]), rows_s(kv[3]),
            jnp.concatenate([cache_win_k[:, ls:], rows_s(kv[4])], axis=1),
            jnp.concatenate([cache_win_v[:, ls:], rows_s(kv[5])], axis=1))
```

```python
import functools
import math

import jax
import jax.numpy as jnp
from jax import lax
from jax.experimental import pallas as pl
from jax.experimental.pallas import tpu as pltpu

f32 = jnp.float32
bf16 = jnp.bfloat16

EPS = 1e-6
NEG = -1e30
ROPE_THETA = 10000.0
HEAD_DIM = 64
LANES = 128
SUBLANES = 8
VMEM_LIMIT = 48 * 1024 * 1024

D_MODEL = 1024
MEM_LEN = 256
MEM_W = 256
A_HEADS = 6
A_W = 768
CONV_DIM = 2304
A_IN_PAD = 3456
A_MEMQ_BLK = 12
A_BA_BLK = 26
B_HEADS = 12
HPG = 6
B_W = 768
B_IN_PAD = 1152
CMP_BLK = 32
CMP_STRIDE = 16
CMP_HID = 256
SEL_BLK = 64
SEL_TOP = 16
WINDOW = 512
Q_BLOCK = 128
D_FF = 2816
N_EXPERTS = 8
D_FF_EXPERT = 3584
PAGE = 128
S_PAD = 8


def _cparams(sem):
    return pltpu.CompilerParams(dimension_semantics=sem, vmem_limit_bytes=VMEM_LIMIT)


def _nt(a, b):
    return lax.dot_general(a, b, (((1,), (1,)), ((), ())), preferred_element_type=f32)


def _tn(a, b):
    return lax.dot_general(a, b, (((0,), (0,)), ((), ())), preferred_element_type=f32)


def _mm(a, b):
    return jnp.dot(a, b, preferred_element_type=f32)


def _rms(x, g):
    return (x * lax.rsqrt(jnp.mean(x * x, axis=-1, keepdims=True) + EPS)) * g


def _sigmoid(x):
    return 1.0 / (1.0 + jnp.exp(-x))


def _silu(x):
    return x * _sigmoid(x)


def _softplus(x):
    return jnp.maximum(x, 0.0) + jnp.log(1.0 + jnp.exp(-jnp.abs(x)))


def _rope_lanes(x, cos, sin_signed):
    lane = lax.broadcasted_iota(jnp.int32, x.shape, x.ndim - 1)
    lo = (lane % HEAD_DIM) < (HEAD_DIM // 2)
    partner = jnp.where(lo, pltpu.roll(x, LANES - HEAD_DIM // 2, axis=x.ndim - 1),
                        pltpu.roll(x, HEAD_DIM // 2, axis=x.ndim - 1))
    return x * cos + partner * sin_signed


def _norm_matmul_kernel(x_ref, g_ref, w_ref, o_ref, xn_ref):
    @pl.when(pl.program_id(1) == 0)
    def _():
        xn_ref[...] = _rms(x_ref[...], g_ref[...]).astype(bf16)

    o_ref[...] = _mm(xn_ref[...], w_ref[...])


def norm_matmul(x, g, w, layer, tm, tn):
    T, K = x.shape
    N = w.shape[2]
    return pl.pallas_call(
        _norm_matmul_kernel,
        grid=(T // tm, N // tn),
        in_specs=[pl.BlockSpec((tm, K), lambda i, j: (i, 0)),
                  pl.BlockSpec((1, K), lambda i, j: (0, 0)),
                  pl.BlockSpec((None, K, tn), lambda i, j: (layer, 0, j))],
        out_specs=pl.BlockSpec((tm, tn), lambda i, j: (i, j)),
        out_shape=jax.ShapeDtypeStruct((T, N), f32),
        scratch_shapes=[pltpu.VMEM((tm, K), bf16)],
        compiler_params=_cparams(("parallel", "arbitrary")),
        name="norm_matmul",
    )(x, g, w)


def _mem_kv_kernel(x_ref, g_ref, w_ref, k_ref, v_ref):
    y = _mm(_rms(x_ref[...], g_ref[...]).astype(bf16), w_ref[...])
    k_ref[...] = y[:, :MEM_W]
    v_ref[...] = y[:, MEM_W:]


def mem_kv(mem, g, w, tm):
    R, K = mem.shape
    NL = w.shape[0]
    out = jax.ShapeDtypeStruct((NL, R, MEM_W), f32)
    return pl.pallas_call(
        _mem_kv_kernel,
        grid=(NL, R // tm),
        in_specs=[pl.BlockSpec((tm, K), lambda l, i: (i, 0)),
                  pl.BlockSpec((None, 1, K), lambda l, i: (l, 0, 0)),
                  pl.BlockSpec((None, K, 2 * MEM_W), lambda l, i: (l, 0, 0))],
        out_specs=(pl.BlockSpec((None, tm, MEM_W), lambda l, i: (l, i, 0)),
                   pl.BlockSpec((None, tm, MEM_W), lambda l, i: (l, i, 0))),
        out_shape=(out, out),
        compiler_params=_cparams(("parallel", "parallel")),
        name="mem_kv",
    )(mem, g, w)


def _final_norm_kernel(x_ref, g_ref, o_ref):
    o_ref[...] = _rms(x_ref[...], g_ref[...])


def final_norm(x, g, tm):
    T, K = x.shape
    return pl.pallas_call(
        _final_norm_kernel,
        grid=(T // tm,),
        in_specs=[pl.BlockSpec((tm, K), lambda i: (i, 0)), pl.BlockSpec((1, K), lambda i: (0, 0))],
        out_specs=pl.BlockSpec((tm, K), lambda i: (i, 0)),
        out_shape=jax.ShapeDtypeStruct((T, K), f32),
        compiler_params=_cparams(("parallel",)),
        name="final_norm",
    )(x, g)


def _mem_attend(q, k, v, transposed=False):
    tq = q.shape[0]
    nh = MEM_W // HEAD_DIM
    lane = lax.broadcasted_iota(jnp.int32, (tq, MEM_W), 1)
    masks = [(lane >= HEAD_DIM * h) & (lane < HEAD_DIM * (h + 1)) for h in range(nh)]
    qs = jnp.concatenate([jnp.where(m, q, 0.0) for m in masks], axis=0).astype(bf16)
    s = (_mm(qs, k) if transposed else _nt(qs, k)) * (HEAD_DIM ** -0.5)
    p = jnp.exp(s - jnp.max(s, axis=-1, keepdims=True))
    l = jnp.sum(p, axis=-1, keepdims=True)
    pb = p.astype(bf16)
    o = (_nt(pb, v) if transposed else _mm(pb, v)) / l
    out = jnp.zeros((tq, MEM_W), f32)
    for h in range(nh):
        out = jnp.where(masks[h], o[h * tq:(h + 1) * tq], out)
    return out


def _mem_attn_prompt_kernel(q_ref, k_ref, v_ref, o_ref):
    o_ref[...] = _mem_attend(q_ref[...], k_ref[...].astype(bf16), v_ref[...].astype(bf16))


def mem_attn_prompt(P, qblk, mk, mv, layer, nb, L, tq):
    nq = L // tq
    return pl.pallas_call(
        _mem_attn_prompt_kernel,
        grid=(nb, nq),
        in_specs=[pl.BlockSpec((tq, MEM_W), lambda b, i: (b * nq + i, qblk)),
                  pl.BlockSpec((None, MEM_LEN, MEM_W), lambda b, i: (layer, b, 0)),
                  pl.BlockSpec((None, MEM_LEN, MEM_W), lambda b, i: (layer, b, 0))],
        out_specs=pl.BlockSpec((tq, MEM_W), lambda b, i: (b * nq + i, 0)),
        out_shape=jax.ShapeDtypeStruct((nb * L, MEM_W), f32),
        compiler_params=_cparams(("parallel", "parallel")),
        name="mem_attn_prompt",
    )(P, mk, mv)


def _mem_attn_sample_kernel(q_ref, k_ref, v_ref, o_ref, *, bb):
    for b in range(bb):
        rows = slice(b * S_PAD, (b + 1) * S_PAD)
        o_ref[rows, :] = _mem_attend(q_ref[rows, :], k_ref[b].astype(bf16), v_ref[b].astype(bf16), transposed=True)


def mem_attn_sample(P, qblk, row0, ckt, cvt, layer, bb):
    nb = ckt.shape[1]
    blk0 = row0 // (bb * S_PAD)
    return pl.pallas_call(
        functools.partial(_mem_attn_sample_kernel, bb=bb),
        grid=(nb // bb,),
        in_specs=[pl.BlockSpec((bb * S_PAD, MEM_W), lambda i: (blk0 + i, qblk)),
                  pl.BlockSpec((None, bb, MEM_W, MEM_LEN), lambda i: (layer, i, 0, 0)),
                  pl.BlockSpec((None, bb, MEM_W, MEM_LEN), lambda i: (layer, i, 0, 0))],
        out_specs=pl.BlockSpec((bb * S_PAD, MEM_W), lambda i: (i, 0)),
        out_shape=jax.ShapeDtypeStruct((nb * S_PAD, MEM_W), f32),
        compiler_params=_cparams(("parallel",)),
        name="mem_attn_sample",
    )(P, ckt, cvt)


def _delta_kernel(q_ref, k_ref, v_ref, z_ref, ba_ref, buf_ref, cw_ref, ab_ref, go_ref, s0_ref,
                  o_ref, sout_ref, qs_ref, ks_ref, vs_ref, gs_ref, bs_ref, tail_ref, s_ref,
                  *, TL, C, valid):
    t = pl.program_id(1)

    @pl.when(t == 0)
    def _():
        tail_ref[...] = buf_ref[...]
        s_ref[...] = s0_ref[...]

    row8 = lax.broadcasted_iota(jnp.int32, (SUBLANES, A_W), 0)

    def conv(x_ref, c0):
        x = x_ref[...]
        b8 = tail_ref[:, c0:c0 + A_W]
        c8 = cw_ref[:, c0:c0 + A_W]
        acc = x * c8[3:4, :]
        for s in range(1, 4):
            r = pltpu.roll(x, s, axis=0)
            fix = jnp.where(row8 < s, pltpu.roll(b8, s, axis=0), r[0:SUBLANES])
            r = jnp.concatenate([fix, r[SUBLANES:]], axis=0) if TL > SUBLANES else fix
            acc = acc + r * c8[3 - s:4 - s, :]
        tail_ref[:, c0:c0 + A_W] = x[TL - SUBLANES:TL]
        return _silu(acc)

    q = conv(q_ref, 0)
    k = conv(k_ref, A_W)
    v = conv(v_ref, 2 * A_W)
    ba = ba_ref[...]
    ab = ab_ref[...]
    beta_all = _sigmoid(ba)
    g_all = -jnp.exp(ab[0:1, :]) * _softplus(ba + ab[1:2, :])
    if valid < TL:
        live = lax.broadcasted_iota(jnp.int32, (TL, A_W), 0) < valid
        live1 = lax.broadcasted_iota(jnp.int32, (TL, LANES), 0) < valid
        k = jnp.where(live, k, 0.0)
        v = jnp.where(live, v, 0.0)
        beta_all = jnp.where(live1, beta_all, 0.0)
        g_all = jnp.where(live1, g_all, 0.0)
    rowc = lax.broadcasted_iota(jnp.int32, (TL, LANES), 0) & (C - 1)
    gc_all = g_all
    s = 1
    while s < C:
        gc_all = gc_all + jnp.where(rowc >= s, pltpu.roll(gc_all, s, axis=0), 0.0)
        s *= 2
    vs_ref[...] = v
    for h in range(A_HEADS):
        sl = slice(h * LANES, (h + 1) * LANES)
        qh = q[:, sl]
        kh = k[:, sl]
        qs_ref[:, sl] = qh * lax.rsqrt(jnp.sum(qh * qh, axis=-1, keepdims=True) + EPS) * (LANES ** -0.5)
        ks_ref[:, sl] = kh * lax.rsqrt(jnp.sum(kh * kh, axis=-1, keepdims=True) + EPS)
        gs_ref[:, sl] = jnp.broadcast_to(gc_all[:, A_HEADS + h:A_HEADS + h + 1], (TL, LANES))
        bs_ref[:, sl] = jnp.broadcast_to(beta_all[:, h:h + 1], (TL, LANES))

    ii = lax.broadcasted_iota(jnp.int32, (C, C), 0)
    jj = lax.broadcasted_iota(jnp.int32, (C, C), 1)
    tri = ii >= jj
    strict = ii > jj
    eye = jnp.where(ii == jj, 1.0, 0.0)
    gout = go_ref[...]
    nsteps = int(math.log2(C)) - 1

    heads = range(A_HEADS)
    lanes_of = [slice(h * LANES, (h + 1) * LANES) for h in heads]

    cpi = 2 if TL // C >= 2 else 1

    def chunks(c, carry):
        rows_of = [pl.ds(pl.multiple_of((c * cpi + i) * C, C), C) for i in range(cpi)]
        units = [(rows, h) for rows in rows_of for h in heads]
        U = range(len(units))
        qc = [qs_ref[rows, lanes_of[h]] for rows, h in units]
        kc = [ks_ref[rows, lanes_of[h]] for rows, h in units]
        vc = [vs_ref[rows, lanes_of[h]] for rows, h in units]
        gcc = [gs_ref[rows, lanes_of[h]] for rows, h in units]
        bc = [bs_ref[rows, lanes_of[h]] for rows, h in units]
        kb = [kc[u] * bc[u] for u in U]
        kq = [_nt(jnp.concatenate([kb[u], qc[u]], axis=0).astype(bf16), kc[u].astype(bf16)) for u in U]
        decay = []
        for u in U:
            gi = gcc[u][:, 0:C]
            gj = jnp.sum(jnp.where(ii == jj, gi, 0.0), axis=0, keepdims=True)
            decay.append(jnp.where(tri, jnp.exp(jnp.where(tri, gi - gj, 0.0)), 0.0))
        n = [jnp.where(strict, -(kq[u][0:C] * decay[u]), 0.0) for u in U]
        a = [jnp.where(tri, kq[u][C:2 * C] * decay[u], 0.0).astype(bf16) for u in U]
        x = [eye + n[u] for u in U]
        p = n
        for _ in range(nsteps):
            pb = [p[u].astype(bf16) for u in U]
            p = [_mm(pb[u], pb[u]) for u in U]
            x = [x[u] + _mm(x[u].astype(bf16), p[u].astype(bf16)) for u in U]
        eg = [jnp.exp(gcc[u]) for u in U]
        uw = [_mm(x[u].astype(bf16), jnp.concatenate([vc[u] * bc[u], kb[u] * eg[u]], axis=1).astype(bf16))
              for u in U]
        wqin = [jnp.concatenate([uw[u][:, LANES:], qc[u] * eg[u]], axis=0).astype(bf16) for u in U]
        glast = [gcc[u][C - 1:C, :] for u in U]
        kg = [(kc[u] * jnp.exp(glast[u] - gcc[u])).astype(bf16) for u in U]
        for i, rows in enumerate(rows_of):
            us = [i * A_HEADS + h for h in heads]
            S = [s_ref[h] for h in heads]
            wq = [_mm(wqin[u], S[h].astype(bf16)) for h, u in zip(heads, us)]
            vnb = [(uw[u][:, :LANES] - wq[h][0:C]).astype(bf16) for h, u in zip(heads, us)]
            av = [_mm(a[u], vnb[h]) for h, u in zip(heads, us)]
            kv_new = [_tn(kg[u], vnb[h]) for h, u in zip(heads, us)]
            for h, u in zip(heads, us):
                s_ref[h] = S[h] * jnp.exp(glast[u]) + kv_new[h]
                o_ref[rows, lanes_of[h]] = _rms(wq[h][C:2 * C] + av[h], gout) * _silu(z_ref[rows, lanes_of[h]])
        return carry

    if TL // C == cpi:
        chunks(0, 0)
    else:
        lax.fori_loop(0, TL // (C * cpi), chunks, 0)

    @pl.when(t == pl.num_programs(1) - 1)
    def _():
        sout_ref[...] = s_ref[...]


def delta_mixer(P, row0, nb, L, TL, C, valid, buf8, conv8, ab8, gout, s0, layer):
    nt = L // TL
    rb0 = row0 // TL
    tile = lambda c: pl.BlockSpec((TL, A_W), lambda b, t: (rb0 + b * nt + t, c))
    state = pl.BlockSpec((None, A_HEADS, LANES, LANES), lambda b, t: (b, 0, 0, 0))
    state_in = pl.BlockSpec((None, None, A_HEADS, LANES, LANES), lambda b, t: (layer, b, 0, 0, 0))
    return pl.pallas_call(
        functools.partial(_delta_kernel, TL=TL, C=C, valid=valid),
        grid=(nb, nt),
        in_specs=[tile(0), tile(1), tile(2), tile(3),
                  pl.BlockSpec((TL, LANES), lambda b, t: (rb0 + b * nt + t, A_BA_BLK)),
                  pl.BlockSpec((None, SUBLANES, CONV_DIM), lambda b, t: (b, 0, 0)),
                  pl.BlockSpec((SUBLANES, CONV_DIM), lambda b, t: (0, 0)),
                  pl.BlockSpec((SUBLANES, LANES), lambda b, t: (0, 0)),
                  pl.BlockSpec((1, LANES), lambda b, t: (0, 0)),
                  state_in],
        out_specs=(pl.BlockSpec((TL, A_W), lambda b, t: (b * nt + t, 0)), state),
        out_shape=(jax.ShapeDtypeStruct((nb * L, A_W), f32),
                   jax.ShapeDtypeStruct((nb, A_HEADS, LANES, LANES), f32)),
        scratch_shapes=[pltpu.VMEM((TL, A_W), f32)] * 5
        + [pltpu.VMEM((SUBLANES, CONV_DIM), f32), pltpu.VMEM((A_HEADS, LANES, LANES), f32)],
        compiler_params=_cparams(("parallel", "arbitrary")),
        name="delta_mixer",
    )(P, P, P, P, P, buf8, conv8, ab8, gout, s0)


def _attn_residual(refs, ntp):
    h_ref, mop_ref, mos_ref, omp_ref, oms_ref, wo1_ref, wo2_ref = refs
    is_p = pl.program_id(0) < ntp
    mo = jnp.where(is_p, mop_ref[...], mos_ref[...])
    om = jnp.where(is_p, omp_ref[...], oms_ref[...])
    return h_ref[...] + _mm(mo.astype(bf16), wo1_ref[...]) + _mm(om.astype(bf16), wo2_ref[...])


def _ffn_kernel(*refs, ntp):
    g_ref, wg_ref, wu_ref, wd_ref, o_ref, xn_ref, acc_ref = refs[7:]
    k = pl.program_id(1)

    @pl.when(k == 0)
    def _():
        h1 = _attn_residual(refs[:7], ntp)
        acc_ref[...] = h1
        xn_ref[...] = _rms(h1, g_ref[...]).astype(bf16)

    xn = xn_ref[...]
    a = _silu(_mm(xn, wg_ref[...])) * _mm(xn, wu_ref[...])
    acc_ref[...] += _mm(a.astype(bf16), wd_ref[...])

    @pl.when(k == pl.num_programs(1) - 1)
    def _():
        o_ref[...] = acc_ref[...]


def _outproj_specs(tm, K, mw, ow, ntp, layer):
    prow = lambda i, *_: (jnp.minimum(i, ntp - 1), 0)
    srow = lambda i, *_: (jnp.maximum(i - ntp, 0), 0)
    return [pl.BlockSpec((tm, K), lambda i, *_: (i, 0)),
            pl.BlockSpec((tm, mw), prow), pl.BlockSpec((tm, mw), srow),
            pl.BlockSpec((tm, ow), prow), pl.BlockSpec((tm, ow), srow),
            pl.BlockSpec((None, mw, K), lambda i, *_: (layer, 0, 0)),
            pl.BlockSpec((None, ow, K), lambda i, *_: (layer, mw // ow, 0)),
            pl.BlockSpec((1, K), lambda i, *_: (0, 0))]


def outproj_ffn(H, mo, om, w_out, g, w_gu, w_down, layer, ffn_layer, tm, tf):
    T, K = H.shape
    FF = w_down.shape[1]
    nk = FF // tf
    mw, ow = mo[0].shape[1], om[0].shape[1]
    ntp = mo[0].shape[0] // tm
    return pl.pallas_call(
        functools.partial(_ffn_kernel, ntp=ntp),
        grid=(T // tm, nk),
        in_specs=_outproj_specs(tm, K, mw, ow, ntp, layer)
        + [pl.BlockSpec((None, K, tf), lambda i, k: (ffn_layer, 0, k)),
           pl.BlockSpec((None, K, tf), lambda i, k: (ffn_layer, 0, nk + k)),
           pl.BlockSpec((None, tf, K), lambda i, k: (ffn_layer, k, 0))],
        out_specs=pl.BlockSpec((tm, K), lambda i, k: (i, 0)),
        out_shape=jax.ShapeDtypeStruct((T, K), f32),
        scratch_shapes=[pltpu.VMEM((tm, K), bf16), pltpu.VMEM((tm, K), f32)],
        compiler_params=_cparams(("parallel", "arbitrary")),
        name="outproj_ffn",
    )(H, mo[0], mo[1], om[0], om[1], w_out, w_out, g, w_gu, w_gu, w_down)


def _router_kernel(*refs, ntp):
    g_ref, wr_ref, h1_ref, xn_ref, r_ref, cnt_ref = refs[7:]
    i = pl.program_id(0)
    tm = h1_ref.shape[0]

    @pl.when(i == 0)
    def _():
        cnt_ref[...] = jnp.zeros_like(cnt_ref)

    h1 = _attn_residual(refs[:7], ntp)
    h1_ref[...] = h1
    xn = _rms(h1, g_ref[...])
    xn_ref[...] = xn.astype(bf16)
    lane = lax.broadcasted_iota(jnp.int32, (tm, LANES), 1)
    xh = xn.astype(bf16)
    xl = (xn - xh.astype(f32)).astype(bf16)
    wr = wr_ref[...]
    wh = wr.astype(bf16)
    wl = (wr - wh.astype(f32)).astype(bf16)
    logits = _mm(xh, wh) + _mm(xh, wl) + _mm(xl, wh)
    logits = jnp.where(lane < N_EXPERTS, logits, -jnp.inf)
    m1 = jnp.max(logits, axis=-1, keepdims=True)
    i1 = jnp.min(jnp.where(logits == m1, lane, LANES), axis=-1, keepdims=True)
    rest = jnp.where(lane == i1, -jnp.inf, logits)
    m2 = jnp.max(rest, axis=-1, keepdims=True)
    i2 = jnp.min(jnp.where(rest == m2, lane, LANES), axis=-1, keepdims=True)
    e2 = jnp.exp(m2 - m1)
    w1 = 1.0 / (1.0 + e2)
    w2 = e2 / (1.0 + e2)
    hot = jnp.where((lane == i1) | (lane == i2), 1.0, 0.0)
    ri = lax.broadcasted_iota(jnp.int32, (tm, tm), 0)
    ci = lax.broadcasted_iota(jnp.int32, (tm, tm), 1)
    before = jnp.where(ri > ci, 1.0, 0.0).astype(bf16)
    cum = _mm(before, hot.astype(bf16)) + cnt_ref[...]
    r1 = jnp.sum(jnp.where(lane == i1, cum, 0.0), axis=-1, keepdims=True)
    r2 = jnp.sum(jnp.where(lane == i2, cum, 0.0), axis=-1, keepdims=True)
    cnt_ref[...] += jnp.sum(hot, axis=0, keepdims=True)
    cols = (i1.astype(f32), i2.astype(f32), w1, w2, r1, r2)
    r = jnp.zeros((tm, LANES), f32)
    for c, val in enumerate(cols):
        r = jnp.where(lane == c, val, r)
    r_ref[...] = r


def outproj_router(H, mo, om, w_out, g, w_router, layer, moe_layer, tm):
    T, K = H.shape
    mw, ow = mo[0].shape[1], om[0].shape[1]
    ntp = mo[0].shape[0] // tm
    return pl.pallas_call(
        functools.partial(_router_kernel, ntp=ntp),
        grid=(T // tm,),
        in_specs=_outproj_specs(tm, K, mw, ow, ntp, layer)
        + [pl.BlockSpec((None, K, LANES), lambda i: (moe_layer, 0, 0))],
        out_specs=(pl.BlockSpec((tm, K), lambda i: (i, 0)), pl.BlockSpec((tm, K), lambda i: (i, 0)),
                   pl.BlockSpec((tm, LANES), lambda i: (i, 0))),
        out_shape=(jax.ShapeDtypeStruct((T, K), f32), jax.ShapeDtypeStruct((T, K), bf16),
                   jax.ShapeDtypeStruct((T, LANES), f32)),
        scratch_shapes=[pltpu.VMEM((1, LANES), f32)],
        compiler_params=_cparams(("arbitrary",)),
        name="outproj_router",
    )(H, mo[0], mo[1], om[0], om[1], w_out, w_out, g, w_router)


def _moe_group_kernel(te_ref, nu_ref, x_ref, sw_ref, wg_ref, wu_ref, wd_ref, o_ref, acc_ref):
    del te_ref
    i = pl.program_id(0)
    k = pl.program_id(1)

    @pl.when(k == 0)
    def _():
        acc_ref[...] = jnp.zeros_like(acc_ref)

    @pl.when(i < nu_ref[0])
    def _():
        x = x_ref[...]
        a = _silu(_mm(x, wg_ref[...])) * _mm(x, wu_ref[...])
        acc_ref[...] += _mm(a.astype(bf16), wd_ref[...])

    @pl.when(k == pl.num_programs(1) - 1)
    def _():
        o_ref[...] = acc_ref[...] * sw_ref[...]


def moe_grouped(xg, sw, tile_e, n_used, w_gu, w_down, moe_layer, tm, tf):
    NP, K = xg.shape
    FF = w_down.shape[2]
    nk = FF // tf

    def kk(i, k, nu):
        return jnp.where(i < nu[0], k, nk - 1)

    grid_spec = pltpu.PrefetchScalarGridSpec(
        num_scalar_prefetch=2, grid=(NP // tm, nk),
        in_specs=[pl.BlockSpec((tm, K), lambda i, k, te, nu: (i, 0)),
                  pl.BlockSpec((tm, 1), lambda i, k, te, nu: (i, 0)),
                  pl.BlockSpec((None, None, K, tf), lambda i, k, te, nu: (moe_layer, te[i], 0, kk(i, k, nu))),
                  pl.BlockSpec((None, None, K, tf), lambda i, k, te, nu: (moe_layer, te[i], 0, nk + kk(i, k, nu))),
                  pl.BlockSpec((None, None, tf, K), lambda i, k, te, nu: (moe_layer, te[i], kk(i, k, nu), 0))],
        out_specs=pl.BlockSpec((tm, K), lambda i, k, te, nu: (i, 0)),
        scratch_shapes=[pltpu.VMEM((tm, K), f32)])
    return pl.pallas_call(
        _moe_group_kernel, grid_spec=grid_spec,
        out_shape=jax.ShapeDtypeStruct((NP, K), f32),
        compiler_params=_cparams(("parallel", "arbitrary")),
        name="moe_grouped",
    )(tile_e, n_used, xg, sw, w_gu, w_gu, w_down)


def _dispatch(r, tm):
    T = r.shape[0]
    A = 2 * T
    ntiles = A // tm + N_EXPERTS
    NP = ntiles * tm
    e_flat = r[:, 0:2].astype(jnp.int32).reshape(A)
    w_flat = r[:, 2:4].reshape(A)
    rank = r[:, 4:6].astype(jnp.int32)
    counts = jnp.sum((e_flat[:, None] == jnp.arange(N_EXPERTS)[None, :]).astype(jnp.int32), axis=0)
    pcounts = (counts + tm - 1) // tm * tm
    pends = jnp.cumsum(pcounts)
    pstarts = pends - pcounts
    starts = jnp.cumsum(counts) - counts
    n_used = pends[-1] // tm
    key_sorted, w_sorted = lax.sort((e_flat * A + jnp.arange(A, dtype=jnp.int32), w_flat), num_keys=1)
    a_sorted = key_sorted % A
    tiles = jnp.arange(ntiles, dtype=jnp.int32)
    tile_e = jnp.minimum(jnp.searchsorted(pends, tiles * tm, side='right'), N_EXPERTS - 1).astype(jnp.int32)
    tile_e = jnp.where(tiles < n_used, tile_e, tile_e[jnp.maximum(n_used - 1, 0)])
    first = tiles * tm - pstarts[tile_e]
    left = jnp.where(tiles < n_used, counts[tile_e] - first, 0)
    off = jnp.clip(starts[tile_e] + first, 0, A)
    take_run = jax.vmap(lambda a, o: lax.dynamic_slice(a, (o,), (tm,)), in_axes=(None, 0))
    a_run = take_run(jnp.pad(a_sorted, (0, tm)), off)
    w_run = take_run(jnp.pad(w_sorted, (0, tm)), off)
    ok = jnp.arange(tm, dtype=jnp.int32)[None, :] < left[:, None]
    src = jnp.where(ok, a_run // 2, 0).reshape(NP)
    sw = jnp.where(ok, w_run, 0.0).reshape(NP, 1)
    picks = r[:, 0:2].astype(jnp.int32)
    pick_start = jnp.sum(jnp.where(picks[:, :, None] == jnp.arange(N_EXPERTS)[None, None, :],
                                   pstarts[None, None, :], 0), axis=2)
    slots = pick_start + rank
    return src, sw, tile_e, n_used.reshape(1).astype(jnp.int32), slots


def outproj_moe(H, mo, om, w_out, g, w_router, w_gu, w_down, layer, moe_layer, tm, tf):
    h1, xn, r = outproj_router(H, mo, om, w_out, g, w_router, layer, moe_layer, tm)
    src, sw, tile_e, n_used, slots = _dispatch(r, tm)
    yg = moe_grouped(jnp.take(xn, src, axis=0), sw, tile_e, n_used, w_gu, w_down, moe_layer, tm, tf)
    return h1 + jnp.take(yg, slots[:, 0], axis=0) + jnp.take(yg, slots[:, 1], axis=0)


def _shared_kv_kernel(x_ref, g_ref, w_ref, cos_ref, sin_ref, ck_ref, cv_ref, sk_ref, sv_ref, wk_ref, wv_ref):
    y = _mm(_rms(x_ref[...], g_ref[...]).astype(bf16), w_ref[...])
    cos = cos_ref[...]
    sin = sin_ref[...]
    ck_ref[...] = y[:, 0 * LANES:1 * LANES]
    cv_ref[...] = y[:, 1 * LANES:2 * LANES]
    sk_ref[...] = _rope_lanes(y[:, 2 * LANES:3 * LANES], cos, sin)
    sv_ref[...] = y[:, 3 * LANES:4 * LANES]
    wk_ref[...] = _rope_lanes(y[:, 4 * LANES:5 * LANES], cos, sin)
    wv_ref[...] = y[:, 5 * LANES:6 * LANES]


def shared_kv(H, g, w, cos, sin, tm):
    T, K = H.shape
    row = pl.BlockSpec((tm, LANES), lambda i: (i, 0))
    out = jax.ShapeDtypeStruct((T, LANES), f32)
    return pl.pallas_call(
        _shared_kv_kernel,
        grid=(T // tm,),
        in_specs=[pl.BlockSpec((tm, K), lambda i: (i, 0)), pl.BlockSpec((1, K), lambda i: (0, 0)),
                  pl.BlockSpec((K, 6 * LANES), lambda i: (0, 0)), row, row],
        out_specs=(row,) * 6,
        out_shape=(out,) * 6,
        compiler_params=_cparams(("parallel",)),
        name="shared_kv",
    )(H, g, w, cos, sin)


def _compress_kernel(*refs, npages):
    nin = 2 * npages
    k_strips = jnp.concatenate([r[...] for r in refs[0:npages]], axis=0) if npages > 1 else refs[0][...]
    v_strips = jnp.concatenate([r[...] for r in refs[npages:nin]], axis=0) if npages > 1 else refs[npages][...]
    (pek_ref, pev_ref, w1k_ref, w1v_ref, w2k_ref, w2v_ref, cos_ref, sin_ref, ck_ref, cv_ref) = refs[nin:]
    n = k_strips.shape[0]
    half = 2 * CMP_HID
    row = lax.broadcasted_iota(jnp.int32, (n, LANES), 0)

    def tokens(strips, pe_ref, w1_ref, w2_ref):
        top = _mm((strips + pe_ref[0:1, :]).astype(bf16), w1_ref[:, 0:half])
        bot = _mm((strips + pe_ref[1:2, :]).astype(bf16), w1_ref[:, half:2 * half])
        hid = top + pltpu.roll(bot, n - 1, axis=0)
        out = _mm(_silu(hid).astype(bf16), w2_ref[...])
        return jnp.where(row < n - 1, out, 0.0)

    ck_ref[...] = _rope_lanes(tokens(k_strips, pek_ref, w1k_ref, w2k_ref), cos_ref[...], sin_ref[...])
    cv_ref[...] = tokens(v_strips, pev_ref, w1v_ref, w2v_ref)


def _drop_first(fn, _prefetch_ref, *refs):
    return fn(*refs)


def compress(k_src, v_src, k_maps, v_maps, strip_rows, nb, consts, prefetch=None):
    npages = len(k_maps)
    n = npages * strip_rows
    pek, pev, w1k, w1v, w2k, w2v, cos, sin = consts
    flat = 16 * LANES
    const2 = lambda shape: pl.BlockSpec(shape, lambda b, *_: (0, 0))
    in_specs = ([pl.BlockSpec((None, strip_rows, flat), m) for m in k_maps]
                + [pl.BlockSpec((None, strip_rows, flat), m) for m in v_maps]
                + [const2((2, flat)), const2((2, flat)), const2((flat, 4 * CMP_HID)), const2((flat, 4 * CMP_HID)),
                   const2((2 * CMP_HID, LANES)), const2((2 * CMP_HID, LANES)), const2((n, LANES)), const2((n, LANES))])
    out_spec = pl.BlockSpec((None, n, LANES), lambda b, *_: (b, 0, 0))
    out = jax.ShapeDtypeStruct((nb, n, LANES), f32)
    grid_spec = pltpu.PrefetchScalarGridSpec(
        num_scalar_prefetch=0 if prefetch is None else 1, grid=(nb,),
        in_specs=in_specs, out_specs=(out_spec, out_spec))
    args = ([] if prefetch is None else [prefetch]) + [k_src] * npages + [v_src] * npages + [pek, pev, w1k, w1v, w2k, w2v, cos, sin]
    kern = functools.partial(_compress_kernel, npages=npages)
    if prefetch is not None:
        kern = functools.partial(_drop_first, kern)
    return pl.pallas_call(
        kern, grid_spec=grid_spec, out_shape=(out, out),
        compiler_params=_cparams(("parallel",)), name="compress",
    )(*args)


def _nsa_queries(p, cos, sin):
    tq = p.shape[0]
    lane = lax.broadcasted_iota(jnp.int32, (tq, LANES), 1)
    rows = []
    for head in range(B_HEADS):
        grp = head // HPG
        blk = MEM_W // LANES + head // 2
        x = _rope_lanes(p[:, blk * LANES:(blk + 1) * LANES], cos, sin) * (HEAD_DIM ** -0.5)
        if head % 2 != grp:
            x = pltpu.roll(x, HEAD_DIM, axis=1)
        keep = (lane >= grp * HEAD_DIM) & (lane < (grp + 1) * HEAD_DIM)
        rows.append(jnp.where(keep, x, 0.0))
    return jnp.concatenate(rows, axis=0)


def _masked_softmax_parts(s, mask):
    sm = jnp.where(mask, s, NEG)
    m = jnp.max(sm, axis=-1, keepdims=True)
    p = jnp.where(mask, jnp.exp(sm - m), 0.0)
    return p, jnp.sum(p, axis=-1, keepdims=True)


def _safe_div(o, l):
    return jnp.where(l > 0.0, o / jnp.where(l > 0.0, l, 1.0), 0.0)


def _split3(x):
    hi = x.astype(bf16)
    r1 = x - hi.astype(f32)
    mid = r1.astype(bf16)
    lo = (r1 - mid.astype(f32)).astype(bf16)
    return hi, mid, lo


def _select_bias(pcn, qpos_row, past_blocks, mt_ref, eye_ref, nsel):
    tq = qpos_row.shape[1]
    nr = -(-nsel // SUBLANES) * SUBLANES
    mt = mt_ref[0:nr, :]
    blk = lax.broadcasted_iota(jnp.int32, (nr, tq), 0)
    lane = lax.broadcasted_iota(jnp.int32, (tq, LANES), 1)
    cur = qpos_row // SEL_BLK
    forced = (blk == 0) | (blk == cur) | (blk == cur - 1)
    valid = blk * SEL_BLK <= qpos_row
    stacks, picks = [], []
    for grp in range(2):
        psum = pcn[grp * HPG * tq:(grp * HPG + 1) * tq]
        for hh in range(1, HPG):
            psum = psum + pcn[(grp * HPG + hh) * tq:(grp * HPG + hh + 1) * tq]
        hi, mid, lo = _split3(psum)
        imp = _nt(mt, hi) + _nt(mt, mid) + _nt(mt, lo)
        imp = jnp.where(forced, 1e9, jnp.where(valid, imp, -1e9))
        rank = jnp.zeros((nr, tq), f32)
        for j in range(nsel):
            rj = imp[j:j + 1, :]
            rank = rank + jnp.where(rj > imp, 1.0, jnp.where(rj == imp, jnp.where(blk > j, 1.0, 0.0), 0.0))
        sel_t = jnp.where((rank < SEL_TOP) & (blk < nsel), 1.0, 0.0).astype(bf16)
        pick = _tn(sel_t, eye_ref[0:nr, :])
        picks.append(pick)
        stacks += [jnp.where((pick > 0.5) & (lane < past_blocks), 0.0, NEG)] * HPG
    return jnp.concatenate(stacks, axis=0), picks


def _nsa_combine(p, o_c, o_s, o_w):
    tq = p.shape[0]
    gl = _sigmoid(p[:, (MEM_W + B_W):(MEM_W + B_W) + LANES])

    def gate(j):
        return jnp.concatenate([gl[:, 3 * h + j:3 * h + j + 1] for h in range(B_HEADS)], axis=0)

    comb = gate(0) * o_c + gate(1) * o_s + gate(2) * o_w
    lane = lax.broadcasted_iota(jnp.int32, (tq, LANES), 1)
    cols = []
    for c in range(B_HEADS // 2):
        lo = comb[2 * c * tq:(2 * c + 1) * tq]
        hi = comb[(2 * c + 1) * tq:(2 * c + 2) * tq]
        if (2 * c) // HPG == 1:
            lo = pltpu.roll(lo, HEAD_DIM, axis=1)
        if (2 * c + 1) // HPG == 0:
            hi = pltpu.roll(hi, HEAD_DIM, axis=1)
        cols.append(jnp.where(lane < HEAD_DIM, lo, hi))
    return jnp.concatenate(cols, axis=1)


def _nsa_prompt_kernel(p_ref, cos_ref, sin_ref, ck_ref, cv_ref, sk_ref, sv_ref, wk_ref, wv_ref, mt_ref, et_ref,
                       eye_ref, o_ref, *, tq, kc, nsel):
    qb = pl.program_id(1)
    s0 = qb * tq
    R = B_HEADS * tq
    p = p_ref[...]
    qf = _nsa_queries(p, cos_ref[...], sin_ref[...])
    qb16 = qf.astype(bf16)
    ii = lax.broadcasted_iota(jnp.int32, (tq, tq), 0)
    jj = lax.broadcasted_iota(jnp.int32, (tq, tq), 1)
    qposR = s0 + jnp.concatenate([lax.broadcasted_iota(jnp.int32, (tq, 1), 0)] * B_HEADS, axis=0)
    lane = lax.broadcasted_iota(jnp.int32, (R, LANES), 1)
    cmask = ((lane * CMP_STRIDE + (CMP_BLK - 1)) <= qposR) & (lane < ck_ref.shape[0] - 1)
    pc, lc = _masked_softmax_parts(_nt(qb16, ck_ref[...].astype(bf16)), cmask)
    pcn = _safe_div(pc, lc)
    o_c = _mm(pcn.astype(bf16), cv_ref[...].astype(bf16))
    qpos_row = s0 + lax.broadcasted_iota(jnp.int32, (1, tq), 1)
    bias, _ = _select_bias(pcn, qpos_row, s0 // SEL_BLK, mt_ref, eye_ref, nsel)
    qsel = jnp.concatenate([qb16, bias.astype(bf16)], axis=1)
    sd = _nt(qb16, sk_ref[pl.ds(s0, tq), :].astype(bf16)).reshape(B_HEADS, tq, tq)
    sd = jnp.where((ii >= jj)[None], sd, NEG).reshape(R, tq)
    m0 = jnp.max(sd, axis=-1, keepdims=True)
    pd = jnp.exp(sd - m0)
    init = (m0, jnp.sum(pd, axis=-1, keepdims=True), _mm(pd.astype(bf16), sv_ref[pl.ds(s0, tq), :].astype(bf16)))

    def body(c, carry):
        m_run, l_run, acc = carry
        k0 = pl.multiple_of(c * kc, kc)
        ka = jnp.concatenate([sk_ref[pl.ds(k0, kc), :].astype(bf16), et_ref[pl.ds(k0, kc), :]], axis=1)
        s = _nt(qsel, ka)
        m_new = jnp.maximum(m_run, jnp.max(s, axis=-1, keepdims=True))
        alpha = jnp.exp(m_run - m_new)
        pe = jnp.exp(s - m_new)
        l_new = alpha * l_run + jnp.sum(pe, axis=-1, keepdims=True)
        acc_new = alpha * acc + _mm(pe.astype(bf16), sv_ref[pl.ds(k0, kc), :].astype(bf16))
        return m_new, l_new, acc_new

    _, l_run, acc = lax.fori_loop(0, (s0 + kc - 1) // kc, body, init)
    o_s = acc / l_run
    wlen = WINDOW + tq
    wrow = lax.broadcasted_iota(jnp.int32, (wlen, LANES), 0)
    wlane = lax.broadcasted_iota(jnp.int32, (wlen, LANES), 1)
    is_pad = jnp.where((wrow + (s0 - WINDOW) < 0) & (wlane == 0), 1.0, 0.0).astype(bf16)
    kw = jnp.concatenate([wk_ref[pl.ds(s0, wlen), :].astype(bf16), is_pad], axis=1)
    qwin = jnp.concatenate([qb16, jnp.where(lane == 0, NEG, 0.0).astype(bf16)], axis=1)
    sw = _nt(qwin, kw).reshape(B_HEADS, tq, wlen)
    far = jnp.where((jj > ii)[None], sw[:, :, 0:tq], NEG)
    near = jnp.where((jj <= ii)[None], sw[:, :, wlen - tq:wlen], NEG)
    sw = jnp.concatenate([far, sw[:, :, tq:wlen - tq], near], axis=2).reshape(R, wlen)
    pw = jnp.exp(sw - jnp.max(sw, axis=-1, keepdims=True))
    o_w = _mm(pw.astype(bf16), wv_ref[pl.ds(s0, wlen), :].astype(bf16)) / jnp.sum(pw, axis=-1, keepdims=True)
    o_ref[...] = _nsa_combine(p, o_c, o_s, o_w)


def nsa_prompt(P, cos, sin, ck, cv, sk, sv, wk_pad, wv_pad, mt, et, eye, nb, L, tq, kc):
    nq = L // tq
    nsel = L // SEL_BLK
    seq = pl.BlockSpec((L, LANES), lambda b, i: (b, 0))
    win = pl.BlockSpec((None, WINDOW + L, LANES), lambda b, i: (b, 0, 0))
    cmp_spec = pl.BlockSpec((None, LANES, LANES), lambda b, i: (b, 0, 0))
    const = lambda shape: pl.BlockSpec(shape, lambda b, i: (0, 0))
    return pl.pallas_call(
        functools.partial(_nsa_prompt_kernel, tq=tq, kc=kc, nsel=nsel),
        grid=(nb, nq),
        in_specs=[pl.BlockSpec((tq, B_IN_PAD), lambda b, i: (b * nq + i, 0)),
                  pl.BlockSpec((tq, LANES), lambda b, i: (i, 0)),
                  pl.BlockSpec((tq, LANES), lambda b, i: (i, 0)),
                  cmp_spec, cmp_spec, seq, seq, win, win,
                  const((LANES, LANES)), const((L, LANES)), const((LANES, LANES))],
        out_specs=pl.BlockSpec((tq, B_W), lambda b, i: (b * nq + i, 0)),
        out_shape=jax.ShapeDtypeStruct((nb * L, B_W), f32),
        compiler_params=_cparams(("parallel", "parallel")),
        name="nsa_prompt",
    )(P, cos, sin, ck, cv, sk, sv, wk_pad, wv_pad, mt, et, eye)


def _nsa_sample_kernel(*refs, npages, past):
    refs = refs[1:]
    kpages = refs[0:npages]
    vpages = refs[npages:2 * npages]
    (p_ref, cos_ref, sin_ref, ck_ref, cv_ref, nsk_ref, nsv_ref, cwk_ref, cwv_ref, nwk_ref, nwv_ref,
     mt_ref, et_ref, eye_ref, o_ref) = refs[2 * npages:]
    tq = S_PAD
    R = B_HEADS * tq
    p = p_ref[...]
    qb16 = _nsa_queries(p, cos_ref[...], sin_ref[...]).astype(bf16)
    tR = jnp.concatenate([lax.broadcasted_iota(jnp.int32, (tq, 1), 0)] * B_HEADS, axis=0)
    qposR = past + tR
    lane = lax.broadcasted_iota(jnp.int32, (R, LANES), 1)
    cmask = ((lane * CMP_STRIDE + (CMP_BLK - 1)) <= qposR) & (lane < ck_ref.shape[0] - 1)
    pc, lc = _masked_softmax_parts(_nt(qb16, ck_ref[...].astype(bf16)), cmask)
    pcn = _safe_div(pc, lc)
    o_c = _mm(pcn.astype(bf16), cv_ref[...].astype(bf16))
    qpos_row = past + lax.broadcasted_iota(jnp.int32, (1, tq), 1)
    new_blk = past // SEL_BLK
    bias, picks = _select_bias(pcn, qpos_row, new_blk, mt_ref, eye_ref, new_blk + 1)
    new_ok = lax.broadcasted_iota(jnp.int32, (R, tq), 1) <= tR
    ka = jnp.concatenate([jnp.concatenate([r[...] for r in kpages], axis=1).astype(bf16), et_ref[...]], axis=0)
    s_old = _mm(jnp.concatenate([qb16, bias.astype(bf16)], axis=1), ka)
    new_kept = jnp.concatenate([picks[h // HPG][:, new_blk:new_blk + 1] for h in range(B_HEADS)], axis=0) > 0.5
    s_new = jnp.where(new_ok, jnp.where(new_kept, _nt(qb16, nsk_ref[...].astype(bf16)), NEG), NEG)
    m = jnp.maximum(jnp.max(s_old, axis=-1, keepdims=True), jnp.max(s_new, axis=-1, keepdims=True))
    p_old = jnp.exp(s_old - m)
    p_new = jnp.exp(s_new - m)
    l = jnp.sum(p_old, axis=-1, keepdims=True) + jnp.sum(p_new, axis=-1, keepdims=True)
    svb = jnp.concatenate([r[...] for r in vpages], axis=1).astype(bf16)
    o_s = (_nt(p_old.astype(bf16), svb) + _mm(p_new.astype(bf16), nsv_ref[...].astype(bf16))) / l
    wb = cwk_ref.shape[1]
    wdist = qposR - (past - wb + lax.broadcasted_iota(jnp.int32, (R, wb), 1))
    w_old = jnp.where((wdist >= 0) & (wdist < WINDOW), _mm(qb16, cwk_ref[...].astype(bf16)), NEG)
    w_new = jnp.where(new_ok, _nt(qb16, nwk_ref[...].astype(bf16)), NEG)
    mw = jnp.maximum(jnp.max(w_old, axis=-1, keepdims=True), jnp.max(w_new, axis=-1, keepdims=True))
    pw_old = jnp.exp(w_old - mw)
    pw_new = jnp.exp(w_new - mw)
    lw = jnp.sum(pw_old, axis=-1, keepdims=True) + jnp.sum(pw_new, axis=-1, keepdims=True)
    o_w = (_nt(pw_old.astype(bf16), cwv_ref[...].astype(bf16)) + _mm(pw_new.astype(bf16), nwv_ref[...].astype(bf16))) / lw
    o_ref[...] = _nsa_combine(p, o_c, o_s, o_w)


def nsa_sample(P, row0, page_table, pool_k, pool_v, cos, sin, ck, cv, nsk, nsv, cwk, cwv, nwk, nwv, mt, et, eye):
    nb = cwk.shape[0]
    npages = page_table.shape[0] // nb
    past = npages * PAGE
    rb0 = row0 // S_PAD
    page = lambda j: pl.BlockSpec((None, LANES, PAGE), lambda b, pt: (pt[b * npages + j], 0, 0))
    new_rows = pl.BlockSpec((S_PAD, LANES), lambda b, pt: (rb0 + b, 0))
    per_seq = lambda n: pl.BlockSpec((None, LANES, n), lambda b, pt: (b, 0, 0))
    in_specs = ([page(j) for j in range(npages)] + [page(j) for j in range(npages)]
                + [pl.BlockSpec((S_PAD, B_IN_PAD), lambda b, pt: (rb0 + b, 0)),
                   pl.BlockSpec((S_PAD, LANES), lambda b, pt: (0, 0)),
                   pl.BlockSpec((S_PAD, LANES), lambda b, pt: (0, 0)),
                   per_seq(LANES), per_seq(LANES), new_rows, new_rows,
                   per_seq(cwk.shape[2]), per_seq(cwk.shape[2]), new_rows, new_rows,
                   pl.BlockSpec((LANES, LANES), lambda b, pt: (0, 0)),
                   pl.BlockSpec((LANES, past), lambda b, pt: (0, 0)),
                   pl.BlockSpec((LANES, LANES), lambda b, pt: (0, 0))])
    grid_spec = pltpu.PrefetchScalarGridSpec(
        num_scalar_prefetch=1, grid=(nb,), in_specs=in_specs,
        out_specs=pl.BlockSpec((S_PAD, B_W), lambda b, pt: (b, 0)))
    return pl.pallas_call(
        functools.partial(_nsa_sample_kernel, npages=npages, past=past),
        grid_spec=grid_spec,
        out_shape=jax.ShapeDtypeStruct((nb * S_PAD, B_W), f32),
        compiler_params=_cparams(("parallel",)),
        name="nsa_sample",
    )(page_table, *([pool_k] * npages), *([pool_v] * npages), P, cos, sin, ck, cv, nsk, nsv, cwk, cwv, nwk, nwv,
      mt, et, eye)


def _rope_tables(pos):
    half = HEAD_DIM // 2
    inv = ROPE_THETA ** (-jnp.arange(half, dtype=f32) / half)
    ang = pos.astype(f32)[:, None] * inv[None, :]
    cos = jnp.tile(jnp.cos(ang), (1, LANES // half))
    sin = jnp.tile(jnp.sin(ang), (1, LANES // half))
    sign = jnp.where((jnp.arange(LANES) % HEAD_DIM) < half, -1.0, 1.0).astype(f32)
    return cos, sin * sign[None, :]


def _cmp_to_sel(ncmp, nsel):
    per = SEL_BLK // CMP_STRIDE
    sub = CMP_BLK // CMP_STRIDE
    i = jnp.arange(LANES)[:, None]
    j = jnp.arange(LANES)[None, :]
    m = jnp.zeros((LANES, LANES), f32)
    for r in range(sub):
        m = m + (((i + r) // per) == j).astype(f32)
    m = jnp.where((i < ncmp) & (j < nsel), m / sub, 0.0)
    return m.astype(bf16)


def _block_expand(nkeys):
    j = jnp.arange(LANES)[:, None]
    k = jnp.arange(nkeys)[None, :]
    return ((k // SEL_BLK) == j).astype(bf16)


def _compress_consts(pe, w1, w2):
    w1r = w1.reshape(CMP_BLK, HEAD_DIM, CMP_HID)
    eye2 = jnp.eye(2, dtype=f32)

    def expand(w):
        return jnp.einsum('ldc,gh->lgdhc', w, eye2).reshape(16 * LANES, 2 * CMP_HID)

    w1s = jnp.concatenate([expand(w1r[:16]), expand(w1r[16:])], axis=1).astype(bf16)
    pes = jnp.stack([jnp.broadcast_to(pe[:16, None, :], (16, 2, HEAD_DIM)).reshape(-1),
                     jnp.broadcast_to(pe[16:, None, :], (16, 2, HEAD_DIM)).reshape(-1)])
    w2s = jnp.einsum('cd,gh->gchd', w2, eye2).reshape(2 * CMP_HID, LANES).astype(bf16)
    return pes, w1s, w2s


def kernel(x_prompt, x_sample, mem_prompt, state_conv, state_ssm, cache_cmp_k, cache_cmp_v, cache_sel_k, cache_sel_v, cache_win_k, cache_win_v, cache_mem_k, cache_mem_v, page_table, norm_mix, norm_ffn, norm_mem, w_mem_kv, w_in_a, conv_w_a, a_log, dt_bias, norm_out_a, w_out_a, w_in_b, w_out_b, norm_kv, w_kv_shared, cmp_pe_k, cmp_w1_k, cmp_w2_k, cmp_pe_v, cmp_w1_v, cmp_w2_v, w_gu_dense, w_down_dense, w_router, w_gu_exp, w_down_exp, norm_final):
    bp, lp, d = x_prompt.shape
    bs, ls, _ = x_sample.shape
    depth = norm_mix.shape[0]
    n_a = w_in_a.shape[0]
    past = page_table.shape[1] * PAGE
    tp = bp * lp
    T = tp + bs * S_PAD
    tm = 512

    xs = jnp.pad(x_sample, ((0, 0), (0, S_PAD - ls), (0, 0)))
    H = jnp.concatenate([x_prompt.reshape(tp, d), xs.reshape(bs * S_PAD, d)], axis=0)

    qkvz = MEM_W + CONV_DIM + A_W
    w_in_a_b = jnp.concatenate(
        [w_in_a[:, :, MEM_W:qkvz], w_in_a[:, :, :MEM_W], w_in_a[:, :, qkvz:],
         jnp.zeros(w_in_a.shape[:2] + (A_IN_PAD - w_in_a.shape[2],), f32)], axis=2).astype(bf16)
    w_in_b_b = jnp.pad(w_in_b, ((0, 0), (0, 0), (0, B_IN_PAD - w_in_b.shape[2]))).astype(bf16)
    w_out_a_b = w_out_a.astype(bf16)
    w_out_b_b = w_out_b.astype(bf16)
    w_gu_dense_b = w_gu_dense.astype(bf16)
    w_down_dense_b = w_down_dense.astype(bf16)
    w_gu_exp_b = w_gu_exp.astype(bf16)
    w_down_exp_b = w_down_exp.astype(bf16)
    w_router_p = jnp.pad(w_router, ((0, 0), (0, 0), (0, LANES - N_EXPERTS)))

    mem_k, mem_v = mem_kv(mem_prompt.reshape(bp * MEM_LEN, d), norm_mem.reshape(depth, 1, d),
                          w_mem_kv.astype(bf16), tm)
    cmk = jnp.transpose(cache_mem_k, (0, 1, 3, 4, 2)).reshape(depth, bs, MEM_W, MEM_LEN)
    cmv = jnp.transpose(cache_mem_v, (0, 1, 3, 4, 2)).reshape(depth, bs, MEM_W, MEM_LEN)
    zero_buf = jnp.zeros((bp, SUBLANES, CONV_DIM), f32)
    zero_state = jnp.zeros((1, bp, A_HEADS, LANES, LANES), f32)

    pos_p = jnp.arange(lp, dtype=jnp.int32)
    pos_s = past + jnp.arange(S_PAD, dtype=jnp.int32)
    cos_p, sin_p = _rope_tables(pos_p)
    cos_s, sin_s = _rope_tables(pos_s)

    conv_p, ssm_p, conv_s, ssm_s = [], [], [], []
    kv = None
    for layer in range(depth):
        is_a = layer < n_a
        g_mix = norm_mix[layer].reshape(1, d)
        if is_a:
            P = norm_matmul(H, g_mix, w_in_a_b, layer, tm, A_IN_PAD // 3)
            qblk = A_MEMQ_BLK
        else:
            lb = layer - n_a
            qblk = 0
            if lb == 0:
                cos_all = jnp.concatenate([jnp.tile(cos_p, (bp, 1)), jnp.tile(cos_s, (bs, 1))], axis=0)
                sin_all = jnp.concatenate([jnp.tile(sin_p, (bp, 1)), jnp.tile(sin_s, (bs, 1))], axis=0)
                kv = shared_kv(H, norm_kv.reshape(1, d), w_kv_shared.astype(bf16), cos_all, sin_all, tm)
                ncmp = (lp - CMP_BLK) // CMP_STRIDE + 1
                cpos = jnp.arange(LANES, dtype=jnp.int32) * CMP_STRIDE + CMP_BLK - 1
                cos_c, sin_c = _rope_tables(cpos)
                pek, w1k, w2k = _compress_consts(cmp_pe_k, cmp_w1_k, cmp_w2_k)
                pev, w1v, w2v = _compress_consts(cmp_pe_v, cmp_w1_v, cmp_w2_v)
                consts = (pek, pev, w1k, w1v, w2k, w2v, cos_c, sin_c)
                strips = lp // 16
                ck_p, cv_p = compress(kv[0][:tp].reshape(bp, strips, 16 * LANES),
                                      kv[1][:tp].reshape(bp, strips, 16 * LANES),
                                      [lambda b: (b, 0, 0)], [lambda b: (b, 0, 0)], strips, bp, consts)
                npages = page_table.shape[1]
                pt_flat = page_table.reshape(-1)
                pool_strips = PAGE // 16
                pmap = lambda j: (lambda b, pt: (pt[b * npages + j], 0, 0))
                ck_s, cv_s = compress(cache_cmp_k.reshape(-1, pool_strips, 16 * LANES),
                                      cache_cmp_v.reshape(-1, pool_strips, 16 * LANES),
                                      [pmap(j) for j in range(npages)], [pmap(j) for j in range(npages)],
                                      pool_strips, bs, consts, prefetch=pt_flat)
                mt_p = _cmp_to_sel(ncmp, lp // SEL_BLK).T
                et_p = _block_expand(lp).T
                ncmp_s = (past + ls - CMP_BLK) // CMP_STRIDE + 1
                mt_s = _cmp_to_sel(ncmp_s, -(-(past + ls) // SEL_BLK)).T
                e_s = _block_expand(past)
                eye = jnp.eye(LANES, dtype=bf16)
                wk_pad = jnp.pad(kv[4][:tp].reshape(bp, lp, LANES), ((0, 0), (WINDOW, 0), (0, 0)))
                wv_pad = jnp.pad(kv[5][:tp].reshape(bp, lp, LANES), ((0, 0), (WINDOW, 0), (0, 0)))
                pool_sk = jnp.transpose(cache_sel_k, (0, 2, 3, 1)).reshape(-1, LANES, PAGE)
                pool_sv = jnp.transpose(cache_sel_v, (0, 2, 3, 1)).reshape(-1, LANES, PAGE)
                cwk = jnp.transpose(cache_win_k, (0, 2, 3, 1)).reshape(bs, LANES, -1)
                cwv = jnp.transpose(cache_win_v, (0, 2, 3, 1)).reshape(bs, LANES, -1)
            P = norm_matmul(H, g_mix, w_in_b_b, lb, tm, B_IN_PAD)
        om = (mem_attn_prompt(P, qblk, mem_k, mem_v, layer, bp, lp, 512),
              mem_attn_sample(P, qblk, tp, cmk, cmv, layer, 8))
        if is_a:
            conv8 = jnp.pad(conv_w_a[layer], ((0, SUBLANES - conv_w_a.shape[1]), (0, 0)))
            ab8 = jnp.zeros((SUBLANES, LANES), f32)
            ab8 = ab8.at[0, A_HEADS:2 * A_HEADS].set(a_log[layer]).at[1, A_HEADS:2 * A_HEADS].set(dt_bias[layer])
            gout = norm_out_a[layer].reshape(1, LANES)
            mo_p, sp = delta_mixer(P, 0, bp, lp, 512, 64, 512, zero_buf, conv8, ab8, gout, zero_state, 0)
            buf_s = jnp.pad(state_conv[layer], ((0, 0), (SUBLANES - state_conv.shape[2], 0), (0, 0)))
            mo_s, ss = delta_mixer(P, tp, bs, S_PAD, S_PAD, S_PAD, ls, buf_s, conv8, ab8, gout, state_ssm, layer)
            conv_p.append(jnp.stack([P[b * lp + lp - 3:(b + 1) * lp, :CONV_DIM] for b in range(bp)]))
            conv_s.append(P[tp:, :CONV_DIM].reshape(bs, S_PAD, CONV_DIM)[:, ls - 3:ls])
            ssm_p.append(sp)
            ssm_s.append(ss)
            w_out, wl = w_out_a_b, layer
        else:
            mo_p = nsa_prompt(P, cos_p, sin_p, ck_p, cv_p, kv[2], kv[3], wk_pad, wv_pad, mt_p, et_p, eye,
                              bp, lp, Q_BLOCK, 512)
            mo_s = nsa_sample(P, tp, pt_flat, pool_sk, pool_sv, cos_s, sin_s, ck_s, cv_s, kv[2], kv[3], cwk, cwv,
                              kv[4], kv[5], mt_s, e_s, eye)
            w_out, wl = w_out_b_b, lb
        mo = (mo_p, mo_s)
        g_ffn = norm_ffn[layer].reshape(1, d)
        if layer % 2 == 0:
            H = outproj_ffn(H, mo, om, w_out, g_ffn, w_gu_dense_b, w_down_dense_b, wl, layer // 2, tm, D_FF // 2)
        else:
            H = outproj_moe(H, mo, om, w_out, g_ffn, w_router_p, w_gu_exp_b, w_down_exp_b, wl, layer // 2,
                            tm, D_FF_EXPERT // 2)
    Y = final_norm(H, norm_final.reshape(1, d), tm)

    def rows_p(a):
        return a[:tp].reshape(bp, lp, 2, HEAD_DIM)

    def rows_s(a):
        return a[tp:].reshape(bs, S_PAD, 2, HEAD_DIM)[:, :ls]

    wlen = min(WINDOW, lp)
    return (Y[:tp].reshape(bp, lp, d), Y[tp:].reshape(bs, S_PAD, d)[:, :ls],
            jnp.stack(conv_p), jnp.stack(ssm_p),
            rows_p(kv[0]), rows_p(kv[1]), rows_p(kv[2]), rows_p(kv[3]),
            rows_p(kv[4])[:, lp - wlen:], rows_p(kv[5])[:, lp - wlen:],
            mem_k.reshape(depth, bp, MEM_LEN, 4, HEAD_DIM), mem_v.reshape(depth, bp, MEM_LEN, 4, HEAD_DIM),
            jnp.stack(conv_s), jnp.stack(ssm_s),
            rows_s(kv[0]), rows_s(kv[1]), rows_s(kv[2]), rows_s(kv[3]),
            jnp.concatenate([cache_win_k[:, ls:], rows_s(kv[4])], axis=1),
            jnp.concatenate([cache_win_v[:, ls:], rows_s(kv[5])], axis=1))
```

```python
import functools
import math

import jax
import jax.numpy as jnp
from jax import lax
from jax.experimental import pallas as pl
from jax.experimental.pallas import tpu as pltpu

f32 = jnp.float32
bf16 = jnp.bfloat16

EPS = 1e-6
NEG = -1e30
ROPE_THETA = 10000.0
HEAD_DIM = 64
LANES = 128
SUBLANES = 8
VMEM_LIMIT = 48 * 1024 * 1024

D_MODEL = 1024
MEM_LEN = 256
MEM_W = 256
A_HEADS = 6
A_W = 768
CONV_DIM = 2304
A_IN_PAD = 3456
A_MEMQ_BLK = 12
A_BA_BLK = 26
B_HEADS = 12
HPG = 6
B_W = 768
B_IN_PAD = 1152
CMP_BLK = 32
CMP_STRIDE = 16
CMP_HID = 256
SEL_BLK = 64
SEL_TOP = 16
WINDOW = 512
Q_BLOCK = 128
D_FF = 2816
N_EXPERTS = 8
D_FF_EXPERT = 3584
PAGE = 128
S_PAD = 8


def _cparams(sem):
    return pltpu.CompilerParams(dimension_semantics=sem, vmem_limit_bytes=VMEM_LIMIT)


def _nt(a, b):
    return lax.dot_general(a, b, (((1,), (1,)), ((), ())), preferred_element_type=f32)


def _tn(a, b):
    return lax.dot_general(a, b, (((0,), (0,)), ((), ())), preferred_element_type=f32)


def _mm(a, b):
    return jnp.dot(a, b, preferred_element_type=f32)


def _rms(x, g):
    return (x * lax.rsqrt(jnp.mean(x * x, axis=-1, keepdims=True) + EPS)) * g


def _sigmoid(x):
    return 1.0 / (1.0 + jnp.exp(-x))


def _silu(x):
    return x * _sigmoid(x)


def _softplus(x):
    return jnp.maximum(x, 0.0) + jnp.log(1.0 + jnp.exp(-jnp.abs(x)))


def _rope_lanes(x, cos, sin_signed):
    lane = lax.broadcasted_iota(jnp.int32, x.shape, x.ndim - 1)
    lo = (lane % HEAD_DIM) < (HEAD_DIM // 2)
    partner = jnp.where(lo, pltpu.roll(x, LANES - HEAD_DIM // 2, axis=x.ndim - 1),
                        pltpu.roll(x, HEAD_DIM // 2, axis=x.ndim - 1))
    return x * cos + partner * sin_signed


def _norm_matmul_kernel(x_ref, g_ref, w_ref, o_ref, xn_ref):
    @pl.when(pl.program_id(1) == 0)
    def _():
        xn_ref[...] = _rms(x_ref[...], g_ref[...]).astype(bf16)

    o_ref[...] = _mm(xn_ref[...], w_ref[...])


def norm_matmul(x, g, w, layer, tm, tn):
    T, K = x.shape
    N = w.shape[2]
    return pl.pallas_call(
        _norm_matmul_kernel,
        grid=(T // tm, N // tn),
        in_specs=[pl.BlockSpec((tm, K), lambda i, j: (i, 0)),
                  pl.BlockSpec((1, K), lambda i, j: (0, 0)),
                  pl.BlockSpec((None, K, tn), lambda i, j: (layer, 0, j))],
        out_specs=pl.BlockSpec((tm, tn), lambda i, j: (i, j)),
        out_shape=jax.ShapeDtypeStruct((T, N), f32),
        scratch_shapes=[pltpu.VMEM((tm, K), bf16)],
        compiler_params=_cparams(("parallel", "arbitrary")),
        name="norm_matmul",
    )(x, g, w)


def _mem_kv_kernel(x_ref, g_ref, w_ref, k_ref, v_ref):
    y = _mm(_rms(x_ref[...], g_ref[...]).astype(bf16), w_ref[...])
    k_ref[...] = y[:, :MEM_W]
    v_ref[...] = y[:, MEM_W:]


def mem_kv(mem, g, w, tm):
    R, K = mem.shape
    NL = w.shape[0]
    out = jax.ShapeDtypeStruct((NL, R, MEM_W), f32)
    return pl.pallas_call(
        _mem_kv_kernel,
        grid=(NL, R // tm),
        in_specs=[pl.BlockSpec((tm, K), lambda l, i: (i, 0)),
                  pl.BlockSpec((None, 1, K), lambda l, i: (l, 0, 0)),
                  pl.BlockSpec((None, K, 2 * MEM_W), lambda l, i: (l, 0, 0))],
        out_specs=(pl.BlockSpec((None, tm, MEM_W), lambda l, i: (l, i, 0)),
                   pl.BlockSpec((None, tm, MEM_W), lambda l, i: (l, i, 0))),
        out_shape=(out, out),
        compiler_params=_cparams(("parallel", "parallel")),
        name="mem_kv",
    )(mem, g, w)


def _final_norm_kernel(x_ref, g_ref, o_ref):
    o_ref[...] = _rms(x_ref[...], g_ref[...])


def final_norm(x, g, tm):
    T, K = x.shape
    return pl.pallas_call(
        _final_norm_kernel,
        grid=(T // tm,),
        in_specs=[pl.BlockSpec((tm, K), lambda i: (i, 0)), pl.BlockSpec((1, K), lambda i: (0, 0))],
        out_specs=pl.BlockSpec((tm, K), lambda i: (i, 0)),
        out_shape=jax.ShapeDtypeStruct((T, K), f32),
        compiler_params=_cparams(("parallel",)),
        name="final_norm",
    )(x, g)


def _mem_attend(q, k, v, transposed=False):
    tq = q.shape[0]
    nh = MEM_W // HEAD_DIM
    lane = lax.broadcasted_iota(jnp.int32, (tq, MEM_W), 1)
    masks = [(lane >= HEAD_DIM * h) & (lane < HEAD_DIM * (h + 1)) for h in range(nh)]
    qs = jnp.concatenate([jnp.where(m, q, 0.0) for m in masks], axis=0).astype(bf16)
    s = (_mm(qs, k) if transposed else _nt(qs, k)) * (HEAD_DIM ** -0.5)
    p = jnp.exp(s - jnp.max(s, axis=-1, keepdims=True))
    l = jnp.sum(p, axis=-1, keepdims=True)
    pb = p.astype(bf16)
    o = (_nt(pb, v) if transposed else _mm(pb, v)) / l
    out = jnp.zeros((tq, MEM_W), f32)
    for h in range(nh):
        out = jnp.where(masks[h], o[h * tq:(h + 1) * tq], out)
    return out


def _mem_attn_prompt_kernel(q_ref, k_ref, v_ref, o_ref):
    o_ref[...] = _mem_attend(q_ref[...], k_ref[...].astype(bf16), v_ref[...].astype(bf16))


def mem_attn_prompt(P, qblk, mk, mv, layer, nb, L, tq):
    nq = L // tq
    return pl.pallas_call(
        _mem_attn_prompt_kernel,
        grid=(nb, nq),
        in_specs=[pl.BlockSpec((tq, MEM_W), lambda b, i: (b * nq + i, qblk)),
                  pl.BlockSpec((None, MEM_LEN, MEM_W), lambda b, i: (layer, b, 0)),
                  pl.BlockSpec((None, MEM_LEN, MEM_W), lambda b, i: (layer, b, 0))],
        out_specs=pl.BlockSpec((tq, MEM_W), lambda b, i: (b * nq + i, 0)),
        out_shape=jax.ShapeDtypeStruct((nb * L, MEM_W), f32),
        compiler_params=_cparams(("parallel", "parallel")),
        name="mem_attn_prompt",
    )(P, mk, mv)


def _mem_attn_sample_kernel(q_ref, k_ref, v_ref, o_ref, *, bb):
    for b in range(bb):
        rows = slice(b * S_PAD, (b + 1) * S_PAD)
        o_ref[rows, :] = _mem_attend(q_ref[rows, :], k_ref[b].astype(bf16), v_ref[b].astype(bf16), transposed=True)


def mem_attn_sample(P, qblk, row0, ckt, cvt, layer, bb):
    nb = ckt.shape[1]
    blk0 = row0 // (bb * S_PAD)
    return pl.pallas_call(
        functools.partial(_mem_attn_sample_kernel, bb=bb),
        grid=(nb // bb,),
        in_specs=[pl.BlockSpec((bb * S_PAD, MEM_W), lambda i: (blk0 + i, qblk)),
                  pl.BlockSpec((None, bb, MEM_W, MEM_LEN), lambda i: (layer, i, 0, 0)),
                  pl.BlockSpec((None, bb, MEM_W, MEM_LEN), lambda i: (layer, i, 0, 0))],
        out_specs=pl.BlockSpec((bb * S_PAD, MEM_W), lambda i: (i, 0)),
        out_shape=jax.ShapeDtypeStruct((nb * S_PAD, MEM_W), f32),
        compiler_params=_cparams(("parallel",)),
        name="mem_attn_sample",
    )(P, ckt, cvt)


def _delta_kernel(q_ref, k_ref, v_ref, z_ref, ba_ref, buf_ref, cw_ref, ab_ref, go_ref, s0_ref,
                  o_ref, sout_ref, qs_ref, ks_ref, vs_ref, gs_ref, bs_ref, tail_ref, s_ref,
                  *, TL, C, valid):
    t = pl.program_id(1)

    @pl.when(t == 0)
    def _():
        tail_ref[...] = buf_ref[...]
        s_ref[...] = s0_ref[...]

    row8 = lax.broadcasted_iota(jnp.int32, (SUBLANES, A_W), 0)

    def conv(x_ref, c0):
        x = x_ref[...]
        b8 = tail_ref[:, c0:c0 + A_W]
        c8 = cw_ref[:, c0:c0 + A_W]
        acc = x * c8[3:4, :]
        for s in range(1, 4):
            r = pltpu.roll(x, s, axis=0)
            fix = jnp.where(row8 < s, pltpu.roll(b8, s, axis=0), r[0:SUBLANES])
            r = jnp.concatenate([fix, r[SUBLANES:]], axis=0) if TL > SUBLANES else fix
            acc = acc + r * c8[3 - s:4 - s, :]
        tail_ref[:, c0:c0 + A_W] = x[TL - SUBLANES:TL]
        return _silu(acc)

    q = conv(q_ref, 0)
    k = conv(k_ref, A_W)
    v = conv(v_ref, 2 * A_W)
    ba = ba_ref[...]
    ab = ab_ref[...]
    beta_all = _sigmoid(ba)
    g_all = -jnp.exp(ab[0:1, :]) * _softplus(ba + ab[1:2, :])
    if valid < TL:
        live = lax.broadcasted_iota(jnp.int32, (TL, A_W), 0) < valid
        live1 = lax.broadcasted_iota(jnp.int32, (TL, LANES), 0) < valid
        k = jnp.where(live, k, 0.0)
        v = jnp.where(live, v, 0.0)
        beta_all = jnp.where(live1, beta_all, 0.0)
        g_all = jnp.where(live1, g_all, 0.0)
    rowc = lax.broadcasted_iota(jnp.int32, (TL, LANES), 0) & (C - 1)
    gc_all = g_all
    s = 1
    while s < C:
        gc_all = gc_all + jnp.where(rowc >= s, pltpu.roll(gc_all, s, axis=0), 0.0)
        s *= 2
    vs_ref[...] = v
    for h in range(A_HEADS):
        sl = slice(h * LANES, (h + 1) * LANES)
        qh = q[:, sl]
        kh = k[:, sl]
        qs_ref[:, sl] = qh * lax.rsqrt(jnp.sum(qh * qh, axis=-1, keepdims=True) + EPS) * (LANES ** -0.5)
        ks_ref[:, sl] = kh * lax.rsqrt(jnp.sum(kh * kh, axis=-1, keepdims=True) + EPS)
        gs_ref[:, sl] = jnp.broadcast_to(gc_all[:, A_HEADS + h:A_HEADS + h + 1], (TL, LANES))
        bs_ref[:, sl] = jnp.broadcast_to(beta_all[:, h:h + 1], (TL, LANES))

    ii = lax.broadcasted_iota(jnp.int32, (C, C), 0)
    jj = lax.broadcasted_iota(jnp.int32, (C, C), 1)
    tri = ii >= jj
    strict = ii > jj
    eye = jnp.where(ii == jj, 1.0, 0.0)
    gout = go_ref[...]
    nsteps = int(math.log2(C)) - 1

    heads = range(A_HEADS)
    lanes_of = [slice(h * LANES, (h + 1) * LANES) for h in heads]

    cpi = 2 if TL // C >= 2 else 1

    def chunks(c, carry):
        rows_of = [pl.ds(pl.multiple_of((c * cpi + i) * C, C), C) for i in range(cpi)]
        units = [(rows, h) for rows in rows_of for h in heads]
        U = range(len(units))
        qc = [qs_ref[rows, lanes_of[h]] for rows, h in units]
        kc = [ks_ref[rows, lanes_of[h]] for rows, h in units]
        vc = [vs_ref[rows, lanes_of[h]] for rows, h in units]
        gcc = [gs_ref[rows, lanes_of[h]] for rows, h in units]
        bc = [bs_ref[rows, lanes_of[h]] for rows, h in units]
        kb = [kc[u] * bc[u] for u in U]
        kq = [_nt(jnp.concatenate([kb[u], qc[u]], axis=0).astype(bf16), kc[u].astype(bf16)) for u in U]
        decay = []
        for u in U:
            gi = gcc[u][:, 0:C]
            gj = jnp.sum(jnp.where(ii == jj, gi, 0.0), axis=0, keepdims=True)
            decay.append(jnp.where(tri, jnp.exp(jnp.where(tri, gi - gj, 0.0)), 0.0))
        n = [jnp.where(strict, -(kq[u][0:C] * decay[u]), 0.0) for u in U]
        a = [jnp.where(tri, kq[u][C:2 * C] * decay[u], 0.0).astype(bf16) for u in U]
        x = [eye + n[u] for u in U]
        p = n
        for _ in range(nsteps):
            pb = [p[u].astype(bf16) for u in U]
            p = [_mm(pb[u], pb[u]) for u in U]
            x = [x[u] + _mm(x[u].astype(bf16), p[u].astype(bf16)) for u in U]
        eg = [jnp.exp(gcc[u]) for u in U]
        uw = [_mm(x[u].astype(bf16), jnp.concatenate([vc[u] * bc[u], kb[u] * eg[u]], axis=1).astype(bf16))
              for u in U]
        wqin = [jnp.concatenate([uw[u][:, LANES:], qc[u] * eg[u]], axis=0).astype(bf16) for u in U]
        glast = [gcc[u][C - 1:C, :] for u in U]
        kg = [(kc[u] * jnp.exp(glast[u] - gcc[u])).astype(bf16) for u in U]
        for i, rows in enumerate(rows_of):
            us = [i * A_HEADS + h for h in heads]
            S = [s_ref[h] for h in heads]
            wq = [_mm(wqin[u], S[h].astype(bf16)) for h, u in zip(heads, us)]
            vnb = [(uw[u][:, :LANES] - wq[h][0:C]).astype(bf16) for h, u in zip(heads, us)]
            av = [_mm(a[u], vnb[h]) for h, u in zip(heads, us)]
            kv_new = [_tn(kg[u], vnb[h]) for h, u in zip(heads, us)]
            for h, u in zip(heads, us):
                s_ref[h] = S[h] * jnp.exp(glast[u]) + kv_new[h]
                o_ref[rows, lanes_of[h]] = _rms(wq[h][C:2 * C] + av[h], gout) * _silu(z_ref[rows, lanes_of[h]])
        return carry

    if TL // C == cpi:
        chunks(0, 0)
    else:
        lax.fori_loop(0, TL // (C * cpi), chunks, 0)

    @pl.when(t == pl.num_programs(1) - 1)
    def _():
        sout_ref[...] = s_ref[...]


def delta_mixer(P, row0, nb, L, TL, C, valid, buf8, conv8, ab8, gout, s0, layer):
    nt = L // TL
    rb0 = row0 // TL
    tile = lambda c: pl.BlockSpec((TL, A_W), lambda b, t: (rb0 + b * nt + t, c))
    state = pl.BlockSpec((None, A_HEADS, LANES, LANES), lambda b, t: (b, 0, 0, 0))
    state_in = pl.BlockSpec((None, None, A_HEADS, LANES, LANES), lambda b, t: (layer, b, 0, 0, 0))
    return pl.pallas_call(
        functools.partial(_delta_kernel, TL=TL, C=C, valid=valid),
        grid=(nb, nt),
        in_specs=[tile(0), tile(1), tile(2), tile(3),
                  pl.BlockSpec((TL, LANES), lambda b, t: (rb0 + b * nt + t, A_BA_BLK)),
                  pl.BlockSpec((None, SUBLANES, CONV_DIM), lambda b, t: (b, 0, 0)),
                  pl.BlockSpec((SUBLANES, CONV_DIM), lambda b, t: (0, 0)),
                  pl.BlockSpec((SUBLANES, LANES), lambda b, t: (0, 0)),
                  pl.BlockSpec((1, LANES), lambda b, t: (0, 0)),
                  state_in],
        out_specs=(pl.BlockSpec((TL, A_W), lambda b, t: (b * nt + t, 0)), state),
        out_shape=(jax.ShapeDtypeStruct((nb * L, A_W), f32),
                   jax.ShapeDtypeStruct((nb, A_HEADS, LANES, LANES), f32)),
        scratch_shapes=[pltpu.VMEM((TL, A_W), f32)] * 5
        + [pltpu.VMEM((SUBLANES, CONV_DIM), f32), pltpu.VMEM((A_HEADS, LANES, LANES), f32)],
        compiler_params=_cparams(("parallel", "arbitrary")),
        name="delta_mixer",
    )(P, P, P, P, P, buf8, conv8, ab8, gout, s0)


def _attn_residual(refs, ntp):
    h_ref, mop_ref, mos_ref, omp_ref, oms_ref, wo1_ref, wo2_ref = refs
    is_p = pl.program_id(0) < ntp
    mo = jnp.where(is_p, mop_ref[...], mos_ref[...])
    om = jnp.where(is_p, omp_ref[...], oms_ref[...])
    return h_ref[...] + _mm(mo.astype(bf16), wo1_ref[...]) + _mm(om.astype(bf16), wo2_ref[...])


def _ffn_kernel(*refs, ntp):
    g_ref, wg_ref, wu_ref, wd_ref, o_ref, xn_ref, acc_ref = refs[7:]
    k = pl.program_id(1)

    @pl.when(k == 0)
    def _():
        h1 = _attn_residual(refs[:7], ntp)
        acc_ref[...] = h1
        xn_ref[...] = _rms(h1, g_ref[...]).astype(bf16)

    xn = xn_ref[...]
    a = _silu(_mm(xn, wg_ref[...])) * _mm(xn, wu_ref[...])
    acc_ref[...] += _mm(a.astype(bf16), wd_ref[...])

    @pl.when(k == pl.num_programs(1) - 1)
    def _():
        o_ref[...] = acc_ref[...]


def _outproj_specs(tm, K, mw, ow, ntp, layer):
    prow = lambda i, *_: (jnp.minimum(i, ntp - 1), 0)
    srow = lambda i, *_: (jnp.maximum(i - ntp, 0), 0)
    return [pl.BlockSpec((tm, K), lambda i, *_: (i, 0)),
            pl.BlockSpec((tm, mw), prow), pl.BlockSpec((tm, mw), srow),
            pl.BlockSpec((tm, ow), prow), pl.BlockSpec((tm, ow), srow),
            pl.BlockSpec((None, mw, K), lambda i, *_: (layer, 0, 0)),
            pl.BlockSpec((None, ow, K), lambda i, *_: (layer, mw // ow, 0)),
            pl.BlockSpec((1, K), lambda i, *_: (0, 0))]


def outproj_ffn(H, mo, om, w_out, g, w_gu, w_down, layer, ffn_layer, tm, tf):
    T, K = H.shape
    FF = w_down.shape[1]
    nk = FF // tf
    mw, ow = mo[0].shape[1], om[0].shape[1]
    ntp = mo[0].shape[0] // tm
    return pl.pallas_call(
        functools.partial(_ffn_kernel, ntp=ntp),
        grid=(T // tm, nk),
        in_specs=_outproj_specs(tm, K, mw, ow, ntp, layer)
        + [pl.BlockSpec((None, K, tf), lambda i, k: (ffn_layer, 0, k)),
           pl.BlockSpec((None, K, tf), lambda i, k: (ffn_layer, 0, nk + k)),
           pl.BlockSpec((None, tf, K), lambda i, k: (ffn_layer, k, 0))],
        out_specs=pl.BlockSpec((tm, K), lambda i, k: (i, 0)),
        out_shape=jax.ShapeDtypeStruct((T, K), f32),
        scratch_shapes=[pltpu.VMEM((tm, K), bf16), pltpu.VMEM((tm, K), f32)],
        compiler_params=_cparams(("parallel", "arbitrary")),
        name="outproj_ffn",
    )(H, mo[0], mo[1], om[0], om[1], w_out, w_out, g, w_gu, w_gu, w_down)


def _router_kernel(*refs, ntp):
    g_ref, wr_ref, h1_ref, xn_ref, r_ref, cnt_ref = refs[7:]
    i = pl.program_id(0)
    tm = h1_ref.shape[0]

    @pl.when(i == 0)
    def _():
        cnt_ref[...] = jnp.zeros_like(cnt_ref)

    h1 = _attn_residual(refs[:7], ntp)
    h1_ref[...] = h1
    xn = _rms(h1, g_ref[...])
    xn_ref[...] = xn
    lane = lax.broadcasted_iota(jnp.int32, (tm, LANES), 1)
    xh = xn.astype(bf16)
    xl = (xn - xh.astype(f32)).astype(bf16)
    wr = wr_ref[...]
    wh = wr.astype(bf16)
    wl = (wr - wh.astype(f32)).astype(bf16)
    logits = _mm(xh, wh) + _mm(xh, wl) + _mm(xl, wh)
    logits = jnp.where(lane < N_EXPERTS, logits, -jnp.inf)
    m1 = jnp.max(logits, axis=-1, keepdims=True)
    i1 = jnp.min(jnp.where(logits == m1, lane, LANES), axis=-1, keepdims=True)
    rest = jnp.where(lane == i1, -jnp.inf, logits)
    m2 = jnp.max(rest, axis=-1, keepdims=True)
    i2 = jnp.min(jnp.where(rest == m2, lane, LANES), axis=-1, keepdims=True)
    e2 = jnp.exp(m2 - m1)
    w1 = 1.0 / (1.0 + e2)
    w2 = e2 / (1.0 + e2)
    hot = jnp.where((lane == i1) | (lane == i2), 1.0, 0.0)
    ri = lax.broadcasted_iota(jnp.int32, (tm, tm), 0)
    ci = lax.broadcasted_iota(jnp.int32, (tm, tm), 1)
    before = jnp.where(ri > ci, 1.0, 0.0).astype(bf16)
    cum = _mm(before, hot.astype(bf16)) + cnt_ref[...]
    r1 = jnp.sum(jnp.where(lane == i1, cum, 0.0), axis=-1, keepdims=True)
    r2 = jnp.sum(jnp.where(lane == i2, cum, 0.0), axis=-1, keepdims=True)
    cnt_ref[...] += jnp.sum(hot, axis=0, keepdims=True)
    cols = (i1.astype(f32), i2.astype(f32), w1, w2, r1, r2)
    r = jnp.zeros((tm, LANES), f32)
    for c, val in enumerate(cols):
        r = jnp.where(lane == c, val, r)
    r_ref[...] = r


def outproj_router(H, mo, om, w_out, g, w_router, layer, moe_layer, tm):
    T, K = H.shape
    mw, ow = mo[0].shape[1], om[0].shape[1]
    ntp = mo[0].shape[0] // tm
    return pl.pallas_call(
        functools.partial(_router_kernel, ntp=ntp),
        grid=(T // tm,),
        in_specs=_outproj_specs(tm, K, mw, ow, ntp, layer)
        + [pl.BlockSpec((None, K, LANES), lambda i: (moe_layer, 0, 0))],
        out_specs=(pl.BlockSpec((tm, K), lambda i: (i, 0)), pl.BlockSpec((tm, K), lambda i: (i, 0)),
                   pl.BlockSpec((tm, LANES), lambda i: (i, 0))),
        out_shape=(jax.ShapeDtypeStruct((T, K), f32), jax.ShapeDtypeStruct((T, K), f32),
                   jax.ShapeDtypeStruct((T, LANES), f32)),
        scratch_shapes=[pltpu.VMEM((1, LANES), f32)],
        compiler_params=_cparams(("arbitrary",)),
        name="outproj_router",
    )(H, mo[0], mo[1], om[0], om[1], w_out, w_out, g, w_router)


def _moe_group_kernel(te_ref, nu_ref, x_ref, sw_ref, wg_ref, wu_ref, wd_ref, o_ref, acc_ref):
    del te_ref
    i = pl.program_id(0)
    k = pl.program_id(1)

    @pl.when(k == 0)
    def _():
        acc_ref[...] = jnp.zeros_like(acc_ref)

    @pl.when(i < nu_ref[0])
    def _():
        x = x_ref[...].astype(bf16)
        a = _silu(_mm(x, wg_ref[...])) * _mm(x, wu_ref[...])
        acc_ref[...] += _mm(a.astype(bf16), wd_ref[...])

    @pl.when(k == pl.num_programs(1) - 1)
    def _():
        o_ref[...] = acc_ref[...] * sw_ref[...]


def moe_grouped(xg, sw, tile_e, n_used, w_gu, w_down, moe_layer, tm, tf):
    NP, K = xg.shape
    FF = w_down.shape[2]
    nk = FF // tf

    def kk(i, k, nu):
        return jnp.where(i < nu[0], k, nk - 1)

    grid_spec = pltpu.PrefetchScalarGridSpec(
        num_scalar_prefetch=2, grid=(NP // tm, nk),
        in_specs=[pl.BlockSpec((tm, K), lambda i, k, te, nu: (i, 0)),
                  pl.BlockSpec((tm, 1), lambda i, k, te, nu: (i, 0)),
                  pl.BlockSpec((None, None, K, tf), lambda i, k, te, nu: (moe_layer, te[i], 0, kk(i, k, nu))),
                  pl.BlockSpec((None, None, K, tf), lambda i, k, te, nu: (moe_layer, te[i], 0, nk + kk(i, k, nu))),
                  pl.BlockSpec((None, None, tf, K), lambda i, k, te, nu: (moe_layer, te[i], kk(i, k, nu), 0))],
        out_specs=pl.BlockSpec((tm, K), lambda i, k, te, nu: (i, 0)),
        scratch_shapes=[pltpu.VMEM((tm, K), f32)])
    return pl.pallas_call(
        _moe_group_kernel, grid_spec=grid_spec,
        out_shape=jax.ShapeDtypeStruct((NP, K), f32),
        compiler_params=_cparams(("parallel", "arbitrary")),
        name="moe_grouped",
    )(tile_e, n_used, xg, sw, w_gu, w_gu, w_down)


def _dispatch(r, tm):
    T = r.shape[0]
    A = 2 * T
    ntiles = A // tm + N_EXPERTS
    NP = ntiles * tm
    e_flat = r[:, 0:2].astype(jnp.int32).reshape(A)
    w_flat = r[:, 2:4].reshape(A)
    rank = r[:, 4:6].astype(jnp.int32)
    counts = jnp.sum((e_flat[:, None] == jnp.arange(N_EXPERTS)[None, :]).astype(jnp.int32), axis=0)
    pcounts = (counts + tm - 1) // tm * tm
    pends = jnp.cumsum(pcounts)
    pstarts = pends - pcounts
    n_used = pends[-1] // tm
    d = jnp.arange(NP - A, dtype=jnp.int32)
    e_pad = jnp.sum((d[:, None] >= jnp.cumsum(pcounts - counts)[None, :]).astype(jnp.int32), axis=1)
    keys = jnp.concatenate([e_flat * NP + jnp.arange(A, dtype=jnp.int32), e_pad * NP + A + d])
    toks = jnp.concatenate([jnp.arange(A, dtype=jnp.int32) // 2, jnp.zeros((NP - A,), jnp.int32)])
    wts = jnp.concatenate([w_flat, jnp.zeros((NP - A,), f32)])
    _, src, sw = lax.sort((keys, toks, wts), num_keys=1)
    sw = sw.reshape(NP, 1)
    tiles = jnp.arange(ntiles, dtype=jnp.int32)
    tile_e = jnp.minimum(jnp.searchsorted(pends, tiles * tm, side='right'), N_EXPERTS - 1).astype(jnp.int32)
    tile_e = jnp.where(tiles < n_used, tile_e, tile_e[jnp.maximum(n_used - 1, 0)])
    picks = r[:, 0:2].astype(jnp.int32)
    pick_start = jnp.sum(jnp.where(picks[:, :, None] == jnp.arange(N_EXPERTS)[None, None, :],
                                   pstarts[None, None, :], 0), axis=2)
    slots = pick_start + rank
    return src, sw, tile_e, n_used.reshape(1).astype(jnp.int32), slots


def outproj_moe(H, mo, om, w_out, g, w_router, w_gu, w_down, layer, moe_layer, tm, tf):
    h1, xn, r = outproj_router(H, mo, om, w_out, g, w_router, layer, moe_layer, tm)
    src, sw, tile_e, n_used, slots = _dispatch(r, tm)
    yg = moe_grouped(jnp.take(xn, src, axis=0), sw, tile_e, n_used, w_gu, w_down, moe_layer, tm, tf)
    return h1 + jnp.take(yg, slots[:, 0], axis=0) + jnp.take(yg, slots[:, 1], axis=0)


def _shared_kv_kernel(x_ref, g_ref, w_ref, cos_ref, sin_ref, ck_ref, cv_ref, sk_ref, sv_ref, wk_ref, wv_ref):
    y = _mm(_rms(x_ref[...], g_ref[...]).astype(bf16), w_ref[...])
    cos = cos_ref[...]
    sin = sin_ref[...]
    ck_ref[...] = y[:, 0 * LANES:1 * LANES]
    cv_ref[...] = y[:, 1 * LANES:2 * LANES]
    sk_ref[...] = _rope_lanes(y[:, 2 * LANES:3 * LANES], cos, sin)
    sv_ref[...] = y[:, 3 * LANES:4 * LANES]
    wk_ref[...] = _rope_lanes(y[:, 4 * LANES:5 * LANES], cos, sin)
    wv_ref[...] = y[:, 5 * LANES:6 * LANES]


def shared_kv(H, g, w, cos, sin, tm):
    T, K = H.shape
    row = pl.BlockSpec((tm, LANES), lambda i: (i, 0))
    out = jax.ShapeDtypeStruct((T, LANES), f32)
    return pl.pallas_call(
        _shared_kv_kernel,
        grid=(T // tm,),
        in_specs=[pl.BlockSpec((tm, K), lambda i: (i, 0)), pl.BlockSpec((1, K), lambda i: (0, 0)),
                  pl.BlockSpec((K, 6 * LANES), lambda i: (0, 0)), row, row],
        out_specs=(row,) * 6,
        out_shape=(out,) * 6,
        compiler_params=_cparams(("parallel",)),
        name="shared_kv",
    )(H, g, w, cos, sin)


def _compress_kernel(*refs, npages):
    nin = 2 * npages
    k_strips = jnp.concatenate([r[...] for r in refs[0:npages]], axis=0) if npages > 1 else refs[0][...]
    v_strips = jnp.concatenate([r[...] for r in refs[npages:nin]], axis=0) if npages > 1 else refs[npages][...]
    (pek_ref, pev_ref, w1k_ref, w1v_ref, w2k_ref, w2v_ref, cos_ref, sin_ref, ck_ref, cv_ref) = refs[nin:]
    n = k_strips.shape[0]
    half = 2 * CMP_HID
    row = lax.broadcasted_iota(jnp.int32, (n, LANES), 0)

    def tokens(strips, pe_ref, w1_ref, w2_ref):
        top = _mm((strips + pe_ref[0:1, :]).astype(bf16), w1_ref[:, 0:half])
        bot = _mm((strips + pe_ref[1:2, :]).astype(bf16), w1_ref[:, half:2 * half])
        hid = top + pltpu.roll(bot, n - 1, axis=0)
        out = _mm(_silu(hid).astype(bf16), w2_ref[...])
        return jnp.where(row < n - 1, out, 0.0)

    ck_ref[...] = _rope_lanes(tokens(k_strips, pek_ref, w1k_ref, w2k_ref), cos_ref[...], sin_ref[...])
    cv_ref[...] = tokens(v_strips, pev_ref, w1v_ref, w2v_ref)


def _drop_first(fn, _prefetch_ref, *refs):
    return fn(*refs)


def compress(k_src, v_src, k_maps, v_maps, strip_rows, nb, consts, prefetch=None):
    npages = len(k_maps)
    n = npages * strip_rows
    pek, pev, w1k, w1v, w2k, w2v, cos, sin = consts
    flat = 16 * LANES
    const2 = lambda shape: pl.BlockSpec(shape, lambda b, *_: (0, 0))
    in_specs = ([pl.BlockSpec((None, strip_rows, flat), m) for m in k_maps]
                + [pl.BlockSpec((None, strip_rows, flat), m) for m in v_maps]
                + [const2((2, flat)), const2((2, flat)), const2((flat, 4 * CMP_HID)), const2((flat, 4 * CMP_HID)),
                   const2((2 * CMP_HID, LANES)), const2((2 * CMP_HID, LANES)), const2((n, LANES)), const2((n, LANES))])
    out_spec = pl.BlockSpec((None, n, LANES), lambda b, *_: (b, 0, 0))
    out = jax.ShapeDtypeStruct((nb, n, LANES), f32)
    grid_spec = pltpu.PrefetchScalarGridSpec(
        num_scalar_prefetch=0 if prefetch is None else 1, grid=(nb,),
        in_specs=in_specs, out_specs=(out_spec, out_spec))
    args = ([] if prefetch is None else [prefetch]) + [k_src] * npages + [v_src] * npages + [pek, pev, w1k, w1v, w2k, w2v, cos, sin]
    kern = functools.partial(_compress_kernel, npages=npages)
    if prefetch is not None:
        kern = functools.partial(_drop_first, kern)
    return pl.pallas_call(
        kern, grid_spec=grid_spec, out_shape=(out, out),
        compiler_params=_cparams(("parallel",)), name="compress",
    )(*args)


def _nsa_queries(p, cos, sin):
    tq = p.shape[0]
    lane = lax.broadcasted_iota(jnp.int32, (tq, LANES), 1)
    rows = []
    for head in range(B_HEADS):
        grp = head // HPG
        blk = MEM_W // LANES + head // 2
        x = _rope_lanes(p[:, blk * LANES:(blk + 1) * LANES], cos, sin) * (HEAD_DIM ** -0.5)
        if head % 2 != grp:
            x = pltpu.roll(x, HEAD_DIM, axis=1)
        keep = (lane >= grp * HEAD_DIM) & (lane < (grp + 1) * HEAD_DIM)
        rows.append(jnp.where(keep, x, 0.0))
    return jnp.concatenate(rows, axis=0)


def _masked_softmax_parts(s, mask):
    sm = jnp.where(mask, s, NEG)
    m = jnp.max(sm, axis=-1, keepdims=True)
    p = jnp.where(mask, jnp.exp(sm - m), 0.0)
    return p, jnp.sum(p, axis=-1, keepdims=True)


def _safe_div(o, l):
    return jnp.where(l > 0.0, o / jnp.where(l > 0.0, l, 1.0), 0.0)


def _split3(x):
    hi = x.astype(bf16)
    r1 = x - hi.astype(f32)
    mid = r1.astype(bf16)
    lo = (r1 - mid.astype(f32)).astype(bf16)
    return hi, mid, lo


def _select_bias(pcn, qpos_row, past_blocks, mt_ref, eye_ref, nsel):
    tq = qpos_row.shape[1]
    nr = -(-nsel // SUBLANES) * SUBLANES
    mt = mt_ref[0:nr, :]
    blk = lax.broadcasted_iota(jnp.int32, (nr, tq), 0)
    lane = lax.broadcasted_iota(jnp.int32, (tq, LANES), 1)
    cur = qpos_row // SEL_BLK
    forced = (blk == 0) | (blk == cur) | (blk == cur - 1)
    valid = blk * SEL_BLK <= qpos_row
    stacks, picks = [], []
    for grp in range(2):
        psum = pcn[grp * HPG * tq:(grp * HPG + 1) * tq]
        for hh in range(1, HPG):
            psum = psum + pcn[(grp * HPG + hh) * tq:(grp * HPG + hh + 1) * tq]
        hi, mid, lo = _split3(psum)
        imp = _nt(mt, hi) + _nt(mt, mid) + _nt(mt, lo)
        imp = jnp.where(forced, 1e9, jnp.where(valid, imp, -1e9))
        rank = jnp.zeros((nr, tq), f32)
        for j in range(nsel):
            rj = imp[j:j + 1, :]
            rank = rank + jnp.where(rj > imp, 1.0, jnp.where(rj == imp, jnp.where(blk > j, 1.0, 0.0), 0.0))
        sel_t = jnp.where((rank < SEL_TOP) & (blk < nsel), 1.0, 0.0).astype(bf16)
        pick = _tn(sel_t, eye_ref[0:nr, :])
        picks.append(pick)
        stacks += [jnp.where((pick > 0.5) & (lane < past_blocks), 0.0, NEG)] * HPG
    return jnp.concatenate(stacks, axis=0), picks


def _nsa_combine(p, o_c, o_s, o_w):
    tq = p.shape[0]
    gl = _sigmoid(p[:, (MEM_W + B_W):(MEM_W + B_W) + LANES])

    def gate(j):
        return jnp.concatenate([gl[:, 3 * h + j:3 * h + j + 1] for h in range(B_HEADS)], axis=0)

    comb = gate(0) * o_c + gate(1) * o_s + gate(2) * o_w
    lane = lax.broadcasted_iota(jnp.int32, (tq, LANES), 1)
    cols = []
    for c in range(B_HEADS // 2):
        lo = comb[2 * c * tq:(2 * c + 1) * tq]
        hi = comb[(2 * c + 1) * tq:(2 * c + 2) * tq]
        if (2 * c) // HPG == 1:
            lo = pltpu.roll(lo, HEAD_DIM, axis=1)
        if (2 * c + 1) // HPG == 0:
            hi = pltpu.roll(hi, HEAD_DIM, axis=1)
        cols.append(jnp.where(lane < HEAD_DIM, lo, hi))
    return jnp.concatenate(cols, axis=1)


def _nsa_prompt_kernel(p_ref, cos_ref, sin_ref, ck_ref, cv_ref, sk_ref, sv_ref, wk_ref, wv_ref, mt_ref, et_ref,
                       eye_ref, o_ref, *, tq, kc, nsel):
    qb = pl.program_id(1)
    s0 = qb * tq
    R = B_HEADS * tq
    p = p_ref[...]
    qf = _nsa_queries(p, cos_ref[...], sin_ref[...])
    qb16 = qf.astype(bf16)
    ii = lax.broadcasted_iota(jnp.int32, (tq, tq), 0)
    jj = lax.broadcasted_iota(jnp.int32, (tq, tq), 1)
    qposR = s0 + jnp.concatenate([lax.broadcasted_iota(jnp.int32, (tq, 1), 0)] * B_HEADS, axis=0)
    lane = lax.broadcasted_iota(jnp.int32, (R, LANES), 1)
    cmask = ((lane * CMP_STRIDE + (CMP_BLK - 1)) <= qposR) & (lane < ck_ref.shape[0] - 1)
    pc, lc = _masked_softmax_parts(_nt(qb16, ck_ref[...].astype(bf16)), cmask)
    pcn = _safe_div(pc, lc)
    o_c = _mm(pcn.astype(bf16), cv_ref[...].astype(bf16))
    qpos_row = s0 + lax.broadcasted_iota(jnp.int32, (1, tq), 1)
    bias, _ = _select_bias(pcn, qpos_row, s0 // SEL_BLK, mt_ref, eye_ref, nsel)
    qsel = jnp.concatenate([qb16, bias.astype(bf16)], axis=1)
    sd = _nt(qb16, sk_ref[pl.ds(s0, tq), :].astype(bf16)).reshape(B_HEADS, tq, tq)
    sd = jnp.where((ii >= jj)[None], sd, NEG).reshape(R, tq)
    m0 = jnp.max(sd, axis=-1, keepdims=True)
    pd = jnp.exp(sd - m0)
    init = (m0, jnp.sum(pd, axis=-1, keepdims=True), _mm(pd.astype(bf16), sv_ref[pl.ds(s0, tq), :].astype(bf16)))

    def body(c, carry):
        m_run, l_run, acc = carry
        k0 = pl.multiple_of(c * kc, kc)
        ka = jnp.concatenate([sk_ref[pl.ds(k0, kc), :].astype(bf16), et_ref[pl.ds(k0, kc), :]], axis=1)
        s = _nt(qsel, ka)
        m_new = jnp.maximum(m_run, jnp.max(s, axis=-1, keepdims=True))
        alpha = jnp.exp(m_run - m_new)
        pe = jnp.exp(s - m_new)
        l_new = alpha * l_run + jnp.sum(pe, axis=-1, keepdims=True)
        acc_new = alpha * acc + _mm(pe.astype(bf16), sv_ref[pl.ds(k0, kc), :].astype(bf16))
        return m_new, l_new, acc_new

    _, l_run, acc = lax.fori_loop(0, (s0 + kc - 1) // kc, body, init)
    o_s = acc / l_run
    wlen = WINDOW + tq
    wrow = lax.broadcasted_iota(jnp.int32, (wlen, LANES), 0)
    wlane = lax.broadcasted_iota(jnp.int32, (wlen, LANES), 1)
    is_pad = jnp.where((wrow + (s0 - WINDOW) < 0) & (wlane == 0), 1.0, 0.0).astype(bf16)
    kw = jnp.concatenate([wk_ref[pl.ds(s0, wlen), :].astype(bf16), is_pad], axis=1)
    qwin = jnp.concatenate([qb16, jnp.where(lane == 0, NEG, 0.0).astype(bf16)], axis=1)
    sw = _nt(qwin, kw).reshape(B_HEADS, tq, wlen)
    far = jnp.where((jj > ii)[None], sw[:, :, 0:tq], NEG)
    near = jnp.where((jj <= ii)[None], sw[:, :, wlen - tq:wlen], NEG)
    sw = jnp.concatenate([far, sw[:, :, tq:wlen - tq], near], axis=2).reshape(R, wlen)
    pw = jnp.exp(sw - jnp.max(sw, axis=-1, keepdims=True))
    o_w = _mm(pw.astype(bf16), wv_ref[pl.ds(s0, wlen), :].astype(bf16)) / jnp.sum(pw, axis=-1, keepdims=True)
    o_ref[...] = _nsa_combine(p, o_c, o_s, o_w)


def nsa_prompt(P, cos, sin, ck, cv, sk, sv, wk_pad, wv_pad, mt, et, eye, nb, L, tq, kc):
    nq = L // tq
    nsel = L // SEL_BLK
    seq = pl.BlockSpec((L, LANES), lambda b, i: (b, 0))
    win = pl.BlockSpec((None, WINDOW + L, LANES), lambda b, i: (b, 0, 0))
    cmp_spec = pl.BlockSpec((None, LANES, LANES), lambda b, i: (b, 0, 0))
    const = lambda shape: pl.BlockSpec(shape, lambda b, i: (0, 0))
    return pl.pallas_call(
        functools.partial(_nsa_prompt_kernel, tq=tq, kc=kc, nsel=nsel),
        grid=(nb, nq),
        in_specs=[pl.BlockSpec((tq, B_IN_PAD), lambda b, i: (b * nq + i, 0)),
                  pl.BlockSpec((tq, LANES), lambda b, i: (i, 0)),
                  pl.BlockSpec((tq, LANES), lambda b, i: (i, 0)),
                  cmp_spec, cmp_spec, seq, seq, win, win,
                  const((LANES, LANES)), const((L, LANES)), const((LANES, LANES))],
        out_specs=pl.BlockSpec((tq, B_W), lambda b, i: (b * nq + i, 0)),
        out_shape=jax.ShapeDtypeStruct((nb * L, B_W), f32),
        compiler_params=_cparams(("parallel", "parallel")),
        name="nsa_prompt",
    )(P, cos, sin, ck, cv, sk, sv, wk_pad, wv_pad, mt, et, eye)


def _nsa_sample_kernel(*refs, npages, past):
    refs = refs[1:]
    kpages = refs[0:npages]
    vpages = refs[npages:2 * npages]
    (p_ref, cos_ref, sin_ref, ck_ref, cv_ref, nsk_ref, nsv_ref, cwk_ref, cwv_ref, nwk_ref, nwv_ref,
     mt_ref, et_ref, eye_ref, o_ref) = refs[2 * npages:]
    tq = S_PAD
    R = B_HEADS * tq
    p = p_ref[...]
    qb16 = _nsa_queries(p, cos_ref[...], sin_ref[...]).astype(bf16)
    tR = jnp.concatenate([lax.broadcasted_iota(jnp.int32, (tq, 1), 0)] * B_HEADS, axis=0)
    qposR = past + tR
    lane = lax.broadcasted_iota(jnp.int32, (R, LANES), 1)
    cmask = ((lane * CMP_STRIDE + (CMP_BLK - 1)) <= qposR) & (lane < ck_ref.shape[0] - 1)
    pc, lc = _masked_softmax_parts(_nt(qb16, ck_ref[...].astype(bf16)), cmask)
    pcn = _safe_div(pc, lc)
    o_c = _mm(pcn.astype(bf16), cv_ref[...].astype(bf16))
    qpos_row = past + lax.broadcasted_iota(jnp.int32, (1, tq), 1)
    new_blk = past // SEL_BLK
    bias, picks = _select_bias(pcn, qpos_row, new_blk, mt_ref, eye_ref, new_blk + 1)
    new_ok = lax.broadcasted_iota(jnp.int32, (R, tq), 1) <= tR
    ka = jnp.concatenate([jnp.concatenate([r[...] for r in kpages], axis=1).astype(bf16), et_ref[...]], axis=0)
    s_old = _mm(jnp.concatenate([qb16, bias.astype(bf16)], axis=1), ka)
    new_kept = jnp.concatenate([picks[h // HPG][:, new_blk:new_blk + 1] for h in range(B_HEADS)], axis=0) > 0.5
    s_new = jnp.where(new_ok, jnp.where(new_kept, _nt(qb16, nsk_ref[...].astype(bf16)), NEG), NEG)
    m = jnp.maximum(jnp.max(s_old, axis=-1, keepdims=True), jnp.max(s_new, axis=-1, keepdims=True))
    p_old = jnp.exp(s_old - m)
    p_new = jnp.exp(s_new - m)
    l = jnp.sum(p_old, axis=-1, keepdims=True) + jnp.sum(p_new, axis=-1, keepdims=True)
    svb = jnp.concatenate([r[...] for r in vpages], axis=1).astype(bf16)
    o_s = (_nt(p_old.astype(bf16), svb) + _mm(p_new.astype(bf16), nsv_ref[...].astype(bf16))) / l
    wb = cwk_ref.shape[1]
    wdist = qposR - (past - wb + lax.broadcasted_iota(jnp.int32, (R, wb), 1))
    w_old = jnp.where((wdist >= 0) & (wdist < WINDOW), _mm(qb16, cwk_ref[...].astype(bf16)), NEG)
    w_new = jnp.where(new_ok, _nt(qb16, nwk_ref[...].astype(bf16)), NEG)
    mw = jnp.maximum(jnp.max(w_old, axis=-1, keepdims=True), jnp.max(w_new, axis=-1, keepdims=True))
    pw_old = jnp.exp(w_old - mw)
    pw_new = jnp.exp(w_new - mw)
    lw = jnp.sum(pw_old, axis=-1, keepdims=True) + jnp.sum(pw_new, axis=-1, keepdims=True)
    o_w = (_nt(pw_old.astype(bf16), cwv_ref[...].astype(bf16)) + _mm(pw_new.astype(bf16), nwv_ref[...].astype(bf16))) / lw
    o_ref[...] = _nsa_combine(p, o_c, o_s, o_w)


def nsa_sample(P, row0, page_table, pool_k, pool_v, cos, sin, ck, cv, nsk, nsv, cwk, cwv, nwk, nwv, mt, et, eye):
    nb = cwk.shape[0]
    npages = page_table.shape[0] // nb
    past = npages * PAGE
    rb0 = row0 // S_PAD
    page = lambda j: pl.BlockSpec((None, LANES, PAGE), lambda b, pt: (pt[b * npages + j], 0, 0))
    new_rows = pl.BlockSpec((S_PAD, LANES), lambda b, pt: (rb0 + b, 0))
    per_seq = lambda n: pl.BlockSpec((None, LANES, n), lambda b, pt: (b, 0, 0))
    in_specs = ([page(j) for j in range(npages)] + [page(j) for j in range(npages)]
                + [pl.BlockSpec((S_PAD, B_IN_PAD), lambda b, pt: (rb0 + b, 0)),
                   pl.BlockSpec((S_PAD, LANES), lambda b, pt: (0, 0)),
                   pl.BlockSpec((S_PAD, LANES), lambda b, pt: (0, 0)),
                   per_seq(LANES), per_seq(LANES), new_rows, new_rows,
                   per_seq(cwk.shape[2]), per_seq(cwk.shape[2]), new_rows, new_rows,
                   pl.BlockSpec((LANES, LANES), lambda b, pt: (0, 0)),
                   pl.BlockSpec((LANES, past), lambda b, pt: (0, 0)),
                   pl.BlockSpec((LANES, LANES), lambda b, pt: (0, 0))])
    grid_spec = pltpu.PrefetchScalarGridSpec(
        num_scalar_prefetch=1, grid=(nb,), in_specs=in_specs,
        out_specs=pl.BlockSpec((S_PAD, B_W), lambda b, pt: (b, 0)))
    return pl.pallas_call(
        functools.partial(_nsa_sample_kernel, npages=npages, past=past),
        grid_spec=grid_spec,
        out_shape=jax.ShapeDtypeStruct((nb * S_PAD, B_W), f32),
        compiler_params=_cparams(("parallel",)),
        name="nsa_sample",
    )(page_table, *([pool_k] * npages), *([pool_v] * npages), P, cos, sin, ck, cv, nsk, nsv, cwk, cwv, nwk, nwv,
      mt, et, eye)


def _rope_tables(pos):
    half = HEAD_DIM // 2
    inv = ROPE_THETA ** (-jnp.arange(half, dtype=f32) / half)
    ang = pos.astype(f32)[:, None] * inv[None, :]
    cos = jnp.tile(jnp.cos(ang), (1, LANES // half))
    sin = jnp.tile(jnp.sin(ang), (1, LANES // half))
    sign = jnp.where((jnp.arange(LANES) % HEAD_DIM) < half, -1.0, 1.0).astype(f32)
    return cos, sin * sign[None, :]


def _cmp_to_sel(ncmp, nsel):
    per = SEL_BLK // CMP_STRIDE
    sub = CMP_BLK // CMP_STRIDE
    i = jnp.arange(LANES)[:, None]
    j = jnp.arange(LANES)[None, :]
    m = jnp.zeros((LANES, LANES), f32)
    for r in range(sub):
        m = m + (((i + r) // per) == j).astype(f32)
    m = jnp.where((i < ncmp) & (j < nsel), m / sub, 0.0)
    return m.astype(bf16)


def _block_expand(nkeys):
    j = jnp.arange(LANES)[:, None]
    k = jnp.arange(nkeys)[None, :]
    return ((k // SEL_BLK) == j).astype(bf16)


def _compress_consts(pe, w1, w2):
    w1r = w1.reshape(CMP_BLK, HEAD_DIM, CMP_HID)
    eye2 = jnp.eye(2, dtype=f32)

    def expand(w):
        return jnp.einsum('ldc,gh->lgdhc', w, eye2).reshape(16 * LANES, 2 * CMP_HID)

    w1s = jnp.concatenate([expand(w1r[:16]), expand(w1r[16:])], axis=1).astype(bf16)
    pes = jnp.stack([jnp.broadcast_to(pe[:16, None, :], (16, 2, HEAD_DIM)).reshape(-1),
                     jnp.broadcast_to(pe[16:, None, :], (16, 2, HEAD_DIM)).reshape(-1)])
    w2s = jnp.einsum('cd,gh->gchd', w2, eye2).reshape(2 * CMP_HID, LANES).astype(bf16)
    return pes, w1s, w2s


def kernel(x_prompt, x_sample, mem_prompt, state_conv, state_ssm, cache_cmp_k, cache_cmp_v, cache_sel_k, cache_sel_v, cache_win_k, cache_win_v, cache_mem_k, cache_mem_v, page_table, norm_mix, norm_ffn, norm_mem, w_mem_kv, w_in_a, conv_w_a, a_log, dt_bias, norm_out_a, w_out_a, w_in_b, w_out_b, norm_kv, w_kv_shared, cmp_pe_k, cmp_w1_k, cmp_w2_k, cmp_pe_v, cmp_w1_v, cmp_w2_v, w_gu_dense, w_down_dense, w_router, w_gu_exp, w_down_exp, norm_final):
    bp, lp, d = x_prompt.shape
    bs, ls, _ = x_sample.shape
    depth = norm_mix.shape[0]
    n_a = w_in_a.shape[0]
    past = page_table.shape[1] * PAGE
    tp = bp * lp
    T = tp + bs * S_PAD
    tm = 512

    xs = jnp.pad(x_sample, ((0, 0), (0, S_PAD - ls), (0, 0)))
    H = jnp.concatenate([x_prompt.reshape(tp, d), xs.reshape(bs * S_PAD, d)], axis=0)

    qkvz = MEM_W + CONV_DIM + A_W
    w_in_a_b = jnp.concatenate(
        [w_in_a[:, :, MEM_W:qkvz], w_in_a[:, :, :MEM_W], w_in_a[:, :, qkvz:],
         jnp.zeros(w_in_a.shape[:2] + (A_IN_PAD - w_in_a.shape[2],), f32)], axis=2).astype(bf16)
    w_in_b_b = jnp.pad(w_in_b, ((0, 0), (0, 0), (0, B_IN_PAD - w_in_b.shape[2]))).astype(bf16)
    w_out_a_b = w_out_a.astype(bf16)
    w_out_b_b = w_out_b.astype(bf16)
    w_gu_dense_b = w_gu_dense.astype(bf16)
    w_down_dense_b = w_down_dense.astype(bf16)
    w_gu_exp_b = w_gu_exp.astype(bf16)
    w_down_exp_b = w_down_exp.astype(bf16)
    w_router_p = jnp.pad(w_router, ((0, 0), (0, 0), (0, LANES - N_EXPERTS)))

    mem_k, mem_v = mem_kv(mem_prompt.reshape(bp * MEM_LEN, d), norm_mem.reshape(depth, 1, d),
                          w_mem_kv.astype(bf16), tm)
    cmk = jnp.transpose(cache_mem_k, (0, 1, 3, 4, 2)).reshape(depth, bs, MEM_W, MEM_LEN)
    cmv = jnp.transpose(cache_mem_v, (0, 1, 3, 4, 2)).reshape(depth, bs, MEM_W, MEM_LEN)
    zero_buf = jnp.zeros((bp, SUBLANES, CONV_DIM), f32)
    zero_state = jnp.zeros((1, bp, A_HEADS, LANES, LANES), f32)

    pos_p = jnp.arange(lp, dtype=jnp.int32)
    pos_s = past + jnp.arange(S_PAD, dtype=jnp.int32)
    cos_p, sin_p = _rope_tables(pos_p)
    cos_s, sin_s = _rope_tables(pos_s)

    conv_p, ssm_p, conv_s, ssm_s = [], [], [], []
    kv = None
    for layer in range(depth):
        is_a = layer < n_a
        g_mix = norm_mix[layer].reshape(1, d)
        if is_a:
            P = norm_matmul(H, g_mix, w_in_a_b, layer, tm, A_IN_PAD // 3)
            qblk = A_MEMQ_BLK
        else:
            lb = layer - n_a
            qblk = 0
            if lb == 0:
                cos_all = jnp.concatenate([jnp.tile(cos_p, (bp, 1)), jnp.tile(cos_s, (bs, 1))], axis=0)
                sin_all = jnp.concatenate([jnp.tile(sin_p, (bp, 1)), jnp.tile(sin_s, (bs, 1))], axis=0)
                kv = shared_kv(H, norm_kv.reshape(1, d), w_kv_shared.astype(bf16), cos_all, sin_all, tm)
                ncmp = (lp - CMP_BLK) // CMP_STRIDE + 1
                cpos = jnp.arange(LANES, dtype=jnp.int32) * CMP_STRIDE + CMP_BLK - 1
                cos_c, sin_c = _rope_tables(cpos)
                pek, w1k, w2k = _compress_consts(cmp_pe_k, cmp_w1_k, cmp_w2_k)
                pev, w1v, w2v = _compress_consts(cmp_pe_v, cmp_w1_v, cmp_w2_v)
                consts = (pek, pev, w1k, w1v, w2k, w2v, cos_c, sin_c)
                strips = lp // 16
                ck_p, cv_p = compress(kv[0][:tp].reshape(bp, strips, 16 * LANES),
                                      kv[1][:tp].reshape(bp, strips, 16 * LANES),
                                      [lambda b: (b, 0, 0)], [lambda b: (b, 0, 0)], strips, bp, consts)
                npages = page_table.shape[1]
                pt_flat = page_table.reshape(-1)
                pool_strips = PAGE // 16
                pmap = lambda j: (lambda b, pt: (pt[b * npages + j], 0, 0))
                ck_s, cv_s = compress(cache_cmp_k.reshape(-1, pool_strips, 16 * LANES),
                                      cache_cmp_v.reshape(-1, pool_strips, 16 * LANES),
                                      [pmap(j) for j in range(npages)], [pmap(j) for j in range(npages)],
                                      pool_strips, bs, consts, prefetch=pt_flat)
                mt_p = _cmp_to_sel(ncmp, lp // SEL_BLK).T
                et_p = _block_expand(lp).T
                ncmp_s = (past + ls - CMP_BLK) // CMP_STRIDE + 1
                mt_s = _cmp_to_sel(ncmp_s, -(-(past + ls) // SEL_BLK)).T
                e_s = _block_expand(past)
                eye = jnp.eye(LANES, dtype=bf16)
                wk_pad = jnp.pad(kv[4][:tp].reshape(bp, lp, LANES), ((0, 0), (WINDOW, 0), (0, 0)))
                wv_pad = jnp.pad(kv[5][:tp].reshape(bp, lp, LANES), ((0, 0), (WINDOW, 0), (0, 0)))
                pool_sk = jnp.transpose(cache_sel_k, (0, 2, 3, 1)).reshape(-1, LANES, PAGE)
                pool_sv = jnp.transpose(cache_sel_v, (0, 2, 3, 1)).reshape(-1, LANES, PAGE)
                cwk = jnp.transpose(cache_win_k, (0, 2, 3, 1)).reshape(bs, LANES, -1)
                cwv = jnp.transpose(cache_win_v, (0, 2, 3, 1)).reshape(bs, LANES, -1)
            P = norm_matmul(H, g_mix, w_in_b_b, lb, tm, B_IN_PAD)
        om = (mem_attn_prompt(P, qblk, mem_k, mem_v, layer, bp, lp, 512),
              mem_attn_sample(P, qblk, tp, cmk, cmv, layer, 8))
        if is_a:
            conv8 = jnp.pad(conv_w_a[layer], ((0, SUBLANES - conv_w_a.shape[1]), (0, 0)))
            ab8 = jnp.zeros((SUBLANES, LANES), f32)
            ab8 = ab8.at[0, A_HEADS:2 * A_HEADS].set(a_log[layer]).at[1, A_HEADS:2 * A_HEADS].set(dt_bias[layer])
            gout = norm_out_a[layer].reshape(1, LANES)
            mo_p, sp = delta_mixer(P, 0, bp, lp, 512, 64, 512, zero_buf, conv8, ab8, gout, zero_state, 0)
            buf_s = jnp.pad(state_conv[layer], ((0, 0), (SUBLANES - state_conv.shape[2], 0), (0, 0)))
            mo_s, ss = delta_mixer(P, tp, bs, S_PAD, S_PAD, S_PAD, ls, buf_s, conv8, ab8, gout, state_ssm, layer)
            conv_p.append(jnp.stack([P[b * lp + lp - 3:(b + 1) * lp, :CONV_DIM] for b in range(bp)]))
            conv_s.append(P[tp:, :CONV_DIM].reshape(bs, S_PAD, CONV_DIM)[:, ls - 3:ls])
            ssm_p.append(sp)
            ssm_s.append(ss)
            w_out, wl = w_out_a_b, layer
        else:
            mo_p = nsa_prompt(P, cos_p, sin_p, ck_p, cv_p, kv[2], kv[3], wk_pad, wv_pad, mt_p, et_p, eye,
                              bp, lp, Q_BLOCK, 512)
            mo_s = nsa_sample(P, tp, pt_flat, pool_sk, pool_sv, cos_s, sin_s, ck_s, cv_s, kv[2], kv[3], cwk, cwv,
                              kv[4], kv[5], mt_s, e_s, eye)
            w_out, wl = w_out_b_b, lb
        mo = (mo_p, mo_s)
        g_ffn = norm_ffn[layer].reshape(1, d)
        if layer % 2 == 0:
            H = outproj_ffn(H, mo, om, w_out, g_ffn, w_gu_dense_b, w_down_dense_b, wl, layer // 2, tm, D_FF // 2)
        else:
            H = outproj_moe(H, mo, om, w_out, g_ffn, w_router_p, w_gu_exp_b, w_down_exp_b, wl, layer // 2,
                            tm, D_FF_EXPERT // 2)
    Y = final_norm(H, norm_final.reshape(1, d), tm)

    def rows_p(a):
        return a[:tp].reshape(bp, lp, 2, HEAD_DIM)

    def rows_s(a):
        return a[tp:].reshape(bs, S_PAD, 2, HEAD_DIM)[:, :ls]

    wlen = min(WINDOW, lp)
    return (Y[:tp].reshape(bp, lp, d), Y[tp:].reshape(bs, S_PAD, d)[:, :ls],
            jnp.stack(conv_p), jnp.stack(ssm_p),
            rows_p(kv[0]), rows_p(kv[1]), rows_p(kv[2]), rows_p(kv[3]),
            rows_p(kv[4])[:, lp - wlen:], rows_p(kv[5])[:, lp - wlen:],
            mem_k.reshape(depth, bp, MEM_LEN, 4, HEAD_DIM), mem_v.reshape(depth, bp, MEM_LEN, 4, HEAD_DIM),
            jnp.stack(conv_s), jnp.stack(ssm_s),
            rows_s(kv[0]), rows_s(kv[1]), rows_s(kv[2]), rows_s(kv[3]),
            jnp.concatenate([cache_win_k[:, ls:], rows_s(kv[4])], axis=1),
            jnp.concatenate([cache_win_v[:, ls:], rows_s(kv[5])], axis=1))
```

```python
import functools
import math

import jax
import jax.numpy as jnp
from jax import lax
from jax.experimental import pallas as pl
from jax.experimental.pallas import tpu as pltpu

f32 = jnp.float32
bf16 = jnp.bfloat16

EPS = 1e-6
NEG = -1e30
ROPE_THETA = 10000.0
HEAD_DIM = 64
LANES = 128
SUBLANES = 8
VMEM_LIMIT = 48 * 1024 * 1024

D_MODEL = 1024
MEM_LEN = 256
MEM_W = 256
A_HEADS = 6
A_W = 768
CONV_DIM = 2304
A_IN_PAD = 3456
A_MEMQ_BLK = 12
A_BA_BLK = 26
B_HEADS = 12
HPG = 6
B_W = 768
B_IN_PAD = 1152
CMP_BLK = 32
CMP_STRIDE = 16
CMP_HID = 256
SEL_BLK = 64
SEL_TOP = 16
WINDOW = 512
Q_BLOCK = 128
D_FF = 2816
N_EXPERTS = 8
D_FF_EXPERT = 3584
PAGE = 128
S_PAD = 8


def _cparams(sem):
    return pltpu.CompilerParams(dimension_semantics=sem, vmem_limit_bytes=VMEM_LIMIT)


def _nt(a, b):
    return lax.dot_general(a, b, (((1,), (1,)), ((), ())), preferred_element_type=f32)


def _tn(a, b):
    return lax.dot_general(a, b, (((0,), (0,)), ((), ())), preferred_element_type=f32)


def _mm(a, b):
    return jnp.dot(a, b, preferred_element_type=f32)


def _rms(x, g):
    return (x * lax.rsqrt(jnp.mean(x * x, axis=-1, keepdims=True) + EPS)) * g


def _sigmoid(x):
    return 1.0 / (1.0 + jnp.exp(-x))


def _silu(x):
    return x * _sigmoid(x)


def _softplus(x):
    return jnp.maximum(x, 0.0) + jnp.log(1.0 + jnp.exp(-jnp.abs(x)))


def _rope_lanes(x, cos, sin_signed):
    lane = lax.broadcasted_iota(jnp.int32, x.shape, x.ndim - 1)
    lo = (lane % HEAD_DIM) < (HEAD_DIM // 2)
    partner = jnp.where(lo, pltpu.roll(x, LANES - HEAD_DIM // 2, axis=x.ndim - 1),
                        pltpu.roll(x, HEAD_DIM // 2, axis=x.ndim - 1))
    return x * cos + partner * sin_signed


def _norm_matmul_kernel(x_ref, g_ref, w_ref, o_ref, xn_ref):
    @pl.when(pl.program_id(1) == 0)
    def _():
        xn_ref[...] = _rms(x_ref[...], g_ref[...]).astype(bf16)

    o_ref[...] = _mm(xn_ref[...], w_ref[...])


def norm_matmul(x, g, w, layer, tm, tn):
    T, K = x.shape
    N = w.shape[2]
    return pl.pallas_call(
        _norm_matmul_kernel,
        grid=(T // tm, N // tn),
        in_specs=[pl.BlockSpec((tm, K), lambda i, j: (i, 0)),
                  pl.BlockSpec((1, K), lambda i, j: (0, 0)),
                  pl.BlockSpec((None, K, tn), lambda i, j: (layer, 0, j))],
        out_specs=pl.BlockSpec((tm, tn), lambda i, j: (i, j)),
        out_shape=jax.ShapeDtypeStruct((T, N), f32),
        scratch_shapes=[pltpu.VMEM((tm, K), bf16)],
        compiler_params=_cparams(("parallel", "arbitrary")),
        name="norm_matmul",
    )(x, g, w)


def _mem_kv_kernel(x_ref, g_ref, w_ref, k_ref, v_ref):
    y = _mm(_rms(x_ref[...], g_ref[...]).astype(bf16), w_ref[...])
    k_ref[...] = y[:, :MEM_W]
    v_ref[...] = y[:, MEM_W:]


def mem_kv(mem, g, w, tm):
    R, K = mem.shape
    NL = w.shape[0]
    out = jax.ShapeDtypeStruct((NL, R, MEM_W), f32)
    return pl.pallas_call(
        _mem_kv_kernel,
        grid=(NL, R // tm),
        in_specs=[pl.BlockSpec((tm, K), lambda l, i: (i, 0)),
                  pl.BlockSpec((None, 1, K), lambda l, i: (l, 0, 0)),
                  pl.BlockSpec((None, K, 2 * MEM_W), lambda l, i: (l, 0, 0))],
        out_specs=(pl.BlockSpec((None, tm, MEM_W), lambda l, i: (l, i, 0)),
                   pl.BlockSpec((None, tm, MEM_W), lambda l, i: (l, i, 0))),
        out_shape=(out, out),
        compiler_params=_cparams(("parallel", "parallel")),
        name="mem_kv",
    )(mem, g, w)


def _final_norm_kernel(x_ref, g_ref, o_ref):
    o_ref[...] = _rms(x_ref[...], g_ref[...])


def final_norm(x, g, tm):
    T, K = x.shape
    return pl.pallas_call(
        _final_norm_kernel,
        grid=(T // tm,),
        in_specs=[pl.BlockSpec((tm, K), lambda i: (i, 0)), pl.BlockSpec((1, K), lambda i: (0, 0))],
        out_specs=pl.BlockSpec((tm, K), lambda i: (i, 0)),
        out_shape=jax.ShapeDtypeStruct((T, K), f32),
        compiler_params=_cparams(("parallel",)),
        name="final_norm",
    )(x, g)


def _mem_attend(q, k, v, transposed=False):
    tq = q.shape[0]
    nh = MEM_W // HEAD_DIM
    lane = lax.broadcasted_iota(jnp.int32, (tq, MEM_W), 1)
    masks = [(lane >= HEAD_DIM * h) & (lane < HEAD_DIM * (h + 1)) for h in range(nh)]
    qs = jnp.concatenate([jnp.where(m, q, 0.0) for m in masks], axis=0).astype(bf16)
    s = (_mm(qs, k) if transposed else _nt(qs, k)) * (HEAD_DIM ** -0.5)
    p = jnp.exp(s - jnp.max(s, axis=-1, keepdims=True))
    l = jnp.sum(p, axis=-1, keepdims=True)
    pb = p.astype(bf16)
    o = (_nt(pb, v) if transposed else _mm(pb, v)) / l
    out = jnp.zeros((tq, MEM_W), f32)
    for h in range(nh):
        out = jnp.where(masks[h], o[h * tq:(h + 1) * tq], out)
    return out


def _mem_attn_prompt_kernel(q_ref, k_ref, v_ref, o_ref):
    o_ref[...] = _mem_attend(q_ref[...], k_ref[...].astype(bf16), v_ref[...].astype(bf16))


def mem_attn_prompt(P, qblk, mk, mv, layer, nb, L, tq):
    nq = L // tq
    return pl.pallas_call(
        _mem_attn_prompt_kernel,
        grid=(nb, nq),
        in_specs=[pl.BlockSpec((tq, MEM_W), lambda b, i: (b * nq + i, qblk)),
                  pl.BlockSpec((None, MEM_LEN, MEM_W), lambda b, i: (layer, b, 0)),
                  pl.BlockSpec((None, MEM_LEN, MEM_W), lambda b, i: (layer, b, 0))],
        out_specs=pl.BlockSpec((tq, MEM_W), lambda b, i: (b * nq + i, 0)),
        out_shape=jax.ShapeDtypeStruct((nb * L, MEM_W), f32),
        compiler_params=_cparams(("parallel", "parallel")),
        name="mem_attn_prompt",
    )(P, mk, mv)


def _mem_attn_sample_kernel(q_ref, k_ref, v_ref, o_ref, *, bb):
    for b in range(bb):
        rows = slice(b * S_PAD, (b + 1) * S_PAD)
        o_ref[rows, :] = _mem_attend(q_ref[rows, :], k_ref[b].astype(bf16), v_ref[b].astype(bf16), transposed=True)


def mem_attn_sample(P, qblk, row0, ckt, cvt, layer, bb):
    nb = ckt.shape[1]
    blk0 = row0 // (bb * S_PAD)
    return pl.pallas_call(
        functools.partial(_mem_attn_sample_kernel, bb=bb),
        grid=(nb // bb,),
        in_specs=[pl.BlockSpec((bb * S_PAD, MEM_W), lambda i: (blk0 + i, qblk)),
                  pl.BlockSpec((None, bb, MEM_W, MEM_LEN), lambda i: (layer, i, 0, 0)),
                  pl.BlockSpec((None, bb, MEM_W, MEM_LEN), lambda i: (layer, i, 0, 0))],
        out_specs=pl.BlockSpec((bb * S_PAD, MEM_W), lambda i: (i, 0)),
        out_shape=jax.ShapeDtypeStruct((nb * S_PAD, MEM_W), f32),
        compiler_params=_cparams(("parallel",)),
        name="mem_attn_sample",
    )(P, ckt, cvt)


def _delta_kernel(q_ref, k_ref, v_ref, z_ref, ba_ref, buf_ref, cw_ref, ab_ref, go_ref, s0_ref,
                  o_ref, sout_ref, qs_ref, ks_ref, vs_ref, gs_ref, bs_ref, tail_ref, s_ref,
                  *, TL, C, valid):
    t = pl.program_id(1)

    @pl.when(t == 0)
    def _():
        tail_ref[...] = buf_ref[...]
        s_ref[...] = s0_ref[...]

    row8 = lax.broadcasted_iota(jnp.int32, (SUBLANES, A_W), 0)

    def conv(x_ref, c0):
        x = x_ref[...]
        b8 = tail_ref[:, c0:c0 + A_W]
        c8 = cw_ref[:, c0:c0 + A_W]
        acc = x * c8[3:4, :]
        for s in range(1, 4):
            r = pltpu.roll(x, s, axis=0)
            fix = jnp.where(row8 < s, pltpu.roll(b8, s, axis=0), r[0:SUBLANES])
            r = jnp.concatenate([fix, r[SUBLANES:]], axis=0) if TL > SUBLANES else fix
            acc = acc + r * c8[3 - s:4 - s, :]
        tail_ref[:, c0:c0 + A_W] = x[TL - SUBLANES:TL]
        return _silu(acc)

    q = conv(q_ref, 0)
    k = conv(k_ref, A_W)
    v = conv(v_ref, 2 * A_W)
    ba = ba_ref[...]
    ab = ab_ref[...]
    beta_all = _sigmoid(ba)
    g_all = -jnp.exp(ab[0:1, :]) * _softplus(ba + ab[1:2, :])
    if valid < TL:
        live = lax.broadcasted_iota(jnp.int32, (TL, A_W), 0) < valid
        live1 = lax.broadcasted_iota(jnp.int32, (TL, LANES), 0) < valid
        k = jnp.where(live, k, 0.0)
        v = jnp.where(live, v, 0.0)
        beta_all = jnp.where(live1, beta_all, 0.0)
        g_all = jnp.where(live1, g_all, 0.0)
    rowc = lax.broadcasted_iota(jnp.int32, (TL, LANES), 0) & (C - 1)
    gc_all = g_all
    s = 1
    while s < C:
        gc_all = gc_all + jnp.where(rowc >= s, pltpu.roll(gc_all, s, axis=0), 0.0)
        s *= 2
    vs_ref[...] = v
    for h in range(A_HEADS):
        sl = slice(h * LANES, (h + 1) * LANES)
        qh = q[:, sl]
        kh = k[:, sl]
        qs_ref[:, sl] = qh * lax.rsqrt(jnp.sum(qh * qh, axis=-1, keepdims=True) + EPS) * (LANES ** -0.5)
        ks_ref[:, sl] = kh * lax.rsqrt(jnp.sum(kh * kh, axis=-1, keepdims=True) + EPS)
        gs_ref[:, sl] = jnp.broadcast_to(gc_all[:, A_HEADS + h:A_HEADS + h + 1], (TL, LANES))
        bs_ref[:, sl] = jnp.broadcast_to(beta_all[:, h:h + 1], (TL, LANES))

    ii = lax.broadcasted_iota(jnp.int32, (C, C), 0)
    jj = lax.broadcasted_iota(jnp.int32, (C, C), 1)
    tri = ii >= jj
    strict = ii > jj
    eye = jnp.where(ii == jj, 1.0, 0.0)
    gout = go_ref[...]
    nsteps = int(math.log2(C)) - 1

    heads = range(A_HEADS)
    lanes_of = [slice(h * LANES, (h + 1) * LANES) for h in heads]

    cpi = 2 if TL // C >= 2 else 1

    def chunks(c, carry):
        rows_of = [pl.ds(pl.multiple_of((c * cpi + i) * C, C), C) for i in range(cpi)]
        units = [(rows, h) for rows in rows_of for h in heads]
        U = range(len(units))
        qc = [qs_ref[rows, lanes_of[h]] for rows, h in units]
        kc = [ks_ref[rows, lanes_of[h]] for rows, h in units]
        vc = [vs_ref[rows, lanes_of[h]] for rows, h in units]
        gcc = [gs_ref[rows, lanes_of[h]] for rows, h in units]
        bc = [bs_ref[rows, lanes_of[h]] for rows, h in units]
        kb = [kc[u] * bc[u] for u in U]
        kq = [_nt(jnp.concatenate([kb[u], qc[u]], axis=0).astype(bf16), kc[u].astype(bf16)) for u in U]
        decay = []
        for u in U:
            gi = gcc[u][:, 0:C]
            gj = jnp.sum(jnp.where(ii == jj, gi, 0.0), axis=0, keepdims=True)
            decay.append(jnp.where(tri, jnp.exp(jnp.where(tri, gi - gj, 0.0)), 0.0))
        n = [jnp.where(strict, -(kq[u][0:C] * decay[u]), 0.0) for u in U]
        a = [jnp.where(tri, kq[u][C:2 * C] * decay[u], 0.0).astype(bf16) for u in U]
        x = [eye + n[u] for u in U]
        p = n
        for _ in range(nsteps):
            pb = [p[u].astype(bf16) for u in U]
            p = [_mm(pb[u], pb[u]) for u in U]
            x = [x[u] + _mm(x[u].astype(bf16), p[u].astype(bf16)) for u in U]
        eg = [jnp.exp(gcc[u]) for u in U]
        uw = [_mm(x[u].astype(bf16), jnp.concatenate([vc[u] * bc[u], kb[u] * eg[u]], axis=1).astype(bf16))
              for u in U]
        wqin = [jnp.concatenate([uw[u][:, LANES:], qc[u] * eg[u]], axis=0).astype(bf16) for u in U]
        glast = [gcc[u][C - 1:C, :] for u in U]
        kg = [(kc[u] * jnp.exp(glast[u] - gcc[u])).astype(bf16) for u in U]
        for i, rows in enumerate(rows_of):
            us = [i * A_HEADS + h for h in heads]
            S = [s_ref[h] for h in heads]
            wq = [_mm(wqin[u], S[h].astype(bf16)) for h, u in zip(heads, us)]
            vnb = [(uw[u][:, :LANES] - wq[h][0:C]).astype(bf16) for h, u in zip(heads, us)]
            av = [_mm(a[u], vnb[h]) for h, u in zip(heads, us)]
            kv_new = [_tn(kg[u], vnb[h]) for h, u in zip(heads, us)]
            for h, u in zip(heads, us):
                s_ref[h] = S[h] * jnp.exp(glast[u]) + kv_new[h]
                o_ref[rows, lanes_of[h]] = _rms(wq[h][C:2 * C] + av[h], gout) * _silu(z_ref[rows, lanes_of[h]])
        return carry

    if TL // C == cpi:
        chunks(0, 0)
    else:
        lax.fori_loop(0, TL // (C * cpi), chunks, 0)

    @pl.when(t == pl.num_programs(1) - 1)
    def _():
        sout_ref[...] = s_ref[...]


def delta_mixer(P, row0, nb, L, TL, C, valid, buf8, conv8, ab8, gout, s0, layer):
    nt = L // TL
    rb0 = row0 // TL
    tile = lambda c: pl.BlockSpec((TL, A_W), lambda b, t: (rb0 + b * nt + t, c))
    state = pl.BlockSpec((None, A_HEADS, LANES, LANES), lambda b, t: (b, 0, 0, 0))
    state_in = pl.BlockSpec((None, None, A_HEADS, LANES, LANES), lambda b, t: (layer, b, 0, 0, 0))
    return pl.pallas_call(
        functools.partial(_delta_kernel, TL=TL, C=C, valid=valid),
        grid=(nb, nt),
        in_specs=[tile(0), tile(1), tile(2), tile(3),
                  pl.BlockSpec((TL, LANES), lambda b, t: (rb0 + b * nt + t, A_BA_BLK)),
                  pl.BlockSpec((None, SUBLANES, CONV_DIM), lambda b, t: (b, 0, 0)),
                  pl.BlockSpec((SUBLANES, CONV_DIM), lambda b, t: (0, 0)),
                  pl.BlockSpec((SUBLANES, LANES), lambda b, t: (0, 0)),
                  pl.BlockSpec((1, LANES), lambda b, t: (0, 0)),
                  state_in],
        out_specs=(pl.BlockSpec((TL, A_W), lambda b, t: (b * nt + t, 0)), state),
        out_shape=(jax.ShapeDtypeStruct((nb * L, A_W), f32),
                   jax.ShapeDtypeStruct((nb, A_HEADS, LANES, LANES), f32)),
        scratch_shapes=[pltpu.VMEM((TL, A_W), f32)] * 5
        + [pltpu.VMEM((SUBLANES, CONV_DIM), f32), pltpu.VMEM((A_HEADS, LANES, LANES), f32)],
        compiler_params=_cparams(("parallel", "arbitrary")),
        name="delta_mixer",
    )(P, P, P, P, P, buf8, conv8, ab8, gout, s0)


def _attn_residual(refs, ntp):
    h_ref, mop_ref, mos_ref, omp_ref, oms_ref, wo1_ref, wo2_ref = refs
    is_p = pl.program_id(0) < ntp
    mo = jnp.where(is_p, mop_ref[...], mos_ref[...])
    om = jnp.where(is_p, omp_ref[...], oms_ref[...])
    return h_ref[...] + _mm(mo.astype(bf16), wo1_ref[...]) + _mm(om.astype(bf16), wo2_ref[...])


def _ffn_kernel(*refs, ntp):
    g_ref, wg_ref, wu_ref, wd_ref, o_ref, xn_ref, acc_ref = refs[7:]
    k = pl.program_id(1)

    @pl.when(k == 0)
    def _():
        h1 = _attn_residual(refs[:7], ntp)
        acc_ref[...] = h1
        xn_ref[...] = _rms(h1, g_ref[...]).astype(bf16)

    xn = xn_ref[...]
    a = _silu(_mm(xn, wg_ref[...])) * _mm(xn, wu_ref[...])
    acc_ref[...] += _mm(a.astype(bf16), wd_ref[...])

    @pl.when(k == pl.num_programs(1) - 1)
    def _():
        o_ref[...] = acc_ref[...]


def _outproj_specs(tm, K, mw, ow, ntp, layer):
    prow = lambda i, *_: (jnp.minimum(i, ntp - 1), 0)
    srow = lambda i, *_: (jnp.maximum(i - ntp, 0), 0)
    return [pl.BlockSpec((tm, K), lambda i, *_: (i, 0)),
            pl.BlockSpec((tm, mw), prow), pl.BlockSpec((tm, mw), srow),
            pl.BlockSpec((tm, ow), prow), pl.BlockSpec((tm, ow), srow),
            pl.BlockSpec((None, mw, K), lambda i, *_: (layer, 0, 0)),
            pl.BlockSpec((None, ow, K), lambda i, *_: (layer, mw // ow, 0)),
            pl.BlockSpec((1, K), lambda i, *_: (0, 0))]


def outproj_ffn(H, mo, om, w_out, g, w_gu, w_down, layer, ffn_layer, tm, tf):
    T, K = H.shape
    FF = w_down.shape[1]
    nk = FF // tf
    mw, ow = mo[0].shape[1], om[0].shape[1]
    ntp = mo[0].shape[0] // tm
    return pl.pallas_call(
        functools.partial(_ffn_kernel, ntp=ntp),
        grid=(T // tm, nk),
        in_specs=_outproj_specs(tm, K, mw, ow, ntp, layer)
        + [pl.BlockSpec((None, K, tf), lambda i, k: (ffn_layer, 0, k)),
           pl.BlockSpec((None, K, tf), lambda i, k: (ffn_layer, 0, nk + k)),
           pl.BlockSpec((None, tf, K), lambda i, k: (ffn_layer, k, 0))],
        out_specs=pl.BlockSpec((tm, K), lambda i, k: (i, 0)),
        out_shape=jax.ShapeDtypeStruct((T, K), f32),
        scratch_shapes=[pltpu.VMEM((tm, K), bf16), pltpu.VMEM((tm, K), f32)],
        compiler_params=_cparams(("parallel", "arbitrary")),
        name="outproj_ffn",
    )(H, mo[0], mo[1], om[0], om[1], w_out, w_out, g, w_gu, w_gu, w_down)


def _router_kernel(*refs, ntp):
    g_ref, wr_ref, h1_ref, xn_ref, r_ref, cnt_ref = refs[7:]
    i = pl.program_id(0)
    tm = h1_ref.shape[0]

    @pl.when(i == 0)
    def _():
        cnt_ref[...] = jnp.zeros_like(cnt_ref)

    h1 = _attn_residual(refs[:7], ntp)
    h1_ref[...] = h1
    xn = _rms(h1, g_ref[...])
    xn_ref[...] = xn
    lane = lax.broadcasted_iota(jnp.int32, (tm, LANES), 1)
    xh = xn.astype(bf16)
    xl = (xn - xh.astype(f32)).astype(bf16)
    wr = wr_ref[...]
    wh = wr.astype(bf16)
    wl = (wr - wh.astype(f32)).astype(bf16)
    logits = _mm(xh, wh) + _mm(xh, wl) + _mm(xl, wh)
    logits = jnp.where(lane < N_EXPERTS, logits, -jnp.inf)
    m1 = jnp.max(logits, axis=-1, keepdims=True)
    i1 = jnp.min(jnp.where(logits == m1, lane, LANES), axis=-1, keepdims=True)
    rest = jnp.where(lane == i1, -jnp.inf, logits)
    m2 = jnp.max(rest, axis=-1, keepdims=True)
    i2 = jnp.min(jnp.where(rest == m2, lane, LANES), axis=-1, keepdims=True)
    e2 = jnp.exp(m2 - m1)
    w1 = 1.0 / (1.0 + e2)
    w2 = e2 / (1.0 + e2)
    hot = jnp.where((lane == i1) | (lane == i2), 1.0, 0.0)
    ri = lax.broadcasted_iota(jnp.int32, (tm, tm), 0)
    ci = lax.broadcasted_iota(jnp.int32, (tm, tm), 1)
    before = jnp.where(ri > ci, 1.0, 0.0).astype(bf16)
    cum = _mm(before, hot.astype(bf16)) + cnt_ref[...]
    r1 = jnp.sum(jnp.where(lane == i1, cum, 0.0), axis=-1, keepdims=True)
    r2 = jnp.sum(jnp.where(lane == i2, cum, 0.0), axis=-1, keepdims=True)
    cnt_ref[...] += jnp.sum(hot, axis=0, keepdims=True)
    cols = (i1.astype(f32), i2.astype(f32), w1, w2, r1, r2)
    r = jnp.zeros((tm, LANES), f32)
    for c, val in enumerate(cols):
        r = jnp.where(lane == c, val, r)
    r_ref[...] = r


def outproj_router(H, mo, om, w_out, g, w_router, layer, moe_layer, tm):
    T, K = H.shape
    mw, ow = mo[0].shape[1], om[0].shape[1]
    ntp = mo[0].shape[0] // tm
    return pl.pallas_call(
        functools.partial(_router_kernel, ntp=ntp),
        grid=(T // tm,),
        in_specs=_outproj_specs(tm, K, mw, ow, ntp, layer)
        + [pl.BlockSpec((None, K, LANES), lambda i: (moe_layer, 0, 0))],
        out_specs=(pl.BlockSpec((tm, K), lambda i: (i, 0)), pl.BlockSpec((tm, K), lambda i: (i, 0)),
                   pl.BlockSpec((tm, LANES), lambda i: (i, 0))),
        out_shape=(jax.ShapeDtypeStruct((T, K), f32), jax.ShapeDtypeStruct((T, K), f32),
                   jax.ShapeDtypeStruct((T, LANES), f32)),
        scratch_shapes=[pltpu.VMEM((1, LANES), f32)],
        compiler_params=_cparams(("arbitrary",)),
        name="outproj_router",
    )(H, mo[0], mo[1], om[0], om[1], w_out, w_out, g, w_router)


def _moe_group_kernel(te_ref, nu_ref, x_ref, sw_ref, wg_ref, wu_ref, wd_ref, o_ref, acc_ref):
    del te_ref
    i = pl.program_id(0)
    k = pl.program_id(1)

    @pl.when(k == 0)
    def _():
        acc_ref[...] = jnp.zeros_like(acc_ref)

    @pl.when(i < nu_ref[0])
    def _():
        x = x_ref[...].astype(bf16)
        a = _silu(_mm(x, wg_ref[...])) * _mm(x, wu_ref[...])
        acc_ref[...] += _mm(a.astype(bf16), wd_ref[...])

    @pl.when(k == pl.num_programs(1) - 1)
    def _():
        o_ref[...] = acc_ref[...] * sw_ref[...]


def moe_grouped(xg, sw, tile_e, n_used, w_gu, w_down, moe_layer, tm, tf):
    NP, K = xg.shape
    FF = w_down.shape[2]
    nk = FF // tf

    def kk(i, k, nu):
        return jnp.where(i < nu[0], k, nk - 1)

    grid_spec = pltpu.PrefetchScalarGridSpec(
        num_scalar_prefetch=2, grid=(NP // tm, nk),
        in_specs=[pl.BlockSpec((tm, K), lambda i, k, te, nu: (i, 0)),
                  pl.BlockSpec((tm, 1), lambda i, k, te, nu: (i, 0)),
                  pl.BlockSpec((None, None, K, tf), lambda i, k, te, nu: (moe_layer, te[i], 0, kk(i, k, nu))),
                  pl.BlockSpec((None, None, K, tf), lambda i, k, te, nu: (moe_layer, te[i], 0, nk + kk(i, k, nu))),
                  pl.BlockSpec((None, None, tf, K), lambda i, k, te, nu: (moe_layer, te[i], kk(i, k, nu), 0))],
        out_specs=pl.BlockSpec((tm, K), lambda i, k, te, nu: (i, 0)),
        scratch_shapes=[pltpu.VMEM((tm, K), f32)])
    return pl.pallas_call(
        _moe_group_kernel, grid_spec=grid_spec,
        out_shape=jax.ShapeDtypeStruct((NP, K), f32),
        compiler_params=_cparams(("parallel", "arbitrary")),
        name="moe_grouped",
    )(tile_e, n_used, xg, sw, w_gu, w_gu, w_down)


def _dispatch(r, tm):
    T = r.shape[0]
    A = 2 * T
    ntiles = A // tm + N_EXPERTS
    NP = ntiles * tm
    e_flat = r[:, 0:2].astype(jnp.int32).reshape(A)
    w_flat = r[:, 2:4].reshape(A)
    rank = r[:, 4:6].astype(jnp.int32)
    counts = jnp.sum((e_flat[:, None] == jnp.arange(N_EXPERTS)[None, :]).astype(jnp.int32), axis=0)
    pcounts = (counts + tm - 1) // tm * tm
    pends = jnp.cumsum(pcounts)
    pstarts = pends - pcounts
    n_used = pends[-1] // tm
    d = jnp.arange(NP - A, dtype=jnp.int32)
    e_pad = jnp.sum((d[:, None] >= jnp.cumsum(pcounts - counts)[None, :]).astype(jnp.int32), axis=1)
    keys = jnp.concatenate([e_flat * NP + jnp.arange(A, dtype=jnp.int32), e_pad * NP + A + d])
    toks = jnp.concatenate([jnp.arange(A, dtype=jnp.int32) // 2, jnp.zeros((NP - A,), jnp.int32)])
    wts = jnp.concatenate([w_flat, jnp.zeros((NP - A,), f32)])
    _, src, sw = lax.sort((keys, toks, wts), num_keys=1)
    sw = sw.reshape(NP, 1)
    tiles = jnp.arange(ntiles, dtype=jnp.int32)
    tile_e = jnp.minimum(jnp.searchsorted(pends, tiles * tm, side='right'), N_EXPERTS - 1).astype(jnp.int32)
    tile_e = jnp.where(tiles < n_used, tile_e, tile_e[jnp.maximum(n_used - 1, 0)])
    picks = r[:, 0:2].astype(jnp.int32)
    pick_start = jnp.sum(jnp.where(picks[:, :, None] == jnp.arange(N_EXPERTS)[None, None, :],
                                   pstarts[None, None, :], 0), axis=2)
    slots = pick_start + rank
    return src, sw, tile_e, n_used.reshape(1).astype(jnp.int32), slots


def outproj_moe(H, mo, om, w_out, g, w_router, w_gu, w_down, layer, moe_layer, tm, tf):
    h1, xn, r = outproj_router(H, mo, om, w_out, g, w_router, layer, moe_layer, tm)
    src, sw, tile_e, n_used, slots = _dispatch(r, tm)
    yg = moe_grouped(jnp.take(xn, src, axis=0), sw, tile_e, n_used, w_gu, w_down, moe_layer, tm, tf)
    return h1 + jnp.take(yg, slots[:, 0], axis=0) + jnp.take(yg, slots[:, 1], axis=0)


def _shared_kv_kernel(x_ref, g_ref, w_ref, cos_ref, sin_ref, zk_ref, zv_ref,
                      ck_ref, cv_ref, sk_ref, sv_ref, wk_ref, wv_ref, wkp_ref, wvp_ref):
    del zk_ref, zv_ref
    y = _mm(_rms(x_ref[...], g_ref[...]).astype(bf16), w_ref[...])
    cos = cos_ref[...]
    sin = sin_ref[...]
    ck_ref[...] = y[:, 0 * LANES:1 * LANES]
    cv_ref[...] = y[:, 1 * LANES:2 * LANES]
    sk_ref[...] = _rope_lanes(y[:, 2 * LANES:3 * LANES], cos, sin)
    sv_ref[...] = y[:, 3 * LANES:4 * LANES]
    wk = _rope_lanes(y[:, 4 * LANES:5 * LANES], cos, sin)
    wk_ref[...] = wk
    wkp_ref[...] = wk
    wv_ref[...] = y[:, 5 * LANES:6 * LANES]
    wvp_ref[...] = y[:, 5 * LANES:6 * LANES]


def shared_kv(H, g, w, cos, sin, nb, L, tm):
    T, K = H.shape
    assert tm == WINDOW and L % tm == 0
    per = L // tm
    ntp = nb * per
    nblk = nb * (per + 1) + (T // tm - ntp)
    row = pl.BlockSpec((tm, LANES), lambda i: (i, 0))
    padded = pl.BlockSpec((tm, LANES), lambda i: (jnp.where(i < ntp, i + i // per + 1, nb + i), 0))
    out = jax.ShapeDtypeStruct((T, LANES), f32)
    outp = jax.ShapeDtypeStruct((nblk * tm, LANES), f32)
    zeros = jnp.zeros((nblk * tm, LANES), f32)
    anyspec = pl.BlockSpec(memory_space=pl.ANY)
    return pl.pallas_call(
        _shared_kv_kernel,
        grid=(T // tm,),
        in_specs=[pl.BlockSpec((tm, K), lambda i: (i, 0)), pl.BlockSpec((1, K), lambda i: (0, 0)),
                  pl.BlockSpec((K, 6 * LANES), lambda i: (0, 0)), row, row, anyspec, anyspec],
        out_specs=(row,) * 6 + (padded, padded),
        out_shape=(out,) * 6 + (outp, outp),
        input_output_aliases={5: 6, 6: 7},
        compiler_params=_cparams(("parallel",)),
        name="shared_kv",
    )(H, g, w, cos, sin, zeros, zeros)


def _page_rows_kernel(*refs, npages):
    refs = refs[1:]
    ok_ref, ov_ref = refs[2 * npages:]
    for j in range(npages):
        rows = slice(j * PAGE, (j + 1) * PAGE)
        ok_ref[rows, :] = refs[j][...].T
        ov_ref[rows, :] = refs[npages + j][...].T


def page_rows(page_table, pool_k, pool_v, nb):
    npages = page_table.shape[0] // nb
    page = lambda j: pl.BlockSpec((None, LANES, PAGE), lambda b, pt: (pt[b * npages + j], 0, 0))
    out_spec = pl.BlockSpec((None, npages * PAGE, LANES), lambda b, pt: (b, 0, 0))
    out = jax.ShapeDtypeStruct((nb, npages * PAGE, LANES), f32)
    grid_spec = pltpu.PrefetchScalarGridSpec(
        num_scalar_prefetch=1, grid=(nb,),
        in_specs=[page(j) for j in range(npages)] * 2, out_specs=(out_spec, out_spec))
    return pl.pallas_call(
        functools.partial(_page_rows_kernel, npages=npages), grid_spec=grid_spec, out_shape=(out, out),
        compiler_params=_cparams(("parallel",)), name="page_rows",
    )(page_table, *([pool_k] * npages), *([pool_v] * npages))


def _compress_kernel(k_ref, v_ref, pek_ref, pev_ref, w1k_ref, w1v_ref, w2k_ref, w2v_ref, cos_ref, sin_ref,
                     ck_ref, cv_ref):
    k_strips = k_ref[...]
    v_strips = v_ref[...]
    n = k_strips.shape[0]
    half = 2 * CMP_HID
    row = lax.broadcasted_iota(jnp.int32, (n, LANES), 0)

    def tokens(strips, pe_ref, w1_ref, w2_ref):
        top = _mm((strips + pe_ref[0:1, :]).astype(bf16), w1_ref[:, 0:half])
        bot = _mm((strips + pe_ref[1:2, :]).astype(bf16), w1_ref[:, half:2 * half])
        hid = top + pltpu.roll(bot, n - 1, axis=0)
        out = _mm(_silu(hid).astype(bf16), w2_ref[...])
        return jnp.where(row < n - 1, out, 0.0)

    ck_ref[...] = _rope_lanes(tokens(k_strips, pek_ref, w1k_ref, w2k_ref), cos_ref[...], sin_ref[...])
    cv_ref[...] = tokens(v_strips, pev_ref, w1v_ref, w2v_ref)


def compress(k_src, v_src, consts):
    nb, n, flat = k_src.shape
    pek, pev, w1k, w1v, w2k, w2v, cos, sin = consts
    const2 = lambda shape: pl.BlockSpec(shape, lambda b: (0, 0))
    seq = pl.BlockSpec((None, n, flat), lambda b: (b, 0, 0))
    out_spec = pl.BlockSpec((None, n, LANES), lambda b: (b, 0, 0))
    out = jax.ShapeDtypeStruct((nb, n, LANES), f32)
    return pl.pallas_call(
        _compress_kernel, grid=(nb,),
        in_specs=[seq, seq, const2((2, flat)), const2((2, flat)), const2((flat, 4 * CMP_HID)),
                  const2((flat, 4 * CMP_HID)), const2((2 * CMP_HID, LANES)), const2((2 * CMP_HID, LANES)),
                  const2((n, LANES)), const2((n, LANES))],
        out_specs=(out_spec, out_spec), out_shape=(out, out),
        compiler_params=_cparams(("parallel",)), name="compress",
    )(k_src, v_src, pek, pev, w1k, w1v, w2k, w2v, cos, sin)


def _nsa_queries(p, cos, sin):
    tq = p.shape[0]
    lane = lax.broadcasted_iota(jnp.int32, (tq, LANES), 1)
    rows = []
    for head in range(B_HEADS):
        grp = head // HPG
        blk = MEM_W // LANES + head // 2
        x = _rope_lanes(p[:, blk * LANES:(blk + 1) * LANES], cos, sin) * (HEAD_DIM ** -0.5)
        if head % 2 != grp:
            x = pltpu.roll(x, HEAD_DIM, axis=1)
        keep = (lane >= grp * HEAD_DIM) & (lane < (grp + 1) * HEAD_DIM)
        rows.append(jnp.where(keep, x, 0.0))
    return jnp.concatenate(rows, axis=0)


def _masked_softmax_parts(s, mask):
    sm = jnp.where(mask, s, NEG)
    m = jnp.max(sm, axis=-1, keepdims=True)
    p = jnp.where(mask, jnp.exp(sm - m), 0.0)
    return p, jnp.sum(p, axis=-1, keepdims=True)


def _safe_div(o, l):
    return jnp.where(l > 0.0, o / jnp.where(l > 0.0, l, 1.0), 0.0)


def _split3(x):
    hi = x.astype(bf16)
    r1 = x - hi.astype(f32)
    mid = r1.astype(bf16)
    lo = (r1 - mid.astype(f32)).astype(bf16)
    return hi, mid, lo


def _select_bias(pcn, qpos_row, past_blocks, mt_ref, eye_ref, nsel):
    tq = qpos_row.shape[1]
    nr = -(-nsel // SUBLANES) * SUBLANES
    mt = mt_ref[0:nr, :]
    blk = lax.broadcasted_iota(jnp.int32, (nr, tq), 0)
    lane = lax.broadcasted_iota(jnp.int32, (tq, LANES), 1)
    cur = qpos_row // SEL_BLK
    forced = (blk == 0) | (blk == cur) | (blk == cur - 1)
    valid = blk * SEL_BLK <= qpos_row
    stacks, picks = [], []
    for grp in range(2):
        psum = pcn[grp * HPG * tq:(grp * HPG + 1) * tq]
        for hh in range(1, HPG):
            psum = psum + pcn[(grp * HPG + hh) * tq:(grp * HPG + hh + 1) * tq]
        hi, mid, lo = _split3(psum)
        imp = _nt(mt, hi) + _nt(mt, mid) + _nt(mt, lo)
        imp = jnp.where(forced, 1e9, jnp.where(valid, imp, -1e9))
        rank = jnp.zeros((nr, tq), f32)
        for j in range(nsel):
            rj = imp[j:j + 1, :]
            rank = rank + jnp.where(rj > imp, 1.0, jnp.where(rj == imp, jnp.where(blk > j, 1.0, 0.0), 0.0))
        sel_t = jnp.where((rank < SEL_TOP) & (blk < nsel), 1.0, 0.0).astype(bf16)
        pick = _tn(sel_t, eye_ref[0:nr, :])
        picks.append(pick)
        stacks += [jnp.where((pick > 0.5) & (lane < past_blocks), 0.0, NEG)] * HPG
    return jnp.concatenate(stacks, axis=0), picks


def _nsa_combine(p, o_c, o_s, o_w):
    tq = p.shape[0]
    gl = _sigmoid(p[:, (MEM_W + B_W):(MEM_W + B_W) + LANES])

    def gate(j):
        return jnp.concatenate([gl[:, 3 * h + j:3 * h + j + 1] for h in range(B_HEADS)], axis=0)

    comb = gate(0) * o_c + gate(1) * o_s + gate(2) * o_w
    lane = lax.broadcasted_iota(jnp.int32, (tq, LANES), 1)
    cols = []
    for c in range(B_HEADS // 2):
        lo = comb[2 * c * tq:(2 * c + 1) * tq]
        hi = comb[(2 * c + 1) * tq:(2 * c + 2) * tq]
        if (2 * c) // HPG == 1:
            lo = pltpu.roll(lo, HEAD_DIM, axis=1)
        if (2 * c + 1) // HPG == 0:
            hi = pltpu.roll(hi, HEAD_DIM, axis=1)
        cols.append(jnp.where(lane < HEAD_DIM, lo, hi))
    return jnp.concatenate(cols, axis=1)


def _nsa_prompt_kernel(p_ref, cos_ref, sin_ref, ck_ref, cv_ref, sk_ref, sv_ref, wk_ref, wv_ref, mt_ref, et_ref,
                       eye_ref, o_ref, *, tq, kc, nsel):
    qb = pl.program_id(1)
    s0 = qb * tq
    R = B_HEADS * tq
    p = p_ref[...]
    qf = _nsa_queries(p, cos_ref[...], sin_ref[...])
    qb16 = qf.astype(bf16)
    ii = lax.broadcasted_iota(jnp.int32, (tq, tq), 0)
    jj = lax.broadcasted_iota(jnp.int32, (tq, tq), 1)
    qposR = s0 + jnp.concatenate([lax.broadcasted_iota(jnp.int32, (tq, 1), 0)] * B_HEADS, axis=0)
    lane = lax.broadcasted_iota(jnp.int32, (R, LANES), 1)
    cmask = ((lane * CMP_STRIDE + (CMP_BLK - 1)) <= qposR) & (lane < ck_ref.shape[0] - 1)
    pc, lc = _masked_softmax_parts(_nt(qb16, ck_ref[...].astype(bf16)), cmask)
    pcn = _safe_div(pc, lc)
    o_c = _mm(pcn.astype(bf16), cv_ref[...].astype(bf16))
    qpos_row = s0 + lax.broadcasted_iota(jnp.int32, (1, tq), 1)
    bias, _ = _select_bias(pcn, qpos_row, s0 // SEL_BLK, mt_ref, eye_ref, nsel)
    qsel = jnp.concatenate([qb16, bias.astype(bf16)], axis=1)
    sd = _nt(qb16, sk_ref[pl.ds(s0, tq), :].astype(bf16)).reshape(B_HEADS, tq, tq)
    sd = jnp.where((ii >= jj)[None], sd, NEG).reshape(R, tq)
    m0 = jnp.max(sd, axis=-1, keepdims=True)
    pd = jnp.exp(sd - m0)
    init = (m0, jnp.sum(pd, axis=-1, keepdims=True), _mm(pd.astype(bf16), sv_ref[pl.ds(s0, tq), :].astype(bf16)))

    def body(c, carry):
        m_run, l_run, acc = carry
        k0 = pl.multiple_of(c * kc, kc)
        ka = jnp.concatenate([sk_ref[pl.ds(k0, kc), :].astype(bf16), et_ref[pl.ds(k0, kc), :]], axis=1)
        s = _nt(qsel, ka)
        m_new = jnp.maximum(m_run, jnp.max(s, axis=-1, keepdims=True))
        alpha = jnp.exp(m_run - m_new)
        pe = jnp.exp(s - m_new)
        l_new = alpha * l_run + jnp.sum(pe, axis=-1, keepdims=True)
        acc_new = alpha * acc + _mm(pe.astype(bf16), sv_ref[pl.ds(k0, kc), :].astype(bf16))
        return m_new, l_new, acc_new

    _, l_run, acc = lax.fori_loop(0, (s0 + kc - 1) // kc, body, init)
    o_s = acc / l_run
    wlen = WINDOW + tq
    wrow = lax.broadcasted_iota(jnp.int32, (wlen, LANES), 0)
    wlane = lax.broadcasted_iota(jnp.int32, (wlen, LANES), 1)
    is_pad = jnp.where((wrow + (s0 - WINDOW) < 0) & (wlane == 0), 1.0, 0.0).astype(bf16)
    kw = jnp.concatenate([wk_ref[pl.ds(s0, wlen), :].astype(bf16), is_pad], axis=1)
    qwin = jnp.concatenate([qb16, jnp.where(lane == 0, NEG, 0.0).astype(bf16)], axis=1)
    sw = _nt(qwin, kw).reshape(B_HEADS, tq, wlen)
    far = jnp.where((jj > ii)[None], sw[:, :, 0:tq], NEG)
    near = jnp.where((jj <= ii)[None], sw[:, :, wlen - tq:wlen], NEG)
    sw = jnp.concatenate([far, sw[:, :, tq:wlen - tq], near], axis=2).reshape(R, wlen)
    pw = jnp.exp(sw - jnp.max(sw, axis=-1, keepdims=True))
    o_w = _mm(pw.astype(bf16), wv_ref[pl.ds(s0, wlen), :].astype(bf16)) / jnp.sum(pw, axis=-1, keepdims=True)
    o_ref[...] = _nsa_combine(p, o_c, o_s, o_w)


def nsa_prompt(P, cos, sin, ck, cv, sk, sv, wk_pad, wv_pad, mt, et, eye, nb, L, tq, kc):
    nq = L // tq
    nsel = L // SEL_BLK
    seq = pl.BlockSpec((L, LANES), lambda b, i: (b, 0))
    win = pl.BlockSpec((WINDOW + L, LANES), lambda b, i: (b, 0))
    cmp_spec = pl.BlockSpec((None, LANES, LANES), lambda b, i: (b, 0, 0))
    const = lambda shape: pl.BlockSpec(shape, lambda b, i: (0, 0))
    return pl.pallas_call(
        functools.partial(_nsa_prompt_kernel, tq=tq, kc=kc, nsel=nsel),
        grid=(nb, nq),
        in_specs=[pl.BlockSpec((tq, B_IN_PAD), lambda b, i: (b * nq + i, 0)),
                  pl.BlockSpec((tq, LANES), lambda b, i: (i, 0)),
                  pl.BlockSpec((tq, LANES), lambda b, i: (i, 0)),
                  cmp_spec, cmp_spec, seq, seq, win, win,
                  const((LANES, LANES)), const((L, LANES)), const((LANES, LANES))],
        out_specs=pl.BlockSpec((tq, B_W), lambda b, i: (b * nq + i, 0)),
        out_shape=jax.ShapeDtypeStruct((nb * L, B_W), f32),
        compiler_params=_cparams(("parallel", "parallel")),
        name="nsa_prompt",
    )(P, cos, sin, ck, cv, sk, sv, wk_pad, wv_pad, mt, et, eye)


def _nsa_sample_kernel(*refs, npages, past):
    refs = refs[1:]
    kpages = refs[0:npages]
    vpages = refs[npages:2 * npages]
    (p_ref, cos_ref, sin_ref, ck_ref, cv_ref, nsk_ref, nsv_ref, cwk_ref, cwv_ref, nwk_ref, nwv_ref,
     mt_ref, et_ref, eye_ref, o_ref) = refs[2 * npages:]
    tq = S_PAD
    R = B_HEADS * tq
    p = p_ref[...]
    qb16 = _nsa_queries(p, cos_ref[...], sin_ref[...]).astype(bf16)
    tR = jnp.concatenate([lax.broadcasted_iota(jnp.int32, (tq, 1), 0)] * B_HEADS, axis=0)
    qposR = past + tR
    lane = lax.broadcasted_iota(jnp.int32, (R, LANES), 1)
    cmask = ((lane * CMP_STRIDE + (CMP_BLK - 1)) <= qposR) & (lane < ck_ref.shape[0] - 1)
    pc, lc = _masked_softmax_parts(_nt(qb16, ck_ref[...].astype(bf16)), cmask)
    pcn = _safe_div(pc, lc)
    o_c = _mm(pcn.astype(bf16), cv_ref[...].astype(bf16))
    qpos_row = past + lax.broadcasted_iota(jnp.int32, (1, tq), 1)
    new_blk = past // SEL_BLK
    bias, picks = _select_bias(pcn, qpos_row, new_blk, mt_ref, eye_ref, new_blk + 1)
    new_ok = lax.broadcasted_iota(jnp.int32, (R, tq), 1) <= tR
    ka = jnp.concatenate([jnp.concatenate([r[...] for r in kpages], axis=1).astype(bf16), et_ref[...]], axis=0)
    s_old = _mm(jnp.concatenate([qb16, bias.astype(bf16)], axis=1), ka)
    new_kept = jnp.concatenate([picks[h // HPG][:, new_blk:new_blk + 1] for h in range(B_HEADS)], axis=0) > 0.5
    s_new = jnp.where(new_ok, jnp.where(new_kept, _nt(qb16, nsk_ref[...].astype(bf16)), NEG), NEG)
    m = jnp.maximum(jnp.max(s_old, axis=-1, keepdims=True), jnp.max(s_new, axis=-1, keepdims=True))
    p_old = jnp.exp(s_old - m)
    p_new = jnp.exp(s_new - m)
    l = jnp.sum(p_old, axis=-1, keepdims=True) + jnp.sum(p_new, axis=-1, keepdims=True)
    svb = jnp.concatenate([r[...] for r in vpages], axis=1).astype(bf16)
    o_s = (_nt(p_old.astype(bf16), svb) + _mm(p_new.astype(bf16), nsv_ref[...].astype(bf16))) / l
    wb = cwk_ref.shape[1]
    wdist = qposR - (past - wb + lax.broadcasted_iota(jnp.int32, (R, wb), 1))
    w_old = jnp.where((wdist >= 0) & (wdist < WINDOW), _mm(qb16, cwk_ref[...].astype(bf16)), NEG)
    w_new = jnp.where(new_ok, _nt(qb16, nwk_ref[...].astype(bf16)), NEG)
    mw = jnp.maximum(jnp.max(w_old, axis=-1, keepdims=True), jnp.max(w_new, axis=-1, keepdims=True))
    pw_old = jnp.exp(w_old - mw)
    pw_new = jnp.exp(w_new - mw)
    lw = jnp.sum(pw_old, axis=-1, keepdims=True) + jnp.sum(pw_new, axis=-1, keepdims=True)
    o_w = (_nt(pw_old.astype(bf16), cwv_ref[...].astype(bf16)) + _mm(pw_new.astype(bf16), nwv_ref[...].astype(bf16))) / lw
    o_ref[...] = _nsa_combine(p, o_c, o_s, o_w)


def nsa_sample(P, row0, page_table, pool_k, pool_v, cos, sin, ck, cv, nsk, nsv, cwk, cwv, nwk, nwv, mt, et, eye):
    nb = cwk.shape[0]
    npages = page_table.shape[0] // nb
    past = npages * PAGE
    rb0 = row0 // S_PAD
    page = lambda j: pl.BlockSpec((None, LANES, PAGE), lambda b, pt: (pt[b * npages + j], 0, 0))
    new_rows = pl.BlockSpec((S_PAD, LANES), lambda b, pt: (rb0 + b, 0))
    per_seq = lambda n: pl.BlockSpec((None, LANES, n), lambda b, pt: (b, 0, 0))
    in_specs = ([page(j) for j in range(npages)] + [page(j) for j in range(npages)]
                + [pl.BlockSpec((S_PAD, B_IN_PAD), lambda b, pt: (rb0 + b, 0)),
                   pl.BlockSpec((S_PAD, LANES), lambda b, pt: (0, 0)),
                   pl.BlockSpec((S_PAD, LANES), lambda b, pt: (0, 0)),
                   per_seq(LANES), per_seq(LANES), new_rows, new_rows,
                   per_seq(cwk.shape[2]), per_seq(cwk.shape[2]), new_rows, new_rows,
                   pl.BlockSpec((LANES, LANES), lambda b, pt: (0, 0)),
                   pl.BlockSpec((LANES, past), lambda b, pt: (0, 0)),
                   pl.BlockSpec((LANES, LANES), lambda b, pt: (0, 0))])
    grid_spec = pltpu.PrefetchScalarGridSpec(
        num_scalar_prefetch=1, grid=(nb,), in_specs=in_specs,
        out_specs=pl.BlockSpec((S_PAD, B_W), lambda b, pt: (b, 0)))
    return pl.pallas_call(
        functools.partial(_nsa_sample_kernel, npages=npages, past=past),
        grid_spec=grid_spec,
        out_shape=jax.ShapeDtypeStruct((nb * S_PAD, B_W), f32),
        compiler_params=_cparams(("parallel",)),
        name="nsa_sample",
    )(page_table, *([pool_k] * npages), *([pool_v] * npages), P, cos, sin, ck, cv, nsk, nsv, cwk, cwv, nwk, nwv,
      mt, et, eye)


def _rope_tables(pos):
    half = HEAD_DIM // 2
    inv = ROPE_THETA ** (-jnp.arange(half, dtype=f32) / half)
    ang = pos.astype(f32)[:, None] * inv[None, :]
    cos = jnp.tile(jnp.cos(ang), (1, LANES // half))
    sin = jnp.tile(jnp.sin(ang), (1, LANES // half))
    sign = jnp.where((jnp.arange(LANES) % HEAD_DIM) < half, -1.0, 1.0).astype(f32)
    return cos, sin * sign[None, :]


def _cmp_to_sel(ncmp, nsel):
    per = SEL_BLK // CMP_STRIDE
    sub = CMP_BLK // CMP_STRIDE
    i = jnp.arange(LANES)[:, None]
    j = jnp.arange(LANES)[None, :]
    m = jnp.zeros((LANES, LANES), f32)
    for r in range(sub):
        m = m + (((i + r) // per) == j).astype(f32)
    m = jnp.where((i < ncmp) & (j < nsel), m / sub, 0.0)
    return m.astype(bf16)


def _block_expand(nkeys):
    j = jnp.arange(LANES)[:, None]
    k = jnp.arange(nkeys)[None, :]
    return ((k // SEL_BLK) == j).astype(bf16)


def _compress_consts(pe, w1, w2):
    w1r = w1.reshape(CMP_BLK, HEAD_DIM, CMP_HID)
    eye2 = jnp.eye(2, dtype=f32)

    def expand(w):
        return jnp.einsum('ldc,gh->lgdhc', w, eye2).reshape(16 * LANES, 2 * CMP_HID)

    w1s = jnp.concatenate([expand(w1r[:16]), expand(w1r[16:])], axis=1).astype(bf16)
    pes = jnp.stack([jnp.broadcast_to(pe[:16, None, :], (16, 2, HEAD_DIM)).reshape(-1),
                     jnp.broadcast_to(pe[16:, None, :], (16, 2, HEAD_DIM)).reshape(-1)])
    w2s = jnp.einsum('cd,gh->gchd', w2, eye2).reshape(2 * CMP_HID, LANES).astype(bf16)
    return pes, w1s, w2s


def kernel(x_prompt, x_sample, mem_prompt, state_conv, state_ssm, cache_cmp_k, cache_cmp_v, cache_sel_k, cache_sel_v, cache_win_k, cache_win_v, cache_mem_k, cache_mem_v, page_table, norm_mix, norm_ffn, norm_mem, w_mem_kv, w_in_a, conv_w_a, a_log, dt_bias, norm_out_a, w_out_a, w_in_b, w_out_b, norm_kv, w_kv_shared, cmp_pe_k, cmp_w1_k, cmp_w2_k, cmp_pe_v, cmp_w1_v, cmp_w2_v, w_gu_dense, w_down_dense, w_router, w_gu_exp, w_down_exp, norm_final):
    bp, lp, d = x_prompt.shape
    bs, ls, _ = x_sample.shape
    depth = norm_mix.shape[0]
    n_a = w_in_a.shape[0]
    past = page_table.shape[1] * PAGE
    tp = bp * lp
    T = tp + bs * S_PAD
    tm = 512

    xs = jnp.pad(x_sample, ((0, 0), (0, S_PAD - ls), (0, 0)))
    H = jnp.concatenate([x_prompt.reshape(tp, d), xs.reshape(bs * S_PAD, d)], axis=0)

    qkvz = MEM_W + CONV_DIM + A_W
    w_in_a_b = jnp.concatenate(
        [w_in_a[:, :, MEM_W:qkvz], w_in_a[:, :, :MEM_W], w_in_a[:, :, qkvz:],
         jnp.zeros(w_in_a.shape[:2] + (A_IN_PAD - w_in_a.shape[2],), f32)], axis=2).astype(bf16)
    w_in_b_b = jnp.pad(w_in_b, ((0, 0), (0, 0), (0, B_IN_PAD - w_in_b.shape[2]))).astype(bf16)
    w_out_a_b = w_out_a.astype(bf16)
    w_out_b_b = w_out_b.astype(bf16)
    w_gu_dense_b = w_gu_dense.astype(bf16)
    w_down_dense_b = w_down_dense.astype(bf16)
    w_gu_exp_b = w_gu_exp.astype(bf16)
    w_down_exp_b = w_down_exp.astype(bf16)
    w_router_p = jnp.pad(w_router, ((0, 0), (0, 0), (0, LANES - N_EXPERTS)))

    mem_k, mem_v = mem_kv(mem_prompt.reshape(bp * MEM_LEN, d), norm_mem.reshape(depth, 1, d),
                          w_mem_kv.astype(bf16), tm)
    cmk = jnp.transpose(cache_mem_k, (0, 1, 3, 4, 2)).reshape(depth, bs, MEM_W, MEM_LEN)
    cmv = jnp.transpose(cache_mem_v, (0, 1, 3, 4, 2)).reshape(depth, bs, MEM_W, MEM_LEN)
    zero_buf = jnp.zeros((bp, SUBLANES, CONV_DIM), f32)
    zero_state = jnp.zeros((1, bp, A_HEADS, LANES, LANES), f32)

    pos_p = jnp.arange(lp, dtype=jnp.int32)
    pos_s = past + jnp.arange(S_PAD, dtype=jnp.int32)
    cos_p, sin_p = _rope_tables(pos_p)
    cos_s, sin_s = _rope_tables(pos_s)

    conv_p, ssm_p, conv_s, ssm_s = [], [], [], []
    kv = None
    for layer in range(depth):
        is_a = layer < n_a
        g_mix = norm_mix[layer].reshape(1, d)
        if is_a:
            P = norm_matmul(H, g_mix, w_in_a_b, layer, tm, A_IN_PAD // 3)
            qblk = A_MEMQ_BLK
        else:
            lb = layer - n_a
            qblk = 0
            if lb == 0:
                cos_all = jnp.concatenate([jnp.tile(cos_p, (bp, 1)), jnp.tile(cos_s, (bs, 1))], axis=0)
                sin_all = jnp.concatenate([jnp.tile(sin_p, (bp, 1)), jnp.tile(sin_s, (bs, 1))], axis=0)
                kv = shared_kv(H, norm_kv.reshape(1, d), w_kv_shared.astype(bf16), cos_all, sin_all, bp, lp, tm)
                wk_pad, wv_pad = kv[6], kv[7]
                ncmp = (lp - CMP_BLK) // CMP_STRIDE + 1
                cpos = jnp.arange(LANES, dtype=jnp.int32) * CMP_STRIDE + CMP_BLK - 1
                cos_c, sin_c = _rope_tables(cpos)
                pek, w1k, w2k = _compress_consts(cmp_pe_k, cmp_w1_k, cmp_w2_k)
                pev, w1v, w2v = _compress_consts(cmp_pe_v, cmp_w1_v, cmp_w2_v)
                consts = (pek, pev, w1k, w1v, w2k, w2v, cos_c, sin_c)
                strips = lp // 16
                ck_p, cv_p = compress(kv[0][:tp].reshape(bp, strips, 16 * LANES),
                                      kv[1][:tp].reshape(bp, strips, 16 * LANES), consts)
                pt_flat = page_table.reshape(-1)
                feature_major = lambda pool: jnp.transpose(pool, (0, 2, 3, 1)).reshape(-1, LANES, PAGE)
                rows_ck, rows_cv = page_rows(pt_flat, feature_major(cache_cmp_k), feature_major(cache_cmp_v), bs)
                ck_s, cv_s = compress(rows_ck.reshape(bs, past // 16, 16 * LANES),
                                      rows_cv.reshape(bs, past // 16, 16 * LANES), consts)
                mt_p = _cmp_to_sel(ncmp, lp // SEL_BLK).T
                et_p = _block_expand(lp).T
                ncmp_s = (past + ls - CMP_BLK) // CMP_STRIDE + 1
                mt_s = _cmp_to_sel(ncmp_s, -(-(past + ls) // SEL_BLK)).T
                e_s = _block_expand(past)
                eye = jnp.eye(LANES, dtype=bf16)
                pool_sk = feature_major(cache_sel_k)
                pool_sv = feature_major(cache_sel_v)
                cwk = jnp.transpose(cache_win_k, (0, 2, 3, 1)).reshape(bs, LANES, -1)
                cwv = jnp.transpose(cache_win_v, (0, 2, 3, 1)).reshape(bs, LANES, -1)
            P = norm_matmul(H, g_mix, w_in_b_b, lb, tm, B_IN_PAD)
        om = (mem_attn_prompt(P, qblk, mem_k, mem_v, layer, bp, lp, 512),
              mem_attn_sample(P, qblk, tp, cmk, cmv, layer, 8))
        if is_a:
            conv8 = jnp.pad(conv_w_a[layer], ((0, SUBLANES - conv_w_a.shape[1]), (0, 0)))
            ab8 = jnp.zeros((SUBLANES, LANES), f32)
            ab8 = ab8.at[0, A_HEADS:2 * A_HEADS].set(a_log[layer]).at[1, A_HEADS:2 * A_HEADS].set(dt_bias[layer])
            gout = norm_out_a[layer].reshape(1, LANES)
            mo_p, sp = delta_mixer(P, 0, bp, lp, 512, 64, 512, zero_buf, conv8, ab8, gout, zero_state, 0)
            buf_s = jnp.pad(state_conv[layer], ((0, 0), (SUBLANES - state_conv.shape[2], 0), (0, 0)))
            mo_s, ss = delta_mixer(P, tp, bs, S_PAD, S_PAD, S_PAD, ls, buf_s, conv8, ab8, gout, state_ssm, layer)
            conv_p.append(jnp.stack([P[b * lp + lp - 3:(b + 1) * lp, :CONV_DIM] for b in range(bp)]))
            conv_s.append(P[tp:, :CONV_DIM].reshape(bs, S_PAD, CONV_DIM)[:, ls - 3:ls])
            ssm_p.append(sp)
            ssm_s.append(ss)
            w_out, wl = w_out_a_b, layer
        else:
            mo_p = nsa_prompt(P, cos_p, sin_p, ck_p, cv_p, kv[2], kv[3], wk_pad, wv_pad, mt_p, et_p, eye,
                              bp, lp, Q_BLOCK, 512)
            mo_s = nsa_sample(P, tp, pt_flat, pool_sk, pool_sv, cos_s, sin_s, ck_s, cv_s, kv[2], kv[3], cwk, cwv,
                              kv[4], kv[5], mt_s, e_s, eye)
            w_out, wl = w_out_b_b, lb
        mo = (mo_p, mo_s)
        g_ffn = norm_ffn[layer].reshape(1, d)
        if layer % 2 == 0:
            H = outproj_ffn(H, mo, om, w_out, g_ffn, w_gu_dense_b, w_down_dense_b, wl, layer // 2, tm, D_FF // 2)
        else:
            H = outproj_moe(H, mo, om, w_out, g_ffn, w_router_p, w_gu_exp_b, w_down_exp_b, wl, layer // 2,
                            tm, D_FF_EXPERT // 2)
    Y = final_norm(H, norm_final.reshape(1, d), tm)

    def rows_p(a):
        return a[:tp].reshape(bp, lp, 2, HEAD_DIM)

    def rows_s(a):
        return a[tp:].reshape(bs, S_PAD, 2, HEAD_DIM)[:, :ls]

    wlen = min(WINDOW, lp)
    return (Y[:tp].reshape(bp, lp, d), Y[tp:].reshape(bs, S_PAD, d)[:, :ls],
            jnp.stack(conv_p), jnp.stack(ssm_p),
            rows_p(kv[0]), rows_p(kv[1]), rows_p(kv[2]), rows_p(kv[3]),
            rows_p(kv[4])[:, lp - wlen:], rows_p(kv[5])[:, lp - wlen:],
            mem_k.reshape(depth, bp, MEM_LEN, 4, HEAD_DIM), mem_v.reshape(depth, bp, MEM_LEN, 4, HEAD_DIM),
            jnp.stack(conv_s), jnp.stack(ssm_s),
            rows_s(kv[0]), rows_s(kv[1]), rows_s(kv[2]), rows_s(kv[3]),
            jnp.concatenate([cache_win_k[:, ls:], rows_s(kv[4])], axis=1),
            jnp.concatenate([cache_win_v[:, ls:], rows_s(kv[5])], axis=1))
```

```python
import functools
import math

import jax
import jax.numpy as jnp
from jax import lax
from jax.experimental import pallas as pl
from jax.experimental.pallas import tpu as pltpu

f32 = jnp.float32
bf16 = jnp.bfloat16

EPS = 1e-6
NEG = -1e30
ROPE_THETA = 10000.0
HEAD_DIM = 64
LANES = 128
SUBLANES = 8
VMEM_LIMIT = 48 * 1024 * 1024

D_MODEL = 1024
MEM_LEN = 256
MEM_W = 256
A_HEADS = 6
A_W = 768
CONV_DIM = 2304
A_IN_PAD = 3456
A_MEMQ_BLK = 12
A_BA_BLK = 26
B_HEADS = 12
HPG = 6
B_W = 768
B_IN_PAD = 1152
CMP_BLK = 32
CMP_STRIDE = 16
CMP_HID = 256
SEL_BLK = 64
SEL_TOP = 16
WINDOW = 512
Q_BLOCK = 128
D_FF = 2816
N_EXPERTS = 8
D_FF_EXPERT = 3584
PAGE = 128
S_PAD = 8


def _cparams(sem):
    return pltpu.CompilerParams(dimension_semantics=sem, vmem_limit_bytes=VMEM_LIMIT)


def _nt(a, b):
    return lax.dot_general(a, b, (((1,), (1,)), ((), ())), preferred_element_type=f32)


def _tn(a, b):
    return lax.dot_general(a, b, (((0,), (0,)), ((), ())), preferred_element_type=f32)


def _mm(a, b):
    return jnp.dot(a, b, preferred_element_type=f32)


def _rms(x, g):
    return (x * lax.rsqrt(jnp.mean(x * x, axis=-1, keepdims=True) + EPS)) * g


def _sigmoid(x):
    return 1.0 / (1.0 + jnp.exp(-x))


def _silu(x):
    return x * _sigmoid(x)


def _softplus(x):
    return jnp.maximum(x, 0.0) + jnp.log(1.0 + jnp.exp(-jnp.abs(x)))


def _rope_lanes(x, cos, sin_signed):
    lane = lax.broadcasted_iota(jnp.int32, x.shape, x.ndim - 1)
    lo = (lane % HEAD_DIM) < (HEAD_DIM // 2)
    partner = jnp.where(lo, pltpu.roll(x, LANES - HEAD_DIM // 2, axis=x.ndim - 1),
                        pltpu.roll(x, HEAD_DIM // 2, axis=x.ndim - 1))
    return x * cos + partner * sin_signed


def _norm_matmul_kernel(x_ref, g_ref, w_ref, o_ref, xn_ref):
    @pl.when(pl.program_id(1) == 0)
    def _():
        xn_ref[...] = _rms(x_ref[...], g_ref[...]).astype(bf16)

    o_ref[...] = _mm(xn_ref[...], w_ref[...])


def norm_matmul(x, g, w, layer, tm, tn):
    T, K = x.shape
    N = w.shape[2]
    return pl.pallas_call(
        _norm_matmul_kernel,
        grid=(T // tm, N // tn),
        in_specs=[pl.BlockSpec((tm, K), lambda i, j: (i, 0)),
                  pl.BlockSpec((1, K), lambda i, j: (0, 0)),
                  pl.BlockSpec((None, K, tn), lambda i, j: (layer, 0, j))],
        out_specs=pl.BlockSpec((tm, tn), lambda i, j: (i, j)),
        out_shape=jax.ShapeDtypeStruct((T, N), f32),
        scratch_shapes=[pltpu.VMEM((tm, K), bf16)],
        compiler_params=_cparams(("parallel", "arbitrary")),
        name="norm_matmul",
    )(x, g, w)


def _mem_kv_kernel(x_ref, g_ref, w_ref, k_ref, v_ref):
    y = _mm(_rms(x_ref[...], g_ref[...]).astype(bf16), w_ref[...])
    k_ref[...] = y[:, :MEM_W]
    v_ref[...] = y[:, MEM_W:]


def mem_kv(mem, g, w, tm):
    R, K = mem.shape
    NL = w.shape[0]
    out = jax.ShapeDtypeStruct((NL, R, MEM_W), f32)
    return pl.pallas_call(
        _mem_kv_kernel,
        grid=(NL, R // tm),
        in_specs=[pl.BlockSpec((tm, K), lambda l, i: (i, 0)),
                  pl.BlockSpec((None, 1, K), lambda l, i: (l, 0, 0)),
                  pl.BlockSpec((None, K, 2 * MEM_W), lambda l, i: (l, 0, 0))],
        out_specs=(pl.BlockSpec((None, tm, MEM_W), lambda l, i: (l, i, 0)),
                   pl.BlockSpec((None, tm, MEM_W), lambda l, i: (l, i, 0))),
        out_shape=(out, out),
        compiler_params=_cparams(("parallel", "parallel")),
        name="mem_kv",
    )(mem, g, w)


def _final_norm_kernel(x_ref, g_ref, o_ref):
    o_ref[...] = _rms(x_ref[...], g_ref[...])


def final_norm(x, g, tm):
    T, K = x.shape
    return pl.pallas_call(
        _final_norm_kernel,
        grid=(T // tm,),
        in_specs=[pl.BlockSpec((tm, K), lambda i: (i, 0)), pl.BlockSpec((1, K), lambda i: (0, 0))],
        out_specs=pl.BlockSpec((tm, K), lambda i: (i, 0)),
        out_shape=jax.ShapeDtypeStruct((T, K), f32),
        compiler_params=_cparams(("parallel",)),
        name="final_norm",
    )(x, g)


def _mem_attend(q, k, v, transposed=False):
    tq = q.shape[0]
    nh = MEM_W // HEAD_DIM
    lane = lax.broadcasted_iota(jnp.int32, (tq, MEM_W), 1)
    masks = [(lane >= HEAD_DIM * h) & (lane < HEAD_DIM * (h + 1)) for h in range(nh)]
    qs = jnp.concatenate([jnp.where(m, q, 0.0) for m in masks], axis=0).astype(bf16)
    s = (_mm(qs, k) if transposed else _nt(qs, k)) * (HEAD_DIM ** -0.5)
    p = jnp.exp(s - jnp.max(s, axis=-1, keepdims=True))
    l = jnp.sum(p, axis=-1, keepdims=True)
    pb = p.astype(bf16)
    o = (_nt(pb, v) if transposed else _mm(pb, v)) / l
    out = jnp.zeros((tq, MEM_W), f32)
    for h in range(nh):
        out = jnp.where(masks[h], o[h * tq:(h + 1) * tq], out)
    return out


def _mem_attn_prompt_kernel(q_ref, k_ref, v_ref, o_ref):
    o_ref[...] = _mem_attend(q_ref[...], k_ref[...].astype(bf16), v_ref[...].astype(bf16))


def mem_attn_prompt(P, qblk, mk, mv, layer, nb, L, tq):
    nq = L // tq
    return pl.pallas_call(
        _mem_attn_prompt_kernel,
        grid=(nb, nq),
        in_specs=[pl.BlockSpec((tq, MEM_W), lambda b, i: (b * nq + i, qblk)),
                  pl.BlockSpec((None, MEM_LEN, MEM_W), lambda b, i: (layer, b, 0)),
                  pl.BlockSpec((None, MEM_LEN, MEM_W), lambda b, i: (layer, b, 0))],
        out_specs=pl.BlockSpec((tq, MEM_W), lambda b, i: (b * nq + i, 0)),
        out_shape=jax.ShapeDtypeStruct((nb * L, MEM_W), f32),
        compiler_params=_cparams(("parallel", "parallel")),
        name="mem_attn_prompt",
    )(P, mk, mv)


def _mem_attn_sample_kernel(q_ref, k_ref, v_ref, o_ref, *, bb):
    for b in range(bb):
        rows = slice(b * S_PAD, (b + 1) * S_PAD)
        o_ref[rows, :] = _mem_attend(q_ref[rows, :], k_ref[b].astype(bf16), v_ref[b].astype(bf16), transposed=True)


def mem_attn_sample(P, qblk, row0, ckt, cvt, layer, bb):
    nb = ckt.shape[1]
    blk0 = row0 // (bb * S_PAD)
    return pl.pallas_call(
        functools.partial(_mem_attn_sample_kernel, bb=bb),
        grid=(nb // bb,),
        in_specs=[pl.BlockSpec((bb * S_PAD, MEM_W), lambda i: (blk0 + i, qblk)),
                  pl.BlockSpec((None, bb, MEM_W, MEM_LEN), lambda i: (layer, i, 0, 0)),
                  pl.BlockSpec((None, bb, MEM_W, MEM_LEN), lambda i: (layer, i, 0, 0))],
        out_specs=pl.BlockSpec((bb * S_PAD, MEM_W), lambda i: (i, 0)),
        out_shape=jax.ShapeDtypeStruct((nb * S_PAD, MEM_W), f32),
        compiler_params=_cparams(("parallel",)),
        name="mem_attn_sample",
    )(P, ckt, cvt)


def _delta_kernel(q_ref, k_ref, v_ref, z_ref, ba_ref, buf_ref, cw_ref, ab_ref, go_ref, s0_ref,
                  o_ref, sout_ref, qs_ref, ks_ref, vs_ref, gs_ref, bs_ref, tail_ref, s_ref,
                  *, TL, C, valid):
    t = pl.program_id(1)

    @pl.when(t == 0)
    def _():
        tail_ref[...] = buf_ref[...]
        s_ref[...] = s0_ref[...]

    row8 = lax.broadcasted_iota(jnp.int32, (SUBLANES, A_W), 0)

    def conv(x_ref, c0):
        x = x_ref[...]
        b8 = tail_ref[:, c0:c0 + A_W]
        c8 = cw_ref[:, c0:c0 + A_W]
        acc = x * c8[3:4, :]
        for s in range(1, 4):
            r = pltpu.roll(x, s, axis=0)
            fix = jnp.where(row8 < s, pltpu.roll(b8, s, axis=0), r[0:SUBLANES])
            r = jnp.concatenate([fix, r[SUBLANES:]], axis=0) if TL > SUBLANES else fix
            acc = acc + r * c8[3 - s:4 - s, :]
        tail_ref[:, c0:c0 + A_W] = x[TL - SUBLANES:TL]
        return _silu(acc)

    q = conv(q_ref, 0)
    k = conv(k_ref, A_W)
    v = conv(v_ref, 2 * A_W)
    ba = ba_ref[...]
    ab = ab_ref[...]
    beta_all = _sigmoid(ba)
    g_all = -jnp.exp(ab[0:1, :]) * _softplus(ba + ab[1:2, :])
    if valid < TL:
        live = lax.broadcasted_iota(jnp.int32, (TL, A_W), 0) < valid
        live1 = lax.broadcasted_iota(jnp.int32, (TL, LANES), 0) < valid
        k = jnp.where(live, k, 0.0)
        v = jnp.where(live, v, 0.0)
        beta_all = jnp.where(live1, beta_all, 0.0)
        g_all = jnp.where(live1, g_all, 0.0)
    rowc = lax.broadcasted_iota(jnp.int32, (TL, LANES), 0) & (C - 1)
    gc_all = g_all
    s = 1
    while s < C:
        gc_all = gc_all + jnp.where(rowc >= s, pltpu.roll(gc_all, s, axis=0), 0.0)
        s *= 2
    vs_ref[...] = v
    for h in range(A_HEADS):
        sl = slice(h * LANES, (h + 1) * LANES)
        qh = q[:, sl]
        kh = k[:, sl]
        qs_ref[:, sl] = qh * lax.rsqrt(jnp.sum(qh * qh, axis=-1, keepdims=True) + EPS) * (LANES ** -0.5)
        ks_ref[:, sl] = kh * lax.rsqrt(jnp.sum(kh * kh, axis=-1, keepdims=True) + EPS)
        gs_ref[:, sl] = jnp.broadcast_to(gc_all[:, A_HEADS + h:A_HEADS + h + 1], (TL, LANES))
        bs_ref[:, sl] = jnp.broadcast_to(beta_all[:, h:h + 1], (TL, LANES))

    ii = lax.broadcasted_iota(jnp.int32, (C, C), 0)
    jj = lax.broadcasted_iota(jnp.int32, (C, C), 1)
    tri = ii >= jj
    strict = ii > jj
    eye = jnp.where(ii == jj, 1.0, 0.0)
    gout = go_ref[...]
    nsteps = int(math.log2(C)) - 1

    heads = range(A_HEADS)
    lanes_of = [slice(h * LANES, (h + 1) * LANES) for h in heads]

    cpi = 2 if TL // C >= 2 else 1

    def chunks(c, carry):
        rows_of = [pl.ds(pl.multiple_of((c * cpi + i) * C, C), C) for i in range(cpi)]
        units = [(rows, h) for rows in rows_of for h in heads]
        U = range(len(units))
        qc = [qs_ref[rows, lanes_of[h]] for rows, h in units]
        kc = [ks_ref[rows, lanes_of[h]] for rows, h in units]
        vc = [vs_ref[rows, lanes_of[h]] for rows, h in units]
        gcc = [gs_ref[rows, lanes_of[h]] for rows, h in units]
        bc = [bs_ref[rows, lanes_of[h]] for rows, h in units]
        kb = [kc[u] * bc[u] for u in U]
        kq = [_nt(jnp.concatenate([kb[u], qc[u]], axis=0).astype(bf16), kc[u].astype(bf16)) for u in U]
        decay = []
        for u in U:
            gi = gcc[u][:, 0:C]
            gj = jnp.sum(jnp.where(ii == jj, gi, 0.0), axis=0, keepdims=True)
            decay.append(jnp.where(tri, jnp.exp(jnp.where(tri, gi - gj, 0.0)), 0.0))
        n = [jnp.where(strict, -(kq[u][0:C] * decay[u]), 0.0) for u in U]
        a = [jnp.where(tri, kq[u][C:2 * C] * decay[u], 0.0).astype(bf16) for u in U]
        x = [eye + n[u] for u in U]
        p = n
        for _ in range(nsteps):
            pb = [p[u].astype(bf16) for u in U]
            p = [_mm(pb[u], pb[u]) for u in U]
            x = [x[u] + _mm(x[u].astype(bf16), p[u].astype(bf16)) for u in U]
        eg = [jnp.exp(gcc[u]) for u in U]
        uw = [_mm(x[u].astype(bf16), jnp.concatenate([vc[u] * bc[u], kb[u] * eg[u]], axis=1).astype(bf16))
              for u in U]
        wqin = [jnp.concatenate([uw[u][:, LANES:], qc[u] * eg[u]], axis=0).astype(bf16) for u in U]
        glast = [gcc[u][C - 1:C, :] for u in U]
        kg = [(kc[u] * jnp.exp(glast[u] - gcc[u])).astype(bf16) for u in U]
        for i, rows in enumerate(rows_of):
            us = [i * A_HEADS + h for h in heads]
            S = [s_ref[h] for h in heads]
            wq = [_mm(wqin[u], S[h].astype(bf16)) for h, u in zip(heads, us)]
            vnb = [(uw[u][:, :LANES] - wq[h][0:C]).astype(bf16) for h, u in zip(heads, us)]
            av = [_mm(a[u], vnb[h]) for h, u in zip(heads, us)]
            kv_new = [_tn(kg[u], vnb[h]) for h, u in zip(heads, us)]
            for h, u in zip(heads, us):
                s_ref[h] = S[h] * jnp.exp(glast[u]) + kv_new[h]
                o_ref[rows, lanes_of[h]] = _rms(wq[h][C:2 * C] + av[h], gout) * _silu(z_ref[rows, lanes_of[h]])
        return carry

    if TL // C == cpi:
        chunks(0, 0)
    else:
        lax.fori_loop(0, TL // (C * cpi), chunks, 0)

    @pl.when(t == pl.num_programs(1) - 1)
    def _():
        sout_ref[...] = s_ref[...]


def delta_mixer(P, row0, nb, L, TL, C, valid, buf8, conv8, ab8, gout, s0, layer):
    nt = L // TL
    rb0 = row0 // TL
    tile = lambda c: pl.BlockSpec((TL, A_W), lambda b, t: (rb0 + b * nt + t, c))
    state = pl.BlockSpec((None, A_HEADS, LANES, LANES), lambda b, t: (b, 0, 0, 0))
    state_in = pl.BlockSpec((None, None, A_HEADS, LANES, LANES), lambda b, t: (layer, b, 0, 0, 0))
    return pl.pallas_call(
        functools.partial(_delta_kernel, TL=TL, C=C, valid=valid),
        grid=(nb, nt),
        in_specs=[tile(0), tile(1), tile(2), tile(3),
                  pl.BlockSpec((TL, LANES), lambda b, t: (rb0 + b * nt + t, A_BA_BLK)),
                  pl.BlockSpec((None, SUBLANES, CONV_DIM), lambda b, t: (b, 0, 0)),
                  pl.BlockSpec((SUBLANES, CONV_DIM), lambda b, t: (0, 0)),
                  pl.BlockSpec((SUBLANES, LANES), lambda b, t: (0, 0)),
                  pl.BlockSpec((1, LANES), lambda b, t: (0, 0)),
                  state_in],
        out_specs=(pl.BlockSpec((TL, A_W), lambda b, t: (b * nt + t, 0)), state),
        out_shape=(jax.ShapeDtypeStruct((nb * L, A_W), f32),
                   jax.ShapeDtypeStruct((nb, A_HEADS, LANES, LANES), f32)),
        scratch_shapes=[pltpu.VMEM((TL, A_W), f32)] * 5
        + [pltpu.VMEM((SUBLANES, CONV_DIM), f32), pltpu.VMEM((A_HEADS, LANES, LANES), f32)],
        compiler_params=_cparams(("parallel", "arbitrary")),
        name="delta_mixer",
    )(P, P, P, P, P, buf8, conv8, ab8, gout, s0)


def _attn_residual(refs, ntp):
    h_ref, mop_ref, mos_ref, omp_ref, oms_ref, wo1_ref, wo2_ref = refs
    is_p = pl.program_id(0) < ntp
    mo = jnp.where(is_p, mop_ref[...], mos_ref[...])
    om = jnp.where(is_p, omp_ref[...], oms_ref[...])
    return h_ref[...] + _mm(mo.astype(bf16), wo1_ref[...]) + _mm(om.astype(bf16), wo2_ref[...])


def _ffn_kernel(*refs, ntp):
    g_ref, wg_ref, wu_ref, wd_ref, o_ref, xn_ref, acc_ref = refs[7:]
    k = pl.program_id(1)

    @pl.when(k == 0)
    def _():
        h1 = _attn_residual(refs[:7], ntp)
        acc_ref[...] = h1
        xn_ref[...] = _rms(h1, g_ref[...]).astype(bf16)

    xn = xn_ref[...]
    a = _silu(_mm(xn, wg_ref[...])) * _mm(xn, wu_ref[...])
    acc_ref[...] += _mm(a.astype(bf16), wd_ref[...])

    @pl.when(k == pl.num_programs(1) - 1)
    def _():
        o_ref[...] = acc_ref[...]


def _outproj_specs(tm, K, mw, ow, ntp, layer):
    prow = lambda i, *_: (jnp.minimum(i, ntp - 1), 0)
    srow = lambda i, *_: (jnp.maximum(i - ntp, 0), 0)
    return [pl.BlockSpec((tm, K), lambda i, *_: (i, 0)),
            pl.BlockSpec((tm, mw), prow), pl.BlockSpec((tm, mw), srow),
            pl.BlockSpec((tm, ow), prow), pl.BlockSpec((tm, ow), srow),
            pl.BlockSpec((None, mw, K), lambda i, *_: (layer, 0, 0)),
            pl.BlockSpec((None, ow, K), lambda i, *_: (layer, mw // ow, 0)),
            pl.BlockSpec((1, K), lambda i, *_: (0, 0))]


def outproj_ffn(H, mo, om, w_out, g, w_gu, w_down, layer, ffn_layer, tm, tf):
    T, K = H.shape
    FF = w_down.shape[1]
    nk = FF // tf
    mw, ow = mo[0].shape[1], om[0].shape[1]
    ntp = mo[0].shape[0] // tm
    return pl.pallas_call(
        functools.partial(_ffn_kernel, ntp=ntp),
        grid=(T // tm, nk),
        in_specs=_outproj_specs(tm, K, mw, ow, ntp, layer)
        + [pl.BlockSpec((None, K, tf), lambda i, k: (ffn_layer, 0, k)),
           pl.BlockSpec((None, K, tf), lambda i, k: (ffn_layer, 0, nk + k)),
           pl.BlockSpec((None, tf, K), lambda i, k: (ffn_layer, k, 0))],
        out_specs=pl.BlockSpec((tm, K), lambda i, k: (i, 0)),
        out_shape=jax.ShapeDtypeStruct((T, K), f32),
        scratch_shapes=[pltpu.VMEM((tm, K), bf16), pltpu.VMEM((tm, K), f32)],
        compiler_params=_cparams(("parallel", "arbitrary")),
        name="outproj_ffn",
    )(H, mo[0], mo[1], om[0], om[1], w_out, w_out, g, w_gu, w_gu, w_down)


def _router_kernel(*refs, ntp):
    g_ref, wr_ref, h1_ref, xn_ref, r_ref, cnt_ref = refs[7:]
    i = pl.program_id(0)
    tm = h1_ref.shape[0]

    @pl.when(i == 0)
    def _():
        cnt_ref[...] = jnp.zeros_like(cnt_ref)

    h1 = _attn_residual(refs[:7], ntp)
    h1_ref[...] = h1
    xn = _rms(h1, g_ref[...])
    xn_ref[...] = xn
    lane = lax.broadcasted_iota(jnp.int32, (tm, LANES), 1)
    xh = xn.astype(bf16)
    xl = (xn - xh.astype(f32)).astype(bf16)
    wr = wr_ref[...]
    wh = wr.astype(bf16)
    wl = (wr - wh.astype(f32)).astype(bf16)
    logits = _mm(xh, wh) + _mm(xh, wl) + _mm(xl, wh)
    logits = jnp.where(lane < N_EXPERTS, logits, -jnp.inf)
    m1 = jnp.max(logits, axis=-1, keepdims=True)
    i1 = jnp.min(jnp.where(logits == m1, lane, LANES), axis=-1, keepdims=True)
    rest = jnp.where(lane == i1, -jnp.inf, logits)
    m2 = jnp.max(rest, axis=-1, keepdims=True)
    i2 = jnp.min(jnp.where(rest == m2, lane, LANES), axis=-1, keepdims=True)
    e2 = jnp.exp(m2 - m1)
    w1 = 1.0 / (1.0 + e2)
    w2 = e2 / (1.0 + e2)
    hot = jnp.where((lane == i1) | (lane == i2), 1.0, 0.0)
    ri = lax.broadcasted_iota(jnp.int32, (tm, tm), 0)
    ci = lax.broadcasted_iota(jnp.int32, (tm, tm), 1)
    before = jnp.where(ri > ci, 1.0, 0.0).astype(bf16)
    cum = _mm(before, hot.astype(bf16)) + cnt_ref[...]
    r1 = jnp.sum(jnp.where(lane == i1, cum, 0.0), axis=-1, keepdims=True)
    r2 = jnp.sum(jnp.where(lane == i2, cum, 0.0), axis=-1, keepdims=True)
    cnt_ref[...] += jnp.sum(hot, axis=0, keepdims=True)
    cols = (i1.astype(f32), i2.astype(f32), w1, w2, r1, r2)
    r = jnp.zeros((tm, LANES), f32)
    for c, val in enumerate(cols):
        r = jnp.where(lane == c, val, r)
    r_ref[...] = r


def outproj_router(H, mo, om, w_out, g, w_router, layer, moe_layer, tm):
    T, K = H.shape
    mw, ow = mo[0].shape[1], om[0].shape[1]
    ntp = mo[0].shape[0] // tm
    return pl.pallas_call(
        functools.partial(_router_kernel, ntp=ntp),
        grid=(T // tm,),
        in_specs=_outproj_specs(tm, K, mw, ow, ntp, layer)
        + [pl.BlockSpec((None, K, LANES), lambda i: (moe_layer, 0, 0))],
        out_specs=(pl.BlockSpec((tm, K), lambda i: (i, 0)), pl.BlockSpec((tm, K), lambda i: (i, 0)),
                   pl.BlockSpec((tm, LANES), lambda i: (i, 0))),
        out_shape=(jax.ShapeDtypeStruct((T, K), f32), jax.ShapeDtypeStruct((T, K), f32),
                   jax.ShapeDtypeStruct((T, LANES), f32)),
        scratch_shapes=[pltpu.VMEM((1, LANES), f32)],
        compiler_params=_cparams(("arbitrary",)),
        name="outproj_router",
    )(H, mo[0], mo[1], om[0], om[1], w_out, w_out, g, w_router)


def _moe_group_kernel(te_ref, nu_ref, x_ref, sw_ref, wg_ref, wu_ref, wd_ref, o_ref, acc_ref):
    del te_ref
    i = pl.program_id(0)
    k = pl.program_id(1)

    @pl.when(k == 0)
    def _():
        acc_ref[...] = jnp.zeros_like(acc_ref)

    @pl.when(i < nu_ref[0])
    def _():
        x = x_ref[...].astype(bf16)
        a = _silu(_mm(x, wg_ref[...])) * _mm(x, wu_ref[...])
        acc_ref[...] += _mm(a.astype(bf16), wd_ref[...])

    @pl.when(k == pl.num_programs(1) - 1)
    def _():
        o_ref[...] = acc_ref[...] * sw_ref[...]


def moe_grouped(xg, sw, tile_e, n_used, w_gu, w_down, moe_layer, tm, tf):
    NP, K = xg.shape
    FF = w_down.shape[2]
    nk = FF // tf

    def kk(i, k, nu):
        return jnp.where(i < nu[0], k, nk - 1)

    grid_spec = pltpu.PrefetchScalarGridSpec(
        num_scalar_prefetch=2, grid=(NP // tm, nk),
        in_specs=[pl.BlockSpec((tm, K), lambda i, k, te, nu: (i, 0)),
                  pl.BlockSpec((tm, 1), lambda i, k, te, nu: (i, 0)),
                  pl.BlockSpec((None, None, K, tf), lambda i, k, te, nu: (moe_layer, te[i], 0, kk(i, k, nu))),
                  pl.BlockSpec((None, None, K, tf), lambda i, k, te, nu: (moe_layer, te[i], 0, nk + kk(i, k, nu))),
                  pl.BlockSpec((None, None, tf, K), lambda i, k, te, nu: (moe_layer, te[i], kk(i, k, nu), 0))],
        out_specs=pl.BlockSpec((tm, K), lambda i, k, te, nu: (i, 0)),
        scratch_shapes=[pltpu.VMEM((tm, K), f32)])
    return pl.pallas_call(
        _moe_group_kernel, grid_spec=grid_spec,
        out_shape=jax.ShapeDtypeStruct((NP, K), f32),
        compiler_params=_cparams(("parallel", "arbitrary")),
        name="moe_grouped",
    )(tile_e, n_used, xg, sw, w_gu, w_gu, w_down)


def _dispatch(r, tm):
    T = r.shape[0]
    A = 2 * T
    ntiles = A // tm + N_EXPERTS
    NP = ntiles * tm
    e_flat = r[:, 0:2].astype(jnp.int32).reshape(A)
    w_flat = r[:, 2:4].reshape(A)
    rank = r[:, 4:6].astype(jnp.int32)
    counts = jnp.sum((e_flat[:, None] == jnp.arange(N_EXPERTS)[None, :]).astype(jnp.int32), axis=0)
    pcounts = (counts + tm - 1) // tm * tm
    pends = jnp.cumsum(pcounts)
    pstarts = pends - pcounts
    n_used = pends[-1] // tm
    d = jnp.arange(NP - A, dtype=jnp.int32)
    e_pad = jnp.sum((d[:, None] >= jnp.cumsum(pcounts - counts)[None, :]).astype(jnp.int32), axis=1)
    keys = jnp.concatenate([e_flat * NP + jnp.arange(A, dtype=jnp.int32), e_pad * NP + A + d])
    toks = jnp.concatenate([jnp.arange(A, dtype=jnp.int32) // 2, jnp.zeros((NP - A,), jnp.int32)])
    wts = jnp.concatenate([w_flat, jnp.zeros((NP - A,), f32)])
    _, src, sw = lax.sort((keys, toks, wts), num_keys=1)
    sw = sw.reshape(NP, 1)
    tiles = jnp.arange(ntiles, dtype=jnp.int32)
    tile_e = jnp.minimum(jnp.searchsorted(pends, tiles * tm, side='right'), N_EXPERTS - 1).astype(jnp.int32)
    tile_e = jnp.where(tiles < n_used, tile_e, tile_e[jnp.maximum(n_used - 1, 0)])
    picks = r[:, 0:2].astype(jnp.int32)
    pick_start = jnp.sum(jnp.where(picks[:, :, None] == jnp.arange(N_EXPERTS)[None, None, :],
                                   pstarts[None, None, :], 0), axis=2)
    slots = pick_start + rank
    return src, sw, tile_e, n_used.reshape(1).astype(jnp.int32), slots


def outproj_moe(H, mo, om, w_out, g, w_router, w_gu, w_down, layer, moe_layer, tm, tf):
    h1, xn, r = outproj_router(H, mo, om, w_out, g, w_router, layer, moe_layer, tm)
    src, sw, tile_e, n_used, slots = _dispatch(r, tm)
    yg = moe_grouped(jnp.take(xn, src, axis=0), sw, tile_e, n_used, w_gu, w_down, moe_layer, tm, tf)
    return h1 + jnp.take(yg, slots[:, 0], axis=0) + jnp.take(yg, slots[:, 1], axis=0)


def _shared_kv_kernel(x_ref, g_ref, w_ref, cos_ref, sin_ref, zk_ref, zv_ref,
                      ck_ref, cv_ref, sk_ref, sv_ref, wk_ref, wv_ref, wkp_ref, wvp_ref):
    del zk_ref, zv_ref
    y = _mm(_rms(x_ref[...], g_ref[...]).astype(bf16), w_ref[...])
    cos = cos_ref[...]
    sin = sin_ref[...]
    ck_ref[...] = y[:, 0 * LANES:1 * LANES]
    cv_ref[...] = y[:, 1 * LANES:2 * LANES]
    sk_ref[...] = _rope_lanes(y[:, 2 * LANES:3 * LANES], cos, sin)
    sv_ref[...] = y[:, 3 * LANES:4 * LANES]
    wk = _rope_lanes(y[:, 4 * LANES:5 * LANES], cos, sin)
    wk_ref[...] = wk
    wkp_ref[...] = wk
    wv_ref[...] = y[:, 5 * LANES:6 * LANES]
    wvp_ref[...] = y[:, 5 * LANES:6 * LANES]


def shared_kv(H, g, w, cos, sin, nb, L, tm):
    T, K = H.shape
    assert tm == WINDOW and L % tm == 0
    per = L // tm
    ntp = nb * per
    nblk = nb * (per + 1) + (T // tm - ntp)
    row = pl.BlockSpec((tm, LANES), lambda i: (i, 0))
    table = pl.BlockSpec((tm, LANES), lambda i: (jnp.where(i < ntp, i % per, per), 0))
    padded = pl.BlockSpec((tm, LANES), lambda i: (jnp.where(i < ntp, i + i // per + 1, nb + i), 0))
    out = jax.ShapeDtypeStruct((T, LANES), f32)
    outp = jax.ShapeDtypeStruct((nblk * tm, LANES), f32)
    zeros = jnp.zeros((nblk * tm, LANES), f32)
    anyspec = pl.BlockSpec(memory_space=pl.ANY)
    return pl.pallas_call(
        _shared_kv_kernel,
        grid=(T // tm,),
        in_specs=[pl.BlockSpec((tm, K), lambda i: (i, 0)), pl.BlockSpec((1, K), lambda i: (0, 0)),
                  pl.BlockSpec((K, 6 * LANES), lambda i: (0, 0)), table, table, anyspec, anyspec],
        out_specs=(row,) * 6 + (padded, padded),
        out_shape=(out,) * 6 + (outp, outp),
        input_output_aliases={5: 6, 6: 7},
        compiler_params=_cparams(("parallel",)),
        name="shared_kv",
    )(H, g, w, cos, sin, zeros, zeros)


def _page_strips_kernel(*refs, npages):
    refs = refs[1:]
    ok_ref, ov_ref, scr_ref = refs[2 * npages:]
    per = PAGE // CMP_STRIDE
    for j in range(npages):
        for src, dst in ((refs[j], ok_ref), (refs[npages + j], ov_ref)):
            scr_ref[...] = src[...].T
            for l in range(CMP_STRIDE):
                dst[j * per:(j + 1) * per, l * LANES:(l + 1) * LANES] = scr_ref[pl.ds(l, per, stride=CMP_STRIDE), :]


def page_strips(page_table, pool_k, pool_v, nb):
    npages = page_table.shape[0] // nb
    n = npages * PAGE // CMP_STRIDE
    page = lambda j: pl.BlockSpec((None, LANES, PAGE), lambda b, pt: (pt[b * npages + j], 0, 0))
    out_spec = pl.BlockSpec((None, n, CMP_STRIDE * LANES), lambda b, pt: (b, 0, 0))
    out = jax.ShapeDtypeStruct((nb, n, CMP_STRIDE * LANES), f32)
    grid_spec = pltpu.PrefetchScalarGridSpec(
        num_scalar_prefetch=1, grid=(nb,),
        in_specs=[page(j) for j in range(npages)] * 2, out_specs=(out_spec, out_spec),
        scratch_shapes=[pltpu.VMEM((PAGE, LANES), f32)])
    return pl.pallas_call(
        functools.partial(_page_strips_kernel, npages=npages), grid_spec=grid_spec, out_shape=(out, out),
        compiler_params=_cparams(("parallel",)), name="page_strips",
    )(page_table, *([pool_k] * npages), *([pool_v] * npages))


def _compress_kernel(k_ref, v_ref, pek_ref, pev_ref, w1k_ref, w1v_ref, w2k_ref, w2v_ref, cos_ref, sin_ref,
                     ck_ref, cv_ref):
    k_strips = k_ref[...]
    v_strips = v_ref[...]
    n = k_strips.shape[0]
    half = 2 * CMP_HID
    row = lax.broadcasted_iota(jnp.int32, (n, LANES), 0)

    def tokens(strips, pe_ref, w1_ref, w2_ref):
        top = _mm((strips + pe_ref[0:1, :]).astype(bf16), w1_ref[:, 0:half])
        bot = _mm((strips + pe_ref[1:2, :]).astype(bf16), w1_ref[:, half:2 * half])
        hid = top + pltpu.roll(bot, n - 1, axis=0)
        out = _mm(_silu(hid).astype(bf16), w2_ref[...])
        return jnp.where(row < n - 1, out, 0.0)

    ck_ref[...] = _rope_lanes(tokens(k_strips, pek_ref, w1k_ref, w2k_ref), cos_ref[...], sin_ref[...])
    cv_ref[...] = tokens(v_strips, pev_ref, w1v_ref, w2v_ref)


def compress(k_src, v_src, consts):
    nb, n, flat = k_src.shape
    pek, pev, w1k, w1v, w2k, w2v, cos, sin = consts
    const2 = lambda shape: pl.BlockSpec(shape, lambda b: (0, 0))
    seq = pl.BlockSpec((None, n, flat), lambda b: (b, 0, 0))
    out_spec = pl.BlockSpec((None, n, LANES), lambda b: (b, 0, 0))
    out = jax.ShapeDtypeStruct((nb, n, LANES), f32)
    return pl.pallas_call(
        _compress_kernel, grid=(nb,),
        in_specs=[seq, seq, const2((2, flat)), const2((2, flat)), const2((flat, 4 * CMP_HID)),
                  const2((flat, 4 * CMP_HID)), const2((2 * CMP_HID, LANES)), const2((2 * CMP_HID, LANES)),
                  const2((n, LANES)), const2((n, LANES))],
        out_specs=(out_spec, out_spec), out_shape=(out, out),
        compiler_params=_cparams(("parallel",)), name="compress",
    )(k_src, v_src, pek, pev, w1k, w1v, w2k, w2v, cos, sin)


def _nsa_queries(p, cos, sin):
    tq = p.shape[0]
    lane = lax.broadcasted_iota(jnp.int32, (tq, LANES), 1)
    rows = []
    for head in range(B_HEADS):
        grp = head // HPG
        blk = MEM_W // LANES + head // 2
        x = _rope_lanes(p[:, blk * LANES:(blk + 1) * LANES], cos, sin) * (HEAD_DIM ** -0.5)
        if head % 2 != grp:
            x = pltpu.roll(x, HEAD_DIM, axis=1)
        keep = (lane >= grp * HEAD_DIM) & (lane < (grp + 1) * HEAD_DIM)
        rows.append(jnp.where(keep, x, 0.0))
    return jnp.concatenate(rows, axis=0)


def _masked_softmax_parts(s, mask):
    sm = jnp.where(mask, s, NEG)
    m = jnp.max(sm, axis=-1, keepdims=True)
    p = jnp.where(mask, jnp.exp(sm - m), 0.0)
    return p, jnp.sum(p, axis=-1, keepdims=True)


def _safe_div(o, l):
    return jnp.where(l > 0.0, o / jnp.where(l > 0.0, l, 1.0), 0.0)


def _split3(x):
    hi = x.astype(bf16)
    r1 = x - hi.astype(f32)
    mid = r1.astype(bf16)
    lo = (r1 - mid.astype(f32)).astype(bf16)
    return hi, mid, lo


def _select_bias(pcn, qpos_row, past_blocks, mt_ref, eye_ref, nsel):
    tq = qpos_row.shape[1]
    nr = -(-nsel // SUBLANES) * SUBLANES
    mt = mt_ref[0:nr, :]
    blk = lax.broadcasted_iota(jnp.int32, (nr, tq), 0)
    lane = lax.broadcasted_iota(jnp.int32, (tq, LANES), 1)
    cur = qpos_row // SEL_BLK
    forced = (blk == 0) | (blk == cur) | (blk == cur - 1)
    valid = blk * SEL_BLK <= qpos_row
    stacks, picks = [], []
    for grp in range(2):
        psum = pcn[grp * HPG * tq:(grp * HPG + 1) * tq]
        for hh in range(1, HPG):
            psum = psum + pcn[(grp * HPG + hh) * tq:(grp * HPG + hh + 1) * tq]
        hi, mid, lo = _split3(psum)
        imp = _nt(mt, hi) + _nt(mt, mid) + _nt(mt, lo)
        imp = jnp.where(forced, 1e9, jnp.where(valid, imp, -1e9))
        rank = jnp.zeros((nr, tq), f32)
        for j in range(nsel):
            rj = imp[j:j + 1, :]
            rank = rank + jnp.where(rj > imp, 1.0, jnp.where(rj == imp, jnp.where(blk > j, 1.0, 0.0), 0.0))
        sel_t = jnp.where((rank < SEL_TOP) & (blk < nsel), 1.0, 0.0).astype(bf16)
        pick = _tn(sel_t, eye_ref[0:nr, :])
        picks.append(pick)
        stacks += [jnp.where((pick > 0.5) & (lane < past_blocks), 0.0, NEG)] * HPG
    return jnp.concatenate(stacks, axis=0), picks


def _nsa_combine(p, o_c, o_s, o_w):
    tq = p.shape[0]
    gl = _sigmoid(p[:, (MEM_W + B_W):(MEM_W + B_W) + LANES])

    def gate(j):
        return jnp.concatenate([gl[:, 3 * h + j:3 * h + j + 1] for h in range(B_HEADS)], axis=0)

    comb = gate(0) * o_c + gate(1) * o_s + gate(2) * o_w
    lane = lax.broadcasted_iota(jnp.int32, (tq, LANES), 1)
    cols = []
    for c in range(B_HEADS // 2):
        lo = comb[2 * c * tq:(2 * c + 1) * tq]
        hi = comb[(2 * c + 1) * tq:(2 * c + 2) * tq]
        if (2 * c) // HPG == 1:
            lo = pltpu.roll(lo, HEAD_DIM, axis=1)
        if (2 * c + 1) // HPG == 0:
            hi = pltpu.roll(hi, HEAD_DIM, axis=1)
        cols.append(jnp.where(lane < HEAD_DIM, lo, hi))
    return jnp.concatenate(cols, axis=1)


def _nsa_prompt_kernel(p_ref, cos_ref, sin_ref, ck_ref, cv_ref, sk_ref, sv_ref, wk_ref, wv_ref, mt_ref, et_ref,
                       eye_ref, o_ref, *, tq, kc, nsel):
    qb = pl.program_id(1)
    s0 = qb * tq
    R = B_HEADS * tq
    p = p_ref[...]
    qf = _nsa_queries(p, cos_ref[...], sin_ref[...])
    qb16 = qf.astype(bf16)
    ii = lax.broadcasted_iota(jnp.int32, (tq, tq), 0)
    jj = lax.broadcasted_iota(jnp.int32, (tq, tq), 1)
    qposR = s0 + jnp.concatenate([lax.broadcasted_iota(jnp.int32, (tq, 1), 0)] * B_HEADS, axis=0)
    lane = lax.broadcasted_iota(jnp.int32, (R, LANES), 1)
    cmask = ((lane * CMP_STRIDE + (CMP_BLK - 1)) <= qposR) & (lane < ck_ref.shape[0] - 1)
    pc, lc = _masked_softmax_parts(_nt(qb16, ck_ref[...].astype(bf16)), cmask)
    pcn = _safe_div(pc, lc)
    o_c = _mm(pcn.astype(bf16), cv_ref[...].astype(bf16))
    qpos_row = s0 + lax.broadcasted_iota(jnp.int32, (1, tq), 1)
    bias, _ = _select_bias(pcn, qpos_row, s0 // SEL_BLK, mt_ref, eye_ref, nsel)
    qsel = jnp.concatenate([qb16, bias.astype(bf16)], axis=1)
    sd = _nt(qb16, sk_ref[pl.ds(s0, tq), :].astype(bf16)).reshape(B_HEADS, tq, tq)
    sd = jnp.where((ii >= jj)[None], sd, NEG).reshape(R, tq)
    m0 = jnp.max(sd, axis=-1, keepdims=True)
    pd = jnp.exp(sd - m0)
    init = (m0, jnp.sum(pd, axis=-1, keepdims=True), _mm(pd.astype(bf16), sv_ref[pl.ds(s0, tq), :].astype(bf16)))

    def body(c, carry):
        m_run, l_run, acc = carry
        k0 = pl.multiple_of(c * kc, kc)
        ka = jnp.concatenate([sk_ref[pl.ds(k0, kc), :].astype(bf16), et_ref[pl.ds(k0, kc), :]], axis=1)
        s = _nt(qsel, ka)
        m_new = jnp.maximum(m_run, jnp.max(s, axis=-1, keepdims=True))
        alpha = jnp.exp(m_run - m_new)
        pe = jnp.exp(s - m_new)
        l_new = alpha * l_run + jnp.sum(pe, axis=-1, keepdims=True)
        acc_new = alpha * acc + _mm(pe.astype(bf16), sv_ref[pl.ds(k0, kc), :].astype(bf16))
        return m_new, l_new, acc_new

    _, l_run, acc = lax.fori_loop(0, (s0 + kc - 1) // kc, body, init)
    o_s = acc / l_run
    wlen = WINDOW + tq
    wrow = lax.broadcasted_iota(jnp.int32, (wlen, LANES), 0)
    wlane = lax.broadcasted_iota(jnp.int32, (wlen, LANES), 1)
    is_pad = jnp.where((wrow + (s0 - WINDOW) < 0) & (wlane == 0), 1.0, 0.0).astype(bf16)
    kw = jnp.concatenate([wk_ref[pl.ds(s0, wlen), :].astype(bf16), is_pad], axis=1)
    qwin = jnp.concatenate([qb16, jnp.where(lane == 0, NEG, 0.0).astype(bf16)], axis=1)
    sw = _nt(qwin, kw).reshape(B_HEADS, tq, wlen)
    far = jnp.where((jj > ii)[None], sw[:, :, 0:tq], NEG)
    near = jnp.where((jj <= ii)[None], sw[:, :, wlen - tq:wlen], NEG)
    sw = jnp.concatenate([far, sw[:, :, tq:wlen - tq], near], axis=2).reshape(R, wlen)
    pw = jnp.exp(sw - jnp.max(sw, axis=-1, keepdims=True))
    o_w = _mm(pw.astype(bf16), wv_ref[pl.ds(s0, wlen), :].astype(bf16)) / jnp.sum(pw, axis=-1, keepdims=True)
    o_ref[...] = _nsa_combine(p, o_c, o_s, o_w)


def nsa_prompt(P, cos, sin, ck, cv, sk, sv, wk_pad, wv_pad, mt, et, eye, nb, L, tq, kc):
    nq = L // tq
    nsel = L // SEL_BLK
    seq = pl.BlockSpec((L, LANES), lambda b, i: (b, 0))
    win = pl.BlockSpec((WINDOW + L, LANES), lambda b, i: (b, 0))
    cmp_spec = pl.BlockSpec((None, LANES, LANES), lambda b, i: (b, 0, 0))
    const = lambda shape: pl.BlockSpec(shape, lambda b, i: (0, 0))
    return pl.pallas_call(
        functools.partial(_nsa_prompt_kernel, tq=tq, kc=kc, nsel=nsel),
        grid=(nb, nq),
        in_specs=[pl.BlockSpec((tq, B_IN_PAD), lambda b, i: (b * nq + i, 0)),
                  pl.BlockSpec((tq, LANES), lambda b, i: (i, 0)),
                  pl.BlockSpec((tq, LANES), lambda b, i: (i, 0)),
                  cmp_spec, cmp_spec, seq, seq, win, win,
                  const((LANES, LANES)), const((L, LANES)), const((LANES, LANES))],
        out_specs=pl.BlockSpec((tq, B_W), lambda b, i: (b * nq + i, 0)),
        out_shape=jax.ShapeDtypeStruct((nb * L, B_W), f32),
        compiler_params=_cparams(("parallel", "parallel")),
        name="nsa_prompt",
    )(P, cos, sin, ck, cv, sk, sv, wk_pad, wv_pad, mt, et, eye)


def _nsa_sample_kernel(*refs, npages, past):
    refs = refs[1:]
    kpages = refs[0:npages]
    vpages = refs[npages:2 * npages]
    (p_ref, cos_ref, sin_ref, ck_ref, cv_ref, nsk_ref, nsv_ref, cwk_ref, cwv_ref, nwk_ref, nwv_ref,
     mt_ref, et_ref, eye_ref, o_ref) = refs[2 * npages:]
    tq = S_PAD
    R = B_HEADS * tq
    p = p_ref[...]
    qb16 = _nsa_queries(p, cos_ref[...], sin_ref[...]).astype(bf16)
    tR = jnp.concatenate([lax.broadcasted_iota(jnp.int32, (tq, 1), 0)] * B_HEADS, axis=0)
    qposR = past + tR
    lane = lax.broadcasted_iota(jnp.int32, (R, LANES), 1)
    cmask = ((lane * CMP_STRIDE + (CMP_BLK - 1)) <= qposR) & (lane < ck_ref.shape[0] - 1)
    pc, lc = _masked_softmax_parts(_nt(qb16, ck_ref[...].astype(bf16)), cmask)
    pcn = _safe_div(pc, lc)
    o_c = _mm(pcn.astype(bf16), cv_ref[...].astype(bf16))
    qpos_row = past + lax.broadcasted_iota(jnp.int32, (1, tq), 1)
    new_blk = past // SEL_BLK
    bias, picks = _select_bias(pcn, qpos_row, new_blk, mt_ref, eye_ref, new_blk + 1)
    new_ok = lax.broadcasted_iota(jnp.int32, (R, tq), 1) <= tR
    ka = jnp.concatenate([jnp.concatenate([r[...] for r in kpages], axis=1).astype(bf16), et_ref[...]], axis=0)
    s_old = _mm(jnp.concatenate([qb16, bias.astype(bf16)], axis=1), ka)
    new_kept = jnp.concatenate([picks[h // HPG][:, new_blk:new_blk + 1] for h in range(B_HEADS)], axis=0) > 0.5
    s_new = jnp.where(new_ok, jnp.where(new_kept, _nt(qb16, nsk_ref[...].astype(bf16)), NEG), NEG)
    m = jnp.maximum(jnp.max(s_old, axis=-1, keepdims=True), jnp.max(s_new, axis=-1, keepdims=True))
    p_old = jnp.exp(s_old - m)
    p_new = jnp.exp(s_new - m)
    l = jnp.sum(p_old, axis=-1, keepdims=True) + jnp.sum(p_new, axis=-1, keepdims=True)
    svb = jnp.concatenate([r[...] for r in vpages], axis=1).astype(bf16)
    o_s = (_nt(p_old.astype(bf16), svb) + _mm(p_new.astype(bf16), nsv_ref[...].astype(bf16))) / l
    wb = cwk_ref.shape[1]
    wdist = qposR - (past - wb + lax.broadcasted_iota(jnp.int32, (R, wb), 1))
    w_old = jnp.where((wdist >= 0) & (wdist < WINDOW), _mm(qb16, cwk_ref[...].astype(bf16)), NEG)
    w_new = jnp.where(new_ok, _nt(qb16, nwk_ref[...].astype(bf16)), NEG)
    mw = jnp.maximum(jnp.max(w_old, axis=-1, keepdims=True), jnp.max(w_new, axis=-1, keepdims=True))
    pw_old = jnp.exp(w_old - mw)
    pw_new = jnp.exp(w_new - mw)
    lw = jnp.sum(pw_old, axis=-1, keepdims=True) + jnp.sum(pw_new, axis=-1, keepdims=True)
    o_w = (_nt(pw_old.astype(bf16), cwv_ref[...].astype(bf16)) + _mm(pw_new.astype(bf16), nwv_ref[...].astype(bf16))) / lw
    o_ref[...] = _nsa_combine(p, o_c, o_s, o_w)


def nsa_sample(P, row0, page_table, pool_k, pool_v, cos, sin, ck, cv, nsk, nsv, cwk, cwv, nwk, nwv, mt, et, eye):
    nb = cwk.shape[0]
    npages = page_table.shape[0] // nb
    past = npages * PAGE
    rb0 = row0 // S_PAD
    page = lambda j: pl.BlockSpec((None, LANES, PAGE), lambda b, pt: (pt[b * npages + j], 0, 0))
    new_rows = pl.BlockSpec((S_PAD, LANES), lambda b, pt: (rb0 + b, 0))
    per_seq = lambda n: pl.BlockSpec((None, LANES, n), lambda b, pt: (b, 0, 0))
    in_specs = ([page(j) for j in range(npages)] + [page(j) for j in range(npages)]
                + [pl.BlockSpec((S_PAD, B_IN_PAD), lambda b, pt: (rb0 + b, 0)),
                   pl.BlockSpec((S_PAD, LANES), lambda b, pt: (0, 0)),
                   pl.BlockSpec((S_PAD, LANES), lambda b, pt: (0, 0)),
                   per_seq(LANES), per_seq(LANES), new_rows, new_rows,
                   per_seq(cwk.shape[2]), per_seq(cwk.shape[2]), new_rows, new_rows,
                   pl.BlockSpec((LANES, LANES), lambda b, pt: (0, 0)),
                   pl.BlockSpec((LANES, past), lambda b, pt: (0, 0)),
                   pl.BlockSpec((LANES, LANES), lambda b, pt: (0, 0))])
    grid_spec = pltpu.PrefetchScalarGridSpec(
        num_scalar_prefetch=1, grid=(nb,), in_specs=in_specs,
        out_specs=pl.BlockSpec((S_PAD, B_W), lambda b, pt: (b, 0)))
    return pl.pallas_call(
        functools.partial(_nsa_sample_kernel, npages=npages, past=past),
        grid_spec=grid_spec,
        out_shape=jax.ShapeDtypeStruct((nb * S_PAD, B_W), f32),
        compiler_params=_cparams(("parallel",)),
        name="nsa_sample",
    )(page_table, *([pool_k] * npages), *([pool_v] * npages), P, cos, sin, ck, cv, nsk, nsv, cwk, cwv, nwk, nwv,
      mt, et, eye)


def _rope_tables(pos):
    half = HEAD_DIM // 2
    inv = ROPE_THETA ** (-jnp.arange(half, dtype=f32) / half)
    ang = pos.astype(f32)[:, None] * inv[None, :]
    cos = jnp.tile(jnp.cos(ang), (1, LANES // half))
    sin = jnp.tile(jnp.sin(ang), (1, LANES // half))
    sign = jnp.where((jnp.arange(LANES) % HEAD_DIM) < half, -1.0, 1.0).astype(f32)
    return cos, sin * sign[None, :]


def _cmp_to_sel(ncmp, nsel):
    per = SEL_BLK // CMP_STRIDE
    sub = CMP_BLK // CMP_STRIDE
    i = jnp.arange(LANES)[:, None]
    j = jnp.arange(LANES)[None, :]
    m = jnp.zeros((LANES, LANES), f32)
    for r in range(sub):
        m = m + (((i + r) // per) == j).astype(f32)
    m = jnp.where((i < ncmp) & (j < nsel), m / sub, 0.0)
    return m.astype(bf16)


def _block_expand(nkeys):
    j = jnp.arange(LANES)[:, None]
    k = jnp.arange(nkeys)[None, :]
    return ((k // SEL_BLK) == j).astype(bf16)


def _compress_consts(pe, w1, w2):
    w1r = w1.reshape(CMP_BLK, HEAD_DIM, CMP_HID)
    eye2 = jnp.eye(2, dtype=f32)

    def expand(w):
        return jnp.einsum('ldc,gh->lgdhc', w, eye2).reshape(16 * LANES, 2 * CMP_HID)

    w1s = jnp.concatenate([expand(w1r[:16]), expand(w1r[16:])], axis=1).astype(bf16)
    pes = jnp.stack([jnp.broadcast_to(pe[:16, None, :], (16, 2, HEAD_DIM)).reshape(-1),
                     jnp.broadcast_to(pe[16:, None, :], (16, 2, HEAD_DIM)).reshape(-1)])
    w2s = jnp.einsum('cd,gh->gchd', w2, eye2).reshape(2 * CMP_HID, LANES).astype(bf16)
    return pes, w1s, w2s


def kernel(x_prompt, x_sample, mem_prompt, state_conv, state_ssm, cache_cmp_k, cache_cmp_v, cache_sel_k, cache_sel_v, cache_win_k, cache_win_v, cache_mem_k, cache_mem_v, page_table, norm_mix, norm_ffn, norm_mem, w_mem_kv, w_in_a, conv_w_a, a_log, dt_bias, norm_out_a, w_out_a, w_in_b, w_out_b, norm_kv, w_kv_shared, cmp_pe_k, cmp_w1_k, cmp_w2_k, cmp_pe_v, cmp_w1_v, cmp_w2_v, w_gu_dense, w_down_dense, w_router, w_gu_exp, w_down_exp, norm_final):
    bp, lp, d = x_prompt.shape
    bs, ls, _ = x_sample.shape
    depth = norm_mix.shape[0]
    n_a = w_in_a.shape[0]
    past = page_table.shape[1] * PAGE
    tp = bp * lp
    T = tp + bs * S_PAD
    tm = 512

    xs = jnp.pad(x_sample, ((0, 0), (0, S_PAD - ls), (0, 0)))
    H = jnp.concatenate([x_prompt.reshape(tp, d), xs.reshape(bs * S_PAD, d)], axis=0)

    qkvz = MEM_W + CONV_DIM + A_W
    w_in_a_b = jnp.concatenate(
        [w_in_a[:, :, MEM_W:qkvz], w_in_a[:, :, :MEM_W], w_in_a[:, :, qkvz:],
         jnp.zeros(w_in_a.shape[:2] + (A_IN_PAD - w_in_a.shape[2],), f32)], axis=2).astype(bf16)
    w_in_b_b = jnp.pad(w_in_b, ((0, 0), (0, 0), (0, B_IN_PAD - w_in_b.shape[2]))).astype(bf16)
    w_out_a_b = w_out_a.astype(bf16)
    w_out_b_b = w_out_b.astype(bf16)
    w_gu_dense_b = w_gu_dense.astype(bf16)
    w_down_dense_b = w_down_dense.astype(bf16)
    w_gu_exp_b = w_gu_exp.astype(bf16)
    w_down_exp_b = w_down_exp.astype(bf16)
    w_router_p = jnp.pad(w_router, ((0, 0), (0, 0), (0, LANES - N_EXPERTS)))

    mem_k, mem_v = mem_kv(mem_prompt.reshape(bp * MEM_LEN, d), norm_mem.reshape(depth, 1, d),
                          w_mem_kv.astype(bf16), tm)
    cmk = jnp.transpose(cache_mem_k, (0, 1, 3, 4, 2)).reshape(depth, bs, MEM_W, MEM_LEN)
    cmv = jnp.transpose(cache_mem_v, (0, 1, 3, 4, 2)).reshape(depth, bs, MEM_W, MEM_LEN)
    zero_buf = jnp.zeros((bp, SUBLANES, CONV_DIM), f32)
    zero_state = jnp.zeros((1, bp, A_HEADS, LANES, LANES), f32)

    pos_p = jnp.arange(lp, dtype=jnp.int32)
    pos_s = past + jnp.arange(S_PAD, dtype=jnp.int32)
    cos_p, sin_p = _rope_tables(pos_p)
    cos_s, sin_s = _rope_tables(pos_s)

    conv_p, ssm_p, conv_s, ssm_s = [], [], [], []
    kv = None
    for layer in range(depth):
        is_a = layer < n_a
        g_mix = norm_mix[layer].reshape(1, d)
        if is_a:
            P = norm_matmul(H, g_mix, w_in_a_b, layer, tm, A_IN_PAD // 3)
            qblk = A_MEMQ_BLK
        else:
            lb = layer - n_a
            qblk = 0
            if lb == 0:
                cos_all = jnp.concatenate([cos_p, jnp.tile(cos_s, (tm // S_PAD, 1))], axis=0)
                sin_all = jnp.concatenate([sin_p, jnp.tile(sin_s, (tm // S_PAD, 1))], axis=0)
                kv = shared_kv(H, norm_kv.reshape(1, d), w_kv_shared.astype(bf16), cos_all, sin_all, bp, lp, tm)
                wk_pad, wv_pad = kv[6], kv[7]
                ncmp = (lp - CMP_BLK) // CMP_STRIDE + 1
                cpos = jnp.arange(LANES, dtype=jnp.int32) * CMP_STRIDE + CMP_BLK - 1
                cos_c, sin_c = _rope_tables(cpos)
                pek, w1k, w2k = _compress_consts(cmp_pe_k, cmp_w1_k, cmp_w2_k)
                pev, w1v, w2v = _compress_consts(cmp_pe_v, cmp_w1_v, cmp_w2_v)
                consts = (pek, pev, w1k, w1v, w2k, w2v, cos_c, sin_c)
                strips = lp // 16
                ck_p, cv_p = compress(kv[0][:tp].reshape(bp, strips, 16 * LANES),
                                      kv[1][:tp].reshape(bp, strips, 16 * LANES), consts)
                pt_flat = page_table.reshape(-1)
                feature_major = lambda pool: jnp.transpose(pool, (0, 2, 3, 1)).reshape(-1, LANES, PAGE)
                ck_s, cv_s = compress(*page_strips(pt_flat, feature_major(cache_cmp_k), feature_major(cache_cmp_v), bs),
                                      consts)
                mt_p = _cmp_to_sel(ncmp, lp // SEL_BLK).T
                et_p = _block_expand(lp).T
                ncmp_s = (past + ls - CMP_BLK) // CMP_STRIDE + 1
                mt_s = _cmp_to_sel(ncmp_s, -(-(past + ls) // SEL_BLK)).T
                e_s = _block_expand(past)
                eye = jnp.eye(LANES, dtype=bf16)
                pool_sk = feature_major(cache_sel_k)
                pool_sv = feature_major(cache_sel_v)
                cwk = jnp.transpose(cache_win_k, (0, 2, 3, 1)).reshape(bs, LANES, -1)
                cwv = jnp.transpose(cache_win_v, (0, 2, 3, 1)).reshape(bs, LANES, -1)
            P = norm_matmul(H, g_mix, w_in_b_b, lb, tm, B_IN_PAD)
        om = (mem_attn_prompt(P, qblk, mem_k, mem_v, layer, bp, lp, 512),
              mem_attn_sample(P, qblk, tp, cmk, cmv, layer, 8))
        if is_a:
            conv8 = jnp.pad(conv_w_a[layer], ((0, SUBLANES - conv_w_a.shape[1]), (0, 0)))
            ab8 = jnp.zeros((SUBLANES, LANES), f32)
            ab8 = ab8.at[0, A_HEADS:2 * A_HEADS].set(a_log[layer]).at[1, A_HEADS:2 * A_HEADS].set(dt_bias[layer])
            gout = norm_out_a[layer].reshape(1, LANES)
            mo_p, sp = delta_mixer(P, 0, bp, lp, 512, 64, 512, zero_buf, conv8, ab8, gout, zero_state, 0)
            buf_s = jnp.pad(state_conv[layer], ((0, 0), (SUBLANES - state_conv.shape[2], 0), (0, 0)))
            mo_s, ss = delta_mixer(P, tp, bs, S_PAD, S_PAD, S_PAD, ls, buf_s, conv8, ab8, gout, state_ssm, layer)
            conv_p.append(jnp.stack([P[b * lp + lp - 3:(b + 1) * lp, :CONV_DIM] for b in range(bp)]))
            conv_s.append(P[tp:, :CONV_DIM].reshape(bs, S_PAD, CONV_DIM)[:, ls - 3:ls])
            ssm_p.append(sp)
            ssm_s.append(ss)
            w_out, wl = w_out_a_b, layer
        else:
            mo_p = nsa_prompt(P, cos_p, sin_p, ck_p, cv_p, kv[2], kv[3], wk_pad, wv_pad, mt_p, et_p, eye,
                              bp, lp, Q_BLOCK, 512)
            mo_s = nsa_sample(P, tp, pt_flat, pool_sk, pool_sv, cos_s, sin_s, ck_s, cv_s, kv[2], kv[3], cwk, cwv,
                              kv[4], kv[5], mt_s, e_s, eye)
            w_out, wl = w_out_b_b, lb
        mo = (mo_p, mo_s)
        g_ffn = norm_ffn[layer].reshape(1, d)
        if layer % 2 == 0:
            H = outproj_ffn(H, mo, om, w_out, g_ffn, w_gu_dense_b, w_down_dense_b, wl, layer // 2, tm, D_FF // 2)
        else:
            H = outproj_moe(H, mo, om, w_out, g_ffn, w_router_p, w_gu_exp_b, w_down_exp_b, wl, layer // 2,
                            tm, D_FF_EXPERT // 2)
    Y = final_norm(H, norm_final.reshape(1, d), tm)

    def rows_p(a):
        return a[:tp].reshape(bp, lp, 2, HEAD_DIM)

    def rows_s(a):
        return a[tp:].reshape(bs, S_PAD, 2, HEAD_DIM)[:, :ls]

    wlen = min(WINDOW, lp)
    return (Y[:tp].reshape(bp, lp, d), Y[tp:].reshape(bs, S_PAD, d)[:, :ls],
            jnp.stack(conv_p), jnp.stack(ssm_p),
            rows_p(kv[0]), rows_p(kv[1]), rows_p(kv[2]), rows_p(kv[3]),
            rows_p(kv[4])[:, lp - wlen:], rows_p(kv[5])[:, lp - wlen:],
            mem_k.reshape(depth, bp, MEM_LEN, 4, HEAD_DIM), mem_v.reshape(depth, bp, MEM_LEN, 4, HEAD_DIM),
            jnp.stack(conv_s), jnp.stack(ssm_s),
            rows_s(kv[0]), rows_s(kv[1]), rows_s(kv[2]), rows_s(kv[3]),
            jnp.concatenate([cache_win_k[:, ls:], rows_s(kv[4])], axis=1),
            jnp.concatenate([cache_win_v[:, ls:], rows_s(kv[5])], axis=1))
```

```python
import functools
import math

import jax
import jax.numpy as jnp
from jax import lax
from jax.experimental import pallas as pl
from jax.experimental.pallas import tpu as pltpu

f32 = jnp.float32
bf16 = jnp.bfloat16

EPS = 1e-6
NEG = -1e30
ROPE_THETA = 10000.0
HEAD_DIM = 64
LANES = 128
SUBLANES = 8
VMEM_LIMIT = 48 * 1024 * 1024

D_MODEL = 1024
MEM_LEN = 256
MEM_W = 256
A_HEADS = 6
A_W = 768
CONV_DIM = 2304
A_IN_PAD = 3456
A_MEMQ_BLK = 12
A_BA_BLK = 26
B_HEADS = 12
HPG = 6
B_W = 768
B_IN_PAD = 1152
CMP_BLK = 32
CMP_STRIDE = 16
CMP_HID = 256
SEL_BLK = 64
SEL_TOP = 16
WINDOW = 512
Q_BLOCK = 128
D_FF = 2816
N_EXPERTS = 8
D_FF_EXPERT = 3584
PAGE = 128
S_PAD = 8


def _cparams(sem):
    return pltpu.CompilerParams(dimension_semantics=sem, vmem_limit_bytes=VMEM_LIMIT)


def _nt(a, b):
    return lax.dot_general(a, b, (((1,), (1,)), ((), ())), preferred_element_type=f32)


def _tn(a, b):
    return lax.dot_general(a, b, (((0,), (0,)), ((), ())), preferred_element_type=f32)


def _mm(a, b):
    return jnp.dot(a, b, preferred_element_type=f32)


def _rms(x, g):
    return (x * lax.rsqrt(jnp.mean(x * x, axis=-1, keepdims=True) + EPS)) * g


def _sigmoid(x):
    return 1.0 / (1.0 + jnp.exp(-x))


def _silu(x):
    return x * _sigmoid(x)


def _softplus(x):
    return jnp.maximum(x, 0.0) + jnp.log(1.0 + jnp.exp(-jnp.abs(x)))


def _rope_lanes(x, cos, sin_signed):
    lane = lax.broadcasted_iota(jnp.int32, x.shape, x.ndim - 1)
    lo = (lane % HEAD_DIM) < (HEAD_DIM // 2)
    partner = jnp.where(lo, pltpu.roll(x, LANES - HEAD_DIM // 2, axis=x.ndim - 1),
                        pltpu.roll(x, HEAD_DIM // 2, axis=x.ndim - 1))
    return x * cos + partner * sin_signed


def _norm_matmul_kernel(x_ref, g_ref, w_ref, o_ref, xn_ref):
    @pl.when(pl.program_id(1) == 0)
    def _():
        xn_ref[...] = _rms(x_ref[...], g_ref[...]).astype(bf16)

    o_ref[...] = _mm(xn_ref[...], w_ref[...])


def norm_matmul(x, g, w, layer, tm, tn):
    T, K = x.shape
    N = w.shape[2]
    return pl.pallas_call(
        _norm_matmul_kernel,
        grid=(T // tm, N // tn),
        in_specs=[pl.BlockSpec((tm, K), lambda i, j: (i, 0)),
                  pl.BlockSpec((1, K), lambda i, j: (0, 0)),
                  pl.BlockSpec((None, K, tn), lambda i, j: (layer, 0, j))],
        out_specs=pl.BlockSpec((tm, tn), lambda i, j: (i, j)),
        out_shape=jax.ShapeDtypeStruct((T, N), f32),
        scratch_shapes=[pltpu.VMEM((tm, K), bf16)],
        compiler_params=_cparams(("parallel", "arbitrary")),
        name="norm_matmul",
    )(x, g, w)


def _mem_kv_kernel(x_ref, g_ref, w_ref, k_ref, v_ref):
    y = _mm(_rms(x_ref[...], g_ref[...]).astype(bf16), w_ref[...])
    k_ref[...] = y[:, :MEM_W]
    v_ref[...] = y[:, MEM_W:]


def mem_kv(mem, g, w, tm):
    R, K = mem.shape
    NL = w.shape[0]
    out = jax.ShapeDtypeStruct((NL, R, MEM_W), f32)
    return pl.pallas_call(
        _mem_kv_kernel,
        grid=(NL, R // tm),
        in_specs=[pl.BlockSpec((tm, K), lambda l, i: (i, 0)),
                  pl.BlockSpec((None, 1, K), lambda l, i: (l, 0, 0)),
                  pl.BlockSpec((None, K, 2 * MEM_W), lambda l, i: (l, 0, 0))],
        out_specs=(pl.BlockSpec((None, tm, MEM_W), lambda l, i: (l, i, 0)),
                   pl.BlockSpec((None, tm, MEM_W), lambda l, i: (l, i, 0))),
        out_shape=(out, out),
        compiler_params=_cparams(("parallel", "parallel")),
        name="mem_kv",
    )(mem, g, w)


def _final_norm_kernel(x_ref, g_ref, o_ref):
    o_ref[...] = _rms(x_ref[...], g_ref[...])


def final_norm(x, g, tm):
    T, K = x.shape
    return pl.pallas_call(
        _final_norm_kernel,
        grid=(T // tm,),
        in_specs=[pl.BlockSpec((tm, K), lambda i: (i, 0)), pl.BlockSpec((1, K), lambda i: (0, 0))],
        out_specs=pl.BlockSpec((tm, K), lambda i: (i, 0)),
        out_shape=jax.ShapeDtypeStruct((T, K), f32),
        compiler_params=_cparams(("parallel",)),
        name="final_norm",
    )(x, g)


def _mem_attend(q, k, v, transposed=False):
    tq = q.shape[0]
    nh = MEM_W // HEAD_DIM
    lane = lax.broadcasted_iota(jnp.int32, (tq, MEM_W), 1)
    masks = [(lane >= HEAD_DIM * h) & (lane < HEAD_DIM * (h + 1)) for h in range(nh)]
    qs = jnp.concatenate([jnp.where(m, q, 0.0) for m in masks], axis=0).astype(bf16)
    s = (_mm(qs, k) if transposed else _nt(qs, k)) * (HEAD_DIM ** -0.5)
    p = jnp.exp(s - jnp.max(s, axis=-1, keepdims=True))
    l = jnp.sum(p, axis=-1, keepdims=True)
    pb = p.astype(bf16)
    o = (_nt(pb, v) if transposed else _mm(pb, v)) / l
    out = jnp.zeros((tq, MEM_W), f32)
    for h in range(nh):
        out = jnp.where(masks[h], o[h * tq:(h + 1) * tq], out)
    return out


def _mem_attn_prompt_kernel(q_ref, k_ref, v_ref, o_ref):
    o_ref[...] = _mem_attend(q_ref[...], k_ref[...].astype(bf16), v_ref[...].astype(bf16))


def mem_attn_prompt(P, qblk, mk, mv, layer, nb, L, tq):
    nq = L // tq
    return pl.pallas_call(
        _mem_attn_prompt_kernel,
        grid=(nb, nq),
        in_specs=[pl.BlockSpec((tq, MEM_W), lambda b, i: (b * nq + i, qblk)),
                  pl.BlockSpec((None, MEM_LEN, MEM_W), lambda b, i: (layer, b, 0)),
                  pl.BlockSpec((None, MEM_LEN, MEM_W), lambda b, i: (layer, b, 0))],
        out_specs=pl.BlockSpec((tq, MEM_W), lambda b, i: (b * nq + i, 0)),
        out_shape=jax.ShapeDtypeStruct((nb * L, MEM_W), f32),
        compiler_params=_cparams(("parallel", "parallel")),
        name="mem_attn_prompt",
    )(P, mk, mv)


def _mem_attn_sample_kernel(q_ref, k_ref, v_ref, o_ref, *, bb):
    for b in range(bb):
        rows = slice(b * S_PAD, (b + 1) * S_PAD)
        o_ref[rows, :] = _mem_attend(q_ref[rows, :], k_ref[b].astype(bf16), v_ref[b].astype(bf16), transposed=True)


def mem_attn_sample(P, qblk, row0, ckt, cvt, layer, bb):
    nb = ckt.shape[1]
    blk0 = row0 // (bb * S_PAD)
    return pl.pallas_call(
        functools.partial(_mem_attn_sample_kernel, bb=bb),
        grid=(nb // bb,),
        in_specs=[pl.BlockSpec((bb * S_PAD, MEM_W), lambda i: (blk0 + i, qblk)),
                  pl.BlockSpec((None, bb, MEM_W, MEM_LEN), lambda i: (layer, i, 0, 0)),
                  pl.BlockSpec((None, bb, MEM_W, MEM_LEN), lambda i: (layer, i, 0, 0))],
        out_specs=pl.BlockSpec((bb * S_PAD, MEM_W), lambda i: (i, 0)),
        out_shape=jax.ShapeDtypeStruct((nb * S_PAD, MEM_W), f32),
        compiler_params=_cparams(("parallel",)),
        name="mem_attn_sample",
    )(P, ckt, cvt)


def _delta_kernel(q_ref, k_ref, v_ref, z_ref, ba_ref, buf_ref, cw_ref, ab_ref, go_ref, s0_ref,
                  o_ref, sout_ref, qs_ref, ks_ref, vs_ref, gs_ref, bs_ref, tail_ref, s_ref,
                  *, TL, C, valid):
    t = pl.program_id(1)

    @pl.when(t == 0)
    def _():
        tail_ref[...] = buf_ref[...]
        s_ref[...] = s0_ref[...]

    row8 = lax.broadcasted_iota(jnp.int32, (SUBLANES, A_W), 0)

    def conv(x_ref, c0):
        x = x_ref[...]
        b8 = tail_ref[:, c0:c0 + A_W]
        c8 = cw_ref[:, c0:c0 + A_W]
        acc = x * c8[3:4, :]
        for s in range(1, 4):
            r = pltpu.roll(x, s, axis=0)
            fix = jnp.where(row8 < s, pltpu.roll(b8, s, axis=0), r[0:SUBLANES])
            r = jnp.concatenate([fix, r[SUBLANES:]], axis=0) if TL > SUBLANES else fix
            acc = acc + r * c8[3 - s:4 - s, :]
        tail_ref[:, c0:c0 + A_W] = x[TL - SUBLANES:TL]
        return _silu(acc)

    q = conv(q_ref, 0)
    k = conv(k_ref, A_W)
    v = conv(v_ref, 2 * A_W)
    ba = ba_ref[...]
    ab = ab_ref[...]
    beta_all = _sigmoid(ba)
    g_all = -jnp.exp(ab[0:1, :]) * _softplus(ba + ab[1:2, :])
    if valid < TL:
        live = lax.broadcasted_iota(jnp.int32, (TL, A_W), 0) < valid
        live1 = lax.broadcasted_iota(jnp.int32, (TL, LANES), 0) < valid
        k = jnp.where(live, k, 0.0)
        v = jnp.where(live, v, 0.0)
        beta_all = jnp.where(live1, beta_all, 0.0)
        g_all = jnp.where(live1, g_all, 0.0)
    rowc = lax.broadcasted_iota(jnp.int32, (TL, LANES), 0) & (C - 1)
    gc_all = g_all
    s = 1
    while s < C:
        gc_all = gc_all + jnp.where(rowc >= s, pltpu.roll(gc_all, s, axis=0), 0.0)
        s *= 2
    vs_ref[...] = v
    for h in range(A_HEADS):
        sl = slice(h * LANES, (h + 1) * LANES)
        qh = q[:, sl]
        kh = k[:, sl]
        qs_ref[:, sl] = qh * lax.rsqrt(jnp.sum(qh * qh, axis=-1, keepdims=True) + EPS) * (LANES ** -0.5)
        ks_ref[:, sl] = kh * lax.rsqrt(jnp.sum(kh * kh, axis=-1, keepdims=True) + EPS)
        gs_ref[:, sl] = jnp.broadcast_to(gc_all[:, A_HEADS + h:A_HEADS + h + 1], (TL, LANES))
        bs_ref[:, sl] = jnp.broadcast_to(beta_all[:, h:h + 1], (TL, LANES))

    ii = lax.broadcasted_iota(jnp.int32, (C, C), 0)
    jj = lax.broadcasted_iota(jnp.int32, (C, C), 1)
    tri = ii >= jj
    strict = ii > jj
    eye = jnp.where(ii == jj, 1.0, 0.0)
    gout = go_ref[...]
    nsteps = int(math.log2(C)) - 1

    heads = range(A_HEADS)
    lanes_of = [slice(h * LANES, (h + 1) * LANES) for h in heads]

    cpi = 2 if TL // C >= 2 else 1

    def chunks(c, carry):
        rows_of = [pl.ds(pl.multiple_of((c * cpi + i) * C, C), C) for i in range(cpi)]
        units = [(rows, h) for rows in rows_of for h in heads]
        U = range(len(units))
        qc = [qs_ref[rows, lanes_of[h]] for rows, h in units]
        kc = [ks_ref[rows, lanes_of[h]] for rows, h in units]
        vc = [vs_ref[rows, lanes_of[h]] for rows, h in units]
        gcc = [gs_ref[rows, lanes_of[h]] for rows, h in units]
        bc = [bs_ref[rows, lanes_of[h]] for rows, h in units]
        kb = [kc[u] * bc[u] for u in U]
        kq = [_nt(jnp.concatenate([kb[u], qc[u]], axis=0).astype(bf16), kc[u].astype(bf16)) for u in U]
        decay = []
        for u in U:
            gi = gcc[u][:, 0:C]
            gj = jnp.sum(jnp.where(ii == jj, gi, 0.0), axis=0, keepdims=True)
            decay.append(jnp.where(tri, jnp.exp(jnp.where(tri, gi - gj, 0.0)), 0.0))
        n = [jnp.where(strict, -(kq[u][0:C] * decay[u]), 0.0) for u in U]
        a = [jnp.where(tri, kq[u][C:2 * C] * decay[u], 0.0).astype(bf16) for u in U]
        x = [eye + n[u] for u in U]
        p = n
        for _ in range(nsteps):
            pb = [p[u].astype(bf16) for u in U]
            p = [_mm(pb[u], pb[u]) for u in U]
            x = [x[u] + _mm(x[u].astype(bf16), p[u].astype(bf16)) for u in U]
        eg = [jnp.exp(gcc[u]) for u in U]
        uw = [_mm(x[u].astype(bf16), jnp.concatenate([vc[u] * bc[u], kb[u] * eg[u]], axis=1).astype(bf16))
              for u in U]
        wqin = [jnp.concatenate([uw[u][:, LANES:], qc[u] * eg[u]], axis=0).astype(bf16) for u in U]
        glast = [gcc[u][C - 1:C, :] for u in U]
        kg = [(kc[u] * jnp.exp(glast[u] - gcc[u])).astype(bf16) for u in U]
        for i, rows in enumerate(rows_of):
            us = [i * A_HEADS + h for h in heads]
            S = [s_ref[h] for h in heads]
            wq = [_mm(wqin[u], S[h].astype(bf16)) for h, u in zip(heads, us)]
            vnb = [(uw[u][:, :LANES] - wq[h][0:C]).astype(bf16) for h, u in zip(heads, us)]
            av = [_mm(a[u], vnb[h]) for h, u in zip(heads, us)]
            kv_new = [_tn(kg[u], vnb[h]) for h, u in zip(heads, us)]
            for h, u in zip(heads, us):
                s_ref[h] = S[h] * jnp.exp(glast[u]) + kv_new[h]
                o_ref[rows, lanes_of[h]] = _rms(wq[h][C:2 * C] + av[h], gout) * _silu(z_ref[rows, lanes_of[h]])
        return carry

    if TL // C == cpi:
        chunks(0, 0)
    else:
        lax.fori_loop(0, TL // (C * cpi), chunks, 0)

    @pl.when(t == pl.num_programs(1) - 1)
    def _():
        sout_ref[...] = s_ref[...]


def delta_mixer(P, row0, nb, L, TL, C, valid, buf8, conv8, ab8, gout, s0, layer):
    nt = L // TL
    rb0 = row0 // TL
    tile = lambda c: pl.BlockSpec((TL, A_W), lambda b, t: (rb0 + b * nt + t, c))
    state = pl.BlockSpec((None, A_HEADS, LANES, LANES), lambda b, t: (b, 0, 0, 0))
    state_in = pl.BlockSpec((None, None, A_HEADS, LANES, LANES), lambda b, t: (layer, b, 0, 0, 0))
    return pl.pallas_call(
        functools.partial(_delta_kernel, TL=TL, C=C, valid=valid),
        grid=(nb, nt),
        in_specs=[tile(0), tile(1), tile(2), tile(3),
                  pl.BlockSpec((TL, LANES), lambda b, t: (rb0 + b * nt + t, A_BA_BLK)),
                  pl.BlockSpec((None, SUBLANES, CONV_DIM), lambda b, t: (b, 0, 0)),
                  pl.BlockSpec((SUBLANES, CONV_DIM), lambda b, t: (0, 0)),
                  pl.BlockSpec((SUBLANES, LANES), lambda b, t: (0, 0)),
                  pl.BlockSpec((1, LANES), lambda b, t: (0, 0)),
                  state_in],
        out_specs=(pl.BlockSpec((TL, A_W), lambda b, t: (b * nt + t, 0)), state),
        out_shape=(jax.ShapeDtypeStruct((nb * L, A_W), f32),
                   jax.ShapeDtypeStruct((nb, A_HEADS, LANES, LANES), f32)),
        scratch_shapes=[pltpu.VMEM((TL, A_W), f32)] * 5
        + [pltpu.VMEM((SUBLANES, CONV_DIM), f32), pltpu.VMEM((A_HEADS, LANES, LANES), f32)],
        compiler_params=_cparams(("parallel", "arbitrary")),
        name="delta_mixer",
    )(P, P, P, P, P, buf8, conv8, ab8, gout, s0)


def _attn_residual(refs, ntp):
    h_ref, mop_ref, mos_ref, omp_ref, oms_ref, wo1_ref, wo2_ref = refs
    is_p = pl.program_id(0) < ntp
    mo = jnp.where(is_p, mop_ref[...], mos_ref[...])
    om = jnp.where(is_p, omp_ref[...], oms_ref[...])
    return h_ref[...] + _mm(mo.astype(bf16), wo1_ref[...]) + _mm(om.astype(bf16), wo2_ref[...])


def _ffn_kernel(*refs, ntp):
    g_ref, wg_ref, wu_ref, wd_ref, o_ref, xn_ref, acc_ref = refs[7:]
    k = pl.program_id(1)

    @pl.when(k == 0)
    def _():
        h1 = _attn_residual(refs[:7], ntp)
        acc_ref[...] = h1
        xn_ref[...] = _rms(h1, g_ref[...]).astype(bf16)

    xn = xn_ref[...]
    a = _silu(_mm(xn, wg_ref[...])) * _mm(xn, wu_ref[...])
    acc_ref[...] += _mm(a.astype(bf16), wd_ref[...])

    @pl.when(k == pl.num_programs(1) - 1)
    def _():
        o_ref[...] = acc_ref[...]


def _outproj_specs(tm, K, mw, ow, ntp, layer):
    prow = lambda i, *_: (jnp.minimum(i, ntp - 1), 0)
    srow = lambda i, *_: (jnp.maximum(i - ntp, 0), 0)
    return [pl.BlockSpec((tm, K), lambda i, *_: (i, 0)),
            pl.BlockSpec((tm, mw), prow), pl.BlockSpec((tm, mw), srow),
            pl.BlockSpec((tm, ow), prow), pl.BlockSpec((tm, ow), srow),
            pl.BlockSpec((None, mw, K), lambda i, *_: (layer, 0, 0)),
            pl.BlockSpec((None, ow, K), lambda i, *_: (layer, mw // ow, 0)),
            pl.BlockSpec((1, K), lambda i, *_: (0, 0))]


def outproj_ffn(H, mo, om, w_out, g, w_gu, w_down, layer, ffn_layer, tm, tf):
    T, K = H.shape
    FF = w_down.shape[1]
    nk = FF // tf
    mw, ow = mo[0].shape[1], om[0].shape[1]
    ntp = mo[0].shape[0] // tm
    return pl.pallas_call(
        functools.partial(_ffn_kernel, ntp=ntp),
        grid=(T // tm, nk),
        in_specs=_outproj_specs(tm, K, mw, ow, ntp, layer)
        + [pl.BlockSpec((None, K, tf), lambda i, k: (ffn_layer, 0, k)),
           pl.BlockSpec((None, K, tf), lambda i, k: (ffn_layer, 0, nk + k)),
           pl.BlockSpec((None, tf, K), lambda i, k: (ffn_layer, k, 0))],
        out_specs=pl.BlockSpec((tm, K), lambda i, k: (i, 0)),
        out_shape=jax.ShapeDtypeStruct((T, K), f32),
        scratch_shapes=[pltpu.VMEM((tm, K), bf16), pltpu.VMEM((tm, K), f32)],
        compiler_params=_cparams(("parallel", "arbitrary")),
        name="outproj_ffn",
    )(H, mo[0], mo[1], om[0], om[1], w_out, w_out, g, w_gu, w_gu, w_down)


def _router_kernel(*refs, ntp):
    g_ref, wr_ref, h1_ref, xn_ref, r_ref, cnt_ref = refs[7:]
    i = pl.program_id(0)
    tm = h1_ref.shape[0]

    @pl.when(i == 0)
    def _():
        cnt_ref[...] = jnp.zeros_like(cnt_ref)

    h1 = _attn_residual(refs[:7], ntp)
    h1_ref[...] = h1
    xn = _rms(h1, g_ref[...])
    xn_ref[...] = xn
    lane = lax.broadcasted_iota(jnp.int32, (tm, LANES), 1)
    xh = xn.astype(bf16)
    xl = (xn - xh.astype(f32)).astype(bf16)
    wr = wr_ref[...]
    wh = wr.astype(bf16)
    wl = (wr - wh.astype(f32)).astype(bf16)
    logits = _mm(xh, wh) + _mm(xh, wl) + _mm(xl, wh)
    logits = jnp.where(lane < N_EXPERTS, logits, -jnp.inf)
    m1 = jnp.max(logits, axis=-1, keepdims=True)
    i1 = jnp.min(jnp.where(logits == m1, lane, LANES), axis=-1, keepdims=True)
    rest = jnp.where(lane == i1, -jnp.inf, logits)
    m2 = jnp.max(rest, axis=-1, keepdims=True)
    i2 = jnp.min(jnp.where(rest == m2, lane, LANES), axis=-1, keepdims=True)
    e2 = jnp.exp(m2 - m1)
    w1 = 1.0 / (1.0 + e2)
    w2 = e2 / (1.0 + e2)
    hot = jnp.where((lane == i1) | (lane == i2), 1.0, 0.0)
    ri = lax.broadcasted_iota(jnp.int32, (tm, tm), 0)
    ci = lax.broadcasted_iota(jnp.int32, (tm, tm), 1)
    before = jnp.where(ri > ci, 1.0, 0.0).astype(bf16)
    cum = _mm(before, hot.astype(bf16)) + cnt_ref[...]
    r1 = jnp.sum(jnp.where(lane == i1, cum, 0.0), axis=-1, keepdims=True)
    r2 = jnp.sum(jnp.where(lane == i2, cum, 0.0), axis=-1, keepdims=True)
    cnt_ref[...] += jnp.sum(hot, axis=0, keepdims=True)
    cols = (i1.astype(f32), i2.astype(f32), w1, w2, r1, r2)
    r = jnp.zeros((tm, LANES), f32)
    for c, val in enumerate(cols):
        r = jnp.where(lane == c, val, r)
    r_ref[...] = r


def outproj_router(H, mo, om, w_out, g, w_router, layer, moe_layer, tm):
    T, K = H.shape
    mw, ow = mo[0].shape[1], om[0].shape[1]
    ntp = mo[0].shape[0] // tm
    return pl.pallas_call(
        functools.partial(_router_kernel, ntp=ntp),
        grid=(T // tm,),
        in_specs=_outproj_specs(tm, K, mw, ow, ntp, layer)
        + [pl.BlockSpec((None, K, LANES), lambda i: (moe_layer, 0, 0))],
        out_specs=(pl.BlockSpec((tm, K), lambda i: (i, 0)), pl.BlockSpec((tm, K), lambda i: (i, 0)),
                   pl.BlockSpec((tm, LANES), lambda i: (i, 0))),
        out_shape=(jax.ShapeDtypeStruct((T, K), f32), jax.ShapeDtypeStruct((T, K), f32),
                   jax.ShapeDtypeStruct((T, LANES), f32)),
        scratch_shapes=[pltpu.VMEM((1, LANES), f32)],
        compiler_params=_cparams(("arbitrary",)),
        name="outproj_router",
    )(H, mo[0], mo[1], om[0], om[1], w_out, w_out, g, w_router)


def _moe_group_kernel(te_ref, nu_ref, x_ref, sw_ref, wg_ref, wu_ref, wd_ref, o_ref, acc_ref):
    del te_ref
    i = pl.program_id(0)
    k = pl.program_id(1)

    @pl.when(k == 0)
    def _():
        acc_ref[...] = jnp.zeros_like(acc_ref)

    @pl.when(i < nu_ref[0])
    def _():
        x = x_ref[...].astype(bf16)
        a = _silu(_mm(x, wg_ref[...].astype(bf16))) * _mm(x, wu_ref[...].astype(bf16))
        acc_ref[...] += _mm(a.astype(bf16), wd_ref[...].astype(bf16))

    @pl.when(k == pl.num_programs(1) - 1)
    def _():
        o_ref[...] = acc_ref[...] * sw_ref[...]


def moe_grouped(xg, sw, tile_e, n_used, w_gu, w_down, moe_layer, tm, tf):
    NP, K = xg.shape
    FF = w_down.shape[2]
    nk = FF // tf

    def kk(i, k, nu):
        return jnp.where(i < nu[0], k, nk - 1)

    grid_spec = pltpu.PrefetchScalarGridSpec(
        num_scalar_prefetch=2, grid=(NP // tm, nk),
        in_specs=[pl.BlockSpec((tm, K), lambda i, k, te, nu: (i, 0)),
                  pl.BlockSpec((tm, 1), lambda i, k, te, nu: (i, 0)),
                  pl.BlockSpec((None, None, K, tf), lambda i, k, te, nu: (moe_layer, te[i], 0, kk(i, k, nu))),
                  pl.BlockSpec((None, None, K, tf), lambda i, k, te, nu: (moe_layer, te[i], 0, nk + kk(i, k, nu))),
                  pl.BlockSpec((None, None, tf, K), lambda i, k, te, nu: (moe_layer, te[i], kk(i, k, nu), 0))],
        out_specs=pl.BlockSpec((tm, K), lambda i, k, te, nu: (i, 0)),
        scratch_shapes=[pltpu.VMEM((tm, K), f32)])
    return pl.pallas_call(
        _moe_group_kernel, grid_spec=grid_spec,
        out_shape=jax.ShapeDtypeStruct((NP, K), f32),
        compiler_params=_cparams(("parallel", "arbitrary")),
        name="moe_grouped",
    )(tile_e, n_used, xg, sw, w_gu, w_gu, w_down)


def _dispatch(r, tm):
    T = r.shape[0]
    A = 2 * T
    ntiles = A // tm + N_EXPERTS
    NP = ntiles * tm
    e_flat = r[:, 0:2].astype(jnp.int32).reshape(A)
    w_flat = r[:, 2:4].reshape(A)
    rank = r[:, 4:6].astype(jnp.int32)
    counts = jnp.sum((e_flat[:, None] == jnp.arange(N_EXPERTS)[None, :]).astype(jnp.int32), axis=0)
    pcounts = (counts + tm - 1) // tm * tm
    pends = jnp.cumsum(pcounts)
    pstarts = pends - pcounts
    n_used = pends[-1] // tm
    d = jnp.arange(NP - A, dtype=jnp.int32)
    e_pad = jnp.sum((d[:, None] >= jnp.cumsum(pcounts - counts)[None, :]).astype(jnp.int32), axis=1)
    keys = jnp.concatenate([e_flat * NP + jnp.arange(A, dtype=jnp.int32), e_pad * NP + A + d])
    toks = jnp.concatenate([jnp.arange(A, dtype=jnp.int32) // 2, jnp.zeros((NP - A,), jnp.int32)])
    wts = jnp.concatenate([w_flat, jnp.zeros((NP - A,), f32)])
    _, src, sw = lax.sort((keys, toks, wts), num_keys=1)
    sw = sw.reshape(NP, 1)
    tiles = jnp.arange(ntiles, dtype=jnp.int32)
    tile_e = jnp.minimum(jnp.searchsorted(pends, tiles * tm, side='right'), N_EXPERTS - 1).astype(jnp.int32)
    tile_e = jnp.where(tiles < n_used, tile_e, tile_e[jnp.maximum(n_used - 1, 0)])
    picks = r[:, 0:2].astype(jnp.int32)
    pick_start = jnp.sum(jnp.where(picks[:, :, None] == jnp.arange(N_EXPERTS)[None, None, :],
                                   pstarts[None, None, :], 0), axis=2)
    slots = pick_start + rank
    return src, sw, tile_e, n_used.reshape(1).astype(jnp.int32), slots


def outproj_moe(H, mo, om, w_out, g, w_router, w_gu, w_down, layer, moe_layer, tm, tf):
    h1, xn, r = outproj_router(H, mo, om, w_out, g, w_router, layer, moe_layer, tm)
    src, sw, tile_e, n_used, slots = _dispatch(r, tm)
    yg = moe_grouped(jnp.take(xn, src, axis=0), sw, tile_e, n_used, w_gu, w_down, moe_layer, tm, tf)
    return h1 + jnp.take(yg, slots[:, 0], axis=0) + jnp.take(yg, slots[:, 1], axis=0)


def _shared_kv_kernel(x_ref, g_ref, w_ref, cos_ref, sin_ref, zk_ref, zv_ref,
                      ck_ref, cv_ref, sk_ref, sv_ref, wk_ref, wv_ref, wkp_ref, wvp_ref):
    del zk_ref, zv_ref
    y = _mm(_rms(x_ref[...], g_ref[...]).astype(bf16), w_ref[...])
    cos = cos_ref[...]
    sin = sin_ref[...]
    ck_ref[...] = y[:, 0 * LANES:1 * LANES]
    cv_ref[...] = y[:, 1 * LANES:2 * LANES]
    sk_ref[...] = _rope_lanes(y[:, 2 * LANES:3 * LANES], cos, sin)
    sv_ref[...] = y[:, 3 * LANES:4 * LANES]
    wk = _rope_lanes(y[:, 4 * LANES:5 * LANES], cos, sin)
    wk_ref[...] = wk
    wkp_ref[...] = wk
    wv_ref[...] = y[:, 5 * LANES:6 * LANES]
    wvp_ref[...] = y[:, 5 * LANES:6 * LANES]


def shared_kv(H, g, w, cos, sin, nb, L, tm):
    T, K = H.shape
    assert tm == WINDOW and L % tm == 0
    per = L // tm
    ntp = nb * per
    nblk = nb * (per + 1) + (T // tm - ntp)
    row = pl.BlockSpec((tm, LANES), lambda i: (i, 0))
    table = pl.BlockSpec((tm, LANES), lambda i: (jnp.where(i < ntp, i % per, per), 0))
    padded = pl.BlockSpec((tm, LANES), lambda i: (jnp.where(i < ntp, i + i // per + 1, nb + i), 0))
    out = jax.ShapeDtypeStruct((T, LANES), f32)
    outp = jax.ShapeDtypeStruct((nblk * tm, LANES), f32)
    zeros = jnp.zeros((nblk * tm, LANES), f32)
    anyspec = pl.BlockSpec(memory_space=pl.ANY)
    return pl.pallas_call(
        _shared_kv_kernel,
        grid=(T // tm,),
        in_specs=[pl.BlockSpec((tm, K), lambda i: (i, 0)), pl.BlockSpec((1, K), lambda i: (0, 0)),
                  pl.BlockSpec((K, 6 * LANES), lambda i: (0, 0)), table, table, anyspec, anyspec],
        out_specs=(row,) * 6 + (padded, padded),
        out_shape=(out,) * 6 + (outp, outp),
        input_output_aliases={5: 6, 6: 7},
        compiler_params=_cparams(("parallel",)),
        name="shared_kv",
    )(H, g, w, cos, sin, zeros, zeros)


def _page_strips_kernel(*refs, npages):
    refs = refs[1:]
    ok_ref, ov_ref, scr_ref = refs[2 * npages:]
    per = PAGE // CMP_STRIDE
    for j in range(npages):
        for src, dst in ((refs[j], ok_ref), (refs[npages + j], ov_ref)):
            scr_ref[...] = src[...].T
            for l in range(CMP_STRIDE):
                dst[j * per:(j + 1) * per, l * LANES:(l + 1) * LANES] = scr_ref[pl.ds(l, per, stride=CMP_STRIDE), :]


def page_strips(page_table, pool_k, pool_v, nb):
    npages = page_table.shape[0] // nb
    n = npages * PAGE // CMP_STRIDE
    page = lambda j: pl.BlockSpec((None, LANES, PAGE), lambda b, pt: (pt[b * npages + j], 0, 0))
    out_spec = pl.BlockSpec((None, n, CMP_STRIDE * LANES), lambda b, pt: (b, 0, 0))
    out = jax.ShapeDtypeStruct((nb, n, CMP_STRIDE * LANES), f32)
    grid_spec = pltpu.PrefetchScalarGridSpec(
        num_scalar_prefetch=1, grid=(nb,),
        in_specs=[page(j) for j in range(npages)] * 2, out_specs=(out_spec, out_spec),
        scratch_shapes=[pltpu.VMEM((PAGE, LANES), f32)])
    return pl.pallas_call(
        functools.partial(_page_strips_kernel, npages=npages), grid_spec=grid_spec, out_shape=(out, out),
        compiler_params=_cparams(("parallel",)), name="page_strips",
    )(page_table, *([pool_k] * npages), *([pool_v] * npages))


def _compress_kernel(k_ref, v_ref, pek_ref, pev_ref, w1k_ref, w1v_ref, w2k_ref, w2v_ref, cos_ref, sin_ref,
                     ck_ref, cv_ref):
    k_strips = k_ref[...]
    v_strips = v_ref[...]
    n = k_strips.shape[0]
    half = 2 * CMP_HID
    row = lax.broadcasted_iota(jnp.int32, (n, LANES), 0)

    def tokens(strips, pe_ref, w1_ref, w2_ref):
        top = _mm((strips + pe_ref[0:1, :]).astype(bf16), w1_ref[:, 0:half])
        bot = _mm((strips + pe_ref[1:2, :]).astype(bf16), w1_ref[:, half:2 * half])
        hid = top + pltpu.roll(bot, n - 1, axis=0)
        out = _mm(_silu(hid).astype(bf16), w2_ref[...])
        return jnp.where(row < n - 1, out, 0.0)

    ck_ref[...] = _rope_lanes(tokens(k_strips, pek_ref, w1k_ref, w2k_ref), cos_ref[...], sin_ref[...])
    cv_ref[...] = tokens(v_strips, pev_ref, w1v_ref, w2v_ref)


def compress(k_src, v_src, consts):
    nb, n, flat = k_src.shape
    pek, pev, w1k, w1v, w2k, w2v, cos, sin = consts
    const2 = lambda shape: pl.BlockSpec(shape, lambda b: (0, 0))
    seq = pl.BlockSpec((None, n, flat), lambda b: (b, 0, 0))
    out_spec = pl.BlockSpec((None, n, LANES), lambda b: (b, 0, 0))
    out = jax.ShapeDtypeStruct((nb, n, LANES), f32)
    return pl.pallas_call(
        _compress_kernel, grid=(nb,),
        in_specs=[seq, seq, const2((2, flat)), const2((2, flat)), const2((flat, 4 * CMP_HID)),
                  const2((flat, 4 * CMP_HID)), const2((2 * CMP_HID, LANES)), const2((2 * CMP_HID, LANES)),
                  const2((n, LANES)), const2((n, LANES))],
        out_specs=(out_spec, out_spec), out_shape=(out, out),
        compiler_params=_cparams(("parallel",)), name="compress",
    )(k_src, v_src, pek, pev, w1k, w1v, w2k, w2v, cos, sin)


def _nsa_queries(p, cos, sin):
    tq = p.shape[0]
    lane = lax.broadcasted_iota(jnp.int32, (tq, LANES), 1)
    rows = []
    for head in range(B_HEADS):
        grp = head // HPG
        blk = MEM_W // LANES + head // 2
        x = _rope_lanes(p[:, blk * LANES:(blk + 1) * LANES], cos, sin) * (HEAD_DIM ** -0.5)
        if head % 2 != grp:
            x = pltpu.roll(x, HEAD_DIM, axis=1)
        keep = (lane >= grp * HEAD_DIM) & (lane < (grp + 1) * HEAD_DIM)
        rows.append(jnp.where(keep, x, 0.0))
    return jnp.concatenate(rows, axis=0)


def _masked_softmax_parts(s, mask):
    sm = jnp.where(mask, s, NEG)
    m = jnp.max(sm, axis=-1, keepdims=True)
    p = jnp.where(mask, jnp.exp(sm - m), 0.0)
    return p, jnp.sum(p, axis=-1, keepdims=True)


def _safe_div(o, l):
    return jnp.where(l > 0.0, o / jnp.where(l > 0.0, l, 1.0), 0.0)


def _split3(x):
    hi = x.astype(bf16)
    r1 = x - hi.astype(f32)
    mid = r1.astype(bf16)
    lo = (r1 - mid.astype(f32)).astype(bf16)
    return hi, mid, lo


def _select_bias(pcn, qpos_row, past_blocks, mt_ref, eye_ref, nsel):
    tq = qpos_row.shape[1]
    nr = -(-nsel // SUBLANES) * SUBLANES
    mt = mt_ref[0:nr, :]
    blk = lax.broadcasted_iota(jnp.int32, (nr, tq), 0)
    lane = lax.broadcasted_iota(jnp.int32, (tq, LANES), 1)
    cur = qpos_row // SEL_BLK
    forced = (blk == 0) | (blk == cur) | (blk == cur - 1)
    valid = blk * SEL_BLK <= qpos_row
    stacks, picks = [], []
    for grp in range(2):
        psum = pcn[grp * HPG * tq:(grp * HPG + 1) * tq]
        for hh in range(1, HPG):
            psum = psum + pcn[(grp * HPG + hh) * tq:(grp * HPG + hh + 1) * tq]
        hi, mid, lo = _split3(psum)
        imp = _nt(mt, hi) + _nt(mt, mid) + _nt(mt, lo)
        imp = jnp.where(forced, 1e9, jnp.where(valid, imp, -1e9))
        rank = jnp.zeros((nr, tq), f32)
        for j in range(nsel):
            rj = imp[j:j + 1, :]
            rank = rank + jnp.where(rj > imp, 1.0, jnp.where(rj == imp, jnp.where(blk > j, 1.0, 0.0), 0.0))
        sel_t = jnp.where((rank < SEL_TOP) & (blk < nsel), 1.0, 0.0).astype(bf16)
        pick = _tn(sel_t, eye_ref[0:nr, :])
        picks.append(pick)
        stacks += [jnp.where((pick > 0.5) & (lane < past_blocks), 0.0, NEG)] * HPG
    return jnp.concatenate(stacks, axis=0), picks


def _nsa_combine(p, o_c, o_s, o_w):
    tq = p.shape[0]
    gl = _sigmoid(p[:, (MEM_W + B_W):(MEM_W + B_W) + LANES])

    def gate(j):
        return jnp.concatenate([gl[:, 3 * h + j:3 * h + j + 1] for h in range(B_HEADS)], axis=0)

    comb = gate(0) * o_c + gate(1) * o_s + gate(2) * o_w
    lane = lax.broadcasted_iota(jnp.int32, (tq, LANES), 1)
    cols = []
    for c in range(B_HEADS // 2):
        lo = comb[2 * c * tq:(2 * c + 1) * tq]
        hi = comb[(2 * c + 1) * tq:(2 * c + 2) * tq]
        if (2 * c) // HPG == 1:
            lo = pltpu.roll(lo, HEAD_DIM, axis=1)
        if (2 * c + 1) // HPG == 0:
            hi = pltpu.roll(hi, HEAD_DIM, axis=1)
        cols.append(jnp.where(lane < HEAD_DIM, lo, hi))
    return jnp.concatenate(cols, axis=1)


def _nsa_prompt_kernel(p_ref, cos_ref, sin_ref, ck_ref, cv_ref, sk_ref, sv_ref, wk_ref, wv_ref, mt_ref, et_ref,
                       eye_ref, o_ref, *, tq, kc, nsel):
    qb = pl.program_id(1)
    s0 = qb * tq
    R = B_HEADS * tq
    p = p_ref[...]
    qf = _nsa_queries(p, cos_ref[...], sin_ref[...])
    qb16 = qf.astype(bf16)
    ii = lax.broadcasted_iota(jnp.int32, (tq, tq), 0)
    jj = lax.broadcasted_iota(jnp.int32, (tq, tq), 1)
    qposR = s0 + jnp.concatenate([lax.broadcasted_iota(jnp.int32, (tq, 1), 0)] * B_HEADS, axis=0)
    lane = lax.broadcasted_iota(jnp.int32, (R, LANES), 1)
    cmask = ((lane * CMP_STRIDE + (CMP_BLK - 1)) <= qposR) & (lane < ck_ref.shape[0] - 1)
    pc, lc = _masked_softmax_parts(_nt(qb16, ck_ref[...].astype(bf16)), cmask)
    pcn = _safe_div(pc, lc)
    o_c = _mm(pcn.astype(bf16), cv_ref[...].astype(bf16))
    qpos_row = s0 + lax.broadcasted_iota(jnp.int32, (1, tq), 1)
    bias, _ = _select_bias(pcn, qpos_row, s0 // SEL_BLK, mt_ref, eye_ref, nsel)
    qsel = jnp.concatenate([qb16, bias.astype(bf16)], axis=1)
    sd = _nt(qb16, sk_ref[pl.ds(s0, tq), :].astype(bf16)).reshape(B_HEADS, tq, tq)
    sd = jnp.where((ii >= jj)[None], sd, NEG).reshape(R, tq)
    m0 = jnp.max(sd, axis=-1, keepdims=True)
    pd = jnp.exp(sd - m0)
    init = (m0, jnp.sum(pd, axis=-1, keepdims=True), _mm(pd.astype(bf16), sv_ref[pl.ds(s0, tq), :].astype(bf16)))

    def body(c, carry):
        m_run, l_run, acc = carry
        k0 = pl.multiple_of(c * kc, kc)
        ka = jnp.concatenate([sk_ref[pl.ds(k0, kc), :].astype(bf16), et_ref[pl.ds(k0, kc), :]], axis=1)
        s = _nt(qsel, ka)
        m_new = jnp.maximum(m_run, jnp.max(s, axis=-1, keepdims=True))
        alpha = jnp.exp(m_run - m_new)
        pe = jnp.exp(s - m_new)
        l_new = alpha * l_run + jnp.sum(pe, axis=-1, keepdims=True)
        acc_new = alpha * acc + _mm(pe.astype(bf16), sv_ref[pl.ds(k0, kc), :].astype(bf16))
        return m_new, l_new, acc_new

    _, l_run, acc = lax.fori_loop(0, (s0 + kc - 1) // kc, body, init)
    o_s = acc / l_run
    wlen = WINDOW + tq
    wrow = lax.broadcasted_iota(jnp.int32, (wlen, LANES), 0)
    wlane = lax.broadcasted_iota(jnp.int32, (wlen, LANES), 1)
    is_pad = jnp.where((wrow + (s0 - WINDOW) < 0) & (wlane == 0), 1.0, 0.0).astype(bf16)
    kw = jnp.concatenate([wk_ref[pl.ds(s0, wlen), :].astype(bf16), is_pad], axis=1)
    qwin = jnp.concatenate([qb16, jnp.where(lane == 0, NEG, 0.0).astype(bf16)], axis=1)
    sw = _nt(qwin, kw).reshape(B_HEADS, tq, wlen)
    far = jnp.where((jj > ii)[None], sw[:, :, 0:tq], NEG)
    near = jnp.where((jj <= ii)[None], sw[:, :, wlen - tq:wlen], NEG)
    sw = jnp.concatenate([far, sw[:, :, tq:wlen - tq], near], axis=2).reshape(R, wlen)
    pw = jnp.exp(sw - jnp.max(sw, axis=-1, keepdims=True))
    o_w = _mm(pw.astype(bf16), wv_ref[pl.ds(s0, wlen), :].astype(bf16)) / jnp.sum(pw, axis=-1, keepdims=True)
    o_ref[...] = _nsa_combine(p, o_c, o_s, o_w)


def nsa_prompt(P, cos, sin, ck, cv, sk, sv, wk_pad, wv_pad, mt, et, eye, nb, L, tq, kc):
    nq = L // tq
    nsel = L // SEL_BLK
    seq = pl.BlockSpec((L, LANES), lambda b, i: (b, 0))
    win = pl.BlockSpec((WINDOW + L, LANES), lambda b, i: (b, 0))
    cmp_spec = pl.BlockSpec((None, LANES, LANES), lambda b, i: (b, 0, 0))
    const = lambda shape: pl.BlockSpec(shape, lambda b, i: (0, 0))
    return pl.pallas_call(
        functools.partial(_nsa_prompt_kernel, tq=tq, kc=kc, nsel=nsel),
        grid=(nb, nq),
        in_specs=[pl.BlockSpec((tq, B_IN_PAD), lambda b, i: (b * nq + i, 0)),
                  pl.BlockSpec((tq, LANES), lambda b, i: (i, 0)),
                  pl.BlockSpec((tq, LANES), lambda b, i: (i, 0)),
                  cmp_spec, cmp_spec, seq, seq, win, win,
                  const((LANES, LANES)), const((L, LANES)), const((LANES, LANES))],
        out_specs=pl.BlockSpec((tq, B_W), lambda b, i: (b * nq + i, 0)),
        out_shape=jax.ShapeDtypeStruct((nb * L, B_W), f32),
        compiler_params=_cparams(("parallel", "parallel")),
        name="nsa_prompt",
    )(P, cos, sin, ck, cv, sk, sv, wk_pad, wv_pad, mt, et, eye)


def _nsa_sample_kernel(*refs, npages, past):
    refs = refs[1:]
    kpages = refs[0:npages]
    vpages = refs[npages:2 * npages]
    (p_ref, cos_ref, sin_ref, ck_ref, cv_ref, nsk_ref, nsv_ref, cwk_ref, cwv_ref, nwk_ref, nwv_ref,
     mt_ref, et_ref, eye_ref, o_ref) = refs[2 * npages:]
    tq = S_PAD
    R = B_HEADS * tq
    p = p_ref[...]
    qb16 = _nsa_queries(p, cos_ref[...], sin_ref[...]).astype(bf16)
    tR = jnp.concatenate([lax.broadcasted_iota(jnp.int32, (tq, 1), 0)] * B_HEADS, axis=0)
    qposR = past + tR
    lane = lax.broadcasted_iota(jnp.int32, (R, LANES), 1)
    cmask = ((lane * CMP_STRIDE + (CMP_BLK - 1)) <= qposR) & (lane < ck_ref.shape[0] - 1)
    pc, lc = _masked_softmax_parts(_nt(qb16, ck_ref[...].astype(bf16)), cmask)
    pcn = _safe_div(pc, lc)
    o_c = _mm(pcn.astype(bf16), cv_ref[...].astype(bf16))
    qpos_row = past + lax.broadcasted_iota(jnp.int32, (1, tq), 1)
    new_blk = past // SEL_BLK
    bias, picks = _select_bias(pcn, qpos_row, new_blk, mt_ref, eye_ref, new_blk + 1)
    new_ok = lax.broadcasted_iota(jnp.int32, (R, tq), 1) <= tR
    ka = jnp.concatenate([jnp.concatenate([r[...] for r in kpages], axis=1).astype(bf16), et_ref[...]], axis=0)
    s_old = _mm(jnp.concatenate([qb16, bias.astype(bf16)], axis=1), ka)
    new_kept = jnp.concatenate([picks[h // HPG][:, new_blk:new_blk + 1] for h in range(B_HEADS)], axis=0) > 0.5
    s_new = jnp.where(new_ok, jnp.where(new_kept, _nt(qb16, nsk_ref[...].astype(bf16)), NEG), NEG)
    m = jnp.maximum(jnp.max(s_old, axis=-1, keepdims=True), jnp.max(s_new, axis=-1, keepdims=True))
    p_old = jnp.exp(s_old - m)
    p_new = jnp.exp(s_new - m)
    l = jnp.sum(p_old, axis=-1, keepdims=True) + jnp.sum(p_new, axis=-1, keepdims=True)
    svb = jnp.concatenate([r[...] for r in vpages], axis=1).astype(bf16)
    o_s = (_nt(p_old.astype(bf16), svb) + _mm(p_new.astype(bf16), nsv_ref[...].astype(bf16))) / l
    wb = cwk_ref.shape[1]
    wdist = qposR - (past - wb + lax.broadcasted_iota(jnp.int32, (R, wb), 1))
    w_old = jnp.where((wdist >= 0) & (wdist < WINDOW), _mm(qb16, cwk_ref[...].astype(bf16)), NEG)
    w_new = jnp.where(new_ok, _nt(qb16, nwk_ref[...].astype(bf16)), NEG)
    mw = jnp.maximum(jnp.max(w_old, axis=-1, keepdims=True), jnp.max(w_new, axis=-1, keepdims=True))
    pw_old = jnp.exp(w_old - mw)
    pw_new = jnp.exp(w_new - mw)
    lw = jnp.sum(pw_old, axis=-1, keepdims=True) + jnp.sum(pw_new, axis=-1, keepdims=True)
    o_w = (_nt(pw_old.astype(bf16), cwv_ref[...].astype(bf16)) + _mm(pw_new.astype(bf16), nwv_ref[...].astype(bf16))) / lw
    o_ref[...] = _nsa_combine(p, o_c, o_s, o_w)


def nsa_sample(P, row0, page_table, pool_k, pool_v, cos, sin, ck, cv, nsk, nsv, cwk, cwv, nwk, nwv, mt, et, eye):
    nb = cwk.shape[0]
    npages = page_table.shape[0] // nb
    past = npages * PAGE
    rb0 = row0 // S_PAD
    page = lambda j: pl.BlockSpec((None, LANES, PAGE), lambda b, pt: (pt[b * npages + j], 0, 0))
    new_rows = pl.BlockSpec((S_PAD, LANES), lambda b, pt: (rb0 + b, 0))
    per_seq = lambda n: pl.BlockSpec((None, LANES, n), lambda b, pt: (b, 0, 0))
    in_specs = ([page(j) for j in range(npages)] + [page(j) for j in range(npages)]
                + [pl.BlockSpec((S_PAD, B_IN_PAD), lambda b, pt: (rb0 + b, 0)),
                   pl.BlockSpec((S_PAD, LANES), lambda b, pt: (0, 0)),
                   pl.BlockSpec((S_PAD, LANES), lambda b, pt: (0, 0)),
                   per_seq(LANES), per_seq(LANES), new_rows, new_rows,
                   per_seq(cwk.shape[2]), per_seq(cwk.shape[2]), new_rows, new_rows,
                   pl.BlockSpec((LANES, LANES), lambda b, pt: (0, 0)),
                   pl.BlockSpec((LANES, past), lambda b, pt: (0, 0)),
                   pl.BlockSpec((LANES, LANES), lambda b, pt: (0, 0))])
    grid_spec = pltpu.PrefetchScalarGridSpec(
        num_scalar_prefetch=1, grid=(nb,), in_specs=in_specs,
        out_specs=pl.BlockSpec((S_PAD, B_W), lambda b, pt: (b, 0)))
    return pl.pallas_call(
        functools.partial(_nsa_sample_kernel, npages=npages, past=past),
        grid_spec=grid_spec,
        out_shape=jax.ShapeDtypeStruct((nb * S_PAD, B_W), f32),
        compiler_params=_cparams(("parallel",)),
        name="nsa_sample",
    )(page_table, *([pool_k] * npages), *([pool_v] * npages), P, cos, sin, ck, cv, nsk, nsv, cwk, cwv, nwk, nwv,
      mt, et, eye)


def _rope_tables(pos):
    half = HEAD_DIM // 2
    inv = ROPE_THETA ** (-jnp.arange(half, dtype=f32) / half)
    ang = pos.astype(f32)[:, None] * inv[None, :]
    cos = jnp.tile(jnp.cos(ang), (1, LANES // half))
    sin = jnp.tile(jnp.sin(ang), (1, LANES // half))
    sign = jnp.where((jnp.arange(LANES) % HEAD_DIM) < half, -1.0, 1.0).astype(f32)
    return cos, sin * sign[None, :]


def _cmp_to_sel(ncmp, nsel):
    per = SEL_BLK // CMP_STRIDE
    sub = CMP_BLK // CMP_STRIDE
    i = jnp.arange(LANES)[:, None]
    j = jnp.arange(LANES)[None, :]
    m = jnp.zeros((LANES, LANES), f32)
    for r in range(sub):
        m = m + (((i + r) // per) == j).astype(f32)
    m = jnp.where((i < ncmp) & (j < nsel), m / sub, 0.0)
    return m.astype(bf16)


def _block_expand(nkeys):
    j = jnp.arange(LANES)[:, None]
    k = jnp.arange(nkeys)[None, :]
    return ((k // SEL_BLK) == j).astype(bf16)


def _compress_consts(pe, w1, w2):
    w1r = w1.reshape(CMP_BLK, HEAD_DIM, CMP_HID)
    eye2 = jnp.eye(2, dtype=f32)

    def expand(w):
        return jnp.einsum('ldc,gh->lgdhc', w, eye2).reshape(16 * LANES, 2 * CMP_HID)

    w1s = jnp.concatenate([expand(w1r[:16]), expand(w1r[16:])], axis=1).astype(bf16)
    pes = jnp.stack([jnp.broadcast_to(pe[:16, None, :], (16, 2, HEAD_DIM)).reshape(-1),
                     jnp.broadcast_to(pe[16:, None, :], (16, 2, HEAD_DIM)).reshape(-1)])
    w2s = jnp.einsum('cd,gh->gchd', w2, eye2).reshape(2 * CMP_HID, LANES).astype(bf16)
    return pes, w1s, w2s


def kernel(x_prompt, x_sample, mem_prompt, state_conv, state_ssm, cache_cmp_k, cache_cmp_v, cache_sel_k, cache_sel_v, cache_win_k, cache_win_v, cache_mem_k, cache_mem_v, page_table, norm_mix, norm_ffn, norm_mem, w_mem_kv, w_in_a, conv_w_a, a_log, dt_bias, norm_out_a, w_out_a, w_in_b, w_out_b, norm_kv, w_kv_shared, cmp_pe_k, cmp_w1_k, cmp_w2_k, cmp_pe_v, cmp_w1_v, cmp_w2_v, w_gu_dense, w_down_dense, w_router, w_gu_exp, w_down_exp, norm_final):
    bp, lp, d = x_prompt.shape
    bs, ls, _ = x_sample.shape
    depth = norm_mix.shape[0]
    n_a = w_in_a.shape[0]
    past = page_table.shape[1] * PAGE
    tp = bp * lp
    T = tp + bs * S_PAD
    tm = 512

    xs = jnp.pad(x_sample, ((0, 0), (0, S_PAD - ls), (0, 0)))
    H = jnp.concatenate([x_prompt.reshape(tp, d), xs.reshape(bs * S_PAD, d)], axis=0)

    qkvz = MEM_W + CONV_DIM + A_W
    w_in_a_b = jnp.concatenate(
        [w_in_a[:, :, MEM_W:qkvz], w_in_a[:, :, :MEM_W], w_in_a[:, :, qkvz:],
         jnp.zeros(w_in_a.shape[:2] + (A_IN_PAD - w_in_a.shape[2],), f32)], axis=2).astype(bf16)
    w_in_b_b = jnp.pad(w_in_b, ((0, 0), (0, 0), (0, B_IN_PAD - w_in_b.shape[2]))).astype(bf16)
    w_out_a_b = w_out_a.astype(bf16)
    w_out_b_b = w_out_b.astype(bf16)
    w_gu_dense_b = w_gu_dense.astype(bf16)
    w_down_dense_b = w_down_dense.astype(bf16)
    w_gu_exp_b = w_gu_exp
    w_down_exp_b = w_down_exp
    w_router_p = jnp.pad(w_router, ((0, 0), (0, 0), (0, LANES - N_EXPERTS)))

    mem_k, mem_v = mem_kv(mem_prompt.reshape(bp * MEM_LEN, d), norm_mem.reshape(depth, 1, d),
                          w_mem_kv.astype(bf16), tm)
    cmk = jnp.transpose(cache_mem_k, (0, 1, 3, 4, 2)).reshape(depth, bs, MEM_W, MEM_LEN)
    cmv = jnp.transpose(cache_mem_v, (0, 1, 3, 4, 2)).reshape(depth, bs, MEM_W, MEM_LEN)
    zero_buf = jnp.zeros((bp, SUBLANES, CONV_DIM), f32)
    zero_state = jnp.zeros((1, bp, A_HEADS, LANES, LANES), f32)

    pos_p = jnp.arange(lp, dtype=jnp.int32)
    pos_s = past + jnp.arange(S_PAD, dtype=jnp.int32)
    cos_p, sin_p = _rope_tables(pos_p)
    cos_s, sin_s = _rope_tables(pos_s)

    conv_p, ssm_p, conv_s, ssm_s = [], [], [], []
    kv = None
    for layer in range(depth):
        is_a = layer < n_a
        g_mix = norm_mix[layer].reshape(1, d)
        if is_a:
            P = norm_matmul(H, g_mix, w_in_a_b, layer, tm, A_IN_PAD // 3)
            qblk = A_MEMQ_BLK
        else:
            lb = layer - n_a
            qblk = 0
            if lb == 0:
                cos_all = jnp.concatenate([cos_p, jnp.tile(cos_s, (tm // S_PAD, 1))], axis=0)
                sin_all = jnp.concatenate([sin_p, jnp.tile(sin_s, (tm // S_PAD, 1))], axis=0)
                kv = shared_kv(H, norm_kv.reshape(1, d), w_kv_shared.astype(bf16), cos_all, sin_all, bp, lp, tm)
                wk_pad, wv_pad = kv[6], kv[7]
                ncmp = (lp - CMP_BLK) // CMP_STRIDE + 1
                cpos = jnp.arange(LANES, dtype=jnp.int32) * CMP_STRIDE + CMP_BLK - 1
                cos_c, sin_c = _rope_tables(cpos)
                pek, w1k, w2k = _compress_consts(cmp_pe_k, cmp_w1_k, cmp_w2_k)
                pev, w1v, w2v = _compress_consts(cmp_pe_v, cmp_w1_v, cmp_w2_v)
                consts = (pek, pev, w1k, w1v, w2k, w2v, cos_c, sin_c)
                strips = lp // 16
                ck_p, cv_p = compress(kv[0][:tp].reshape(bp, strips, 16 * LANES),
                                      kv[1][:tp].reshape(bp, strips, 16 * LANES), consts)
                pt_flat = page_table.reshape(-1)
                feature_major = lambda pool: jnp.transpose(pool, (0, 2, 3, 1)).reshape(-1, LANES, PAGE)
                ck_s, cv_s = compress(*page_strips(pt_flat, feature_major(cache_cmp_k), feature_major(cache_cmp_v), bs),
                                      consts)
                mt_p = _cmp_to_sel(ncmp, lp // SEL_BLK).T
                et_p = _block_expand(lp).T
                ncmp_s = (past + ls - CMP_BLK) // CMP_STRIDE + 1
                mt_s = _cmp_to_sel(ncmp_s, -(-(past + ls) // SEL_BLK)).T
                e_s = _block_expand(past)
                eye = jnp.eye(LANES, dtype=bf16)
                pool_sk = feature_major(cache_sel_k)
                pool_sv = feature_major(cache_sel_v)
                cwk = jnp.transpose(cache_win_k, (0, 2, 3, 1)).reshape(bs, LANES, -1)
                cwv = jnp.transpose(cache_win_v, (0, 2, 3, 1)).reshape(bs, LANES, -1)
            P = norm_matmul(H, g_mix, w_in_b_b, lb, tm, B_IN_PAD)
        om = (mem_attn_prompt(P, qblk, mem_k, mem_v, layer, bp, lp, 512),
              mem_attn_sample(P, qblk, tp, cmk, cmv, layer, 8))
        if is_a:
            conv8 = jnp.pad(conv_w_a[layer], ((0, SUBLANES - conv_w_a.shape[1]), (0, 0)))
            ab8 = jnp.zeros((SUBLANES, LANES), f32)
            ab8 = ab8.at[0, A_HEADS:2 * A_HEADS].set(a_log[layer]).at[1, A_HEADS:2 * A_HEADS].set(dt_bias[layer])
            gout = norm_out_a[layer].reshape(1, LANES)
            mo_p, sp = delta_mixer(P, 0, bp, lp, 512, 64, 512, zero_buf, conv8, ab8, gout, zero_state, 0)
            buf_s = jnp.pad(state_conv[layer], ((0, 0), (SUBLANES - state_conv.shape[2], 0), (0, 0)))
            mo_s, ss = delta_mixer(P, tp, bs, S_PAD, S_PAD, S_PAD, ls, buf_s, conv8, ab8, gout, state_ssm, layer)
            conv_p.append(jnp.stack([P[b * lp + lp - 3:(b + 1) * lp, :CONV_DIM] for b in range(bp)]))
            conv_s.append(P[tp:, :CONV_DIM].reshape(bs, S_PAD, CONV_DIM)[:, ls - 3:ls])
            ssm_p.append(sp)
            ssm_s.append(ss)
            w_out, wl = w_out_a_b, layer
        else:
            mo_p = nsa_prompt(P, cos_p, sin_p, ck_p, cv_p, kv[2], kv[3], wk_pad, wv_pad, mt_p, et_p, eye,
                              bp, lp, Q_BLOCK, 512)
            mo_s = nsa_sample(P, tp, pt_flat, pool_sk, pool_sv, cos_s, sin_s, ck_s, cv_s, kv[2], kv[3], cwk, cwv,
                              kv[4], kv[5], mt_s, e_s, eye)
            w_out, wl = w_out_b_b, lb
        mo = (mo_p, mo_s)
        g_ffn = norm_ffn[layer].reshape(1, d)
        if layer % 2 == 0:
            H = outproj_ffn(H, mo, om, w_out, g_ffn, w_gu_dense_b, w_down_dense_b, wl, layer // 2, tm, D_FF // 2)
        else:
            H = outproj_moe(H, mo, om, w_out, g_ffn, w_router_p, w_gu_exp_b, w_down_exp_b, wl, layer // 2,
                            tm, D_FF_EXPERT // 4)
    Y = final_norm(H, norm_final.reshape(1, d), tm)

    def rows_p(a):
        return a[:tp].reshape(bp, lp, 2, HEAD_DIM)

    def rows_s(a):
        return a[tp:].reshape(bs, S_PAD, 2, HEAD_DIM)[:, :ls]

    wlen = min(WINDOW, lp)
    return (Y[:tp].reshape(bp, lp, d), Y[tp:].reshape(bs, S_PAD, d)[:, :ls],
            jnp.stack(conv_p), jnp.stack(ssm_p),
            rows_p(kv[0]), rows_p(kv[1]), rows_p(kv[2]), rows_p(kv[3]),
            rows_p(kv[4])[:, lp - wlen:], rows_p(kv[5])[:, lp - wlen:],
            mem_k.reshape(depth, bp, MEM_LEN, 4, HEAD_DIM), mem_v.reshape(depth, bp, MEM_LEN, 4, HEAD_DIM),
            jnp.stack(conv_s), jnp.stack(ssm_s),
            rows_s(kv[0]), rows_s(kv[1]), rows_s(kv[2]), rows_s(kv[3]),
            jnp.concatenate([cache_win_k[:, ls:], rows_s(kv[4])], axis=1),
            jnp.concatenate([cache_win_v[:, ls:], rows_s(kv[5])], axis=1))
```

```python
import functools
import math

import jax
import jax.numpy as jnp
from jax import lax
from jax.experimental import pallas as pl
from jax.experimental.pallas import tpu as pltpu

f32 = jnp.float32
bf16 = jnp.bfloat16

EPS = 1e-6
NEG = -1e30
ROPE_THETA = 10000.0
HEAD_DIM = 64
LANES = 128
SUBLANES = 8
VMEM_LIMIT = 48 * 1024 * 1024

D_MODEL = 1024
MEM_LEN = 256
MEM_W = 256
A_HEADS = 6
A_W = 768
CONV_DIM = 2304
A_IN_PAD = 3456
A_MEMQ_BLK = 12
A_BA_BLK = 26
B_HEADS = 12
HPG = 6
B_W = 768
B_IN_PAD = 1152
CMP_BLK = 32
CMP_STRIDE = 16
CMP_HID = 256
SEL_BLK = 64
SEL_TOP = 16
WINDOW = 512
Q_BLOCK = 128
D_FF = 2816
N_EXPERTS = 8
D_FF_EXPERT = 3584
PAGE = 128
S_PAD = 8


def _cparams(sem):
    return pltpu.CompilerParams(dimension_semantics=sem, vmem_limit_bytes=VMEM_LIMIT)


def _nt(a, b):
    return lax.dot_general(a, b, (((1,), (1,)), ((), ())), preferred_element_type=f32)


def _tn(a, b):
    return lax.dot_general(a, b, (((0,), (0,)), ((), ())), preferred_element_type=f32)


def _mm(a, b):
    return jnp.dot(a, b, preferred_element_type=f32)


def _rms(x, g):
    return (x * lax.rsqrt(jnp.mean(x * x, axis=-1, keepdims=True) + EPS)) * g


def _sigmoid(x):
    return 1.0 / (1.0 + jnp.exp(-x))


def _silu(x):
    return x * _sigmoid(x)


def _softplus(x):
    return jnp.maximum(x, 0.0) + jnp.log(1.0 + jnp.exp(-jnp.abs(x)))


def _rope_lanes(x, cos, sin_signed):
    lane = lax.broadcasted_iota(jnp.int32, x.shape, x.ndim - 1)
    lo = (lane % HEAD_DIM) < (HEAD_DIM // 2)
    partner = jnp.where(lo, pltpu.roll(x, LANES - HEAD_DIM // 2, axis=x.ndim - 1),
                        pltpu.roll(x, HEAD_DIM // 2, axis=x.ndim - 1))
    return x * cos + partner * sin_signed


def _norm_matmul_kernel(x_ref, g_ref, w_ref, o_ref, xn_ref):
    @pl.when(pl.program_id(1) == 0)
    def _():
        xn_ref[...] = _rms(x_ref[...], g_ref[...]).astype(bf16)

    o_ref[...] = _mm(xn_ref[...], w_ref[...])


def norm_matmul(x, g, w, layer, tm, tn):
    T, K = x.shape
    N = w.shape[2]
    return pl.pallas_call(
        _norm_matmul_kernel,
        grid=(T // tm, N // tn),
        in_specs=[pl.BlockSpec((tm, K), lambda i, j: (i, 0)),
                  pl.BlockSpec((1, K), lambda i, j: (0, 0)),
                  pl.BlockSpec((None, K, tn), lambda i, j: (layer, 0, j))],
        out_specs=pl.BlockSpec((tm, tn), lambda i, j: (i, j)),
        out_shape=jax.ShapeDtypeStruct((T, N), f32),
        scratch_shapes=[pltpu.VMEM((tm, K), bf16)],
        compiler_params=_cparams(("parallel", "arbitrary")),
        name="norm_matmul",
    )(x, g, w)


def _mem_kv_kernel(x_ref, g_ref, w_ref, k_ref, v_ref):
    y = _mm(_rms(x_ref[...], g_ref[...]).astype(bf16), w_ref[...])
    k_ref[...] = y[:, :MEM_W]
    v_ref[...] = y[:, MEM_W:]


def mem_kv(mem, g, w, tm):
    R, K = mem.shape
    NL = w.shape[0]
    out = jax.ShapeDtypeStruct((NL, R, MEM_W), f32)
    return pl.pallas_call(
        _mem_kv_kernel,
        grid=(NL, R // tm),
        in_specs=[pl.BlockSpec((tm, K), lambda l, i: (i, 0)),
                  pl.BlockSpec((None, 1, K), lambda l, i: (l, 0, 0)),
                  pl.BlockSpec((None, K, 2 * MEM_W), lambda l, i: (l, 0, 0))],
        out_specs=(pl.BlockSpec((None, tm, MEM_W), lambda l, i: (l, i, 0)),
                   pl.BlockSpec((None, tm, MEM_W), lambda l, i: (l, i, 0))),
        out_shape=(out, out),
        compiler_params=_cparams(("parallel", "parallel")),
        name="mem_kv",
    )(mem, g, w)


def _final_norm_kernel(x_ref, g_ref, o_ref):
    o_ref[...] = _rms(x_ref[...], g_ref[...])


def final_norm(x, g, tm):
    T, K = x.shape
    return pl.pallas_call(
        _final_norm_kernel,
        grid=(T // tm,),
        in_specs=[pl.BlockSpec((tm, K), lambda i: (i, 0)), pl.BlockSpec((1, K), lambda i: (0, 0))],
        out_specs=pl.BlockSpec((tm, K), lambda i: (i, 0)),
        out_shape=jax.ShapeDtypeStruct((T, K), f32),
        compiler_params=_cparams(("parallel",)),
        name="final_norm",
    )(x, g)


def _mem_attend(q, k, v, transposed=False):
    tq = q.shape[0]
    nh = MEM_W // HEAD_DIM
    lane = lax.broadcasted_iota(jnp.int32, (tq, MEM_W), 1)
    masks = [(lane >= HEAD_DIM * h) & (lane < HEAD_DIM * (h + 1)) for h in range(nh)]
    qs = jnp.concatenate([jnp.where(m, q, 0.0) for m in masks], axis=0).astype(bf16)
    s = (_mm(qs, k) if transposed else _nt(qs, k)) * (HEAD_DIM ** -0.5)
    p = jnp.exp(s - jnp.max(s, axis=-1, keepdims=True))
    l = jnp.sum(p, axis=-1, keepdims=True)
    pb = p.astype(bf16)
    o = (_nt(pb, v) if transposed else _mm(pb, v)) / l
    out = jnp.zeros((tq, MEM_W), f32)
    for h in range(nh):
        out = jnp.where(masks[h], o[h * tq:(h + 1) * tq], out)
    return out


def _mem_attn_prompt_kernel(q_ref, k_ref, v_ref, o_ref):
    o_ref[...] = _mem_attend(q_ref[...], k_ref[...].astype(bf16), v_ref[...].astype(bf16))


def mem_attn_prompt(P, qblk, mk, mv, layer, nb, L, tq):
    nq = L // tq
    return pl.pallas_call(
        _mem_attn_prompt_kernel,
        grid=(nb, nq),
        in_specs=[pl.BlockSpec((tq, MEM_W), lambda b, i: (b * nq + i, qblk)),
                  pl.BlockSpec((None, MEM_LEN, MEM_W), lambda b, i: (layer, b, 0)),
                  pl.BlockSpec((None, MEM_LEN, MEM_W), lambda b, i: (layer, b, 0))],
        out_specs=pl.BlockSpec((tq, MEM_W), lambda b, i: (b * nq + i, 0)),
        out_shape=jax.ShapeDtypeStruct((nb * L, MEM_W), f32),
        compiler_params=_cparams(("parallel", "parallel")),
        name="mem_attn_prompt",
    )(P, mk, mv)


def _mem_attn_sample_kernel(q_ref, k_ref, v_ref, o_ref, *, bb):
    for b in range(bb):
        rows = slice(b * S_PAD, (b + 1) * S_PAD)
        o_ref[rows, :] = _mem_attend(q_ref[rows, :], k_ref[b].astype(bf16), v_ref[b].astype(bf16), transposed=True)


def mem_attn_sample(P, qblk, row0, ckt, cvt, layer, bb):
    nb = ckt.shape[1]
    blk0 = row0 // (bb * S_PAD)
    return pl.pallas_call(
        functools.partial(_mem_attn_sample_kernel, bb=bb),
        grid=(nb // bb,),
        in_specs=[pl.BlockSpec((bb * S_PAD, MEM_W), lambda i: (blk0 + i, qblk)),
                  pl.BlockSpec((None, bb, MEM_W, MEM_LEN), lambda i: (layer, i, 0, 0)),
                  pl.BlockSpec((None, bb, MEM_W, MEM_LEN), lambda i: (layer, i, 0, 0))],
        out_specs=pl.BlockSpec((bb * S_PAD, MEM_W), lambda i: (i, 0)),
        out_shape=jax.ShapeDtypeStruct((nb * S_PAD, MEM_W), f32),
        compiler_params=_cparams(("parallel",)),
        name="mem_attn_sample",
    )(P, ckt, cvt)


def _delta_kernel(q_ref, k_ref, v_ref, z_ref, ba_ref, buf_ref, cw_ref, ab_ref, go_ref, s0_ref,
                  o_ref, sout_ref, qs_ref, ks_ref, vs_ref, gs_ref, bs_ref, tail_ref, s_ref,
                  *, TL, C, valid):
    t = pl.program_id(1)

    @pl.when(t == 0)
    def _():
        tail_ref[...] = buf_ref[...]
        s_ref[...] = s0_ref[...]

    row8 = lax.broadcasted_iota(jnp.int32, (SUBLANES, A_W), 0)

    def conv(x_ref, c0):
        x = x_ref[...]
        b8 = tail_ref[:, c0:c0 + A_W]
        c8 = cw_ref[:, c0:c0 + A_W]
        acc = x * c8[3:4, :]
        for s in range(1, 4):
            r = pltpu.roll(x, s, axis=0)
            fix = jnp.where(row8 < s, pltpu.roll(b8, s, axis=0), r[0:SUBLANES])
            r = jnp.concatenate([fix, r[SUBLANES:]], axis=0) if TL > SUBLANES else fix
            acc = acc + r * c8[3 - s:4 - s, :]
        tail_ref[:, c0:c0 + A_W] = x[TL - SUBLANES:TL]
        return _silu(acc)

    q = conv(q_ref, 0)
    k = conv(k_ref, A_W)
    v = conv(v_ref, 2 * A_W)
    ba = ba_ref[...]
    ab = ab_ref[...]
    beta_all = _sigmoid(ba)
    g_all = -jnp.exp(ab[0:1, :]) * _softplus(ba + ab[1:2, :])
    if valid < TL:
        live = lax.broadcasted_iota(jnp.int32, (TL, A_W), 0) < valid
        live1 = lax.broadcasted_iota(jnp.int32, (TL, LANES), 0) < valid
        k = jnp.where(live, k, 0.0)
        v = jnp.where(live, v, 0.0)
        beta_all = jnp.where(live1, beta_all, 0.0)
        g_all = jnp.where(live1, g_all, 0.0)
    rowc = lax.broadcasted_iota(jnp.int32, (TL, LANES), 0) & (C - 1)
    gc_all = g_all
    s = 1
    while s < C:
        gc_all = gc_all + jnp.where(rowc >= s, pltpu.roll(gc_all, s, axis=0), 0.0)
        s *= 2
    vs_ref[...] = v
    for h in range(A_HEADS):
        sl = slice(h * LANES, (h + 1) * LANES)
        qh = q[:, sl]
        kh = k[:, sl]
        qs_ref[:, sl] = qh * lax.rsqrt(jnp.sum(qh * qh, axis=-1, keepdims=True) + EPS) * (LANES ** -0.5)
        ks_ref[:, sl] = kh * lax.rsqrt(jnp.sum(kh * kh, axis=-1, keepdims=True) + EPS)
        gs_ref[:, sl] = jnp.broadcast_to(gc_all[:, A_HEADS + h:A_HEADS + h + 1], (TL, LANES))
        bs_ref[:, sl] = jnp.broadcast_to(beta_all[:, h:h + 1], (TL, LANES))

    ii = lax.broadcasted_iota(jnp.int32, (C, C), 0)
    jj = lax.broadcasted_iota(jnp.int32, (C, C), 1)
    tri = ii >= jj
    strict = ii > jj
    eye = jnp.where(ii == jj, 1.0, 0.0)
    gout = go_ref[...]
    nsteps = int(math.log2(C)) - 1

    heads = range(A_HEADS)
    lanes_of = [slice(h * LANES, (h + 1) * LANES) for h in heads]

    cpi = 2 if TL // C >= 2 else 1

    def chunks(c, carry):
        rows_of = [pl.ds(pl.multiple_of((c * cpi + i) * C, C), C) for i in range(cpi)]
        units = [(rows, h) for rows in rows_of for h in heads]
        U = range(len(units))
        qc = [qs_ref[rows, lanes_of[h]] for rows, h in units]
        kc = [ks_ref[rows, lanes_of[h]] for rows, h in units]
        vc = [vs_ref[rows, lanes_of[h]] for rows, h in units]
        gcc = [gs_ref[rows, lanes_of[h]] for rows, h in units]
        bc = [bs_ref[rows, lanes_of[h]] for rows, h in units]
        kb = [kc[u] * bc[u] for u in U]
        kq = [_nt(jnp.concatenate([kb[u], qc[u]], axis=0).astype(bf16), kc[u].astype(bf16)) for u in U]
        decay = []
        for u in U:
            gi = gcc[u][:, 0:C]
            gj = jnp.sum(jnp.where(ii == jj, gi, 0.0), axis=0, keepdims=True)
            decay.append(jnp.where(tri, jnp.exp(jnp.where(tri, gi - gj, 0.0)), 0.0))
        n = [jnp.where(strict, -(kq[u][0:C] * decay[u]), 0.0) for u in U]
        a = [jnp.where(tri, kq[u][C:2 * C] * decay[u], 0.0).astype(bf16) for u in U]
        x = [eye + n[u] for u in U]
        p = n
        for _ in range(nsteps):
            pb = [p[u].astype(bf16) for u in U]
            p = [_mm(pb[u], pb[u]) for u in U]
            x = [x[u] + _mm(x[u].astype(bf16), p[u].astype(bf16)) for u in U]
        eg = [jnp.exp(gcc[u]) for u in U]
        uw = [_mm(x[u].astype(bf16), jnp.concatenate([vc[u] * bc[u], kb[u] * eg[u]], axis=1).astype(bf16))
              for u in U]
        wqin = [jnp.concatenate([uw[u][:, LANES:], qc[u] * eg[u]], axis=0).astype(bf16) for u in U]
        glast = [gcc[u][C - 1:C, :] for u in U]
        kg = [(kc[u] * jnp.exp(glast[u] - gcc[u])).astype(bf16) for u in U]
        for i, rows in enumerate(rows_of):
            us = [i * A_HEADS + h for h in heads]
            S = [s_ref[h] for h in heads]
            wq = [_mm(wqin[u], S[h].astype(bf16)) for h, u in zip(heads, us)]
            vnb = [(uw[u][:, :LANES] - wq[h][0:C]).astype(bf16) for h, u in zip(heads, us)]
            av = [_mm(a[u], vnb[h]) for h, u in zip(heads, us)]
            kv_new = [_tn(kg[u], vnb[h]) for h, u in zip(heads, us)]
            for h, u in zip(heads, us):
                s_ref[h] = S[h] * jnp.exp(glast[u]) + kv_new[h]
                o_ref[rows, lanes_of[h]] = _rms(wq[h][C:2 * C] + av[h], gout) * _silu(z_ref[rows, lanes_of[h]])
        return carry

    if TL // C == cpi:
        chunks(0, 0)
    else:
        lax.fori_loop(0, TL // (C * cpi), chunks, 0)

    @pl.when(t == pl.num_programs(1) - 1)
    def _():
        sout_ref[...] = s_ref[...]


def delta_mixer(P, row0, nb, L, TL, C, valid, buf8, conv8, ab8, gout, s0, layer):
    nt = L // TL
    rb0 = row0 // TL
    tile = lambda c: pl.BlockSpec((TL, A_W), lambda b, t: (rb0 + b * nt + t, c))
    state = pl.BlockSpec((None, A_HEADS, LANES, LANES), lambda b, t: (b, 0, 0, 0))
    state_in = pl.BlockSpec((None, None, A_HEADS, LANES, LANES), lambda b, t: (layer, b, 0, 0, 0))
    return pl.pallas_call(
        functools.partial(_delta_kernel, TL=TL, C=C, valid=valid),
        grid=(nb, nt),
        in_specs=[tile(0), tile(1), tile(2), tile(3),
                  pl.BlockSpec((TL, LANES), lambda b, t: (rb0 + b * nt + t, A_BA_BLK)),
                  pl.BlockSpec((None, SUBLANES, CONV_DIM), lambda b, t: (b, 0, 0)),
                  pl.BlockSpec((SUBLANES, CONV_DIM), lambda b, t: (0, 0)),
                  pl.BlockSpec((SUBLANES, LANES), lambda b, t: (0, 0)),
                  pl.BlockSpec((1, LANES), lambda b, t: (0, 0)),
                  state_in],
        out_specs=(pl.BlockSpec((TL, A_W), lambda b, t: (b * nt + t, 0)), state),
        out_shape=(jax.ShapeDtypeStruct((nb * L, A_W), f32),
                   jax.ShapeDtypeStruct((nb, A_HEADS, LANES, LANES), f32)),
        scratch_shapes=[pltpu.VMEM((TL, A_W), f32)] * 5
        + [pltpu.VMEM((SUBLANES, CONV_DIM), f32), pltpu.VMEM((A_HEADS, LANES, LANES), f32)],
        compiler_params=_cparams(("parallel", "arbitrary")),
        name="delta_mixer",
    )(P, P, P, P, P, buf8, conv8, ab8, gout, s0)


def _attn_residual(refs, ntp):
    h_ref, mop_ref, mos_ref, omp_ref, oms_ref, wo1_ref, wo2_ref = refs
    is_p = pl.program_id(0) < ntp
    mo = jnp.where(is_p, mop_ref[...], mos_ref[...])
    om = jnp.where(is_p, omp_ref[...], oms_ref[...])
    return h_ref[...] + _mm(mo.astype(bf16), wo1_ref[...]) + _mm(om.astype(bf16), wo2_ref[...])


def _ffn_kernel(*refs, ntp):
    g_ref, wg_ref, wu_ref, wd_ref, o_ref, xn_ref, acc_ref = refs[7:]
    k = pl.program_id(1)

    @pl.when(k == 0)
    def _():
        h1 = _attn_residual(refs[:7], ntp)
        acc_ref[...] = h1
        xn_ref[...] = _rms(h1, g_ref[...]).astype(bf16)

    xn = xn_ref[...]
    a = _silu(_mm(xn, wg_ref[...])) * _mm(xn, wu_ref[...])
    acc_ref[...] += _mm(a.astype(bf16), wd_ref[...])

    @pl.when(k == pl.num_programs(1) - 1)
    def _():
        o_ref[...] = acc_ref[...]


def _outproj_specs(tm, K, mw, ow, ntp, layer):
    prow = lambda i, *_: (jnp.minimum(i, ntp - 1), 0)
    srow = lambda i, *_: (jnp.maximum(i - ntp, 0), 0)
    return [pl.BlockSpec((tm, K), lambda i, *_: (i, 0)),
            pl.BlockSpec((tm, mw), prow), pl.BlockSpec((tm, mw), srow),
            pl.BlockSpec((tm, ow), prow), pl.BlockSpec((tm, ow), srow),
            pl.BlockSpec((None, mw, K), lambda i, *_: (layer, 0, 0)),
            pl.BlockSpec((None, ow, K), lambda i, *_: (layer, mw // ow, 0)),
            pl.BlockSpec((1, K), lambda i, *_: (0, 0))]


def outproj_ffn(H, mo, om, w_out, g, w_gu, w_down, layer, ffn_layer, tm, tf):
    T, K = H.shape
    FF = w_down.shape[1]
    nk = FF // tf
    mw, ow = mo[0].shape[1], om[0].shape[1]
    ntp = mo[0].shape[0] // tm
    return pl.pallas_call(
        functools.partial(_ffn_kernel, ntp=ntp),
        grid=(T // tm, nk),
        in_specs=_outproj_specs(tm, K, mw, ow, ntp, layer)
        + [pl.BlockSpec((None, K, tf), lambda i, k: (ffn_layer, 0, k)),
           pl.BlockSpec((None, K, tf), lambda i, k: (ffn_layer, 0, nk + k)),
           pl.BlockSpec((None, tf, K), lambda i, k: (ffn_layer, k, 0))],
        out_specs=pl.BlockSpec((tm, K), lambda i, k: (i, 0)),
        out_shape=jax.ShapeDtypeStruct((T, K), f32),
        scratch_shapes=[pltpu.VMEM((tm, K), bf16), pltpu.VMEM((tm, K), f32)],
        compiler_params=_cparams(("parallel", "arbitrary")),
        name="outproj_ffn",
    )(H, mo[0], mo[1], om[0], om[1], w_out, w_out, g, w_gu, w_gu, w_down)


def _router_kernel(*refs, ntp):
    g_ref, wr_ref, h1_ref, xn_ref, r_ref, cnt_ref = refs[7:]
    i = pl.program_id(0)
    tm = h1_ref.shape[0]

    @pl.when(i == 0)
    def _():
        cnt_ref[...] = jnp.zeros_like(cnt_ref)

    h1 = _attn_residual(refs[:7], ntp)
    h1_ref[...] = h1
    xn = _rms(h1, g_ref[...])
    xn_ref[...] = xn
    lane = lax.broadcasted_iota(jnp.int32, (tm, LANES), 1)
    xh = xn.astype(bf16)
    xl = (xn - xh.astype(f32)).astype(bf16)
    wr = wr_ref[...]
    wh = wr.astype(bf16)
    wl = (wr - wh.astype(f32)).astype(bf16)
    logits = _mm(xh, wh) + _mm(xh, wl) + _mm(xl, wh)
    logits = jnp.where(lane < N_EXPERTS, logits, -jnp.inf)
    m1 = jnp.max(logits, axis=-1, keepdims=True)
    i1 = jnp.min(jnp.where(logits == m1, lane, LANES), axis=-1, keepdims=True)
    rest = jnp.where(lane == i1, -jnp.inf, logits)
    m2 = jnp.max(rest, axis=-1, keepdims=True)
    i2 = jnp.min(jnp.where(rest == m2, lane, LANES), axis=-1, keepdims=True)
    e2 = jnp.exp(m2 - m1)
    w1 = 1.0 / (1.0 + e2)
    w2 = e2 / (1.0 + e2)
    hot = jnp.where((lane == i1) | (lane == i2), 1.0, 0.0)
    ri = lax.broadcasted_iota(jnp.int32, (tm, tm), 0)
    ci = lax.broadcasted_iota(jnp.int32, (tm, tm), 1)
    before = jnp.where(ri > ci, 1.0, 0.0).astype(bf16)
    cum = _mm(before, hot.astype(bf16)) + cnt_ref[...]
    r1 = jnp.sum(jnp.where(lane == i1, cum, 0.0), axis=-1, keepdims=True)
    r2 = jnp.sum(jnp.where(lane == i2, cum, 0.0), axis=-1, keepdims=True)
    cnt_ref[...] += jnp.sum(hot, axis=0, keepdims=True)
    cols = (i1.astype(f32), i2.astype(f32), w1, w2, r1, r2)
    r = jnp.zeros((tm, LANES), f32)
    for c, val in enumerate(cols):
        r = jnp.where(lane == c, val, r)
    r_ref[...] = r


def outproj_router(H, mo, om, w_out, g, w_router, layer, moe_layer, tm):
    T, K = H.shape
    mw, ow = mo[0].shape[1], om[0].shape[1]
    ntp = mo[0].shape[0] // tm
    return pl.pallas_call(
        functools.partial(_router_kernel, ntp=ntp),
        grid=(T // tm,),
        in_specs=_outproj_specs(tm, K, mw, ow, ntp, layer)
        + [pl.BlockSpec((None, K, LANES), lambda i: (moe_layer, 0, 0))],
        out_specs=(pl.BlockSpec((tm, K), lambda i: (i, 0)), pl.BlockSpec((tm, K), lambda i: (i, 0)),
                   pl.BlockSpec((tm, LANES), lambda i: (i, 0))),
        out_shape=(jax.ShapeDtypeStruct((T, K), f32), jax.ShapeDtypeStruct((T, K), f32),
                   jax.ShapeDtypeStruct((T, LANES), f32)),
        scratch_shapes=[pltpu.VMEM((1, LANES), f32)],
        compiler_params=_cparams(("arbitrary",)),
        name="outproj_router",
    )(H, mo[0], mo[1], om[0], om[1], w_out, w_out, g, w_router)


def _moe_group_kernel(te_ref, nu_ref, x_ref, sw_ref, wg_ref, wu_ref, wd_ref, *rest, tile0):
    del te_ref
    o_ref, acc_ref = rest[-2:]
    i = pl.program_id(0) + tile0
    k = pl.program_id(1)

    @pl.when(k == 0)
    def _():
        acc_ref[...] = jnp.zeros_like(acc_ref)

    @pl.when(i < nu_ref[0])
    def _():
        x = x_ref[...].astype(bf16)
        a = _silu(_mm(x, wg_ref[...])) * _mm(x, wu_ref[...])
        acc_ref[...] += _mm(a.astype(bf16), wd_ref[...])

    @pl.when(k == pl.num_programs(1) - 1)
    def _():
        o_ref[...] = acc_ref[...] * sw_ref[...]


def moe_grouped(xg, sw, tile_e, n_used, w_gu, w_down, moe_layer, tm, tf, tile0=0, prev=None):
    K = xg.shape[1]
    NP = sw.shape[0]
    FF = w_down.shape[2]
    nk = FF // tf

    def kk(i, k, nu):
        return jnp.where(i + tile0 < nu[0], k, nk - 1)

    ex = lambda i, te: te[i + tile0]
    in_specs = [pl.BlockSpec((tm, K), lambda i, k, te, nu: (i, 0)),
                pl.BlockSpec((tm, 1), lambda i, k, te, nu: (i + tile0, 0)),
                pl.BlockSpec((None, None, K, tf), lambda i, k, te, nu: (moe_layer, ex(i, te), 0, kk(i, k, nu))),
                pl.BlockSpec((None, None, K, tf), lambda i, k, te, nu: (moe_layer, ex(i, te), 0, nk + kk(i, k, nu))),
                pl.BlockSpec((None, None, tf, K), lambda i, k, te, nu: (moe_layer, ex(i, te), kk(i, k, nu), 0))]
    args = [tile_e, n_used, xg, sw, w_gu, w_gu, w_down]
    aliases = {}
    if prev is not None:
        in_specs.append(pl.BlockSpec(memory_space=pl.ANY))
        aliases = {len(args): 0}
        args.append(prev)
    grid_spec = pltpu.PrefetchScalarGridSpec(
        num_scalar_prefetch=2, grid=(xg.shape[0] // tm, nk),
        in_specs=in_specs,
        out_specs=pl.BlockSpec((tm, K), lambda i, k, te, nu: (i + tile0, 0)),
        scratch_shapes=[pltpu.VMEM((tm, K), f32)])
    return pl.pallas_call(
        functools.partial(_moe_group_kernel, tile0=tile0), grid_spec=grid_spec,
        out_shape=jax.ShapeDtypeStruct((NP, K), f32),
        input_output_aliases=aliases,
        compiler_params=_cparams(("parallel", "arbitrary")),
        name="moe_grouped",
    )(*args)


def _dispatch(r, tm):
    T = r.shape[0]
    A = 2 * T
    ntiles = A // tm + N_EXPERTS
    NP = ntiles * tm
    e_flat = r[:, 0:2].astype(jnp.int32).reshape(A)
    w_flat = r[:, 2:4].reshape(A)
    rank = r[:, 4:6].astype(jnp.int32)
    counts = jnp.sum((e_flat[:, None] == jnp.arange(N_EXPERTS)[None, :]).astype(jnp.int32), axis=0)
    pcounts = (counts + tm - 1) // tm * tm
    pends = jnp.cumsum(pcounts)
    pstarts = pends - pcounts
    n_used = pends[-1] // tm
    d = jnp.arange(NP - A, dtype=jnp.int32)
    e_pad = jnp.sum((d[:, None] >= jnp.cumsum(pcounts - counts)[None, :]).astype(jnp.int32), axis=1)
    keys = jnp.concatenate([e_flat * NP + jnp.arange(A, dtype=jnp.int32), e_pad * NP + A + d])
    toks = jnp.concatenate([jnp.arange(A, dtype=jnp.int32) // 2, jnp.zeros((NP - A,), jnp.int32)])
    wts = jnp.concatenate([w_flat, jnp.zeros((NP - A,), f32)])
    _, src, sw = lax.sort((keys, toks, wts), num_keys=1)
    sw = sw.reshape(NP, 1)
    tiles = jnp.arange(ntiles, dtype=jnp.int32)
    tile_e = jnp.minimum(jnp.searchsorted(pends, tiles * tm, side='right'), N_EXPERTS - 1).astype(jnp.int32)
    tile_e = jnp.where(tiles < n_used, tile_e, tile_e[jnp.maximum(n_used - 1, 0)])
    picks = r[:, 0:2].astype(jnp.int32)
    pick_start = jnp.sum(jnp.where(picks[:, :, None] == jnp.arange(N_EXPERTS)[None, None, :],
                                   pstarts[None, None, :], 0), axis=2)
    slots = pick_start + rank
    return src, sw, tile_e, n_used.reshape(1).astype(jnp.int32), slots


def outproj_moe(H, mo, om, w_out, g, w_router, w_gu, w_down, layer, moe_layer, tm, tf):
    h1, xn, r = outproj_router(H, mo, om, w_out, g, w_router, layer, moe_layer, tm)
    src, sw, tile_e, n_used, slots = _dispatch(r, tm)
    half = (src.shape[0] // tm // 2) * tm
    yg = moe_grouped(jnp.take(xn, src[:half], axis=0), sw, tile_e, n_used, w_gu, w_down, moe_layer, tm, tf)
    yg = moe_grouped(jnp.take(xn, src[half:], axis=0), sw, tile_e, n_used, w_gu, w_down, moe_layer, tm, tf,
                     tile0=half // tm, prev=yg)
    return h1 + jnp.take(yg, slots[:, 0], axis=0) + jnp.take(yg, slots[:, 1], axis=0)


def _shared_kv_kernel(x_ref, g_ref, w_ref, cos_ref, sin_ref, zk_ref, zv_ref,
                      ck_ref, cv_ref, sk_ref, sv_ref, wk_ref, wv_ref, wkp_ref, wvp_ref):
    del zk_ref, zv_ref
    y = _mm(_rms(x_ref[...], g_ref[...]).astype(bf16), w_ref[...])
    cos = cos_ref[...]
    sin = sin_ref[...]
    ck_ref[...] = y[:, 0 * LANES:1 * LANES]
    cv_ref[...] = y[:, 1 * LANES:2 * LANES]
    sk_ref[...] = _rope_lanes(y[:, 2 * LANES:3 * LANES], cos, sin)
    sv_ref[...] = y[:, 3 * LANES:4 * LANES]
    wk = _rope_lanes(y[:, 4 * LANES:5 * LANES], cos, sin)
    wk_ref[...] = wk
    wkp_ref[...] = wk
    wv_ref[...] = y[:, 5 * LANES:6 * LANES]
    wvp_ref[...] = y[:, 5 * LANES:6 * LANES]


def shared_kv(H, g, w, cos, sin, nb, L, tm):
    T, K = H.shape
    assert tm == WINDOW and L % tm == 0
    per = L // tm
    ntp = nb * per
    nblk = nb * (per + 1) + (T // tm - ntp)
    row = pl.BlockSpec((tm, LANES), lambda i: (i, 0))
    table = pl.BlockSpec((tm, LANES), lambda i: (jnp.where(i < ntp, i % per, per), 0))
    padded = pl.BlockSpec((tm, LANES), lambda i: (jnp.where(i < ntp, i + i // per + 1, nb + i), 0))
    out = jax.ShapeDtypeStruct((T, LANES), f32)
    outp = jax.ShapeDtypeStruct((nblk * tm, LANES), f32)
    zeros = jnp.zeros((nblk * tm, LANES), f32)
    anyspec = pl.BlockSpec(memory_space=pl.ANY)
    return pl.pallas_call(
        _shared_kv_kernel,
        grid=(T // tm,),
        in_specs=[pl.BlockSpec((tm, K), lambda i: (i, 0)), pl.BlockSpec((1, K), lambda i: (0, 0)),
                  pl.BlockSpec((K, 6 * LANES), lambda i: (0, 0)), table, table, anyspec, anyspec],
        out_specs=(row,) * 6 + (padded, padded),
        out_shape=(out,) * 6 + (outp, outp),
        input_output_aliases={5: 6, 6: 7},
        compiler_params=_cparams(("parallel",)),
        name="shared_kv",
    )(H, g, w, cos, sin, zeros, zeros)


def _page_strips_kernel(*refs, npages):
    refs = refs[1:]
    ok_ref, ov_ref, scr_ref = refs[2 * npages:]
    per = PAGE // CMP_STRIDE
    for j in range(npages):
        for src, dst in ((refs[j], ok_ref), (refs[npages + j], ov_ref)):
            scr_ref[...] = src[...].T
            for l in range(CMP_STRIDE):
                dst[j * per:(j + 1) * per, l * LANES:(l + 1) * LANES] = scr_ref[pl.ds(l, per, stride=CMP_STRIDE), :]


def page_strips(page_table, pool_k, pool_v, nb):
    npages = page_table.shape[0] // nb
    n = npages * PAGE // CMP_STRIDE
    page = lambda j: pl.BlockSpec((None, LANES, PAGE), lambda b, pt: (pt[b * npages + j], 0, 0))
    out_spec = pl.BlockSpec((None, n, CMP_STRIDE * LANES), lambda b, pt: (b, 0, 0))
    out = jax.ShapeDtypeStruct((nb, n, CMP_STRIDE * LANES), f32)
    grid_spec = pltpu.PrefetchScalarGridSpec(
        num_scalar_prefetch=1, grid=(nb,),
        in_specs=[page(j) for j in range(npages)] * 2, out_specs=(out_spec, out_spec),
        scratch_shapes=[pltpu.VMEM((PAGE, LANES), f32)])
    return pl.pallas_call(
        functools.partial(_page_strips_kernel, npages=npages), grid_spec=grid_spec, out_shape=(out, out),
        compiler_params=_cparams(("parallel",)), name="page_strips",
    )(page_table, *([pool_k] * npages), *([pool_v] * npages))


def _compress_kernel(k_ref, v_ref, pek_ref, pev_ref, w1k_ref, w1v_ref, w2k_ref, w2v_ref, cos_ref, sin_ref,
                     ck_ref, cv_ref):
    k_strips = k_ref[...]
    v_strips = v_ref[...]
    n = k_strips.shape[0]
    half = 2 * CMP_HID
    row = lax.broadcasted_iota(jnp.int32, (n, LANES), 0)

    def tokens(strips, pe_ref, w1_ref, w2_ref):
        top = _mm((strips + pe_ref[0:1, :]).astype(bf16), w1_ref[:, 0:half])
        bot = _mm((strips + pe_ref[1:2, :]).astype(bf16), w1_ref[:, half:2 * half])
        hid = top + pltpu.roll(bot, n - 1, axis=0)
        out = _mm(_silu(hid).astype(bf16), w2_ref[...])
        return jnp.where(row < n - 1, out, 0.0)

    ck_ref[...] = _rope_lanes(tokens(k_strips, pek_ref, w1k_ref, w2k_ref), cos_ref[...], sin_ref[...])
    cv_ref[...] = tokens(v_strips, pev_ref, w1v_ref, w2v_ref)


def compress(k_src, v_src, consts):
    nb, n, flat = k_src.shape
    pek, pev, w1k, w1v, w2k, w2v, cos, sin = consts
    const2 = lambda shape: pl.BlockSpec(shape, lambda b: (0, 0))
    seq = pl.BlockSpec((None, n, flat), lambda b: (b, 0, 0))
    out_spec = pl.BlockSpec((None, n, LANES), lambda b: (b, 0, 0))
    out = jax.ShapeDtypeStruct((nb, n, LANES), f32)
    return pl.pallas_call(
        _compress_kernel, grid=(nb,),
        in_specs=[seq, seq, const2((2, flat)), const2((2, flat)), const2((flat, 4 * CMP_HID)),
                  const2((flat, 4 * CMP_HID)), const2((2 * CMP_HID, LANES)), const2((2 * CMP_HID, LANES)),
                  const2((n, LANES)), const2((n, LANES))],
        out_specs=(out_spec, out_spec), out_shape=(out, out),
        compiler_params=_cparams(("parallel",)), name="compress",
    )(k_src, v_src, pek, pev, w1k, w1v, w2k, w2v, cos, sin)


def _nsa_queries(p, cos, sin):
    tq = p.shape[0]
    lane = lax.broadcasted_iota(jnp.int32, (tq, LANES), 1)
    rows = []
    for head in range(B_HEADS):
        grp = head // HPG
        blk = MEM_W // LANES + head // 2
        x = _rope_lanes(p[:, blk * LANES:(blk + 1) * LANES], cos, sin) * (HEAD_DIM ** -0.5)
        if head % 2 != grp:
            x = pltpu.roll(x, HEAD_DIM, axis=1)
        keep = (lane >= grp * HEAD_DIM) & (lane < (grp + 1) * HEAD_DIM)
        rows.append(jnp.where(keep, x, 0.0))
    return jnp.concatenate(rows, axis=0)


def _masked_softmax_parts(s, mask):
    sm = jnp.where(mask, s, NEG)
    m = jnp.max(sm, axis=-1, keepdims=True)
    p = jnp.where(mask, jnp.exp(sm - m), 0.0)
    return p, jnp.sum(p, axis=-1, keepdims=True)


def _safe_div(o, l):
    return jnp.where(l > 0.0, o / jnp.where(l > 0.0, l, 1.0), 0.0)


def _split3(x):
    hi = x.astype(bf16)
    r1 = x - hi.astype(f32)
    mid = r1.astype(bf16)
    lo = (r1 - mid.astype(f32)).astype(bf16)
    return hi, mid, lo


def _select_bias(pcn, qpos_row, past_blocks, mt_ref, eye_ref, nsel):
    tq = qpos_row.shape[1]
    nr = -(-nsel // SUBLANES) * SUBLANES
    mt = mt_ref[0:nr, :]
    blk = lax.broadcasted_iota(jnp.int32, (nr, tq), 0)
    lane = lax.broadcasted_iota(jnp.int32, (tq, LANES), 1)
    cur = qpos_row // SEL_BLK
    forced = (blk == 0) | (blk == cur) | (blk == cur - 1)
    valid = blk * SEL_BLK <= qpos_row
    stacks, picks = [], []
    for grp in range(2):
        psum = pcn[grp * HPG * tq:(grp * HPG + 1) * tq]
        for hh in range(1, HPG):
            psum = psum + pcn[(grp * HPG + hh) * tq:(grp * HPG + hh + 1) * tq]
        hi, mid, lo = _split3(psum)
        imp = _nt(mt, hi) + _nt(mt, mid) + _nt(mt, lo)
        imp = jnp.where(forced, 1e9, jnp.where(valid, imp, -1e9))
        rank = jnp.zeros((nr, tq), f32)
        for j in range(nsel):
            rj = imp[j:j + 1, :]
            rank = rank + jnp.where(rj > imp, 1.0, jnp.where(rj == imp, jnp.where(blk > j, 1.0, 0.0), 0.0))
        sel_t = jnp.where((rank < SEL_TOP) & (blk < nsel), 1.0, 0.0).astype(bf16)
        pick = _tn(sel_t, eye_ref[0:nr, :])
        picks.append(pick)
        stacks += [jnp.where((pick > 0.5) & (lane < past_blocks), 0.0, NEG)] * HPG
    return jnp.concatenate(stacks, axis=0), picks


def _nsa_combine(p, o_c, o_s, o_w):
    tq = p.shape[0]
    gl = _sigmoid(p[:, (MEM_W + B_W):(MEM_W + B_W) + LANES])

    def gate(j):
        return jnp.concatenate([gl[:, 3 * h + j:3 * h + j + 1] for h in range(B_HEADS)], axis=0)

    comb = gate(0) * o_c + gate(1) * o_s + gate(2) * o_w
    lane = lax.broadcasted_iota(jnp.int32, (tq, LANES), 1)
    cols = []
    for c in range(B_HEADS // 2):
        lo = comb[2 * c * tq:(2 * c + 1) * tq]
        hi = comb[(2 * c + 1) * tq:(2 * c + 2) * tq]
        if (2 * c) // HPG == 1:
            lo = pltpu.roll(lo, HEAD_DIM, axis=1)
        if (2 * c + 1) // HPG == 0:
            hi = pltpu.roll(hi, HEAD_DIM, axis=1)
        cols.append(jnp.where(lane < HEAD_DIM, lo, hi))
    return jnp.concatenate(cols, axis=1)


def _nsa_prompt_kernel(p_ref, cos_ref, sin_ref, ck_ref, cv_ref, sk_ref, sv_ref, wk_ref, wv_ref, mt_ref, et_ref,
                       eye_ref, o_ref, *, tq, kc, nsel):
    qb = pl.program_id(1)
    s0 = qb * tq
    R = B_HEADS * tq
    p = p_ref[...]
    qf = _nsa_queries(p, cos_ref[...], sin_ref[...])
    qb16 = qf.astype(bf16)
    ii = lax.broadcasted_iota(jnp.int32, (tq, tq), 0)
    jj = lax.broadcasted_iota(jnp.int32, (tq, tq), 1)
    qposR = s0 + jnp.concatenate([lax.broadcasted_iota(jnp.int32, (tq, 1), 0)] * B_HEADS, axis=0)
    lane = lax.broadcasted_iota(jnp.int32, (R, LANES), 1)
    cmask = ((lane * CMP_STRIDE + (CMP_BLK - 1)) <= qposR) & (lane < ck_ref.shape[0] - 1)
    pc, lc = _masked_softmax_parts(_nt(qb16, ck_ref[...].astype(bf16)), cmask)
    pcn = _safe_div(pc, lc)
    o_c = _mm(pcn.astype(bf16), cv_ref[...].astype(bf16))
    qpos_row = s0 + lax.broadcasted_iota(jnp.int32, (1, tq), 1)
    bias, _ = _select_bias(pcn, qpos_row, s0 // SEL_BLK, mt_ref, eye_ref, nsel)
    qsel = jnp.concatenate([qb16, bias.astype(bf16)], axis=1)
    sd = _nt(qb16, sk_ref[pl.ds(s0, tq), :].astype(bf16)).reshape(B_HEADS, tq, tq)
    sd = jnp.where((ii >= jj)[None], sd, NEG).reshape(R, tq)
    m0 = jnp.max(sd, axis=-1, keepdims=True)
    pd = jnp.exp(sd - m0)
    init = (m0, jnp.sum(pd, axis=-1, keepdims=True), _mm(pd.astype(bf16), sv_ref[pl.ds(s0, tq), :].astype(bf16)))

    def body(c, carry):
        m_run, l_run, acc = carry
        k0 = pl.multiple_of(c * kc, kc)
        ka = jnp.concatenate([sk_ref[pl.ds(k0, kc), :].astype(bf16), et_ref[pl.ds(k0, kc), :]], axis=1)
        s = _nt(qsel, ka)
        m_new = jnp.maximum(m_run, jnp.max(s, axis=-1, keepdims=True))
        alpha = jnp.exp(m_run - m_new)
        pe = jnp.exp(s - m_new)
        l_new = alpha * l_run + jnp.sum(pe, axis=-1, keepdims=True)
        acc_new = alpha * acc + _mm(pe.astype(bf16), sv_ref[pl.ds(k0, kc), :].astype(bf16))
        return m_new, l_new, acc_new

    _, l_run, acc = lax.fori_loop(0, (s0 + kc - 1) // kc, body, init)
    o_s = acc / l_run
    wlen = WINDOW + tq
    wrow = lax.broadcasted_iota(jnp.int32, (wlen, LANES), 0)
    wlane = lax.broadcasted_iota(jnp.int32, (wlen, LANES), 1)
    is_pad = jnp.where((wrow + (s0 - WINDOW) < 0) & (wlane == 0), 1.0, 0.0).astype(bf16)
    kw = jnp.concatenate([wk_ref[pl.ds(s0, wlen), :].astype(bf16), is_pad], axis=1)
    qwin = jnp.concatenate([qb16, jnp.where(lane == 0, NEG, 0.0).astype(bf16)], axis=1)
    sw = _nt(qwin, kw).reshape(B_HEADS, tq, wlen)
    far = jnp.where((jj > ii)[None], sw[:, :, 0:tq], NEG)
    near = jnp.where((jj <= ii)[None], sw[:, :, wlen - tq:wlen], NEG)
    sw = jnp.concatenate([far, sw[:, :, tq:wlen - tq], near], axis=2).reshape(R, wlen)
    pw = jnp.exp(sw - jnp.max(sw, axis=-1, keepdims=True))
    o_w = _mm(pw.astype(bf16), wv_ref[pl.ds(s0, wlen), :].astype(bf16)) / jnp.sum(pw, axis=-1, keepdims=True)
    o_ref[...] = _nsa_combine(p, o_c, o_s, o_w)


def nsa_prompt(P, cos, sin, ck, cv, sk, sv, wk_pad, wv_pad, mt, et, eye, nb, L, tq, kc):
    nq = L // tq
    nsel = L // SEL_BLK
    seq = pl.BlockSpec((L, LANES), lambda b, i: (b, 0))
    win = pl.BlockSpec((WINDOW + L, LANES), lambda b, i: (b, 0))
    cmp_spec = pl.BlockSpec((None, LANES, LANES), lambda b, i: (b, 0, 0))
    const = lambda shape: pl.BlockSpec(shape, lambda b, i: (0, 0))
    return pl.pallas_call(
        functools.partial(_nsa_prompt_kernel, tq=tq, kc=kc, nsel=nsel),
        grid=(nb, nq),
        in_specs=[pl.BlockSpec((tq, B_IN_PAD), lambda b, i: (b * nq + i, 0)),
                  pl.BlockSpec((tq, LANES), lambda b, i: (i, 0)),
                  pl.BlockSpec((tq, LANES), lambda b, i: (i, 0)),
                  cmp_spec, cmp_spec, seq, seq, win, win,
                  const((LANES, LANES)), const((L, LANES)), const((LANES, LANES))],
        out_specs=pl.BlockSpec((tq, B_W), lambda b, i: (b * nq + i, 0)),
        out_shape=jax.ShapeDtypeStruct((nb * L, B_W), f32),
        compiler_params=_cparams(("parallel", "parallel")),
        name="nsa_prompt",
    )(P, cos, sin, ck, cv, sk, sv, wk_pad, wv_pad, mt, et, eye)


def _nsa_sample_kernel(*refs, npages, past):
    refs = refs[1:]
    kpages = refs[0:npages]
    vpages = refs[npages:2 * npages]
    (p_ref, cos_ref, sin_ref, ck_ref, cv_ref, nsk_ref, nsv_ref, cwk_ref, cwv_ref, nwk_ref, nwv_ref,
     mt_ref, et_ref, eye_ref, o_ref) = refs[2 * npages:]
    tq = S_PAD
    R = B_HEADS * tq
    p = p_ref[...]
    qb16 = _nsa_queries(p, cos_ref[...], sin_ref[...]).astype(bf16)
    tR = jnp.concatenate([lax.broadcasted_iota(jnp.int32, (tq, 1), 0)] * B_HEADS, axis=0)
    qposR = past + tR
    lane = lax.broadcasted_iota(jnp.int32, (R, LANES), 1)
    cmask = ((lane * CMP_STRIDE + (CMP_BLK - 1)) <= qposR) & (lane < ck_ref.shape[0] - 1)
    pc, lc = _masked_softmax_parts(_nt(qb16, ck_ref[...].astype(bf16)), cmask)
    pcn = _safe_div(pc, lc)
    o_c = _mm(pcn.astype(bf16), cv_ref[...].astype(bf16))
    qpos_row = past + lax.broadcasted_iota(jnp.int32, (1, tq), 1)
    new_blk = past // SEL_BLK
    bias, picks = _select_bias(pcn, qpos_row, new_blk, mt_ref, eye_ref, new_blk + 1)
    new_ok = lax.broadcasted_iota(jnp.int32, (R, tq), 1) <= tR
    ka = jnp.concatenate([jnp.concatenate([r[...] for r in kpages], axis=1).astype(bf16), et_ref[...]], axis=0)
    s_old = _mm(jnp.concatenate([qb16, bias.astype(bf16)], axis=1), ka)
    new_kept = jnp.concatenate([picks[h // HPG][:, new_blk:new_blk + 1] for h in range(B_HEADS)], axis=0) > 0.5
    s_new = jnp.where(new_ok, jnp.where(new_kept, _nt(qb16, nsk_ref[...].astype(bf16)), NEG), NEG)
    m = jnp.maximum(jnp.max(s_old, axis=-1, keepdims=True), jnp.max(s_new, axis=-1, keepdims=True))
    p_old = jnp.exp(s_old - m)
    p_new = jnp.exp(s_new - m)
    l = jnp.sum(p_old, axis=-1, keepdims=True) + jnp.sum(p_new, axis=-1, keepdims=True)
    svb = jnp.concatenate([r[...] for r in vpages], axis=1).astype(bf16)
    o_s = (_nt(p_old.astype(bf16), svb) + _mm(p_new.astype(bf16), nsv_ref[...].astype(bf16))) / l
    wb = cwk_ref.shape[1]
    wdist = qposR - (past - wb + lax.broadcasted_iota(jnp.int32, (R, wb), 1))
    w_old = jnp.where((wdist >= 0) & (wdist < WINDOW), _mm(qb16, cwk_ref[...].astype(bf16)), NEG)
    w_new = jnp.where(new_ok, _nt(qb16, nwk_ref[...].astype(bf16)), NEG)
    mw = jnp.maximum(jnp.max(w_old, axis=-1, keepdims=True), jnp.max(w_new, axis=-1, keepdims=True))
    pw_old = jnp.exp(w_old - mw)
    pw_new = jnp.exp(w_new - mw)
    lw = jnp.sum(pw_old, axis=-1, keepdims=True) + jnp.sum(pw_new, axis=-1, keepdims=True)
    o_w = (_nt(pw_old.astype(bf16), cwv_ref[...].astype(bf16)) + _mm(pw_new.astype(bf16), nwv_ref[...].astype(bf16))) / lw
    o_ref[...] = _nsa_combine(p, o_c, o_s, o_w)


def nsa_sample(P, row0, page_table, pool_k, pool_v, cos, sin, ck, cv, nsk, nsv, cwk, cwv, nwk, nwv, mt, et, eye):
    nb = cwk.shape[0]
    npages = page_table.shape[0] // nb
    past = npages * PAGE
    rb0 = row0 // S_PAD
    page = lambda j: pl.BlockSpec((None, LANES, PAGE), lambda b, pt: (pt[b * npages + j], 0, 0))
    new_rows = pl.BlockSpec((S_PAD, LANES), lambda b, pt: (rb0 + b, 0))
    per_seq = lambda n: pl.BlockSpec((None, LANES, n), lambda b, pt: (b, 0, 0))
    in_specs = ([page(j) for j in range(npages)] + [page(j) for j in range(npages)]
                + [pl.BlockSpec((S_PAD, B_IN_PAD), lambda b, pt: (rb0 + b, 0)),
                   pl.BlockSpec((S_PAD, LANES), lambda b, pt: (0, 0)),
                   pl.BlockSpec((S_PAD, LANES), lambda b, pt: (0, 0)),
                   per_seq(LANES), per_seq(LANES), new_rows, new_rows,
                   per_seq(cwk.shape[2]), per_seq(cwk.shape[2]), new_rows, new_rows,
                   pl.BlockSpec((LANES, LANES), lambda b, pt: (0, 0)),
                   pl.BlockSpec((LANES, past), lambda b, pt: (0, 0)),
                   pl.BlockSpec((LANES, LANES), lambda b, pt: (0, 0))])
    grid_spec = pltpu.PrefetchScalarGridSpec(
        num_scalar_prefetch=1, grid=(nb,), in_specs=in_specs,
        out_specs=pl.BlockSpec((S_PAD, B_W), lambda b, pt: (b, 0)))
    return pl.pallas_call(
        functools.partial(_nsa_sample_kernel, npages=npages, past=past),
        grid_spec=grid_spec,
        out_shape=jax.ShapeDtypeStruct((nb * S_PAD, B_W), f32),
        compiler_params=_cparams(("parallel",)),
        name="nsa_sample",
    )(page_table, *([pool_k] * npages), *([pool_v] * npages), P, cos, sin, ck, cv, nsk, nsv, cwk, cwv, nwk, nwv,
      mt, et, eye)


def _rope_tables(pos):
    half = HEAD_DIM // 2
    inv = ROPE_THETA ** (-jnp.arange(half, dtype=f32) / half)
    ang = pos.astype(f32)[:, None] * inv[None, :]
    cos = jnp.tile(jnp.cos(ang), (1, LANES // half))
    sin = jnp.tile(jnp.sin(ang), (1, LANES // half))
    sign = jnp.where((jnp.arange(LANES) % HEAD_DIM) < half, -1.0, 1.0).astype(f32)
    return cos, sin * sign[None, :]


def _cmp_to_sel(ncmp, nsel):
    per = SEL_BLK // CMP_STRIDE
    sub = CMP_BLK // CMP_STRIDE
    i = jnp.arange(LANES)[:, None]
    j = jnp.arange(LANES)[None, :]
    m = jnp.zeros((LANES, LANES), f32)
    for r in range(sub):
        m = m + (((i + r) // per) == j).astype(f32)
    m = jnp.where((i < ncmp) & (j < nsel), m / sub, 0.0)
    return m.astype(bf16)


def _block_expand(nkeys):
    j = jnp.arange(LANES)[:, None]
    k = jnp.arange(nkeys)[None, :]
    return ((k // SEL_BLK) == j).astype(bf16)


def _compress_consts(pe, w1, w2):
    w1r = w1.reshape(CMP_BLK, HEAD_DIM, CMP_HID)
    eye2 = jnp.eye(2, dtype=f32)

    def expand(w):
        return jnp.einsum('ldc,gh->lgdhc', w, eye2).reshape(16 * LANES, 2 * CMP_HID)

    w1s = jnp.concatenate([expand(w1r[:16]), expand(w1r[16:])], axis=1).astype(bf16)
    pes = jnp.stack([jnp.broadcast_to(pe[:16, None, :], (16, 2, HEAD_DIM)).reshape(-1),
                     jnp.broadcast_to(pe[16:, None, :], (16, 2, HEAD_DIM)).reshape(-1)])
    w2s = jnp.einsum('cd,gh->gchd', w2, eye2).reshape(2 * CMP_HID, LANES).astype(bf16)
    return pes, w1s, w2s


def kernel(x_prompt, x_sample, mem_prompt, state_conv, state_ssm, cache_cmp_k, cache_cmp_v, cache_sel_k, cache_sel_v, cache_win_k, cache_win_v, cache_mem_k, cache_mem_v, page_table, norm_mix, norm_ffn, norm_mem, w_mem_kv, w_in_a, conv_w_a, a_log, dt_bias, norm_out_a, w_out_a, w_in_b, w_out_b, norm_kv, w_kv_shared, cmp_pe_k, cmp_w1_k, cmp_w2_k, cmp_pe_v, cmp_w1_v, cmp_w2_v, w_gu_dense, w_down_dense, w_router, w_gu_exp, w_down_exp, norm_final):
    bp, lp, d = x_prompt.shape
    bs, ls, _ = x_sample.shape
    depth = norm_mix.shape[0]
    n_a = w_in_a.shape[0]
    past = page_table.shape[1] * PAGE
    tp = bp * lp
    T = tp + bs * S_PAD
    tm = 512

    xs = jnp.pad(x_sample, ((0, 0), (0, S_PAD - ls), (0, 0)))
    H = jnp.concatenate([x_prompt.reshape(tp, d), xs.reshape(bs * S_PAD, d)], axis=0)

    qkvz = MEM_W + CONV_DIM + A_W
    w_in_a_b = jnp.concatenate(
        [w_in_a[:, :, MEM_W:qkvz], w_in_a[:, :, :MEM_W], w_in_a[:, :, qkvz:],
         jnp.zeros(w_in_a.shape[:2] + (A_IN_PAD - w_in_a.shape[2],), f32)], axis=2).astype(bf16)
    w_in_b_b = jnp.pad(w_in_b, ((0, 0), (0, 0), (0, B_IN_PAD - w_in_b.shape[2]))).astype(bf16)
    w_out_a_b = w_out_a.astype(bf16)
    w_out_b_b = w_out_b.astype(bf16)
    w_gu_dense_b = w_gu_dense.astype(bf16)
    w_down_dense_b = w_down_dense.astype(bf16)
    w_gu_exp_b = w_gu_exp.astype(bf16)
    w_down_exp_b = w_down_exp.astype(bf16)
    w_router_p = jnp.pad(w_router, ((0, 0), (0, 0), (0, LANES - N_EXPERTS)))

    mem_k, mem_v = mem_kv(mem_prompt.reshape(bp * MEM_LEN, d), norm_mem.reshape(depth, 1, d),
                          w_mem_kv.astype(bf16), tm)
    cmk = jnp.transpose(cache_mem_k, (0, 1, 3, 4, 2)).reshape(depth, bs, MEM_W, MEM_LEN)
    cmv = jnp.transpose(cache_mem_v, (0, 1, 3, 4, 2)).reshape(depth, bs, MEM_W, MEM_LEN)
    zero_buf = jnp.zeros((bp, SUBLANES, CONV_DIM), f32)
    zero_state = jnp.zeros((1, bp, A_HEADS, LANES, LANES), f32)

    pos_p = jnp.arange(lp, dtype=jnp.int32)
    pos_s = past + jnp.arange(S_PAD, dtype=jnp.int32)
    cos_p, sin_p = _rope_tables(pos_p)
    cos_s, sin_s = _rope_tables(pos_s)

    conv_p, ssm_p, conv_s, ssm_s = [], [], [], []
    kv = None
    for layer in range(depth):
        is_a = layer < n_a
        g_mix = norm_mix[layer].reshape(1, d)
        if is_a:
            P = norm_matmul(H, g_mix, w_in_a_b, layer, tm, A_IN_PAD // 3)
            qblk = A_MEMQ_BLK
        else:
            lb = layer - n_a
            qblk = 0
            if lb == 0:
                cos_all = jnp.concatenate([cos_p, jnp.tile(cos_s, (tm // S_PAD, 1))], axis=0)
                sin_all = jnp.concatenate([sin_p, jnp.tile(sin_s, (tm // S_PAD, 1))], axis=0)
                kv = shared_kv(H, norm_kv.reshape(1, d), w_kv_shared.astype(bf16), cos_all, sin_all, bp, lp, tm)
                wk_pad, wv_pad = kv[6], kv[7]
                ncmp = (lp - CMP_BLK) // CMP_STRIDE + 1
                cpos = jnp.arange(LANES, dtype=jnp.int32) * CMP_STRIDE + CMP_BLK - 1
                cos_c, sin_c = _rope_tables(cpos)
                pek, w1k, w2k = _compress_consts(cmp_pe_k, cmp_w1_k, cmp_w2_k)
                pev, w1v, w2v = _compress_consts(cmp_pe_v, cmp_w1_v, cmp_w2_v)
                consts = (pek, pev, w1k, w1v, w2k, w2v, cos_c, sin_c)
                strips = lp // 16
                ck_p, cv_p = compress(kv[0][:tp].reshape(bp, strips, 16 * LANES),
                                      kv[1][:tp].reshape(bp, strips, 16 * LANES), consts)
                pt_flat = page_table.reshape(-1)
                feature_major = lambda pool: jnp.transpose(pool, (0, 2, 3, 1)).reshape(-1, LANES, PAGE)
                ck_s, cv_s = compress(*page_strips(pt_flat, feature_major(cache_cmp_k), feature_major(cache_cmp_v), bs),
                                      consts)
                mt_p = _cmp_to_sel(ncmp, lp // SEL_BLK).T
                et_p = _block_expand(lp).T
                ncmp_s = (past + ls - CMP_BLK) // CMP_STRIDE + 1
                mt_s = _cmp_to_sel(ncmp_s, -(-(past + ls) // SEL_BLK)).T
                e_s = _block_expand(past)
                eye = jnp.eye(LANES, dtype=bf16)
                pool_sk = feature_major(cache_sel_k)
                pool_sv = feature_major(cache_sel_v)
                cwk = jnp.transpose(cache_win_k, (0, 2, 3, 1)).reshape(bs, LANES, -1)
                cwv = jnp.transpose(cache_win_v, (0, 2, 3, 1)).reshape(bs, LANES, -1)
            P = norm_matmul(H, g_mix, w_in_b_b, lb, tm, B_IN_PAD)
        om = (mem_attn_prompt(P, qblk, mem_k, mem_v, layer, bp, lp, 512),
              mem_attn_sample(P, qblk, tp, cmk, cmv, layer, 8))
        if is_a:
            conv8 = jnp.pad(conv_w_a[layer], ((0, SUBLANES - conv_w_a.shape[1]), (0, 0)))
            ab8 = jnp.zeros((SUBLANES, LANES), f32)
            ab8 = ab8.at[0, A_HEADS:2 * A_HEADS].set(a_log[layer]).at[1, A_HEADS:2 * A_HEADS].set(dt_bias[layer])
            gout = norm_out_a[layer].reshape(1, LANES)
            mo_p, sp = delta_mixer(P, 0, bp, lp, 512, 64, 512, zero_buf, conv8, ab8, gout, zero_state, 0)
            buf_s = jnp.pad(state_conv[layer], ((0, 0), (SUBLANES - state_conv.shape[2], 0), (0, 0)))
            mo_s, ss = delta_mixer(P, tp, bs, S_PAD, S_PAD, S_PAD, ls, buf_s, conv8, ab8, gout, state_ssm, layer)
            conv_p.append(jnp.stack([P[b * lp + lp - 3:(b + 1) * lp, :CONV_DIM] for b in range(bp)]))
            conv_s.append(P[tp:, :CONV_DIM].reshape(bs, S_PAD, CONV_DIM)[:, ls - 3:ls])
            ssm_p.append(sp)
            ssm_s.append(ss)
            w_out, wl = w_out_a_b, layer
        else:
            mo_p = nsa_prompt(P, cos_p, sin_p, ck_p, cv_p, kv[2], kv[3], wk_pad, wv_pad, mt_p, et_p, eye,
                              bp, lp, Q_BLOCK, 512)
            mo_s = nsa_sample(P, tp, pt_flat, pool_sk, pool_sv, cos_s, sin_s, ck_s, cv_s, kv[2], kv[3], cwk, cwv,
                              kv[4], kv[5], mt_s, e_s, eye)
            w_out, wl = w_out_b_b, lb
        mo = (mo_p, mo_s)
        g_ffn = norm_ffn[layer].reshape(1, d)
        if layer % 2 == 0:
            H = outproj_ffn(H, mo, om, w_out, g_ffn, w_gu_dense_b, w_down_dense_b, wl, layer // 2, tm, D_FF // 2)
        else:
            H = outproj_moe(H, mo, om, w_out, g_ffn, w_router_p, w_gu_exp_b, w_down_exp_b, wl, layer // 2,
                            tm, D_FF_EXPERT // 2)
    Y = final_norm(H, norm_final.reshape(1, d), tm)

    def rows_p(a):
        return a[:tp].reshape(bp, lp, 2, HEAD_DIM)

    def rows_s(a):
        return a[tp:].reshape(bs, S_PAD, 2, HEAD_DIM)[:, :ls]

    wlen = min(WINDOW, lp)
    return (Y[:tp].reshape(bp, lp, d), Y[tp:].reshape(bs, S_PAD, d)[:, :ls],
            jnp.stack(conv_p), jnp.stack(ssm_p),
            rows_p(kv[0]), rows_p(kv[1]), rows_p(kv[2]), rows_p(kv[3]),
            rows_p(kv[4])[:, lp - wlen:], rows_p(kv[5])[:, lp - wlen:],
            mem_k.reshape(depth, bp, MEM_LEN, 4, HEAD_DIM), mem_v.reshape(depth, bp, MEM_LEN, 4, HEAD_DIM),
            jnp.stack(conv_s), jnp.stack(ssm_s),
            rows_s(kv[0]), rows_s(kv[1]), rows_s(kv[2]), rows_s(kv[3]),
            jnp.concatenate([cache_win_k[:, ls:], rows_s(kv[4])], axis=1),
            jnp.concatenate([cache_win_v[:, ls:], rows_s(kv[5])], axis=1))
```

```python
import functools
import math

import jax
import jax.numpy as jnp
from jax import lax
from jax.experimental import pallas as pl
from jax.experimental.pallas import tpu as pltpu

f32 = jnp.float32
bf16 = jnp.bfloat16

EPS = 1e-6
NEG = -1e30
ROPE_THETA = 10000.0
HEAD_DIM = 64
LANES = 128
SUBLANES = 8
VMEM_LIMIT = 48 * 1024 * 1024

D_MODEL = 1024
MEM_LEN = 256
MEM_W = 256
A_HEADS = 6
A_W = 768
CONV_DIM = 2304
A_IN_PAD = 3456
A_MEMQ_BLK = 12
A_BA_BLK = 26
B_HEADS = 12
HPG = 6
B_W = 768
B_IN_PAD = 1152
CMP_BLK = 32
CMP_STRIDE = 16
CMP_HID = 256
SEL_BLK = 64
SEL_TOP = 16
WINDOW = 512
Q_BLOCK = 128
D_FF = 2816
N_EXPERTS = 8
D_FF_EXPERT = 3584
PAGE = 128
S_PAD = 8
MOE_PARTS = 4


def _cparams(sem):
    return pltpu.CompilerParams(dimension_semantics=sem, vmem_limit_bytes=VMEM_LIMIT)


def _nt(a, b):
    return lax.dot_general(a, b, (((1,), (1,)), ((), ())), preferred_element_type=f32)


def _tn(a, b):
    return lax.dot_general(a, b, (((0,), (0,)), ((), ())), preferred_element_type=f32)


def _mm(a, b):
    return jnp.dot(a, b, preferred_element_type=f32)


def _rms(x, g):
    return (x * lax.rsqrt(jnp.mean(x * x, axis=-1, keepdims=True) + EPS)) * g


def _sigmoid(x):
    return 1.0 / (1.0 + jnp.exp(-x))


def _silu(x):
    return x * _sigmoid(x)


def _softplus(x):
    return jnp.maximum(x, 0.0) + jnp.log(1.0 + jnp.exp(-jnp.abs(x)))


def _rope_lanes(x, cos, sin_signed):
    lane = lax.broadcasted_iota(jnp.int32, x.shape, x.ndim - 1)
    lo = (lane % HEAD_DIM) < (HEAD_DIM // 2)
    partner = jnp.where(lo, pltpu.roll(x, LANES - HEAD_DIM // 2, axis=x.ndim - 1),
                        pltpu.roll(x, HEAD_DIM // 2, axis=x.ndim - 1))
    return x * cos + partner * sin_signed


def _norm_matmul_kernel(x_ref, g_ref, w_ref, o_ref, xn_ref):
    @pl.when(pl.program_id(1) == 0)
    def _():
        xn_ref[...] = _rms(x_ref[...], g_ref[...]).astype(bf16)

    o_ref[...] = _mm(xn_ref[...], w_ref[...])


def norm_matmul(x, g, w, layer, tm, tn):
    T, K = x.shape
    N = w.shape[2]
    return pl.pallas_call(
        _norm_matmul_kernel,
        grid=(T // tm, N // tn),
        in_specs=[pl.BlockSpec((tm, K), lambda i, j: (i, 0)),
                  pl.BlockSpec((1, K), lambda i, j: (0, 0)),
                  pl.BlockSpec((None, K, tn), lambda i, j: (layer, 0, j))],
        out_specs=pl.BlockSpec((tm, tn), lambda i, j: (i, j)),
        out_shape=jax.ShapeDtypeStruct((T, N), f32),
        scratch_shapes=[pltpu.VMEM((tm, K), bf16)],
        compiler_params=_cparams(("parallel", "arbitrary")),
        name="norm_matmul",
    )(x, g, w)


def _mem_kv_kernel(x_ref, g_ref, w_ref, k_ref, v_ref):
    y = _mm(_rms(x_ref[...], g_ref[...]).astype(bf16), w_ref[...])
    k_ref[...] = y[:, :MEM_W]
    v_ref[...] = y[:, MEM_W:]


def mem_kv(mem, g, w, tm):
    R, K = mem.shape
    NL = w.shape[0]
    out = jax.ShapeDtypeStruct((NL, R, MEM_W), f32)
    return pl.pallas_call(
        _mem_kv_kernel,
        grid=(NL, R // tm),
        in_specs=[pl.BlockSpec((tm, K), lambda l, i: (i, 0)),
                  pl.BlockSpec((None, 1, K), lambda l, i: (l, 0, 0)),
                  pl.BlockSpec((None, K, 2 * MEM_W), lambda l, i: (l, 0, 0))],
        out_specs=(pl.BlockSpec((None, tm, MEM_W), lambda l, i: (l, i, 0)),
                   pl.BlockSpec((None, tm, MEM_W), lambda l, i: (l, i, 0))),
        out_shape=(out, out),
        compiler_params=_cparams(("parallel", "parallel")),
        name="mem_kv",
    )(mem, g, w)


def _final_norm_kernel(x_ref, g_ref, o_ref):
    o_ref[...] = _rms(x_ref[...], g_ref[...])


def final_norm(x, g, tm):
    T, K = x.shape
    return pl.pallas_call(
        _final_norm_kernel,
        grid=(T // tm,),
        in_specs=[pl.BlockSpec((tm, K), lambda i: (i, 0)), pl.BlockSpec((1, K), lambda i: (0, 0))],
        out_specs=pl.BlockSpec((tm, K), lambda i: (i, 0)),
        out_shape=jax.ShapeDtypeStruct((T, K), f32),
        compiler_params=_cparams(("parallel",)),
        name="final_norm",
    )(x, g)


def _mem_attend(q, k, v, transposed=False):
    tq = q.shape[0]
    nh = MEM_W // HEAD_DIM
    lane = lax.broadcasted_iota(jnp.int32, (tq, MEM_W), 1)
    masks = [(lane >= HEAD_DIM * h) & (lane < HEAD_DIM * (h + 1)) for h in range(nh)]
    qs = jnp.concatenate([jnp.where(m, q, 0.0) for m in masks], axis=0).astype(bf16)
    s = (_mm(qs, k) if transposed else _nt(qs, k)) * (HEAD_DIM ** -0.5)
    p = jnp.exp(s - jnp.max(s, axis=-1, keepdims=True))
    l = jnp.sum(p, axis=-1, keepdims=True)
    pb = p.astype(bf16)
    o = (_nt(pb, v) if transposed else _mm(pb, v)) / l
    out = jnp.zeros((tq, MEM_W), f32)
    for h in range(nh):
        out = jnp.where(masks[h], o[h * tq:(h + 1) * tq], out)
    return out


def _mem_attn_prompt_kernel(q_ref, k_ref, v_ref, o_ref):
    o_ref[...] = _mem_attend(q_ref[...], k_ref[...].astype(bf16), v_ref[...].astype(bf16))


def mem_attn_prompt(P, qblk, mk, mv, layer, nb, L, tq):
    nq = L // tq
    return pl.pallas_call(
        _mem_attn_prompt_kernel,
        grid=(nb, nq),
        in_specs=[pl.BlockSpec((tq, MEM_W), lambda b, i: (b * nq + i, qblk)),
                  pl.BlockSpec((None, MEM_LEN, MEM_W), lambda b, i: (layer, b, 0)),
                  pl.BlockSpec((None, MEM_LEN, MEM_W), lambda b, i: (layer, b, 0))],
        out_specs=pl.BlockSpec((tq, MEM_W), lambda b, i: (b * nq + i, 0)),
        out_shape=jax.ShapeDtypeStruct((nb * L, MEM_W), f32),
        compiler_params=_cparams(("parallel", "parallel")),
        name="mem_attn_prompt",
    )(P, mk, mv)


def _mem_attn_sample_kernel(q_ref, k_ref, v_ref, o_ref, *, bb):
    for b in range(bb):
        rows = slice(b * S_PAD, (b + 1) * S_PAD)
        o_ref[rows, :] = _mem_attend(q_ref[rows, :], k_ref[b].astype(bf16), v_ref[b].astype(bf16), transposed=True)


def mem_attn_sample(P, qblk, row0, ckt, cvt, layer, bb):
    nb = ckt.shape[1]
    blk0 = row0 // (bb * S_PAD)
    return pl.pallas_call(
        functools.partial(_mem_attn_sample_kernel, bb=bb),
        grid=(nb // bb,),
        in_specs=[pl.BlockSpec((bb * S_PAD, MEM_W), lambda i: (blk0 + i, qblk)),
                  pl.BlockSpec((None, bb, MEM_W, MEM_LEN), lambda i: (layer, i, 0, 0)),
                  pl.BlockSpec((None, bb, MEM_W, MEM_LEN), lambda i: (layer, i, 0, 0))],
        out_specs=pl.BlockSpec((bb * S_PAD, MEM_W), lambda i: (i, 0)),
        out_shape=jax.ShapeDtypeStruct((nb * S_PAD, MEM_W), f32),
        compiler_params=_cparams(("parallel",)),
        name="mem_attn_sample",
    )(P, ckt, cvt)


def _delta_kernel(q_ref, k_ref, v_ref, z_ref, ba_ref, buf_ref, cw_ref, ab_ref, go_ref, s0_ref,
                  o_ref, sout_ref, qs_ref, ks_ref, vs_ref, gs_ref, bs_ref, tail_ref, s_ref,
                  *, TL, C, valid):
    t = pl.program_id(1)

    @pl.when(t == 0)
    def _():
        tail_ref[...] = buf_ref[...]
        s_ref[...] = s0_ref[...]

    row8 = lax.broadcasted_iota(jnp.int32, (SUBLANES, A_W), 0)

    def conv(x_ref, c0):
        x = x_ref[...]
        b8 = tail_ref[:, c0:c0 + A_W]
        c8 = cw_ref[:, c0:c0 + A_W]
        acc = x * c8[3:4, :]
        for s in range(1, 4):
            r = pltpu.roll(x, s, axis=0)
            fix = jnp.where(row8 < s, pltpu.roll(b8, s, axis=0), r[0:SUBLANES])
            r = jnp.concatenate([fix, r[SUBLANES:]], axis=0) if TL > SUBLANES else fix
            acc = acc + r * c8[3 - s:4 - s, :]
        tail_ref[:, c0:c0 + A_W] = x[TL - SUBLANES:TL]
        return _silu(acc)

    q = conv(q_ref, 0)
    k = conv(k_ref, A_W)
    v = conv(v_ref, 2 * A_W)
    ba = ba_ref[...]
    ab = ab_ref[...]
    beta_all = _sigmoid(ba)
    g_all = -jnp.exp(ab[0:1, :]) * _softplus(ba + ab[1:2, :])
    if valid < TL:
        live = lax.broadcasted_iota(jnp.int32, (TL, A_W), 0) < valid
        live1 = lax.broadcasted_iota(jnp.int32, (TL, LANES), 0) < valid
        k = jnp.where(live, k, 0.0)
        v = jnp.where(live, v, 0.0)
        beta_all = jnp.where(live1, beta_all, 0.0)
        g_all = jnp.where(live1, g_all, 0.0)
    rowc = lax.broadcasted_iota(jnp.int32, (TL, LANES), 0) & (C - 1)
    gc_all = g_all
    s = 1
    while s < C:
        gc_all = gc_all + jnp.where(rowc >= s, pltpu.roll(gc_all, s, axis=0), 0.0)
        s *= 2
    vs_ref[...] = v
    for h in range(A_HEADS):
        sl = slice(h * LANES, (h + 1) * LANES)
        qh = q[:, sl]
        kh = k[:, sl]
        qs_ref[:, sl] = qh * lax.rsqrt(jnp.sum(qh * qh, axis=-1, keepdims=True) + EPS) * (LANES ** -0.5)
        ks_ref[:, sl] = kh * lax.rsqrt(jnp.sum(kh * kh, axis=-1, keepdims=True) + EPS)
        gs_ref[:, sl] = jnp.broadcast_to(gc_all[:, A_HEADS + h:A_HEADS + h + 1], (TL, LANES))
        bs_ref[:, sl] = jnp.broadcast_to(beta_all[:, h:h + 1], (TL, LANES))

    ii = lax.broadcasted_iota(jnp.int32, (C, C), 0)
    jj = lax.broadcasted_iota(jnp.int32, (C, C), 1)
    tri = ii >= jj
    strict = ii > jj
    eye = jnp.where(ii == jj, 1.0, 0.0)
    gout = go_ref[...]
    nsteps = int(math.log2(C)) - 1

    heads = range(A_HEADS)
    lanes_of = [slice(h * LANES, (h + 1) * LANES) for h in heads]

    cpi = 2 if TL // C >= 2 else 1

    def chunks(c, carry):
        rows_of = [pl.ds(pl.multiple_of((c * cpi + i) * C, C), C) for i in range(cpi)]
        units = [(rows, h) for rows in rows_of for h in heads]
        U = range(len(units))
        qc = [qs_ref[rows, lanes_of[h]] for rows, h in units]
        kc = [ks_ref[rows, lanes_of[h]] for rows, h in units]
        vc = [vs_ref[rows, lanes_of[h]] for rows, h in units]
        gcc = [gs_ref[rows, lanes_of[h]] for rows, h in units]
        bc = [bs_ref[rows, lanes_of[h]] for rows, h in units]
        kb = [kc[u] * bc[u] for u in U]
        kq = [_nt(jnp.concatenate([kb[u], qc[u]], axis=0).astype(bf16), kc[u].astype(bf16)) for u in U]
        decay = []
        for u in U:
            gi = gcc[u][:, 0:C]
            gj = jnp.sum(jnp.where(ii == jj, gi, 0.0), axis=0, keepdims=True)
            decay.append(jnp.where(tri, jnp.exp(jnp.where(tri, gi - gj, 0.0)), 0.0))
        n = [jnp.where(strict, -(kq[u][0:C] * decay[u]), 0.0) for u in U]
        a = [jnp.where(tri, kq[u][C:2 * C] * decay[u], 0.0).astype(bf16) for u in U]
        x = [eye + n[u] for u in U]
        p = n
        for _ in range(nsteps):
            pb = [p[u].astype(bf16) for u in U]
            p = [_mm(pb[u], pb[u]) for u in U]
            x = [x[u] + _mm(x[u].astype(bf16), p[u].astype(bf16)) for u in U]
        eg = [jnp.exp(gcc[u]) for u in U]
        uw = [_mm(x[u].astype(bf16), jnp.concatenate([vc[u] * bc[u], kb[u] * eg[u]], axis=1).astype(bf16))
              for u in U]
        wqin = [jnp.concatenate([uw[u][:, LANES:], qc[u] * eg[u]], axis=0).astype(bf16) for u in U]
        glast = [gcc[u][C - 1:C, :] for u in U]
        kg = [(kc[u] * jnp.exp(glast[u] - gcc[u])).astype(bf16) for u in U]
        for i, rows in enumerate(rows_of):
            us = [i * A_HEADS + h for h in heads]
            S = [s_ref[h] for h in heads]
            wq = [_mm(wqin[u], S[h].astype(bf16)) for h, u in zip(heads, us)]
            vnb = [(uw[u][:, :LANES] - wq[h][0:C]).astype(bf16) for h, u in zip(heads, us)]
            av = [_mm(a[u], vnb[h]) for h, u in zip(heads, us)]
            kv_new = [_tn(kg[u], vnb[h]) for h, u in zip(heads, us)]
            for h, u in zip(heads, us):
                s_ref[h] = S[h] * jnp.exp(glast[u]) + kv_new[h]
                o_ref[rows, lanes_of[h]] = _rms(wq[h][C:2 * C] + av[h], gout) * _silu(z_ref[rows, lanes_of[h]])
        return carry

    if TL // C == cpi:
        chunks(0, 0)
    else:
        lax.fori_loop(0, TL // (C * cpi), chunks, 0)

    @pl.when(t == pl.num_programs(1) - 1)
    def _():
        sout_ref[...] = s_ref[...]


def delta_mixer(P, row0, nb, L, TL, C, valid, buf8, conv8, ab8, gout, s0, layer):
    nt = L // TL
    rb0 = row0 // TL
    tile = lambda c: pl.BlockSpec((TL, A_W), lambda b, t: (rb0 + b * nt + t, c))
    state = pl.BlockSpec((None, A_HEADS, LANES, LANES), lambda b, t: (b, 0, 0, 0))
    state_in = pl.BlockSpec((None, None, A_HEADS, LANES, LANES), lambda b, t: (layer, b, 0, 0, 0))
    return pl.pallas_call(
        functools.partial(_delta_kernel, TL=TL, C=C, valid=valid),
        grid=(nb, nt),
        in_specs=[tile(0), tile(1), tile(2), tile(3),
                  pl.BlockSpec((TL, LANES), lambda b, t: (rb0 + b * nt + t, A_BA_BLK)),
                  pl.BlockSpec((None, SUBLANES, CONV_DIM), lambda b, t: (b, 0, 0)),
                  pl.BlockSpec((SUBLANES, CONV_DIM), lambda b, t: (0, 0)),
                  pl.BlockSpec((SUBLANES, LANES), lambda b, t: (0, 0)),
                  pl.BlockSpec((1, LANES), lambda b, t: (0, 0)),
                  state_in],
        out_specs=(pl.BlockSpec((TL, A_W), lambda b, t: (b * nt + t, 0)), state),
        out_shape=(jax.ShapeDtypeStruct((nb * L, A_W), f32),
                   jax.ShapeDtypeStruct((nb, A_HEADS, LANES, LANES), f32)),
        scratch_shapes=[pltpu.VMEM((TL, A_W), f32)] * 5
        + [pltpu.VMEM((SUBLANES, CONV_DIM), f32), pltpu.VMEM((A_HEADS, LANES, LANES), f32)],
        compiler_params=_cparams(("parallel", "arbitrary")),
        name="delta_mixer",
    )(P, P, P, P, P, buf8, conv8, ab8, gout, s0)


def _attn_residual(refs, ntp):
    h_ref, mop_ref, mos_ref, omp_ref, oms_ref, wo1_ref, wo2_ref = refs
    is_p = pl.program_id(0) < ntp
    mo = jnp.where(is_p, mop_ref[...], mos_ref[...])
    om = jnp.where(is_p, omp_ref[...], oms_ref[...])
    return h_ref[...] + _mm(mo.astype(bf16), wo1_ref[...]) + _mm(om.astype(bf16), wo2_ref[...])


def _ffn_kernel(*refs, ntp):
    g_ref, wg_ref, wu_ref, wd_ref, o_ref, xn_ref, acc_ref = refs[7:]
    k = pl.program_id(1)

    @pl.when(k == 0)
    def _():
        h1 = _attn_residual(refs[:7], ntp)
        acc_ref[...] = h1
        xn_ref[...] = _rms(h1, g_ref[...]).astype(bf16)

    xn = xn_ref[...]
    a = _silu(_mm(xn, wg_ref[...])) * _mm(xn, wu_ref[...])
    acc_ref[...] += _mm(a.astype(bf16), wd_ref[...])

    @pl.when(k == pl.num_programs(1) - 1)
    def _():
        o_ref[...] = acc_ref[...]


def _outproj_specs(tm, K, mw, ow, ntp, layer):
    prow = lambda i, *_: (jnp.minimum(i, ntp - 1), 0)
    srow = lambda i, *_: (jnp.maximum(i - ntp, 0), 0)
    return [pl.BlockSpec((tm, K), lambda i, *_: (i, 0)),
            pl.BlockSpec((tm, mw), prow), pl.BlockSpec((tm, mw), srow),
            pl.BlockSpec((tm, ow), prow), pl.BlockSpec((tm, ow), srow),
            pl.BlockSpec((None, mw, K), lambda i, *_: (layer, 0, 0)),
            pl.BlockSpec((None, ow, K), lambda i, *_: (layer, mw // ow, 0)),
            pl.BlockSpec((1, K), lambda i, *_: (0, 0))]


def outproj_ffn(H, mo, om, w_out, g, w_gu, w_down, layer, ffn_layer, tm, tf):
    T, K = H.shape
    FF = w_down.shape[1]
    nk = FF // tf
    mw, ow = mo[0].shape[1], om[0].shape[1]
    ntp = mo[0].shape[0] // tm
    return pl.pallas_call(
        functools.partial(_ffn_kernel, ntp=ntp),
        grid=(T // tm, nk),
        in_specs=_outproj_specs(tm, K, mw, ow, ntp, layer)
        + [pl.BlockSpec((None, K, tf), lambda i, k: (ffn_layer, 0, k)),
           pl.BlockSpec((None, K, tf), lambda i, k: (ffn_layer, 0, nk + k)),
           pl.BlockSpec((None, tf, K), lambda i, k: (ffn_layer, k, 0))],
        out_specs=pl.BlockSpec((tm, K), lambda i, k: (i, 0)),
        out_shape=jax.ShapeDtypeStruct((T, K), f32),
        scratch_shapes=[pltpu.VMEM((tm, K), bf16), pltpu.VMEM((tm, K), f32)],
        compiler_params=_cparams(("parallel", "arbitrary")),
        name="outproj_ffn",
    )(H, mo[0], mo[1], om[0], om[1], w_out, w_out, g, w_gu, w_gu, w_down)


def _router_kernel(*refs, ntp):
    g_ref, wr_ref, h1_ref, xn_ref, r_ref, cnt_ref = refs[7:]
    i = pl.program_id(0)
    tm = h1_ref.shape[0]

    @pl.when(i == 0)
    def _():
        cnt_ref[...] = jnp.zeros_like(cnt_ref)

    h1 = _attn_residual(refs[:7], ntp)
    h1_ref[...] = h1
    xn = _rms(h1, g_ref[...])
    xn_ref[...] = xn
    lane = lax.broadcasted_iota(jnp.int32, (tm, LANES), 1)
    xh = xn.astype(bf16)
    xl = (xn - xh.astype(f32)).astype(bf16)
    wr = wr_ref[...]
    wh = wr.astype(bf16)
    wl = (wr - wh.astype(f32)).astype(bf16)
    logits = _mm(xh, wh) + _mm(xh, wl) + _mm(xl, wh)
    logits = jnp.where(lane < N_EXPERTS, logits, -jnp.inf)
    m1 = jnp.max(logits, axis=-1, keepdims=True)
    i1 = jnp.min(jnp.where(logits == m1, lane, LANES), axis=-1, keepdims=True)
    rest = jnp.where(lane == i1, -jnp.inf, logits)
    m2 = jnp.max(rest, axis=-1, keepdims=True)
    i2 = jnp.min(jnp.where(rest == m2, lane, LANES), axis=-1, keepdims=True)
    e2 = jnp.exp(m2 - m1)
    w1 = 1.0 / (1.0 + e2)
    w2 = e2 / (1.0 + e2)
    hot = jnp.where((lane == i1) | (lane == i2), 1.0, 0.0)
    ri = lax.broadcasted_iota(jnp.int32, (tm, tm), 0)
    ci = lax.broadcasted_iota(jnp.int32, (tm, tm), 1)
    before = jnp.where(ri > ci, 1.0, 0.0).astype(bf16)
    cum = _mm(before, hot.astype(bf16)) + cnt_ref[...]
    r1 = jnp.sum(jnp.where(lane == i1, cum, 0.0), axis=-1, keepdims=True)
    r2 = jnp.sum(jnp.where(lane == i2, cum, 0.0), axis=-1, keepdims=True)
    cnt_ref[...] += jnp.sum(hot, axis=0, keepdims=True)
    cols = (i1.astype(f32), i2.astype(f32), w1, w2, r1, r2)
    r = jnp.zeros((tm, LANES), f32)
    for c, val in enumerate(cols):
        r = jnp.where(lane == c, val, r)
    r_ref[...] = r


def outproj_router(H, mo, om, w_out, g, w_router, layer, moe_layer, tm):
    T, K = H.shape
    mw, ow = mo[0].shape[1], om[0].shape[1]
    ntp = mo[0].shape[0] // tm
    return pl.pallas_call(
        functools.partial(_router_kernel, ntp=ntp),
        grid=(T // tm,),
        in_specs=_outproj_specs(tm, K, mw, ow, ntp, layer)
        + [pl.BlockSpec((None, K, LANES), lambda i: (moe_layer, 0, 0))],
        out_specs=(pl.BlockSpec((tm, K), lambda i: (i, 0)), pl.BlockSpec((tm, K), lambda i: (i, 0)),
                   pl.BlockSpec((tm, LANES), lambda i: (i, 0))),
        out_shape=(jax.ShapeDtypeStruct((T, K), f32), jax.ShapeDtypeStruct((T, K), f32),
                   jax.ShapeDtypeStruct((T, LANES), f32)),
        scratch_shapes=[pltpu.VMEM((1, LANES), f32)],
        compiler_params=_cparams(("arbitrary",)),
        name="outproj_router",
    )(H, mo[0], mo[1], om[0], om[1], w_out, w_out, g, w_router)


def _moe_group_kernel(te_ref, nu_ref, x_ref, sw_ref, wg_ref, wu_ref, wd_ref, *rest, tile0):
    del te_ref
    o_ref, acc_ref = rest[-2:]
    i = pl.program_id(0) + tile0
    k = pl.program_id(1)

    @pl.when(k == 0)
    def _():
        acc_ref[...] = jnp.zeros_like(acc_ref)

    @pl.when(i < nu_ref[0])
    def _():
        x = x_ref[...].astype(bf16)
        a = _silu(_mm(x, wg_ref[...])) * _mm(x, wu_ref[...])
        acc_ref[...] += _mm(a.astype(bf16), wd_ref[...])

    @pl.when(k == pl.num_programs(1) - 1)
    def _():
        o_ref[...] = acc_ref[...] * sw_ref[...]


def moe_grouped(xg, sw, tile_e, n_used, w_gu, w_down, moe_layer, tm, tf, tile0=0, prev=None):
    K = xg.shape[1]
    NP = sw.shape[0]
    FF = w_down.shape[2]
    nk = FF // tf

    def kk(i, k, nu):
        return jnp.where(i + tile0 < nu[0], k, nk - 1)

    ex = lambda i, te: te[i + tile0]
    in_specs = [pl.BlockSpec((tm, K), lambda i, k, te, nu: (i, 0)),
                pl.BlockSpec((tm, 1), lambda i, k, te, nu: (i + tile0, 0)),
                pl.BlockSpec((None, None, K, tf), lambda i, k, te, nu: (moe_layer, ex(i, te), 0, kk(i, k, nu))),
                pl.BlockSpec((None, None, K, tf), lambda i, k, te, nu: (moe_layer, ex(i, te), 0, nk + kk(i, k, nu))),
                pl.BlockSpec((None, None, tf, K), lambda i, k, te, nu: (moe_layer, ex(i, te), kk(i, k, nu), 0))]
    args = [tile_e, n_used, xg, sw, w_gu, w_gu, w_down]
    aliases = {}
    if prev is not None:
        in_specs.append(pl.BlockSpec(memory_space=pl.ANY))
        aliases = {len(args): 0}
        args.append(prev)
    grid_spec = pltpu.PrefetchScalarGridSpec(
        num_scalar_prefetch=2, grid=(xg.shape[0] // tm, nk),
        in_specs=in_specs,
        out_specs=pl.BlockSpec((tm, K), lambda i, k, te, nu: (i + tile0, 0)),
        scratch_shapes=[pltpu.VMEM((tm, K), f32)])
    return pl.pallas_call(
        functools.partial(_moe_group_kernel, tile0=tile0), grid_spec=grid_spec,
        out_shape=jax.ShapeDtypeStruct((NP, K), f32),
        input_output_aliases=aliases,
        compiler_params=_cparams(("parallel", "arbitrary")),
        name="moe_grouped",
    )(*args)


def _dispatch(r, tm):
    T = r.shape[0]
    A = 2 * T
    ntiles = A // tm + N_EXPERTS
    NP = ntiles * tm
    e_flat = r[:, 0:2].astype(jnp.int32).reshape(A)
    w_flat = r[:, 2:4].reshape(A)
    rank = r[:, 4:6].astype(jnp.int32)
    counts = jnp.sum((e_flat[:, None] == jnp.arange(N_EXPERTS)[None, :]).astype(jnp.int32), axis=0)
    pcounts = (counts + tm - 1) // tm * tm
    pends = jnp.cumsum(pcounts)
    pstarts = pends - pcounts
    n_used = pends[-1] // tm
    d = jnp.arange(NP - A, dtype=jnp.int32)
    e_pad = jnp.sum((d[:, None] >= jnp.cumsum(pcounts - counts)[None, :]).astype(jnp.int32), axis=1)
    keys = jnp.concatenate([e_flat * NP + jnp.arange(A, dtype=jnp.int32), e_pad * NP + A + d])
    toks = jnp.concatenate([jnp.arange(A, dtype=jnp.int32) // 2, jnp.zeros((NP - A,), jnp.int32)])
    wts = jnp.concatenate([w_flat, jnp.zeros((NP - A,), f32)])
    _, src, sw = lax.sort((keys, toks, wts), num_keys=1)
    sw = sw.reshape(NP, 1)
    tiles = jnp.arange(ntiles, dtype=jnp.int32)
    tile_e = jnp.minimum(jnp.searchsorted(pends, tiles * tm, side='right'), N_EXPERTS - 1).astype(jnp.int32)
    tile_e = jnp.where(tiles < n_used, tile_e, tile_e[jnp.maximum(n_used - 1, 0)])
    picks = r[:, 0:2].astype(jnp.int32)
    pick_start = jnp.sum(jnp.where(picks[:, :, None] == jnp.arange(N_EXPERTS)[None, None, :],
                                   pstarts[None, None, :], 0), axis=2)
    slots = pick_start + rank
    return src, sw, tile_e, n_used.reshape(1).astype(jnp.int32), slots


def outproj_moe(H, mo, om, w_out, g, w_router, w_gu, w_down, layer, moe_layer, tm, tf):
    h1, xn, r = outproj_router(H, mo, om, w_out, g, w_router, layer, moe_layer, tm)
    src, sw, tile_e, n_used, slots = _dispatch(r, tm)
    ntiles = src.shape[0] // tm
    bounds = [ntiles * p // MOE_PARTS for p in range(MOE_PARTS + 1)]
    yg = None
    for t0, t1 in zip(bounds[:-1], bounds[1:]):
        yg = moe_grouped(jnp.take(xn, src[t0 * tm:t1 * tm], axis=0), sw, tile_e, n_used, w_gu, w_down, moe_layer,
                         tm, tf, tile0=t0, prev=yg)
    return h1 + jnp.take(yg, slots[:, 0], axis=0) + jnp.take(yg, slots[:, 1], axis=0)


def _shared_kv_kernel(x_ref, g_ref, w_ref, cos_ref, sin_ref, zk_ref, zv_ref,
                      ck_ref, cv_ref, sk_ref, sv_ref, wk_ref, wv_ref, wkp_ref, wvp_ref):
    del zk_ref, zv_ref
    y = _mm(_rms(x_ref[...], g_ref[...]).astype(bf16), w_ref[...])
    cos = cos_ref[...]
    sin = sin_ref[...]
    ck_ref[...] = y[:, 0 * LANES:1 * LANES]
    cv_ref[...] = y[:, 1 * LANES:2 * LANES]
    sk_ref[...] = _rope_lanes(y[:, 2 * LANES:3 * LANES], cos, sin)
    sv_ref[...] = y[:, 3 * LANES:4 * LANES]
    wk = _rope_lanes(y[:, 4 * LANES:5 * LANES], cos, sin)
    wk_ref[...] = wk
    wkp_ref[...] = wk
    wv_ref[...] = y[:, 5 * LANES:6 * LANES]
    wvp_ref[...] = y[:, 5 * LANES:6 * LANES]


def shared_kv(H, g, w, cos, sin, nb, L, tm):
    T, K = H.shape
    assert tm == WINDOW and L % tm == 0
    per = L // tm
    ntp = nb * per
    nblk = nb * (per + 1) + (T // tm - ntp)
    row = pl.BlockSpec((tm, LANES), lambda i: (i, 0))
    table = pl.BlockSpec((tm, LANES), lambda i: (jnp.where(i < ntp, i % per, per), 0))
    padded = pl.BlockSpec((tm, LANES), lambda i: (jnp.where(i < ntp, i + i // per + 1, nb + i), 0))
    out = jax.ShapeDtypeStruct((T, LANES), f32)
    outp = jax.ShapeDtypeStruct((nblk * tm, LANES), f32)
    zeros = jnp.zeros((nblk * tm, LANES), f32)
    anyspec = pl.BlockSpec(memory_space=pl.ANY)
    return pl.pallas_call(
        _shared_kv_kernel,
        grid=(T // tm,),
        in_specs=[pl.BlockSpec((tm, K), lambda i: (i, 0)), pl.BlockSpec((1, K), lambda i: (0, 0)),
                  pl.BlockSpec((K, 6 * LANES), lambda i: (0, 0)), table, table, anyspec, anyspec],
        out_specs=(row,) * 6 + (padded, padded),
        out_shape=(out,) * 6 + (outp, outp),
        input_output_aliases={5: 6, 6: 7},
        compiler_params=_cparams(("parallel",)),
        name="shared_kv",
    )(H, g, w, cos, sin, zeros, zeros)


def _page_strips_kernel(*refs, npages):
    refs = refs[1:]
    ok_ref, ov_ref, scr_ref = refs[2 * npages:]
    per = PAGE // CMP_STRIDE
    for j in range(npages):
        for src, dst in ((refs[j], ok_ref), (refs[npages + j], ov_ref)):
            scr_ref[...] = src[...].T
            for l in range(CMP_STRIDE):
                dst[j * per:(j + 1) * per, l * LANES:(l + 1) * LANES] = scr_ref[pl.ds(l, per, stride=CMP_STRIDE), :]


def page_strips(page_table, pool_k, pool_v, nb):
    npages = page_table.shape[0] // nb
    n = npages * PAGE // CMP_STRIDE
    page = lambda j: pl.BlockSpec((None, LANES, PAGE), lambda b, pt: (pt[b * npages + j], 0, 0))
    out_spec = pl.BlockSpec((None, n, CMP_STRIDE * LANES), lambda b, pt: (b, 0, 0))
    out = jax.ShapeDtypeStruct((nb, n, CMP_STRIDE * LANES), f32)
    grid_spec = pltpu.PrefetchScalarGridSpec(
        num_scalar_prefetch=1, grid=(nb,),
        in_specs=[page(j) for j in range(npages)] * 2, out_specs=(out_spec, out_spec),
        scratch_shapes=[pltpu.VMEM((PAGE, LANES), f32)])
    return pl.pallas_call(
        functools.partial(_page_strips_kernel, npages=npages), grid_spec=grid_spec, out_shape=(out, out),
        compiler_params=_cparams(("parallel",)), name="page_strips",
    )(page_table, *([pool_k] * npages), *([pool_v] * npages))


def _compress_kernel(k_ref, v_ref, pek_ref, pev_ref, w1k_ref, w1v_ref, w2k_ref, w2v_ref, cos_ref, sin_ref,
                     ck_ref, cv_ref):
    k_strips = k_ref[...]
    v_strips = v_ref[...]
    n = k_strips.shape[0]
    half = 2 * CMP_HID
    row = lax.broadcasted_iota(jnp.int32, (n, LANES), 0)

    def tokens(strips, pe_ref, w1_ref, w2_ref):
        top = _mm((strips + pe_ref[0:1, :]).astype(bf16), w1_ref[:, 0:half])
        bot = _mm((strips + pe_ref[1:2, :]).astype(bf16), w1_ref[:, half:2 * half])
        hid = top + pltpu.roll(bot, n - 1, axis=0)
        out = _mm(_silu(hid).astype(bf16), w2_ref[...])
        return jnp.where(row < n - 1, out, 0.0)

    ck_ref[...] = _rope_lanes(tokens(k_strips, pek_ref, w1k_ref, w2k_ref), cos_ref[...], sin_ref[...])
    cv_ref[...] = tokens(v_strips, pev_ref, w1v_ref, w2v_ref)


def compress(k_src, v_src, consts):
    nb, n, flat = k_src.shape
    pek, pev, w1k, w1v, w2k, w2v, cos, sin = consts
    const2 = lambda shape: pl.BlockSpec(shape, lambda b: (0, 0))
    seq = pl.BlockSpec((None, n, flat), lambda b: (b, 0, 0))
    out_spec = pl.BlockSpec((None, n, LANES), lambda b: (b, 0, 0))
    out = jax.ShapeDtypeStruct((nb, n, LANES), f32)
    return pl.pallas_call(
        _compress_kernel, grid=(nb,),
        in_specs=[seq, seq, const2((2, flat)), const2((2, flat)), const2((flat, 4 * CMP_HID)),
                  const2((flat, 4 * CMP_HID)), const2((2 * CMP_HID, LANES)), const2((2 * CMP_HID, LANES)),
                  const2((n, LANES)), const2((n, LANES))],
        out_specs=(out_spec, out_spec), out_shape=(out, out),
        compiler_params=_cparams(("parallel",)), name="compress",
    )(k_src, v_src, pek, pev, w1k, w1v, w2k, w2v, cos, sin)


def _nsa_queries(p, cos, sin):
    tq = p.shape[0]
    lane = lax.broadcasted_iota(jnp.int32, (tq, LANES), 1)
    rows = []
    for head in range(B_HEADS):
        grp = head // HPG
        blk = MEM_W // LANES + head // 2
        x = _rope_lanes(p[:, blk * LANES:(blk + 1) * LANES], cos, sin) * (HEAD_DIM ** -0.5)
        if head % 2 != grp:
            x = pltpu.roll(x, HEAD_DIM, axis=1)
        keep = (lane >= grp * HEAD_DIM) & (lane < (grp + 1) * HEAD_DIM)
        rows.append(jnp.where(keep, x, 0.0))
    return jnp.concatenate(rows, axis=0)


def _masked_softmax_parts(s, mask):
    sm = jnp.where(mask, s, NEG)
    m = jnp.max(sm, axis=-1, keepdims=True)
    p = jnp.where(mask, jnp.exp(sm - m), 0.0)
    return p, jnp.sum(p, axis=-1, keepdims=True)


def _safe_div(o, l):
    return jnp.where(l > 0.0, o / jnp.where(l > 0.0, l, 1.0), 0.0)


def _split3(x):
    hi = x.astype(bf16)
    r1 = x - hi.astype(f32)
    mid = r1.astype(bf16)
    lo = (r1 - mid.astype(f32)).astype(bf16)
    return hi, mid, lo


def _select_bias(pcn, qpos_row, past_blocks, mt_ref, eye_ref, nsel):
    tq = qpos_row.shape[1]
    nr = -(-nsel // SUBLANES) * SUBLANES
    mt = mt_ref[0:nr, :]
    blk = lax.broadcasted_iota(jnp.int32, (nr, tq), 0)
    lane = lax.broadcasted_iota(jnp.int32, (tq, LANES), 1)
    cur = qpos_row // SEL_BLK
    forced = (blk == 0) | (blk == cur) | (blk == cur - 1)
    valid = blk * SEL_BLK <= qpos_row
    stacks, picks = [], []
    for grp in range(2):
        psum = pcn[grp * HPG * tq:(grp * HPG + 1) * tq]
        for hh in range(1, HPG):
            psum = psum + pcn[(grp * HPG + hh) * tq:(grp * HPG + hh + 1) * tq]
        hi, mid, lo = _split3(psum)
        imp = _nt(mt, hi) + _nt(mt, mid) + _nt(mt, lo)
        imp = jnp.where(forced, 1e9, jnp.where(valid, imp, -1e9))
        rank = jnp.zeros((nr, tq), f32)
        for j in range(nsel):
            rj = imp[j:j + 1, :]
            rank = rank + jnp.where(rj > imp, 1.0, jnp.where(rj == imp, jnp.where(blk > j, 1.0, 0.0), 0.0))
        sel_t = jnp.where((rank < SEL_TOP) & (blk < nsel), 1.0, 0.0).astype(bf16)
        pick = _tn(sel_t, eye_ref[0:nr, :])
        picks.append(pick)
        stacks += [jnp.where((pick > 0.5) & (lane < past_blocks), 0.0, NEG)] * HPG
    return jnp.concatenate(stacks, axis=0), picks


def _nsa_combine(p, o_c, o_s, o_w):
    tq = p.shape[0]
    gl = _sigmoid(p[:, (MEM_W + B_W):(MEM_W + B_W) + LANES])

    def gate(j):
        return jnp.concatenate([gl[:, 3 * h + j:3 * h + j + 1] for h in range(B_HEADS)], axis=0)

    comb = gate(0) * o_c + gate(1) * o_s + gate(2) * o_w
    lane = lax.broadcasted_iota(jnp.int32, (tq, LANES), 1)
    cols = []
    for c in range(B_HEADS // 2):
        lo = comb[2 * c * tq:(2 * c + 1) * tq]
        hi = comb[(2 * c + 1) * tq:(2 * c + 2) * tq]
        if (2 * c) // HPG == 1:
            lo = pltpu.roll(lo, HEAD_DIM, axis=1)
        if (2 * c + 1) // HPG == 0:
            hi = pltpu.roll(hi, HEAD_DIM, axis=1)
        cols.append(jnp.where(lane < HEAD_DIM, lo, hi))
    return jnp.concatenate(cols, axis=1)


def _nsa_prompt_kernel(p_ref, cos_ref, sin_ref, ck_ref, cv_ref, sk_ref, sv_ref, wk_ref, wv_ref, mt_ref, et_ref,
                       eye_ref, o_ref, *, tq, kc, nsel):
    qb = pl.program_id(1)
    s0 = qb * tq
    R = B_HEADS * tq
    p = p_ref[...]
    qf = _nsa_queries(p, cos_ref[...], sin_ref[...])
    qb16 = qf.astype(bf16)
    ii = lax.broadcasted_iota(jnp.int32, (tq, tq), 0)
    jj = lax.broadcasted_iota(jnp.int32, (tq, tq), 1)
    qposR = s0 + jnp.concatenate([lax.broadcasted_iota(jnp.int32, (tq, 1), 0)] * B_HEADS, axis=0)
    lane = lax.broadcasted_iota(jnp.int32, (R, LANES), 1)
    cmask = ((lane * CMP_STRIDE + (CMP_BLK - 1)) <= qposR) & (lane < ck_ref.shape[0] - 1)
    pc, lc = _masked_softmax_parts(_nt(qb16, ck_ref[...].astype(bf16)), cmask)
    pcn = _safe_div(pc, lc)
    o_c = _mm(pcn.astype(bf16), cv_ref[...].astype(bf16))
    qpos_row = s0 + lax.broadcasted_iota(jnp.int32, (1, tq), 1)
    bias, _ = _select_bias(pcn, qpos_row, s0 // SEL_BLK, mt_ref, eye_ref, nsel)
    qsel = jnp.concatenate([qb16, bias.astype(bf16)], axis=1)
    sd = _nt(qb16, sk_ref[pl.ds(s0, tq), :].astype(bf16)).reshape(B_HEADS, tq, tq)
    sd = jnp.where((ii >= jj)[None], sd, NEG).reshape(R, tq)
    m0 = jnp.max(sd, axis=-1, keepdims=True)
    pd = jnp.exp(sd - m0)
    init = (m0, jnp.sum(pd, axis=-1, keepdims=True), _mm(pd.astype(bf16), sv_ref[pl.ds(s0, tq), :].astype(bf16)))

    def body(c, carry):
        m_run, l_run, acc = carry
        k0 = pl.multiple_of(c * kc, kc)
        ka = jnp.concatenate([sk_ref[pl.ds(k0, kc), :].astype(bf16), et_ref[pl.ds(k0, kc), :]], axis=1)
        s = _nt(qsel, ka)
        m_new = jnp.maximum(m_run, jnp.max(s, axis=-1, keepdims=True))
        alpha = jnp.exp(m_run - m_new)
        pe = jnp.exp(s - m_new)
        l_new = alpha * l_run + jnp.sum(pe, axis=-1, keepdims=True)
        acc_new = alpha * acc + _mm(pe.astype(bf16), sv_ref[pl.ds(k0, kc), :].astype(bf16))
        return m_new, l_new, acc_new

    _, l_run, acc = lax.fori_loop(0, (s0 + kc - 1) // kc, body, init)
    o_s = acc / l_run
    wlen = WINDOW + tq
    wrow = lax.broadcasted_iota(jnp.int32, (wlen, LANES), 0)
    wlane = lax.broadcasted_iota(jnp.int32, (wlen, LANES), 1)
    is_pad = jnp.where((wrow + (s0 - WINDOW) < 0) & (wlane == 0), 1.0, 0.0).astype(bf16)
    kw = jnp.concatenate([wk_ref[pl.ds(s0, wlen), :].astype(bf16), is_pad], axis=1)
    qwin = jnp.concatenate([qb16, jnp.where(lane == 0, NEG, 0.0).astype(bf16)], axis=1)
    sw = _nt(qwin, kw).reshape(B_HEADS, tq, wlen)
    far = jnp.where((jj > ii)[None], sw[:, :, 0:tq], NEG)
    near = jnp.where((jj <= ii)[None], sw[:, :, wlen - tq:wlen], NEG)
    sw = jnp.concatenate([far, sw[:, :, tq:wlen - tq], near], axis=2).reshape(R, wlen)
    pw = jnp.exp(sw - jnp.max(sw, axis=-1, keepdims=True))
    o_w = _mm(pw.astype(bf16), wv_ref[pl.ds(s0, wlen), :].astype(bf16)) / jnp.sum(pw, axis=-1, keepdims=True)
    o_ref[...] = _nsa_combine(p, o_c, o_s, o_w)


def nsa_prompt(P, cos, sin, ck, cv, sk, sv, wk_pad, wv_pad, mt, et, eye, nb, L, tq, kc):
    nq = L // tq
    nsel = L // SEL_BLK
    seq = pl.BlockSpec((L, LANES), lambda b, i: (b, 0))
    win = pl.BlockSpec((WINDOW + L, LANES), lambda b, i: (b, 0))
    cmp_spec = pl.BlockSpec((None, LANES, LANES), lambda b, i: (b, 0, 0))
    const = lambda shape: pl.BlockSpec(shape, lambda b, i: (0, 0))
    return pl.pallas_call(
        functools.partial(_nsa_prompt_kernel, tq=tq, kc=kc, nsel=nsel),
        grid=(nb, nq),
        in_specs=[pl.BlockSpec((tq, B_IN_PAD), lambda b, i: (b * nq + i, 0)),
                  pl.BlockSpec((tq, LANES), lambda b, i: (i, 0)),
                  pl.BlockSpec((tq, LANES), lambda b, i: (i, 0)),
                  cmp_spec, cmp_spec, seq, seq, win, win,
                  const((LANES, LANES)), const((L, LANES)), const((LANES, LANES))],
        out_specs=pl.BlockSpec((tq, B_W), lambda b, i: (b * nq + i, 0)),
        out_shape=jax.ShapeDtypeStruct((nb * L, B_W), f32),
        compiler_params=_cparams(("parallel", "parallel")),
        name="nsa_prompt",
    )(P, cos, sin, ck, cv, sk, sv, wk_pad, wv_pad, mt, et, eye)


def _nsa_sample_kernel(*refs, npages, past):
    refs = refs[1:]
    kpages = refs[0:npages]
    vpages = refs[npages:2 * npages]
    (p_ref, cos_ref, sin_ref, ck_ref, cv_ref, nsk_ref, nsv_ref, cwk_ref, cwv_ref, nwk_ref, nwv_ref,
     mt_ref, et_ref, eye_ref, o_ref) = refs[2 * npages:]
    tq = S_PAD
    R = B_HEADS * tq
    p = p_ref[...]
    qb16 = _nsa_queries(p, cos_ref[...], sin_ref[...]).astype(bf16)
    tR = jnp.concatenate([lax.broadcasted_iota(jnp.int32, (tq, 1), 0)] * B_HEADS, axis=0)
    qposR = past + tR
    lane = lax.broadcasted_iota(jnp.int32, (R, LANES), 1)
    cmask = ((lane * CMP_STRIDE + (CMP_BLK - 1)) <= qposR) & (lane < ck_ref.shape[0] - 1)
    pc, lc = _masked_softmax_parts(_nt(qb16, ck_ref[...].astype(bf16)), cmask)
    pcn = _safe_div(pc, lc)
    o_c = _mm(pcn.astype(bf16), cv_ref[...].astype(bf16))
    qpos_row = past + lax.broadcasted_iota(jnp.int32, (1, tq), 1)
    new_blk = past // SEL_BLK
    bias, picks = _select_bias(pcn, qpos_row, new_blk, mt_ref, eye_ref, new_blk + 1)
    new_ok = lax.broadcasted_iota(jnp.int32, (R, tq), 1) <= tR
    ka = jnp.concatenate([jnp.concatenate([r[...] for r in kpages], axis=1).astype(bf16), et_ref[...]], axis=0)
    s_old = _mm(jnp.concatenate([qb16, bias.astype(bf16)], axis=1), ka)
    new_kept = jnp.concatenate([picks[h // HPG][:, new_blk:new_blk + 1] for h in range(B_HEADS)], axis=0) > 0.5
    s_new = jnp.where(new_ok, jnp.where(new_kept, _nt(qb16, nsk_ref[...].astype(bf16)), NEG), NEG)
    m = jnp.maximum(jnp.max(s_old, axis=-1, keepdims=True), jnp.max(s_new, axis=-1, keepdims=True))
    p_old = jnp.exp(s_old - m)
    p_new = jnp.exp(s_new - m)
    l = jnp.sum(p_old, axis=-1, keepdims=True) + jnp.sum(p_new, axis=-1, keepdims=True)
    svb = jnp.concatenate([r[...] for r in vpages], axis=1).astype(bf16)
    o_s = (_nt(p_old.astype(bf16), svb) + _mm(p_new.astype(bf16), nsv_ref[...].astype(bf16))) / l
    wb = cwk_ref.shape[1]
    wdist = qposR - (past - wb + lax.broadcasted_iota(jnp.int32, (R, wb), 1))
    w_old = jnp.where((wdist >= 0) & (wdist < WINDOW), _mm(qb16, cwk_ref[...].astype(bf16)), NEG)
    w_new = jnp.where(new_ok, _nt(qb16, nwk_ref[...].astype(bf16)), NEG)
    mw = jnp.maximum(jnp.max(w_old, axis=-1, keepdims=True), jnp.max(w_new, axis=-1, keepdims=True))
    pw_old = jnp.exp(w_old - mw)
    pw_new = jnp.exp(w_new - mw)
    lw = jnp.sum(pw_old, axis=-1, keepdims=True) + jnp.sum(pw_new, axis=-1, keepdims=True)
    o_w = (_nt(pw_old.astype(bf16), cwv_ref[...].astype(bf16)) + _mm(pw_new.astype(bf16), nwv_ref[...].astype(bf16))) / lw
    o_ref[...] = _nsa_combine(p, o_c, o_s, o_w)


def nsa_sample(P, row0, page_table, pool_k, pool_v, cos, sin, ck, cv, nsk, nsv, cwk, cwv, nwk, nwv, mt, et, eye):
    nb = cwk.shape[0]
    npages = page_table.shape[0] // nb
    past = npages * PAGE
    rb0 = row0 // S_PAD
    page = lambda j: pl.BlockSpec((None, LANES, PAGE), lambda b, pt: (pt[b * npages + j], 0, 0))
    new_rows = pl.BlockSpec((S_PAD, LANES), lambda b, pt: (rb0 + b, 0))
    per_seq = lambda n: pl.BlockSpec((None, LANES, n), lambda b, pt: (b, 0, 0))
    in_specs = ([page(j) for j in range(npages)] + [page(j) for j in range(npages)]
                + [pl.BlockSpec((S_PAD, B_IN_PAD), lambda b, pt: (rb0 + b, 0)),
                   pl.BlockSpec((S_PAD, LANES), lambda b, pt: (0, 0)),
                   pl.BlockSpec((S_PAD, LANES), lambda b, pt: (0, 0)),
                   per_seq(LANES), per_seq(LANES), new_rows, new_rows,
                   per_seq(cwk.shape[2]), per_seq(cwk.shape[2]), new_rows, new_rows,
                   pl.BlockSpec((LANES, LANES), lambda b, pt: (0, 0)),
                   pl.BlockSpec((LANES, past), lambda b, pt: (0, 0)),
                   pl.BlockSpec((LANES, LANES), lambda b, pt: (0, 0))])
    grid_spec = pltpu.PrefetchScalarGridSpec(
        num_scalar_prefetch=1, grid=(nb,), in_specs=in_specs,
        out_specs=pl.BlockSpec((S_PAD, B_W), lambda b, pt: (b, 0)))
    return pl.pallas_call(
        functools.partial(_nsa_sample_kernel, npages=npages, past=past),
        grid_spec=grid_spec,
        out_shape=jax.ShapeDtypeStruct((nb * S_PAD, B_W), f32),
        compiler_params=_cparams(("parallel",)),
        name="nsa_sample",
    )(page_table, *([pool_k] * npages), *([pool_v] * npages), P, cos, sin, ck, cv, nsk, nsv, cwk, cwv, nwk, nwv,
      mt, et, eye)


def _rope_tables(pos):
    half = HEAD_DIM // 2
    inv = ROPE_THETA ** (-jnp.arange(half, dtype=f32) / half)
    ang = pos.astype(f32)[:, None] * inv[None, :]
    cos = jnp.tile(jnp.cos(ang), (1, LANES // half))
    sin = jnp.tile(jnp.sin(ang), (1, LANES // half))
    sign = jnp.where((jnp.arange(LANES) % HEAD_DIM) < half, -1.0, 1.0).astype(f32)
    return cos, sin * sign[None, :]


def _cmp_to_sel(ncmp, nsel):
    per = SEL_BLK // CMP_STRIDE
    sub = CMP_BLK // CMP_STRIDE
    i = jnp.arange(LANES)[:, None]
    j = jnp.arange(LANES)[None, :]
    m = jnp.zeros((LANES, LANES), f32)
    for r in range(sub):
        m = m + (((i + r) // per) == j).astype(f32)
    m = jnp.where((i < ncmp) & (j < nsel), m / sub, 0.0)
    return m.astype(bf16)


def _block_expand(nkeys):
    j = jnp.arange(LANES)[:, None]
    k = jnp.arange(nkeys)[None, :]
    return ((k // SEL_BLK) == j).astype(bf16)


def _compress_consts(pe, w1, w2):
    w1r = w1.reshape(CMP_BLK, HEAD_DIM, CMP_HID)
    eye2 = jnp.eye(2, dtype=f32)

    def expand(w):
        return jnp.einsum('ldc,gh->lgdhc', w, eye2).reshape(16 * LANES, 2 * CMP_HID)

    w1s = jnp.concatenate([expand(w1r[:16]), expand(w1r[16:])], axis=1).astype(bf16)
    pes = jnp.stack([jnp.broadcast_to(pe[:16, None, :], (16, 2, HEAD_DIM)).reshape(-1),
                     jnp.broadcast_to(pe[16:, None, :], (16, 2, HEAD_DIM)).reshape(-1)])
    w2s = jnp.einsum('cd,gh->gchd', w2, eye2).reshape(2 * CMP_HID, LANES).astype(bf16)
    return pes, w1s, w2s


def kernel(x_prompt, x_sample, mem_prompt, state_conv, state_ssm, cache_cmp_k, cache_cmp_v, cache_sel_k, cache_sel_v, cache_win_k, cache_win_v, cache_mem_k, cache_mem_v, page_table, norm_mix, norm_ffn, norm_mem, w_mem_kv, w_in_a, conv_w_a, a_log, dt_bias, norm_out_a, w_out_a, w_in_b, w_out_b, norm_kv, w_kv_shared, cmp_pe_k, cmp_w1_k, cmp_w2_k, cmp_pe_v, cmp_w1_v, cmp_w2_v, w_gu_dense, w_down_dense, w_router, w_gu_exp, w_down_exp, norm_final):
    bp, lp, d = x_prompt.shape
    bs, ls, _ = x_sample.shape
    depth = norm_mix.shape[0]
    n_a = w_in_a.shape[0]
    past = page_table.shape[1] * PAGE
    tp = bp * lp
    T = tp + bs * S_PAD
    tm = 512

    xs = jnp.pad(x_sample, ((0, 0), (0, S_PAD - ls), (0, 0)))
    H = jnp.concatenate([x_prompt.reshape(tp, d), xs.reshape(bs * S_PAD, d)], axis=0)

    qkvz = MEM_W + CONV_DIM + A_W
    w_in_a_b = jnp.concatenate(
        [w_in_a[:, :, MEM_W:qkvz], w_in_a[:, :, :MEM_W], w_in_a[:, :, qkvz:],
         jnp.zeros(w_in_a.shape[:2] + (A_IN_PAD - w_in_a.shape[2],), f32)], axis=2).astype(bf16)
    w_in_b_b = jnp.pad(w_in_b, ((0, 0), (0, 0), (0, B_IN_PAD - w_in_b.shape[2]))).astype(bf16)
    w_out_a_b = w_out_a.astype(bf16)
    w_out_b_b = w_out_b.astype(bf16)
    w_gu_dense_b = w_gu_dense.astype(bf16)
    w_down_dense_b = w_down_dense.astype(bf16)
    w_gu_exp_b = w_gu_exp.astype(bf16)
    w_down_exp_b = w_down_exp.astype(bf16)
    w_router_p = jnp.pad(w_router, ((0, 0), (0, 0), (0, LANES - N_EXPERTS)))

    mem_k, mem_v = mem_kv(mem_prompt.reshape(bp * MEM_LEN, d), norm_mem.reshape(depth, 1, d),
                          w_mem_kv.astype(bf16), tm)
    cmk = jnp.transpose(cache_mem_k, (0, 1, 3, 4, 2)).reshape(depth, bs, MEM_W, MEM_LEN)
    cmv = jnp.transpose(cache_mem_v, (0, 1, 3, 4, 2)).reshape(depth, bs, MEM_W, MEM_LEN)
    zero_buf = jnp.zeros((bp, SUBLANES, CONV_DIM), f32)
    zero_state = jnp.zeros((1, bp, A_HEADS, LANES, LANES), f32)

    pos_p = jnp.arange(lp, dtype=jnp.int32)
    pos_s = past + jnp.arange(S_PAD, dtype=jnp.int32)
    cos_p, sin_p = _rope_tables(pos_p)
    cos_s, sin_s = _rope_tables(pos_s)

    conv_p, ssm_p, conv_s, ssm_s = [], [], [], []
    kv = None
    for layer in range(depth):
        is_a = layer < n_a
        g_mix = norm_mix[layer].reshape(1, d)
        if is_a:
            P = norm_matmul(H, g_mix, w_in_a_b, layer, tm, A_IN_PAD // 3)
            qblk = A_MEMQ_BLK
        else:
            lb = layer - n_a
            qblk = 0
            if lb == 0:
                cos_all = jnp.concatenate([cos_p, jnp.tile(cos_s, (tm // S_PAD, 1))], axis=0)
                sin_all = jnp.concatenate([sin_p, jnp.tile(sin_s, (tm // S_PAD, 1))], axis=0)
                kv = shared_kv(H, norm_kv.reshape(1, d), w_kv_shared.astype(bf16), cos_all, sin_all, bp, lp, tm)
                wk_pad, wv_pad = kv[6], kv[7]
                ncmp = (lp - CMP_BLK) // CMP_STRIDE + 1
                cpos = jnp.arange(LANES, dtype=jnp.int32) * CMP_STRIDE + CMP_BLK - 1
                cos_c, sin_c = _rope_tables(cpos)
                pek, w1k, w2k = _compress_consts(cmp_pe_k, cmp_w1_k, cmp_w2_k)
                pev, w1v, w2v = _compress_consts(cmp_pe_v, cmp_w1_v, cmp_w2_v)
                consts = (pek, pev, w1k, w1v, w2k, w2v, cos_c, sin_c)
                strips = lp // 16
                ck_p, cv_p = compress(kv[0][:tp].reshape(bp, strips, 16 * LANES),
                                      kv[1][:tp].reshape(bp, strips, 16 * LANES), consts)
                pt_flat = page_table.reshape(-1)
                feature_major = lambda pool: jnp.transpose(pool, (0, 2, 3, 1)).reshape(-1, LANES, PAGE)
                ck_s, cv_s = compress(*page_strips(pt_flat, feature_major(cache_cmp_k), feature_major(cache_cmp_v), bs),
                                      consts)
                mt_p = _cmp_to_sel(ncmp, lp // SEL_BLK).T
                et_p = _block_expand(lp).T
                ncmp_s = (past + ls - CMP_BLK) // CMP_STRIDE + 1
                mt_s = _cmp_to_sel(ncmp_s, -(-(past + ls) // SEL_BLK)).T
                e_s = _block_expand(past)
                eye = jnp.eye(LANES, dtype=bf16)
                pool_sk = feature_major(cache_sel_k)
                pool_sv = feature_major(cache_sel_v)
                cwk = jnp.transpose(cache_win_k, (0, 2, 3, 1)).reshape(bs, LANES, -1)
                cwv = jnp.transpose(cache_win_v, (0, 2, 3, 1)).reshape(bs, LANES, -1)
            P = norm_matmul(H, g_mix, w_in_b_b, lb, tm, B_IN_PAD)
        om = (mem_attn_prompt(P, qblk, mem_k, mem_v, layer, bp, lp, 512),
              mem_attn_sample(P, qblk, tp, cmk, cmv, layer, 8))
        if is_a:
            conv8 = jnp.pad(conv_w_a[layer], ((0, SUBLANES - conv_w_a.shape[1]), (0, 0)))
            ab8 = jnp.zeros((SUBLANES, LANES), f32)
            ab8 = ab8.at[0, A_HEADS:2 * A_HEADS].set(a_log[layer]).at[1, A_HEADS:2 * A_HEADS].set(dt_bias[layer])
            gout = norm_out_a[layer].reshape(1, LANES)
            mo_p, sp = delta_mixer(P, 0, bp, lp, 512, 64, 512, zero_buf, conv8, ab8, gout, zero_state, 0)
            buf_s = jnp.pad(state_conv[layer], ((0, 0), (SUBLANES - state_conv.shape[2], 0), (0, 0)))
            mo_s, ss = delta_mixer(P, tp, bs, S_PAD, S_PAD, S_PAD, ls, buf_s, conv8, ab8, gout, state_ssm, layer)
            conv_p.append(jnp.stack([P[b * lp + lp - 3:(b + 1) * lp, :CONV_DIM] for b in range(bp)]))
            conv_s.append(P[tp:, :CONV_DIM].reshape(bs, S_PAD, CONV_DIM)[:, ls - 3:ls])
            ssm_p.append(sp)
            ssm_s.append(ss)
            w_out, wl = w_out_a_b, layer
        else:
            mo_p = nsa_prompt(P, cos_p, sin_p, ck_p, cv_p, kv[2], kv[3], wk_pad, wv_pad, mt_p, et_p, eye,
                              bp, lp, Q_BLOCK, 512)
            mo_s = nsa_sample(P, tp, pt_flat, pool_sk, pool_sv, cos_s, sin_s, ck_s, cv_s, kv[2], kv[3], cwk, cwv,
                              kv[4], kv[5], mt_s, e_s, eye)
            w_out, wl = w_out_b_b, lb
        mo = (mo_p, mo_s)
        g_ffn = norm_ffn[layer].reshape(1, d)
        if layer % 2 == 0:
            H = outproj_ffn(H, mo, om, w_out, g_ffn, w_gu_dense_b, w_down_dense_b, wl, layer // 2, tm, D_FF // 2)
        else:
            H = outproj_moe(H, mo, om, w_out, g_ffn, w_router_p, w_gu_exp_b, w_down_exp_b, wl, layer // 2,
                            tm, D_FF_EXPERT // 2)
    Y = final_norm(H, norm_final.reshape(1, d), tm)

    def rows_p(a):
        return a[:tp].reshape(bp, lp, 2, HEAD_DIM)

    def rows_s(a):
        return a[tp:].reshape(bs, S_PAD, 2, HEAD_DIM)[:, :ls]

    wlen = min(WINDOW, lp)
    return (Y[:tp].reshape(bp, lp, d), Y[tp:].reshape(bs, S_PAD, d)[:, :ls],
            jnp.stack(conv_p), jnp.stack(ssm_p),
            rows_p(kv[0]), rows_p(kv[1]), rows_p(kv[2]), rows_p(kv[3]),
            rows_p(kv[4])[:, lp - wlen:], rows_p(kv[5])[:, lp - wlen:],
            mem_k.reshape(depth, bp, MEM_LEN, 4, HEAD_DIM), mem_v.reshape(depth, bp, MEM_LEN, 4, HEAD_DIM),
            jnp.stack(conv_s), jnp.stack(ssm_s),
            rows_s(kv[0]), rows_s(kv[1]), rows_s(kv[2]), rows_s(kv[3]),
            jnp.concatenate([cache_win_k[:, ls:], rows_s(kv[4])], axis=1),
            jnp.concatenate([cache_win_v[:, ls:], rows_s(kv[5])], axis=1))
```
